```python
import math
import jax, jax.numpy as jnp
from jax import lax
import numpy as np

D_MODEL = 1024
BATCH = 16
SEQ = 2048
DEPTH = 1

NSA_HEADS = 8
NSA_KV_HEADS = 2
NSA_HEAD_DIM = 64
CMP_BLOCK = 32
CMP_STRIDE = 16
CMP_HIDDEN = 256
SEL_BLOCK = 64
SEL_TOPK = 8
WINDOW = 512
Q_BLOCK = 64
FORCE_BONUS = 1e4
GLA_HEADS = 4
GLA_KEY_DIM = 64
GLA_VAL_DIM = 128
GLA_CHUNK = 64
GLA_GATE_RANK = 16
GLA_TAU = 16.0
ROPE_THETA = 500000.0
ROPE_DIM = NSA_HEAD_DIM // 4
N_EXPERTS = 32
TOP_K = 4
D_FF = 1024
SWIGLU_LIMIT = 7.0
SWIGLU_ALPHA = 1.702
EXPERT_BLOCK = 256
EPS = 1e-6
NEG_INF = -1e30

NSA_WIDTH = NSA_HEADS * NSA_HEAD_DIM
NSA_KV_WIDTH = NSA_KV_HEADS * NSA_HEAD_DIM
GLA_KEY_WIDTH = GLA_HEADS * GLA_KEY_DIM
GLA_WIDTH = GLA_HEADS * GLA_VAL_DIM
IN_SIZES = (NSA_WIDTH, NSA_KV_WIDTH, NSA_KV_WIDTH, NSA_KV_WIDTH, NSA_KV_WIDTH, NSA_KV_WIDTH, NSA_KV_WIDTH,
            NSA_HEADS * 3, GLA_KEY_WIDTH, GLA_KEY_WIDTH, GLA_WIDTH, GLA_GATE_RANK, GLA_WIDTH)
IN_WIDTH = sum(IN_SIZES)

kernel_name = 'hybrid_nsa_gla_moe_block'


def rms_norm(t, g):
    tf = t.astype(jnp.float32)
    y = tf * lax.rsqrt(jnp.mean(tf * tf, axis=-1, keepdims=True) + EPS) * g.astype(jnp.float32)
    return y.astype(t.dtype)


def masked_softmax(s, mask):
    s = jnp.where(mask, s.astype(jnp.float32), NEG_INF)
    m = jnp.max(s, axis=-1, keepdims=True)
    p = jnp.exp(s - m) * mask
    den = jnp.sum(p, axis=-1, keepdims=True)
    return p / jnp.where(den > 0, den, 1.0)


def apply_partial_rope(t, positions):
    half = ROPE_DIM // 2
    inv_freq = jnp.exp(-math.log(ROPE_THETA) * jnp.arange(0, ROPE_DIM, 2, dtype=jnp.float32) / ROPE_DIM)
    ang = positions.astype(jnp.float32)[..., None] * inv_freq
    cos = jnp.cos(ang)[:, :, None, :]
    sin = jnp.sin(ang)[:, :, None, :]
    tf = t.astype(jnp.float32)
    x1 = tf[..., :half]
    x2 = tf[..., half:ROPE_DIM]
    out = jnp.concatenate([x1 * cos - x2 * sin, x2 * cos + x1 * sin, tf[..., ROPE_DIM:]], axis=-1)
    return out.astype(t.dtype)


def compress_blocks(t, pos_emb, w1, w2):
    B, S, Hk, dh = t.shape
    n_cmp = (S - CMP_BLOCK) // CMP_STRIDE + 1
    blk_idx = jnp.arange(n_cmp)[:, None] * CMP_STRIDE + jnp.arange(CMP_BLOCK)[None, :]
    tb = t[:, blk_idx] + pos_emb[None, None, :, None, :]
    tb = tb.transpose(0, 1, 3, 2, 4).reshape(B, n_cmp, Hk, CMP_BLOCK * dh)
    return jax.nn.silu(tb @ w1) @ w2


def nsa_mixer(q, k_c, v_c, k_s, v_s, k_w, v_w, gate_logits, positions, q_norm_g, k_norm_g,
              cmp_pos_k, cmp_w1_k, cmp_w2_k, cmp_pos_v, cmp_w1_v, cmp_w2_v):
    B, S = q.shape[0], q.shape[1]
    Hk = NSA_KV_HEADS
    G = NSA_HEADS // Hk
    dh = NSA_HEAD_DIM
    scale = dh ** -0.5
    kv = lambda t: t.reshape(B, S, Hk, dh)
    q = rms_norm(q.reshape(B, S, NSA_HEADS, dh), q_norm_g)
    q_rot = apply_partial_rope(q, positions)
    qg = q.reshape(B, S, Hk, G, dh)
    t_idx = jnp.arange(S)

    kc = rms_norm(compress_blocks(kv(k_c), cmp_pos_k, cmp_w1_k, cmp_w2_k), k_norm_g[0])
    vc = compress_blocks(kv(v_c), cmp_pos_v, cmp_w1_v, cmp_w2_v)
    n_cmp = kc.shape[1]
    cmp_start = jnp.arange(n_cmp) * CMP_STRIDE
    mask_c = (cmp_start + CMP_BLOCK - 1)[None, :] <= t_idx[:, None]
    s_c = jnp.einsum('bskgd,bnkd->bkgsn', qg, kc) * scale
    p_c = masked_softmax(s_c, mask_c)
    o_cmp = jnp.einsum('bkgsn,bnkd->bskgd', p_c.astype(vc.dtype), vc)

    n_sel = S // SEL_BLOCK
    k_sel = min(SEL_TOPK, n_sel)
    j = jnp.arange(n_sel)
    overlap = ((cmp_start[None, :] < (j[:, None] + 1) * SEL_BLOCK) &
               (cmp_start[None, :] + CMP_BLOCK > j[:, None] * SEL_BLOCK)).astype(jnp.float32)
    imp = jnp.einsum('bkgsn,jn->bksj', p_c, overlap)
    cur = t_idx // SEL_BLOCK
    valid = j[None, :] <= cur[:, None]
    forced = (j[None, :] == 0) | (j[None, :] == cur[:, None]) | (j[None, :] == cur[:, None] - 1)
    score = jnp.where(valid, imp + jnp.where(forced, FORCE_BONUS, 0.0), NEG_INF)
    _, sel_idx = lax.top_k(score, k_sel)

    ks = apply_partial_rope(rms_norm(kv(k_s), k_norm_g[1]), positions)
    ks_blocks = ks.reshape(B, n_sel, SEL_BLOCK, Hk, dh).transpose(0, 3, 1, 2, 4)
    vs_blocks = kv(v_s).reshape(B, n_sel, SEL_BLOCK, Hk, dh).transpose(0, 3, 1, 2, 4)
    kw = apply_partial_rope(rms_norm(kv(k_w), k_norm_g[2]), positions).transpose(0, 2, 1, 3)
    vw = kv(v_w).transpose(0, 2, 1, 3)
    pad = ((0, 0), (0, 0), (WINDOW, 0), (0, 0))
    kw_pad = jnp.pad(kw, pad)
    vw_pad = jnp.pad(vw, pad)

    n_qb = S // Q_BLOCK
    q_blocks = q_rot.reshape(B, n_qb, Q_BLOCK, Hk, G, dh).swapaxes(0, 1)
    idx_blocks = sel_idx.reshape(B, Hk, n_qb, Q_BLOCK, k_sel).transpose(2, 0, 1, 3, 4)
    b_ix = jnp.arange(B)[:, None, None, None]
    h_ix = jnp.arange(Hk)[None, :, None, None]

    def block_attn(args):
        c, qb, ib = args
        tq = c * Q_BLOCK + jnp.arange(Q_BLOCK)
        kg = ks_blocks[b_ix, h_ix, ib]
        vg = vs_blocks[b_ix, h_ix, ib]
        s = jnp.einsum('bqkgd,bkqnld->bkgqnl', qb, kg) * scale
        kpos = ib[..., None] * SEL_BLOCK + jnp.arange(SEL_BLOCK)
        m = (kpos <= tq[None, None, :, None, None])[:, :, None]
        shp = s.shape
        p = masked_softmax(s.reshape(shp[:4] + (k_sel * SEL_BLOCK,)),
                           m.reshape(B, Hk, 1, Q_BLOCK, k_sel * SEL_BLOCK)).reshape(shp)
        o_sel = jnp.einsum('bkgqnl,bkqnld->bqkgd', p.astype(vg.dtype), vg)
        kwin = lax.dynamic_slice_in_dim(kw_pad, c * Q_BLOCK, Q_BLOCK + WINDOW, axis=2)
        vwin = lax.dynamic_slice_in_dim(vw_pad, c * Q_BLOCK, Q_BLOCK + WINDOW, axis=2)
        jpos = c * Q_BLOCK - WINDOW + jnp.arange(Q_BLOCK + WINDOW)
        mw = (jpos[None, :] <= tq[:, None]) & (jpos[None, :] > tq[:, None] - WINDOW) & (jpos[None, :] >= 0)
        s_w = jnp.einsum('bqkgd,bkjd->bkgqj', qb, kwin) * scale
        p_w = masked_softmax(s_w, mw)
        o_win = jnp.einsum('bkgqj,bkjd->bqkgd', p_w.astype(vwin.dtype), vwin)
        return o_sel, o_win

    o_sel, o_win = lax.map(block_attn, (jnp.arange(n_qb), q_blocks, idx_blocks))
    o_sel = o_sel.swapaxes(0, 1).reshape(B, S, Hk, G, dh)
    o_win = o_win.swapaxes(0, 1).reshape(B, S, Hk, G, dh)

    gates = jax.nn.sigmoid(gate_logits.astype(jnp.float32)).reshape(B, S, Hk, G, 3).astype(o_cmp.dtype)
    o = gates[..., 0:1] * o_cmp + gates[..., 1:2] * o_sel + gates[..., 2:3] * o_win
    return o.reshape(B, S, NSA_WIDTH)


def gla_mixer(q, k, v, gate_lr, r, w_gate, b_gate, norm_g):
    B, S = q.shape[0], q.shape[1]
    H, dk, dv, C = GLA_HEADS, GLA_KEY_DIM, GLA_VAL_DIM, GLA_CHUNK
    n_ch = S // C
    f32 = jnp.float32
    log_a = jax.nn.log_sigmoid((gate_lr @ w_gate + b_gate).astype(f32)) / GLA_TAU

    def to_chunks(t, d):
        return t.astype(f32).reshape(B, n_ch, C, H, d).transpose(0, 3, 1, 2, 4)

    qf = to_chunks(q, dk) * (dk ** -0.5)
    kf = to_chunks(k, dk)
    vf = to_chunks(v, dv)
    b = jnp.cumsum(to_chunks(log_a, dk), axis=3)
    b_last = b[:, :, :, -1:, :]
    q_dec = qf * jnp.exp(b)
    k_dec = kf * jnp.exp(-b)
    k_state = kf * jnp.exp(b_last - b)
    causal = jnp.tril(jnp.ones((C, C), dtype=bool))
    attn = jnp.where(causal, jnp.einsum('bhncd,bhnsd->bhncs', q_dec, k_dec), 0.0)
    o_intra = jnp.einsum('bhncs,bhnse->bhnce', attn, vf)
    u = jnp.einsum('bhnsd,bhnse->bhnde', k_state, vf)
    decay = jnp.exp(b_last[:, :, :, 0, :])

    def step(state, inp):
        d, u_n = inp
        return d[..., None] * state + u_n, state

    _, s_prev = lax.scan(step, jnp.zeros((B, H, dk, dv), f32),
                         (decay.transpose(2, 0, 1, 3), u.transpose(2, 0, 1, 3, 4)))
    s_prev = s_prev.transpose(1, 2, 0, 3, 4)
    o_inter = jnp.einsum('bhncd,bhnde->bhnce', q_dec, s_prev)
    o = (o_intra + o_inter).transpose(0, 2, 3, 1, 4).reshape(B, S, H, dv)
    o = rms_norm(o, norm_g).reshape(B, S, GLA_WIDTH)
    return (o * jax.nn.silu(r.astype(f32))).astype(q.dtype)


def clamped_swiglu(h):
    x_glu = jnp.minimum(h[..., :D_FF], SWIGLU_LIMIT)
    x_lin = jnp.clip(h[..., D_FF:], -SWIGLU_LIMIT, SWIGLU_LIMIT)
    return x_glu * jax.nn.sigmoid(SWIGLU_ALPHA * x_glu) * (x_lin + 1.0)


def moe_ffn(h, w_router, b_router, w_gate_up, b_gate_up, w_down, b_down):
    B, S, D = h.shape
    xf = h.reshape(-1, D)
    M = xf.shape[0]
    logits = (xf @ w_router + b_router).astype(jnp.float32)
    top_vals, top_idx = lax.top_k(logits, TOP_K)
    gate = jax.nn.softmax(top_vals, axis=-1)
    A = M * TOP_K
    flat_e = top_idx.reshape(-1)
    flat_tok = jnp.arange(A, dtype=jnp.int32) // TOP_K
    order = jnp.argsort(flat_e)
    sorted_e = flat_e[order]
    sorted_tok = flat_tok[order]
    counts = jnp.zeros((N_EXPERTS,), jnp.int32).at[flat_e].add(1)
    padded = (counts + EXPERT_BLOCK - 1) // EXPERT_BLOCK * EXPERT_BLOCK
    start = jnp.cumsum(counts) - counts
    pend = jnp.cumsum(padded)
    pstart = pend - padded
    dest = pstart[sorted_e] + (jnp.arange(A, dtype=jnp.int32) - start[sorted_e])
    n_blocks = -(-(A + N_EXPERTS * (EXPERT_BLOCK - 1)) // EXPERT_BLOCK)
    R = n_blocks * EXPERT_BLOCK
    xs = jnp.zeros((R, D), xf.dtype).at[dest].set(xf[sorted_tok])
    block_e = jnp.minimum(jnp.searchsorted(pend, jnp.arange(n_blocks) * EXPERT_BLOCK, side='right'),
                          N_EXPERTS - 1)

    def expert_block(args):
        xb, e = args
        hgu = xb @ w_gate_up[e] + b_gate_up[e]
        return clamped_swiglu(hgu) @ w_down[e] + b_down[e]

    ys = lax.map(expert_block, (xs.reshape(n_blocks, EXPERT_BLOCK, D), block_e)).reshape(R, D)
    y_sorted = ys[dest] * gate.reshape(-1)[order][:, None].astype(ys.dtype)
    out = jax.ops.segment_sum(y_sorted, sorted_tok, num_segments=M)
    return out.reshape(B, S, D)


def setup_inputs(seed: int = 0) -> dict:
    key = jax.random.key(seed)
    ks = jax.random.split(key, 24)
    f32 = jnp.float32
    L = DEPTH
    dh = NSA_HEAD_DIM

    def nrm(k, shape, scale):
        return jax.random.normal(k, shape, f32) * scale

    x = nrm(ks[0], (BATCH, SEQ, D_MODEL), 1.0)
    offsets = jax.random.randint(ks[1], (BATCH, 1), 0, 1024, dtype=jnp.int32)
    positions = offsets + jnp.arange(SEQ, dtype=jnp.int32)[None, :]
    return {
        'x': x,
        'positions': positions,
        'norm1_g': 1.0 + nrm(ks[2], (L, D_MODEL), 0.02),
        'w_in': nrm(ks[3], (L, D_MODEL, IN_WIDTH), D_MODEL ** -0.5),
        'nsa_q_norm_g': 1.0 + nrm(ks[4], (L, dh), 0.02),
        'nsa_k_norm_g': 1.0 + nrm(ks[5], (L, 3, dh), 0.02),
        'cmp_pos_k': nrm(ks[6], (L, CMP_BLOCK, dh), 0.02),
        'cmp_w1_k': nrm(ks[7], (L, CMP_BLOCK * dh, CMP_HIDDEN), (CMP_BLOCK * dh) ** -0.5),
        'cmp_w2_k': nrm(ks[8], (L, CMP_HIDDEN, dh), CMP_HIDDEN ** -0.5),
        'cmp_pos_v': nrm(ks[9], (L, CMP_BLOCK, dh), 0.02),
        'cmp_w1_v': nrm(ks[10], (L, CMP_BLOCK * dh, CMP_HIDDEN), (CMP_BLOCK * dh) ** -0.5),
        'cmp_w2_v': nrm(ks[11], (L, CMP_HIDDEN, dh), CMP_HIDDEN ** -0.5),
        'gla_w_gate': nrm(ks[12], (L, GLA_GATE_RANK, GLA_KEY_WIDTH), GLA_GATE_RANK ** -0.5),
        'gla_b_gate': nrm(ks[13], (L, GLA_KEY_WIDTH), 0.1),
        'gla_norm_g': 1.0 + nrm(ks[14], (L, GLA_VAL_DIM), 0.02),
        'w_out': nrm(ks[15], (L, NSA_WIDTH + GLA_WIDTH, D_MODEL), (NSA_WIDTH + GLA_WIDTH) ** -0.5),
        'norm2_g': 1.0 + nrm(ks[16], (L, D_MODEL), 0.02),
        'w_router': nrm(ks[17], (L, D_MODEL, N_EXPERTS), D_MODEL ** -0.5),
        'b_router': nrm(ks[18], (L, N_EXPERTS), 0.01),
        'w_gate_up': nrm(ks[19], (L, N_EXPERTS, D_MODEL, 2 * D_FF), D_MODEL ** -0.5),
        'b_gate_up': nrm(ks[20], (L, N_EXPERTS, 2 * D_FF), 0.01),
        'w_down': nrm(ks[21], (L, N_EXPERTS, D_FF, D_MODEL), D_FF ** -0.5),
        'b_down': nrm(ks[22], (L, N_EXPERTS, D_MODEL), 0.01),
    }


def reference(x, positions, norm1_g, w_in, nsa_q_norm_g, nsa_k_norm_g, cmp_pos_k, cmp_w1_k, cmp_w2_k,
              cmp_pos_v, cmp_w1_v, cmp_w2_v, gla_w_gate, gla_b_gate, gla_norm_g, w_out, norm2_g,
              w_router, b_router, w_gate_up, b_gate_up, w_down, b_down):
    split_points = [int(p) for p in np.cumsum(IN_SIZES)[:-1]]
    for l in range(DEPTH):
        h = rms_norm(x, norm1_g[l])
        proj = h @ w_in[l]
        (nq, nkc, nvc, nks, nvs, nkw, nvw, ngate, gq, gk, gv, glr, gr) = jnp.split(proj, split_points, axis=-1)
        nsa_out = nsa_mixer(nq, nkc, nvc, nks, nvs, nkw, nvw, ngate, positions, nsa_q_norm_g[l], nsa_k_norm_g[l],
                            cmp_pos_k[l], cmp_w1_k[l], cmp_w2_k[l], cmp_pos_v[l], cmp_w1_v[l], cmp_w2_v[l])
        gla_out = gla_mixer(gq, gk, gv, glr, gr, gla_w_gate[l], gla_b_gate[l], gla_norm_g[l])
        x = x + jnp.concatenate([nsa_out, gla_out], axis=-1) @ w_out[l]
        h2 = rms_norm(x, norm2_g[l])
        x = x + moe_ffn(h2, w_router[l], b_router[l], w_gate_up[l], b_gate_up[l], w_down[l], b_down[l])
    return x
```

```python
import functools
import math

import jax
import jax.numpy as jnp
from jax import lax
from jax.experimental import pallas as pl
from jax.experimental.pallas import tpu as pltpu

f32 = jnp.float32
bf16 = jnp.bfloat16

NSA_HEADS = 8
NSA_KV_HEADS = 2
NSA_GROUP = NSA_HEADS // NSA_KV_HEADS
HEAD_DIM = 64
CMP_BLOCK = 32
CMP_STRIDE = 16
CMP_HIDDEN = 256
SEL_BLOCK = 64
SEL_TOPK = 8
WINDOW = 512
FORCE_BONUS = 1e4
GLA_HEADS = 4
GLA_KEY_DIM = 64
GLA_VAL_DIM = 128
GLA_CHUNK = 64
GLA_GATE_RANK = 16
GLA_TAU = 16.0
ROPE_THETA = 500000.0
ROPE_DIM = HEAD_DIM // 4
N_EXPERTS = 32
TOP_K = 4
D_FF = 1024
SWIGLU_LIMIT = 7.0
SWIGLU_ALPHA = 1.702
EPS = 1e-6
NEG_INF = -1e30

NSA_WIDTH = NSA_HEADS * HEAD_DIM
NSA_KV_WIDTH = NSA_KV_HEADS * HEAD_DIM
GLA_KEY_WIDTH = GLA_HEADS * GLA_KEY_DIM
GLA_WIDTH = GLA_HEADS * GLA_VAL_DIM
IN_SIZES = (NSA_WIDTH,) + (NSA_KV_WIDTH,) * 6 + (
    NSA_HEADS * 3, GLA_KEY_WIDTH, GLA_KEY_WIDTH, GLA_WIDTH, GLA_GATE_RANK, GLA_WIDTH)

LANES = 128
VMEM_LIMIT = 48 * 1024 * 1024
MOE_BLOCK = 256

_NT = (((1,), (1,)), ((), ()))
_TN = (((0,), (0,)), ((), ()))


def _cparams(*sem):
    return pltpu.CompilerParams(dimension_semantics=sem, vmem_limit_bytes=VMEM_LIMIT)


def _rms(t, g):
    return t * lax.rsqrt(jnp.mean(t * t, axis=-1, keepdims=True) + EPS) * g


def _split_bf16(t):
    hi = t.astype(bf16)
    lo = (t - hi.astype(f32)).astype(bf16)
    return hi, lo


def _rope(t, cos_f, sin_f, perm):
    hi, lo = _split_bf16(t)
    rot = (jnp.dot(hi, perm, preferred_element_type=f32) +
           jnp.dot(lo, perm, preferred_element_type=f32))
    return t * cos_f + rot * sin_f


def _inproj_kernel(x_ref, g_ref, w_ref, ws_ref, o_ref, os_ref):
    x = x_ref[...]
    h = _rms(x, g_ref[...]).astype(bf16)
    n = o_ref.shape[1]
    step = 512
    for c in range(0, n, step):
        e = min(c + step, n)
        o_ref[:, c:e] = jnp.dot(h, w_ref[:, c:e], preferred_element_type=f32).astype(bf16)
    os_ref[...] = jnp.dot(h, ws_ref[...], preferred_element_type=f32)


def _inproj(x2, g, w_main, w_small, tm=512):
    m, d = x2.shape
    n = w_main.shape[1]
    return pl.pallas_call(
        _inproj_kernel,
        grid=(m // tm,),
        in_specs=[pl.BlockSpec((tm, d), lambda i: (i, 0)),
                  pl.BlockSpec((1, d), lambda i: (0, 0)),
                  pl.BlockSpec((d, n), lambda i: (0, 0)),
                  pl.BlockSpec((d, LANES), lambda i: (0, 0))],
        out_specs=[pl.BlockSpec((tm, n), lambda i: (i, 0)),
                   pl.BlockSpec((tm, LANES), lambda i: (i, 0))],
        out_shape=[jax.ShapeDtypeStruct((m, n), bf16),
                   jax.ShapeDtypeStruct((m, LANES), f32)],
        compiler_params=_cparams("parallel"),
        name="inproj",
    )(x2, g, w_main, w_small)


def _kvprep_kernel(kch_ref, vch_ref, ks_ref, kw_ref, cos_ref, sin_ref, kg_ref, perm_ref,
                   posk_ref, w1k_ref, w2k_ref, posv_ref, w1v_ref, w2v_ref,
                   kc_ref, vc_ref, kso_ref, kwo_ref):
    half = CMP_STRIDE * HEAD_DIM

    def compress(h_ref, pos_ref, w1_ref, w2_ref):
        hb = h_ref[0, 0].astype(f32)
        n = hb.shape[0]
        top = (hb + pos_ref[:, :half]).astype(bf16)
        bot = (hb + pos_ref[:, half:]).astype(bf16)
        a = jnp.dot(top, w1_ref[:half, :], preferred_element_type=f32)
        b = jnp.dot(bot, w1_ref[half:, :], preferred_element_type=f32)
        pre = a + pltpu.roll(b, n - 1, 0)
        hid = pre * jax.nn.sigmoid(pre)
        return jnp.dot(hid.astype(bf16), w2_ref[...], preferred_element_type=f32)

    kc = compress(kch_ref, posk_ref, w1k_ref, w2k_ref)
    kc_ref[0, 0] = _rms(kc, kg_ref[0:1, :]).astype(bf16)
    vc_ref[0, 0] = compress(vch_ref, posv_ref, w1v_ref, w2v_ref).astype(bf16)
    cos_f = cos_ref[0]
    sin_f = sin_ref[0]
    perm = perm_ref[...]
    ks = _rms(ks_ref[0, 0].astype(f32), kg_ref[1:2, :])
    kso_ref[0, 0] = _rope(ks, cos_f, sin_f, perm).astype(bf16)
    kw = _rms(kw_ref[0, 0].astype(f32), kg_ref[2:3, :])
    kwo_ref[0, 0] = _rope(kw, cos_f, sin_f, perm).astype(bf16)


def _kvprep(kch, vch, ks, kw, cos_f, sin_f, kg, perm, posk, w1k, w2k, posv, w1v, w2v):
    b, hk, nh, hw = kch.shape
    s = ks.shape[2]
    per_head = lambda r, c: pl.BlockSpec((1, 1, r, c), lambda i, j: (i, j, 0, 0))
    whole = lambda a: pl.BlockSpec(a.shape, lambda i, j: (0,) * a.ndim)
    tab = pl.BlockSpec((1, s, HEAD_DIM), lambda i, j: (i, 0, 0))
    return pl.pallas_call(
        _kvprep_kernel,
        grid=(b, hk),
        in_specs=[per_head(nh, hw), per_head(nh, hw), per_head(s, HEAD_DIM), per_head(s, HEAD_DIM),
                  tab, tab, whole(kg), whole(perm),
                  whole(posk), whole(w1k), whole(w2k), whole(posv), whole(w1v), whole(w2v)],
        out_specs=[per_head(nh, HEAD_DIM), per_head(nh, HEAD_DIM),
                   per_head(s, HEAD_DIM), per_head(s, HEAD_DIM)],
        out_shape=[jax.ShapeDtypeStruct((b, hk, nh, HEAD_DIM), bf16),
                   jax.ShapeDtypeStruct((b, hk, nh, HEAD_DIM), bf16),
                   jax.ShapeDtypeStruct((b, hk, s, HEAD_DIM), bf16),
                   jax.ShapeDtypeStruct((b, hk, s, HEAD_DIM), bf16)],
        compiler_params=_cparams("parallel", "parallel"),
        name="nsa_kvprep",
    )(kch, vch, ks, kw, cos_f, sin_f, kg, perm, posk, w1k, w2k, posv, w1v, w2v)


def _nsa_kernel(q_ref, cos_ref, sin_ref, qg_ref, perm_ref, kc_ref, vc_ref, ks_ref, vs_ref,
                kw_ref, vw_ref, gate_ref, o_ref, *, tq, ck, cw):
    g = NSA_GROUP
    rows = g * tq
    qi = pl.program_id(2)
    t0 = qi * tq
    scale = HEAD_DIM ** -0.5

    q = _rms(q_ref[0, 0].astype(f32), qg_ref[...])
    q_cmp = (q * scale).astype(bf16).reshape(rows, HEAD_DIM)
    q_rot = _rope(q.reshape(rows, HEAD_DIM),
                  jnp.concatenate([cos_ref[0]] * g, axis=0),
                  jnp.concatenate([sin_ref[0]] * g, axis=0), perm_ref[...])
    q_rot = (q_rot * scale).astype(bf16)

    tq_col = t0 + lax.broadcasted_iota(jnp.int32, (tq, 1), 0)

    def per_head(bias_q, s):
        n = s.shape[-1]
        return (s.reshape(g, tq, n) + bias_q[None]).reshape(rows, n)

    kc = kc_ref[0, 0]
    ncp = kc.shape[0]
    n_idx = lax.broadcasted_iota(jnp.int32, (1, ncp), 1)
    ok_c = (n_idx * CMP_STRIDE + (CMP_BLOCK - 1)) <= tq_col
    okf_c = jnp.where(ok_c, 1.0, 0.0)
    s_c = lax.dot_general(q_cmp, kc, _NT, preferred_element_type=f32)
    s_c = per_head(jnp.where(ok_c, 0.0, NEG_INF), s_c)
    m_c = jnp.max(s_c, axis=-1, keepdims=True)
    p_c = (jnp.exp(s_c - m_c).reshape(g, tq, ncp) * okf_c[None]).reshape(rows, ncp)
    den = jnp.sum(p_c, axis=-1, keepdims=True)
    p_c = p_c / jnp.where(den > 0, den, 1.0)
    o_cmp = jnp.dot(p_c.astype(bf16), vc_ref[0, 0], preferred_element_type=f32)

    pg = jnp.sum(p_c.reshape(g, tq, ncp), axis=0)
    nn = lax.broadcasted_iota(jnp.int32, (ncp, LANES), 0)
    jj = lax.broadcasted_iota(jnp.int32, (ncp, LANES), 1)
    r_cs = SEL_BLOCK // CMP_STRIDE
    overlap = (nn * CMP_STRIDE < (jj + 1) * SEL_BLOCK) & (nn * CMP_STRIDE + CMP_BLOCK > jj * SEL_BLOCK)
    overlap = jnp.where(overlap, 1.0, 0.0).astype(bf16)
    del r_cs
    pg_hi, pg_lo = _split_bf16(pg)
    imp = (jnp.dot(pg_hi, overlap, preferred_element_type=f32) +
           jnp.dot(pg_lo, overlap, preferred_element_type=f32))
    j_idx = lax.broadcasted_iota(jnp.int32, (1, LANES), 1)
    cur = tq_col // SEL_BLOCK
    valid = j_idx <= cur
    forced = (j_idx == 0) | (j_idx == cur) | (j_idx == cur - 1)
    score = jnp.where(valid, imp + jnp.where(forced, FORCE_BONUS, 0.0), NEG_INF)
    sel = jnp.zeros((tq, LANES), f32)
    for _ in range(SEL_TOPK):
        m = jnp.max(score, axis=-1, keepdims=True)
        first = jnp.min(jnp.where(score == m, j_idx, LANES), axis=-1, keepdims=True)
        hit = j_idx == first
        sel = jnp.where(hit, 1.0, sel)
        score = jnp.where(hit, -jnp.inf, score)
    sel = jnp.where(valid, sel, 0.0).astype(bf16)

    def flash_step(carry, s, v):
        m_prev, l_prev, acc = carry
        m_new = jnp.maximum(m_prev, jnp.max(s, axis=-1, keepdims=True))
        alpha = jnp.exp(m_prev - m_new)
        p = jnp.exp(s - m_new)
        l_new = alpha * l_prev + jnp.sum(p, axis=-1, keepdims=True)
        acc = alpha * acc + jnp.dot(p.astype(bf16), v, preferred_element_type=f32)
        return m_new, l_new, acc

    init = (jnp.full((rows, 1), NEG_INF, f32), jnp.zeros((rows, 1), f32),
            jnp.zeros((rows, HEAD_DIM), f32))

    def sel_body(c, carry):
        k0 = pl.multiple_of(c * ck, ck)
        k = ks_ref[0, 0, pl.ds(k0, ck), :]
        v = vs_ref[0, 0, pl.ds(k0, ck), :]
        kpos = k0 + lax.broadcasted_iota(jnp.int32, (1, ck), 1)
        blk = lax.broadcasted_iota(jnp.int32, (LANES, ck), 0)
        expand = jnp.where(blk == (k0 + lax.broadcasted_iota(jnp.int32, (LANES, ck), 1)) // SEL_BLOCK,
                           1.0, 0.0).astype(bf16)
        picked = jnp.dot(sel, expand, preferred_element_type=f32)
        ok = (picked > 0.5) & (kpos <= tq_col)
        s = lax.dot_general(q_rot, k, _NT, preferred_element_type=f32)
        s = per_head(jnp.where(ok, 0.0, NEG_INF), s)
        return flash_step(carry, s, v)

    n_sel_chunks = (t0 + tq + ck - 1) // ck
    _, l_s, acc_s = lax.fori_loop(0, n_sel_chunks, sel_body, init)
    o_sel = acc_s / l_s

    def win_body(c, carry):
        k0 = pl.multiple_of(c * cw, cw)
        k = kw_ref[0, 0, pl.ds(k0, cw), :]
        v = vw_ref[0, 0, pl.ds(k0, cw), :]
        kpos = k0 + lax.broadcasted_iota(jnp.int32, (1, cw), 1)
        ok = (kpos <= tq_col) & (kpos > tq_col - WINDOW)
        s = lax.dot_general(q_rot, k, _NT, preferred_element_type=f32)
        s = per_head(jnp.where(ok, 0.0, NEG_INF), s)
        return flash_step(carry, s, v)

    w_lo = jnp.maximum(t0 - WINDOW, 0) // cw
    w_hi = (t0 + tq + cw - 1) // cw
    _, l_w, acc_w = lax.fori_loop(w_lo, w_hi, win_body, init)
    o_win = acc_w / l_w

    gates = jax.nn.sigmoid(gate_ref[0, 0]).reshape(rows, 3)
    o = gates[:, 0:1] * o_cmp + gates[:, 1:2] * o_sel + gates[:, 2:3] * o_win
    o_ref[0, 0] = o.reshape(g, tq, HEAD_DIM).astype(o_ref.dtype)


def _nsa_attention(q, cos_f, sin_f, qg, perm, kc, vc, ks, vs, kw, vw, gates, tq=128, ck=256, cw=128):
    b, hk, g, s, dh = q.shape
    ncp = kc.shape[2]
    qspec = pl.BlockSpec((1, 1, g, tq, dh), lambda i, j, t: (i, j, 0, t, 0))
    tab = pl.BlockSpec((1, tq, dh), lambda i, j, t: (i, t, 0))
    whole = lambda a: pl.BlockSpec(a.shape, lambda i, j, t: (0,) * a.ndim)
    kv = lambda n: pl.BlockSpec((1, 1, n, dh), lambda i, j, t: (i, j, 0, 0))
    return pl.pallas_call(
        functools.partial(_nsa_kernel, tq=tq, ck=ck, cw=cw),
        grid=(b, hk, s // tq),
        in_specs=[qspec, tab, tab, whole(qg), whole(perm), kv(ncp), kv(ncp), kv(s), kv(s), kv(s), kv(s),
                  pl.BlockSpec((1, 1, g, tq, 3), lambda i, j, t: (i, j, 0, t, 0))],
        out_specs=qspec,
        out_shape=jax.ShapeDtypeStruct((b, hk, g, s, dh), bf16),
        compiler_params=_cparams("parallel", "parallel", "arbitrary"),
        name="nsa_attention",
    )(q, cos_f, sin_f, qg, perm, kc, vc, ks, vs, kw, vw, gates)


def _gla_kernel(q_ref, k_ref, v_ref, lr_ref, wg_ref, bg_ref, r_ref, ng_ref, o_ref):
    c = GLA_CHUNK
    s = q_ref.shape[2]
    dk, dv = GLA_KEY_DIM, GLA_VAL_DIM
    row = lax.broadcasted_iota(jnp.int32, (c, c), 0)
    col = lax.broadcasted_iota(jnp.int32, (c, c), 1)
    causal = col <= row
    tri = jnp.where(causal, 1.0, 0.0)
    wg = wg_ref[0]
    bg = bg_ref[0]
    ng = ng_ref[...]

    def body(n, state_t):
        r0 = pl.multiple_of(n * c, c)
        z = jnp.dot(lr_ref[0, pl.ds(r0, c), :], wg, preferred_element_type=f32,
                    precision=lax.Precision.HIGHEST) + bg
        log_a = -(jnp.maximum(-z, 0.0) + jnp.log(1.0 + jnp.exp(-jnp.abs(z)))) / GLA_TAU
        bcum = jnp.dot(tri, log_a, preferred_element_type=f32, precision=lax.Precision.HIGHEST)
        b_last = bcum[c - 1:c, :]
        qc = q_ref[0, 0, pl.ds(r0, c), :].astype(f32) * (dk ** -0.5)
        kc = k_ref[0, 0, pl.ds(r0, c), :].astype(f32)
        vc = v_ref[0, 0, pl.ds(r0, c), :]
        q_dec = (qc * jnp.exp(bcum)).astype(bf16)
        k_dec = (kc * jnp.exp(-bcum)).astype(bf16)
        k_state = (kc * jnp.exp(b_last - bcum)).astype(bf16)
        attn = lax.dot_general(q_dec, k_dec, _NT, preferred_element_type=f32)
        attn = jnp.where(causal, attn, 0.0).astype(bf16)
        o = jnp.dot(attn, vc, preferred_element_type=f32)
        o = o + lax.dot_general(q_dec, state_t.astype(bf16), _NT, preferred_element_type=f32)
        u_t = lax.dot_general(vc, k_state, _TN, preferred_element_type=f32)
        state_t = state_t * jnp.exp(b_last) + u_t
        o = _rms(o, ng)
        rr = r_ref[0, 0, pl.ds(r0, c), :].astype(f32)
        o_ref[0, 0, pl.ds(r0, c), :] = (o * (rr * jax.nn.sigmoid(rr))).astype(o_ref.dtype)
        return state_t

    lax.fori_loop(0, s // c, body, jnp.zeros((dv, dk), f32))


def _gla(q, k, v, lr, wg, bg, r, ng):
    b, h, s, dk = q.shape
    dv = v.shape[3]
    hd = lambda d: pl.BlockSpec((1, 1, s, d), lambda i, j: (i, j, 0, 0))
    return pl.pallas_call(
        _gla_kernel,
        grid=(b, h),
        in_specs=[hd(dk), hd(dk), hd(dv),
                  pl.BlockSpec((1, s, GLA_GATE_RANK), lambda i, j: (i, 0, 0)),
                  pl.BlockSpec((1, GLA_GATE_RANK, dk), lambda i, j: (j, 0, 0)),
                  pl.BlockSpec((1, 1, dk), lambda i, j: (j, 0, 0)),
                  hd(dv),
                  pl.BlockSpec((1, dv), lambda i, j: (0, 0))],
        out_specs=hd(dv),
        out_shape=jax.ShapeDtypeStruct((b, h, s, dv), bf16),
        compiler_params=_cparams("parallel", "parallel"),
        name="gla",
    )(q, k, v, lr, wg, bg, r, ng)


def _outproj_kernel(a_ref, b_ref, x_ref, wo_ref, g2_ref, wrh_ref, wrl_ref, br_ref,
                    x1_ref, h2_ref, rt_ref):
    na = a_ref.shape[1]
    y = (jnp.dot(a_ref[...], wo_ref[:na, :], preferred_element_type=f32) +
         jnp.dot(b_ref[...], wo_ref[na:, :], preferred_element_type=f32))
    x1 = x_ref[...] + y
    x1_ref[...] = x1
    h2 = _rms(x1, g2_ref[...])
    h2_ref[...] = h2.astype(bf16)
    hi, lo = _split_bf16(h2)
    logits = (jnp.dot(hi, wrh_ref[...], preferred_element_type=f32) +
              jnp.dot(lo, wrh_ref[...], preferred_element_type=f32) +
              jnp.dot(hi, wrl_ref[...], preferred_element_type=f32)) + br_ref[...]
    lane = lax.broadcasted_iota(jnp.int32, (1, LANES), 1)
    score = jnp.where(lane < N_EXPERTS, logits, -jnp.inf)
    top = jnp.max(score, axis=-1, keepdims=True)
    dense = jnp.zeros_like(score)
    packed = jnp.zeros_like(score)
    for k in range(TOP_K):
        m = jnp.max(score, axis=-1, keepdims=True)
        first = jnp.min(jnp.where(score == m, lane, LANES), axis=-1, keepdims=True)
        hit = lane == first
        w = jnp.exp(m - top)
        dense = jnp.where(hit, w, dense)
        packed = jnp.where(lane == N_EXPERTS + k, first.astype(f32), packed)
        packed = jnp.where(lane == N_EXPERTS + TOP_K + k, w, packed)
        score = jnp.where(hit, -jnp.inf, score)
    tot = jnp.sum(dense, axis=-1, keepdims=True)
    gate_lanes = (lane >= N_EXPERTS + TOP_K) & (lane < N_EXPERTS + 2 * TOP_K)
    rt_ref[...] = dense / tot + jnp.where(gate_lanes, packed / tot, packed)


def _outproj_router(a, b, x2, wo, g2, wr_hi, wr_lo, br, tm=512):
    m, d = x2.shape
    na, nb = a.shape[1], b.shape[1]
    row = lambda n: pl.BlockSpec((tm, n), lambda i: (i, 0))
    whole = lambda t: pl.BlockSpec(t.shape, lambda i: (0,) * t.ndim)
    return pl.pallas_call(
        _outproj_kernel,
        grid=(m // tm,),
        in_specs=[row(na), row(nb), row(d), whole(wo), whole(g2), whole(wr_hi), whole(wr_lo), whole(br)],
        out_specs=[row(d), row(d), row(LANES)],
        out_shape=[jax.ShapeDtypeStruct((m, d), f32),
                   jax.ShapeDtypeStruct((m, d), bf16),
                   jax.ShapeDtypeStruct((m, LANES), f32)],
        compiler_params=_cparams("parallel"),
        name="outproj_router",
    )(a, b, x2, wo, g2, wr_hi, wr_lo, br)


def _moe_kernel(be_ref, nb_ref, xs_ref, wgu_ref, bgu_ref, wd_ref, bd_ref, gt_ref, o_ref):
    i = pl.program_id(0)

    @pl.when(i < nb_ref[0])
    def _():
        h = jnp.dot(xs_ref[...], wgu_ref[0], preferred_element_type=f32) + bgu_ref[0]
        x_glu = jnp.minimum(h[:, :D_FF], SWIGLU_LIMIT)
        x_lin = jnp.clip(h[:, D_FF:], -SWIGLU_LIMIT, SWIGLU_LIMIT)
        act = x_glu * jax.nn.sigmoid(SWIGLU_ALPHA * x_glu) * (x_lin + 1.0)
        y = jnp.dot(act.astype(bf16), wd_ref[0], preferred_element_type=f32) + bd_ref[0]
        o_ref[...] = (y * gt_ref[...]).astype(o_ref.dtype)

    @pl.when(i >= nb_ref[0])
    def _():
        o_ref[...] = jnp.zeros_like(o_ref)


def _moe_experts(block_e, n_used, xs, wgu, bgu, wd, bd, gate_rows):
    r, d = xs.shape
    nblk = r // MOE_BLOCK
    return pl.pallas_call(
        _moe_kernel,
        grid_spec=pltpu.PrefetchScalarGridSpec(
            num_scalar_prefetch=2,
            grid=(nblk,),
            in_specs=[pl.BlockSpec((MOE_BLOCK, d), lambda i, be, nb: (i, 0)),
                      pl.BlockSpec((1, d, 2 * D_FF), lambda i, be, nb: (be[i], 0, 0)),
                      pl.BlockSpec((1, 1, 2 * D_FF), lambda i, be, nb: (be[i], 0, 0)),
                      pl.BlockSpec((1, D_FF, d), lambda i, be, nb: (be[i], 0, 0)),
                      pl.BlockSpec((1, 1, d), lambda i, be, nb: (be[i], 0, 0)),
                      pl.BlockSpec((MOE_BLOCK, 1), lambda i, be, nb: (i, 0))],
            out_specs=pl.BlockSpec((MOE_BLOCK, d), lambda i, be, nb: (i, 0)),
        ),
        out_shape=jax.ShapeDtypeStruct((r, d), bf16),
        compiler_params=_cparams("arbitrary"),
        name="moe_experts",
    )(block_e, n_used, xs, wgu, bgu, wd, bd, gate_rows)


def _rope_tables(positions):
    half = ROPE_DIM // 2
    inv_freq = jnp.exp(-math.log(ROPE_THETA) * jnp.arange(0, ROPE_DIM, 2, dtype=f32) / ROPE_DIM)
    ang = positions.astype(f32)[..., None] * inv_freq
    cos, sin = jnp.cos(ang), jnp.sin(ang)
    rest = HEAD_DIM - ROPE_DIM
    b, s = positions.shape
    cos_f = jnp.concatenate([cos, cos, jnp.ones((b, s, rest), f32)], axis=-1)
    sin_f = jnp.concatenate([-sin, sin, jnp.zeros((b, s, rest), f32)], axis=-1)
    perm = jnp.zeros((HEAD_DIM, HEAD_DIM), f32)
    idx = jnp.arange(half)
    perm = perm.at[idx + half, idx].set(1.0).at[idx, idx + half].set(1.0)
    return cos_f, sin_f, perm.astype(bf16)


def _layer(x, positions, norm1_g, w_in, q_norm_g, k_norm_g, cmp_pos_k, cmp_w1_k, cmp_w2_k,
           cmp_pos_v, cmp_w1_v, cmp_w2_v, gla_w_gate, gla_b_gate, gla_norm_g, w_out, norm2_g,
           w_router, b_router, w_gate_up, b_gate_up, w_down, b_down):
    b, s, d = x.shape
    m = b * s
    hk, g, dh = NSA_KV_HEADS, NSA_GROUP, HEAD_DIM
    x2 = x.reshape(m, d)

    offs = [0]
    for sz in IN_SIZES:
        offs.append(offs[-1] + sz)
    seg = lambda i: w_in[:, offs[i]:offs[i + 1]]
    w_main = jnp.concatenate([seg(i) for i in (0, 1, 2, 3, 4, 5, 6, 8, 9, 10, 12)], axis=1).astype(bf16)
    n_small = IN_SIZES[7] + IN_SIZES[11]
    w_small = jnp.concatenate([seg(7), seg(11), jnp.zeros((d, LANES - n_small), f32)], axis=1).astype(bf16)
    proj, proj_s = _inproj(x2, norm1_g.reshape(1, d), w_main, w_small)
    proj = proj.reshape(b, s, -1)
    proj_s = proj_s.reshape(b, s, LANES)

    o = [0]
    for sz in (NSA_WIDTH,) + (NSA_KV_WIDTH,) * 6 + (GLA_KEY_WIDTH, GLA_KEY_WIDTH, GLA_WIDTH, GLA_WIDTH):
        o.append(o[-1] + sz)
    col = lambda i: proj[:, :, o[i]:o[i + 1]]
    nq, nkc, nvc, nks, nvs, nkw, nvw, gq, gk, gv, gr = [col(i) for i in range(11)]
    ngate = proj_s[:, :, :IN_SIZES[7]]
    glr = proj_s[:, :, IN_SIZES[7]:n_small]

    cos_f, sin_f, perm = _rope_tables(positions)
    heads = lambda t: t.reshape(b, s, hk, dh).transpose(0, 2, 1, 3)
    halves = lambda t: (t.reshape(b, s // CMP_STRIDE, CMP_STRIDE, hk, dh)
                        .transpose(0, 3, 1, 2, 4).reshape(b, hk, s // CMP_STRIDE, CMP_STRIDE * dh))
    kc, vc, ks_rot, kw_rot = _kvprep(
        halves(nkc), halves(nvc), heads(nks), heads(nkw), cos_f, sin_f, k_norm_g, perm,
        cmp_pos_k.reshape(1, -1), cmp_w1_k.astype(bf16), cmp_w2_k.astype(bf16),
        cmp_pos_v.reshape(1, -1), cmp_w1_v.astype(bf16), cmp_w2_v.astype(bf16))
    ncp = -(-kc.shape[2] // LANES) * LANES
    if ncp != kc.shape[2]:
        padn = ((0, 0), (0, 0), (0, ncp - kc.shape[2]), (0, 0))
        kc, vc = jnp.pad(kc, padn), jnp.pad(vc, padn)

    q5 = nq.reshape(b, s, hk, g, dh).transpose(0, 2, 3, 1, 4)
    gates5 = ngate.reshape(b, s, hk, g, 3).transpose(0, 2, 3, 1, 4)
    nsa = _nsa_attention(q5, cos_f, sin_f, q_norm_g.reshape(1, dh), perm, kc, vc,
                         ks_rot, heads(nvs), kw_rot, heads(nvw), gates5)
    nsa_out = nsa.transpose(0, 3, 1, 2, 4).reshape(m, NSA_WIDTH)

    gh = lambda t, dd: t.reshape(b, s, GLA_HEADS, dd).transpose(0, 2, 1, 3)
    gla = _gla(gh(gq, GLA_KEY_DIM), gh(gk, GLA_KEY_DIM), gh(gv, GLA_VAL_DIM), glr,
               gla_w_gate.reshape(GLA_GATE_RANK, GLA_HEADS, GLA_KEY_DIM).transpose(1, 0, 2),
               gla_b_gate.reshape(GLA_HEADS, 1, GLA_KEY_DIM), gh(gr, GLA_VAL_DIM),
               gla_norm_g.reshape(1, GLA_VAL_DIM))
    gla_out = gla.transpose(0, 2, 1, 3).reshape(m, GLA_WIDTH)

    padr = lambda t: jnp.concatenate([t, jnp.zeros(t.shape[:-1] + (LANES - t.shape[-1],), t.dtype)], axis=-1)
    wr_hi = w_router.astype(bf16)
    wr_lo = (w_router - wr_hi.astype(f32)).astype(bf16)
    x1, h2, rt = _outproj_router(nsa_out, gla_out, x2, w_out.astype(bf16), norm2_g.reshape(1, d),
                                 padr(wr_hi), padr(wr_lo), padr(b_router.reshape(1, -1)))

    top_idx = rt[:, N_EXPERTS:N_EXPERTS + TOP_K].astype(jnp.int32)
    gate = rt[:, N_EXPERTS + TOP_K:N_EXPERTS + 2 * TOP_K]
    a = m * TOP_K
    flat_e = top_idx.reshape(-1)
    order = jnp.argsort(flat_e)
    sorted_e = flat_e[order]
    sorted_tok = (order // TOP_K).astype(jnp.int32)
    counts = jnp.zeros((N_EXPERTS,), jnp.int32).at[flat_e].add(1)
    padded = (counts + MOE_BLOCK - 1) // MOE_BLOCK * MOE_BLOCK
    start = jnp.cumsum(counts) - counts
    pend = jnp.cumsum(padded)
    pstart = pend - padded
    dest = pstart[sorted_e] + (jnp.arange(a, dtype=jnp.int32) - start[sorted_e])
    nblk = -(-(a + N_EXPERTS * (MOE_BLOCK - 1)) // MOE_BLOCK)
    rtot = nblk * MOE_BLOCK
    row_tok = jnp.zeros((rtot,), jnp.int32).at[dest].set(sorted_tok)
    row_gate = jnp.zeros((rtot,), f32).at[dest].set(gate.reshape(-1)[order])
    block_e = jnp.minimum(jnp.searchsorted(pend, jnp.arange(nblk, dtype=jnp.int32) * MOE_BLOCK, side='right'),
                          N_EXPERTS - 1).astype(jnp.int32)
    n_used = (pend[-1] // MOE_BLOCK).astype(jnp.int32).reshape(1)
    xs = jnp.take(h2, row_tok, axis=0)
    ys = _moe_experts(block_e, n_used, xs, w_gate_up.astype(bf16), b_gate_up.reshape(N_EXPERTS, 1, -1),
                      w_down.astype(bf16), b_down.reshape(N_EXPERTS, 1, -1), row_gate.reshape(rtot, 1))
    pos = jnp.zeros((a,), jnp.int32).at[order].set(dest).reshape(m, TOP_K)
    moe = jnp.sum(jnp.take(ys, pos, axis=0).astype(f32), axis=1)
    return (x1 + moe).reshape(b, s, d)


def kernel(x, positions, norm1_g, w_in, nsa_q_norm_g, nsa_k_norm_g, cmp_pos_k, cmp_w1_k, cmp_w2_k,
           cmp_pos_v, cmp_w1_v, cmp_w2_v, gla_w_gate, gla_b_gate, gla_norm_g, w_out, norm2_g,
           w_router, b_router, w_gate_up, b_gate_up, w_down, b_down):
    for l in range(norm1_g.shape[0]):
        x = _layer(x, positions, norm1_g[l], w_in[l], nsa_q_norm_g[l], nsa_k_norm_g[l],
                   cmp_pos_k[l], cmp_w1_k[l], cmp_w2_k[l], cmp_pos_v[l], cmp_w1_v[l], cmp_w2_v[l],
                   gla_w_gate[l], gla_b_gate[l], gla_norm_g[l], w_out[l], norm2_g[l],
                   w_router[l], b_router[l], w_gate_up[l], b_gate_up[l], w_down[l], b_down[l])
    return x
```

```python
import functools
import math

import jax
import jax.numpy as jnp
from jax import lax
from jax.experimental import pallas as pl
from jax.experimental.pallas import tpu as pltpu

f32 = jnp.float32
bf16 = jnp.bfloat16

NSA_HEADS = 8
NSA_KV_HEADS = 2
NSA_GROUP = NSA_HEADS // NSA_KV_HEADS
HEAD_DIM = 64
CMP_BLOCK = 32
CMP_STRIDE = 16
CMP_HIDDEN = 256
SEL_BLOCK = 64
SEL_TOPK = 8
WINDOW = 512
FORCE_BONUS = 1e4
GLA_HEADS = 4
GLA_KEY_DIM = 64
GLA_VAL_DIM = 128
GLA_CHUNK = 64
GLA_GATE_RANK = 16
GLA_TAU = 16.0
ROPE_THETA = 500000.0
ROPE_DIM = HEAD_DIM // 4
N_EXPERTS = 32
TOP_K = 4
D_FF = 1024
SWIGLU_LIMIT = 7.0
SWIGLU_ALPHA = 1.702
EPS = 1e-6
NEG_INF = -1e30

NSA_WIDTH = NSA_HEADS * HEAD_DIM
NSA_KV_WIDTH = NSA_KV_HEADS * HEAD_DIM
GLA_KEY_WIDTH = GLA_HEADS * GLA_KEY_DIM
GLA_WIDTH = GLA_HEADS * GLA_VAL_DIM
IN_SIZES = (NSA_WIDTH,) + (NSA_KV_WIDTH,) * 6 + (
    NSA_HEADS * 3, GLA_KEY_WIDTH, GLA_KEY_WIDTH, GLA_WIDTH, GLA_GATE_RANK, GLA_WIDTH)
MAIN_SIZES = (NSA_WIDTH,) + (NSA_KV_WIDTH,) * 6 + (GLA_KEY_WIDTH, GLA_KEY_WIDTH, GLA_WIDTH, GLA_WIDTH)
MAIN_OFFS = tuple(sum(MAIN_SIZES[:i]) for i in range(len(MAIN_SIZES) + 1))

LANES = 128
VMEM_LIMIT = 48 * 1024 * 1024
MOE_BLOCK = 256

_NT = (((1,), (1,)), ((), ()))
_TN = (((0,), (0,)), ((), ()))


def _cparams(*sem):
    return pltpu.CompilerParams(dimension_semantics=sem, vmem_limit_bytes=VMEM_LIMIT)


def _rms(t, g):
    return t * lax.rsqrt(jnp.mean(t * t, axis=-1, keepdims=True) + EPS) * g


def _split_bf16(t):
    hi = t.astype(bf16)
    lo = (t - hi.astype(f32)).astype(bf16)
    return hi, lo


def _rope(t, cos_f, sin_f, perm):
    hi, lo = _split_bf16(t)
    rot = (jnp.dot(hi, perm, preferred_element_type=f32) +
           jnp.dot(lo, perm, preferred_element_type=f32))
    return t * cos_f + rot * sin_f


def _inproj_kernel(x_ref, g_ref, w_ref, ws_ref, o_ref, os_ref):
    x = x_ref[...]
    h = _rms(x, g_ref[...]).astype(bf16)
    n = o_ref.shape[1]
    step = 512
    for c in range(0, n, step):
        e = min(c + step, n)
        o_ref[:, c:e] = jnp.dot(h, w_ref[:, c:e], preferred_element_type=f32).astype(bf16)
    os_ref[...] = jnp.dot(h, ws_ref[...], preferred_element_type=f32)


def _inproj(x2, g, w_main, w_small, tm=512):
    m, d = x2.shape
    n = w_main.shape[1]
    ns = w_small.shape[1]
    return pl.pallas_call(
        _inproj_kernel,
        grid=(m // tm,),
        in_specs=[pl.BlockSpec((tm, d), lambda i: (i, 0)),
                  pl.BlockSpec((1, d), lambda i: (0, 0)),
                  pl.BlockSpec((d, n), lambda i: (0, 0)),
                  pl.BlockSpec((d, ns), lambda i: (0, 0))],
        out_specs=[pl.BlockSpec((tm, n), lambda i: (i, 0)),
                   pl.BlockSpec((tm, ns), lambda i: (i, 0))],
        out_shape=[jax.ShapeDtypeStruct((m, n), bf16),
                   jax.ShapeDtypeStruct((m, ns), f32)],
        compiler_params=_cparams("parallel"),
        name="inproj",
    )(x2, g, w_main, w_small)


def _kvprep_kernel(kch_ref, vch_ref, ks_ref, kw_ref, cos_ref, sin_ref, kg_ref, perm_ref,
                   posk_ref, w1k_ref, w2k_ref, posv_ref, w1v_ref, w2v_ref,
                   kc_ref, vc_ref, kso_ref, kwo_ref):
    half = CMP_STRIDE * HEAD_DIM

    def compress(h_ref, pos_ref, w1_ref, w2_ref):
        hb = h_ref[0, 0].astype(f32)
        n = hb.shape[0]
        top = (hb + pos_ref[:, :half]).astype(bf16)
        bot = (hb + pos_ref[:, half:]).astype(bf16)
        a = jnp.dot(top, w1_ref[:half, :], preferred_element_type=f32)
        b = jnp.dot(bot, w1_ref[half:, :], preferred_element_type=f32)
        pre = a + pltpu.roll(b, n - 1, 0)
        hid = pre * jax.nn.sigmoid(pre)
        return jnp.dot(hid.astype(bf16), w2_ref[...], preferred_element_type=f32)

    kc = compress(kch_ref, posk_ref, w1k_ref, w2k_ref)
    kc_ref[0, 0] = _rms(kc, kg_ref[0:1, :]).astype(bf16)
    vc_ref[0, 0] = compress(vch_ref, posv_ref, w1v_ref, w2v_ref).astype(bf16)
    cos_f = cos_ref[0]
    sin_f = sin_ref[0]
    perm = perm_ref[...]
    ks = _rms(ks_ref[0, 0].astype(f32), kg_ref[1:2, :])
    kso_ref[0, 0] = _rope(ks, cos_f, sin_f, perm).astype(bf16)
    kw = _rms(kw_ref[0, 0].astype(f32), kg_ref[2:3, :])
    kwo_ref[0, 0] = _rope(kw, cos_f, sin_f, perm).astype(bf16)


def _kvprep(kch, vch, ks, kw, cos_f, sin_f, kg, perm, posk, w1k, w2k, posv, w1v, w2v):
    b, hk, nh, hw = kch.shape
    s = ks.shape[2]
    per_head = lambda r, c: pl.BlockSpec((1, 1, r, c), lambda i, j: (i, j, 0, 0))
    whole = lambda a: pl.BlockSpec(a.shape, lambda i, j: (0,) * a.ndim)
    tab = pl.BlockSpec((1, s, HEAD_DIM), lambda i, j: (i, 0, 0))
    return pl.pallas_call(
        _kvprep_kernel,
        grid=(b, hk),
        in_specs=[per_head(nh, hw), per_head(nh, hw), per_head(s, HEAD_DIM), per_head(s, HEAD_DIM),
                  tab, tab, whole(kg), whole(perm),
                  whole(posk), whole(w1k), whole(w2k), whole(posv), whole(w1v), whole(w2v)],
        out_specs=[per_head(nh, HEAD_DIM), per_head(nh, HEAD_DIM),
                   per_head(s, HEAD_DIM), per_head(s, HEAD_DIM)],
        out_shape=[jax.ShapeDtypeStruct((b, hk, nh, HEAD_DIM), bf16),
                   jax.ShapeDtypeStruct((b, hk, nh, HEAD_DIM), bf16),
                   jax.ShapeDtypeStruct((b, hk, s, HEAD_DIM), bf16),
                   jax.ShapeDtypeStruct((b, hk, s, HEAD_DIM), bf16)],
        compiler_params=_cparams("parallel", "parallel"),
        name="nsa_kvprep",
    )(kch, vch, ks, kw, cos_f, sin_f, kg, perm, posk, w1k, w2k, posv, w1v, w2v)


def _nsa_kernel(q_ref, cos_ref, sin_ref, qg_ref, kc_ref, vct_ref, ks_ref, vst_ref, kw_ref, vwt_ref,
                gate_ref, o_ref, sel_ref, *, tq, ck):
    g = NSA_GROUP
    qi = pl.program_id(2)
    t0 = pl.multiple_of(qi * tq, tq)
    scale = HEAD_DIM ** -0.5 * math.log2(math.e)
    per_head = lambda t: jnp.concatenate([t] * g, axis=1)

    qt = q_ref[0].astype(f32).T
    qt = jnp.concatenate([qt[i * HEAD_DIM:(i + 1) * HEAD_DIM] for i in range(g)], axis=1)
    qn = qt * lax.rsqrt(jnp.mean(qt * qt, axis=0, keepdims=True) + EPS) * qg_ref[...]
    half = ROPE_DIM // 2
    cos8, sin8 = per_head(cos_ref[0]), per_head(sin_ref[0])
    x1, x2 = qn[:half], qn[half:ROPE_DIM]
    q_rot = jnp.concatenate([x1 * cos8 - x2 * sin8, x2 * cos8 + x1 * sin8, qn[ROPE_DIM:]], axis=0)
    q_cmp = (qn * scale).astype(bf16)
    q_rot = (q_rot * scale).astype(bf16)
    tq_row = t0 + lax.broadcasted_iota(jnp.int32, (1, tq), 1)

    kc = kc_ref[0, 0]
    ncp = kc.shape[0]
    n_col = lax.broadcasted_iota(jnp.int32, (ncp, 1), 0)
    ok_c = (n_col * CMP_STRIDE + (CMP_BLOCK - 1)) <= tq_row
    s_c = jnp.dot(kc, q_cmp, preferred_element_type=f32) + per_head(jnp.where(ok_c, 0.0, NEG_INF))
    m_c = jnp.max(s_c, axis=0, keepdims=True)
    p_c = jnp.exp2(s_c - m_c) * per_head(jnp.where(ok_c, 1.0, 0.0))
    den = jnp.sum(p_c, axis=0, keepdims=True)
    p_c = p_c * (1.0 / jnp.where(den > 0, den, 1.0))
    o_cmp = jnp.dot(vct_ref[0, 0], p_c.astype(bf16), preferred_element_type=f32)

    pg = p_c[:, :tq]
    for i in range(1, g):
        pg = pg + p_c[:, i * tq:(i + 1) * tq]
    nj = sel_ref.shape[0]
    jj = lax.broadcasted_iota(jnp.int32, (nj, ncp), 0)
    nn = lax.broadcasted_iota(jnp.int32, (nj, ncp), 1)
    overlap = (nn * CMP_STRIDE < (jj + 1) * SEL_BLOCK) & (nn * CMP_STRIDE + CMP_BLOCK > jj * SEL_BLOCK)
    overlap = jnp.where(overlap, 1.0, 0.0).astype(bf16)
    pg_hi, pg_lo = _split_bf16(pg)
    imp = (jnp.dot(overlap, pg_hi, preferred_element_type=f32) +
           jnp.dot(overlap, pg_lo, preferred_element_type=f32))
    j_col = lax.broadcasted_iota(jnp.int32, (nj, 1), 0)
    j_f = j_col.astype(f32)
    cur = tq_row // SEL_BLOCK
    valid = j_col <= cur
    forced = (j_col == 0) | (j_col == cur) | (j_col == cur - 1)
    score = jnp.where(valid, imp + jnp.where(forced, FORCE_BONUS, 0.0), NEG_INF)
    sel = jnp.zeros((nj, tq), f32)
    for _ in range(SEL_TOPK):
        m = jnp.max(score, axis=0, keepdims=True)
        first = jnp.min(jnp.where(score == m, j_f, float(nj)), axis=0, keepdims=True)
        hit = j_f == first
        sel = jnp.where(hit, 1.0, sel)
        score = jnp.where(hit, -jnp.inf, score)
    sel_ref[...] = jnp.where(valid, sel, 0.0)

    wl = WINDOW + tq
    w0 = pl.multiple_of(jnp.maximum(t0 - WINDOW, 0), LANES)
    kpos_w = w0 + lax.broadcasted_iota(jnp.int32, (wl, 1), 0)
    ok_w = (kpos_w <= tq_row) & (kpos_w > tq_row - WINDOW)
    s_w = jnp.dot(kw_ref[0, 0, pl.ds(w0, wl), :], q_rot, preferred_element_type=f32)
    s_w = s_w + per_head(jnp.where(ok_w, 0.0, NEG_INF))
    p_w = jnp.exp2(s_w - jnp.max(s_w, axis=0, keepdims=True))
    l_w = jnp.sum(p_w, axis=0, keepdims=True)
    o_win = jnp.dot(vwt_ref[0, 0, :, pl.ds(w0, wl)], p_w.astype(bf16), preferred_element_type=f32)
    o_win = o_win * (1.0 / l_w)

    bpc = ck // SEL_BLOCK

    def sel_body(c, carry):
        m_prev, l_prev, acc = carry
        k0 = pl.multiple_of(c * ck, ck)
        kpos = k0 + lax.broadcasted_iota(jnp.int32, (ck, 1), 0)
        picked = jnp.concatenate(
            [jnp.broadcast_to(sel_ref[pl.ds(c * bpc + i, 1), :], (SEL_BLOCK, tq)) for i in range(bpc)], axis=0)
        ok = (picked > 0.5) & (kpos <= tq_row)
        s = jnp.dot(ks_ref[0, 0, pl.ds(k0, ck), :], q_rot, preferred_element_type=f32)
        s = s + per_head(jnp.where(ok, 0.0, NEG_INF))
        m_new = jnp.maximum(m_prev, jnp.max(s, axis=0, keepdims=True))
        alpha = jnp.exp2(m_prev - m_new)
        p = jnp.exp2(s - m_new)
        l_new = alpha * l_prev + jnp.sum(p, axis=0, keepdims=True)
        acc = alpha * acc + jnp.dot(vst_ref[0, 0, :, pl.ds(k0, ck)], p.astype(bf16),
                                    preferred_element_type=f32)
        return m_new, l_new, acc

    init = (jnp.full((1, g * tq), NEG_INF, f32), jnp.zeros((1, g * tq), f32),
            jnp.zeros((HEAD_DIM, g * tq), f32))
    _, l_s, acc_s = lax.fori_loop(0, (t0 + tq + ck - 1) // ck, sel_body, init)
    o_sel = acc_s * (1.0 / l_s)

    gt = gate_ref[0].T
    gate = lambda j: jax.nn.sigmoid(jnp.concatenate([gt[i * 3 + j:i * 3 + j + 1] for i in range(g)], axis=1))
    ot = gate(0) * o_cmp + gate(1) * o_sel + gate(2) * o_win
    o2 = jnp.concatenate([ot[:, i * tq:(i + 1) * tq] for i in range(g)], axis=0)
    o_ref[0] = o2.T.astype(o_ref.dtype)


def _nsa_attention(proj, cos_t, sin_t, qg, kc, vct, ks, vst, kw, vwt, gate_logits, tq=128, ck=256):
    b, s, _ = proj.shape
    hk, g, dh = NSA_KV_HEADS, NSA_GROUP, HEAD_DIM
    ncp = kc.shape[2]
    tab = pl.BlockSpec((1, ROPE_DIM // 2, tq), lambda i, j, t: (i, 0, t))
    kv = lambda n: pl.BlockSpec((1, 1, n, dh), lambda i, j, t: (i, j, 0, 0))
    kvt = lambda n: pl.BlockSpec((1, 1, dh, n), lambda i, j, t: (i, j, 0, 0))
    return pl.pallas_call(
        functools.partial(_nsa_kernel, tq=tq, ck=ck),
        grid=(b, hk, s // tq),
        in_specs=[pl.BlockSpec((1, tq, g * dh), lambda i, j, t: (i, t, j)),
                  tab, tab, pl.BlockSpec((dh, 1), lambda i, j, t: (0, 0)),
                  kv(ncp), kvt(ncp), kv(s), kvt(s), kv(s), kvt(s),
                  pl.BlockSpec((1, tq, LANES), lambda i, j, t: (i, t, j))],
        out_specs=pl.BlockSpec((1, tq, g * dh), lambda i, j, t: (i, t, j)),
        out_shape=jax.ShapeDtypeStruct((b, s, NSA_WIDTH), bf16),
        scratch_shapes=[pltpu.VMEM((s // SEL_BLOCK, tq), f32)],
        compiler_params=_cparams("parallel", "parallel", "arbitrary"),
        name="nsa_attention",
    )(proj, cos_t, sin_t, qg, kc, vct, ks, vst, kw, vwt, gate_logits)


def _gla_kernel(q_ref, k_ref, v_ref, lr_ref, wg_ref, bg_ref, r_ref, ng_ref, o_ref, *, grp):
    c = GLA_CHUNK
    s = q_ref.shape[1]
    dk, dv = GLA_KEY_DIM, GLA_VAL_DIM
    rows = grp * c
    ri = lax.broadcasted_iota(jnp.int32, (rows, rows), 0)
    ci = lax.broadcasted_iota(jnp.int32, (rows, rows), 1)
    causal = (ci <= ri) & (ci // c == ri // c)
    tri = jnp.where(causal, 1.0, 0.0)
    lane = lax.broadcasted_iota(jnp.int32, (1, 2 * dk), 1)
    head_mask = [lane < dk, lane >= dk]
    wg = wg_ref[...]
    bg = bg_ref[...]
    ng = ng_ref[...]

    def body(n, states):
        r0 = pl.multiple_of(n * rows, rows)
        z = jnp.dot(lr_ref[0, pl.ds(r0, rows), :GLA_GATE_RANK], wg, preferred_element_type=f32,
                    precision=lax.Precision.HIGHEST) + bg
        log_a = -(jnp.maximum(-z, 0.0) + jnp.log(1.0 + jnp.exp(-jnp.abs(z)))) / GLA_TAU
        bcum = jnp.dot(tri, log_a, preferred_element_type=f32, precision=lax.Precision.HIGHEST)
        qf = q_ref[0, pl.ds(r0, rows), :].astype(f32) * (dk ** -0.5)
        kf = k_ref[0, pl.ds(r0, rows), :].astype(f32)
        q_dec = qf * jnp.exp(bcum)
        k_dec = (kf * jnp.exp(-bcum)).astype(bf16)
        b_last = [bcum[(i + 1) * c - 1:(i + 1) * c, :] for i in range(grp)]
        k_state = jnp.concatenate(
            [kf[i * c:(i + 1) * c] * jnp.exp(b_last[i] - bcum[i * c:(i + 1) * c]) for i in range(grp)],
            axis=0).astype(bf16)
        new_states = []
        for h in range(2):
            q_h = jnp.where(head_mask[h], q_dec, 0.0).astype(bf16)
            v_h = v_ref[0, pl.ds(r0, rows), h * dv:(h + 1) * dv]
            attn = lax.dot_general(q_h, k_dec, _NT, preferred_element_type=f32)
            attn = jnp.where(causal, attn, 0.0).astype(bf16)
            o_intra = jnp.dot(attn, v_h, preferred_element_type=f32)
            st = states[h]
            o_inter = []
            for i in range(grp):
                o_inter.append(lax.dot_general(q_h[i * c:(i + 1) * c], st.astype(bf16), _NT,
                                               preferred_element_type=f32))
                u_t = lax.dot_general(v_h[i * c:(i + 1) * c], k_state[i * c:(i + 1) * c], _TN,
                                      preferred_element_type=f32)
                st = st * jnp.exp(b_last[i]) + u_t
            new_states.append(st)
            o = _rms(o_intra + jnp.concatenate(o_inter, axis=0), ng)
            rr = r_ref[0, pl.ds(r0, rows), h * dv:(h + 1) * dv].astype(f32)
            o_ref[0, pl.ds(r0, rows), h * dv:(h + 1) * dv] = (o * (rr * jax.nn.sigmoid(rr))).astype(o_ref.dtype)
        return tuple(new_states)

    zero = jnp.zeros((dv, 2 * dk), f32)
    lax.fori_loop(0, s // rows, body, (zero, zero))


def _gla(proj, proj_s, wg, bg, ng, grp=4):
    b, s, _ = proj.shape
    dk2, dv2 = 2 * GLA_KEY_DIM, 2 * GLA_VAL_DIM
    oq, ok_, ov, orr = (MAIN_OFFS[7] // dk2, MAIN_OFFS[8] // dk2, MAIN_OFFS[9] // dv2, MAIN_OFFS[10] // dv2)
    col = lambda w, o: pl.BlockSpec((1, s, w), lambda i, j: (i, 0, o + j))
    return pl.pallas_call(
        functools.partial(_gla_kernel, grp=grp),
        grid=(b, GLA_HEADS // 2),
        in_specs=[col(dk2, oq), col(dk2, ok_), col(dv2, ov),
                  pl.BlockSpec((1, s, LANES), lambda i, j: (i, 0, 2)),
                  pl.BlockSpec((GLA_GATE_RANK, dk2), lambda i, j: (0, j)),
                  pl.BlockSpec((1, dk2), lambda i, j: (0, j)),
                  col(dv2, orr),
                  pl.BlockSpec((1, GLA_VAL_DIM), lambda i, j: (0, 0))],
        out_specs=pl.BlockSpec((1, s, dv2), lambda i, j: (i, 0, j)),
        out_shape=jax.ShapeDtypeStruct((b, s, GLA_WIDTH), bf16),
        compiler_params=_cparams("parallel", "parallel"),
        name="gla",
    )(proj, proj, proj, proj_s, wg, bg, proj, ng)


def _outproj_kernel(a_ref, b_ref, x_ref, wo_ref, g2_ref, wrh_ref, wrl_ref, br_ref,
                    x1_ref, h2_ref, rt_ref, cnt_ref, carry_ref):
    @pl.when(pl.program_id(0) == 0)
    def _():
        carry_ref[...] = jnp.zeros_like(carry_ref)

    na = a_ref.shape[1]
    tm = a_ref.shape[0]
    y = (jnp.dot(a_ref[...], wo_ref[:na, :], preferred_element_type=f32) +
         jnp.dot(b_ref[...], wo_ref[na:, :], preferred_element_type=f32))
    x1 = x_ref[...] + y
    x1_ref[...] = x1
    h2 = _rms(x1, g2_ref[...])
    h2_ref[...] = h2.astype(bf16)
    hi, lo = _split_bf16(h2)
    logits = (jnp.dot(hi, wrh_ref[...], preferred_element_type=f32) +
              jnp.dot(lo, wrh_ref[...], preferred_element_type=f32) +
              jnp.dot(hi, wrl_ref[...], preferred_element_type=f32)) + br_ref[...]
    lane = lax.broadcasted_iota(jnp.int32, (1, LANES), 1)
    score = jnp.where(lane < N_EXPERTS, logits, -jnp.inf)
    top = jnp.max(score, axis=-1, keepdims=True)
    hits, firsts, weights = [], [], []
    for _ in range(TOP_K):
        m = jnp.max(score, axis=-1, keepdims=True)
        first = jnp.min(jnp.where(score == m, lane, LANES), axis=-1, keepdims=True)
        hit = lane == first
        hits.append(hit)
        firsts.append(first.astype(f32))
        weights.append(jnp.exp(m - top))
        score = jnp.where(hit, -jnp.inf, score)
    tot = weights[0] + weights[1] + weights[2] + weights[3]
    onehot = jnp.where(hits[0] | hits[1] | hits[2] | hits[3], 1.0, 0.0)
    ri = lax.broadcasted_iota(jnp.int32, (tm, tm), 0)
    ci = lax.broadcasted_iota(jnp.int32, (tm, tm), 1)
    before = jnp.where(ci < ri, 1.0, 0.0).astype(bf16)
    rank = carry_ref[...] + jnp.dot(before, onehot.astype(bf16), preferred_element_type=f32)
    carry_ref[...] = carry_ref[...] + jnp.sum(onehot, axis=0, keepdims=True)
    cnt_ref[...] = carry_ref[...]
    packed = jnp.zeros((tm, LANES), f32)
    for k in range(TOP_K):
        rank_k = jnp.sum(jnp.where(hits[k], rank, 0.0), axis=-1, keepdims=True)
        packed = jnp.where(lane == k, firsts[k], packed)
        packed = jnp.where(lane == TOP_K + k, weights[k] / tot, packed)
        packed = jnp.where(lane == 2 * TOP_K + k, rank_k, packed)
    rt_ref[...] = packed


def _outproj_router(a, b, x2, wo, g2, wr_hi, wr_lo, br, tm=512):
    m, d = x2.shape
    na, nb = a.shape[1], b.shape[1]
    row = lambda n: pl.BlockSpec((tm, n), lambda i: (i, 0))
    whole = lambda t: pl.BlockSpec(t.shape, lambda i: (0,) * t.ndim)
    return pl.pallas_call(
        _outproj_kernel,
        grid=(m // tm,),
        in_specs=[row(na), row(nb), row(d), whole(wo), whole(g2), whole(wr_hi), whole(wr_lo), whole(br)],
        out_specs=[row(d), row(d), row(LANES), pl.BlockSpec((1, LANES), lambda i: (0, 0))],
        out_shape=[jax.ShapeDtypeStruct((m, d), f32),
                   jax.ShapeDtypeStruct((m, d), bf16),
                   jax.ShapeDtypeStruct((m, LANES), f32),
                   jax.ShapeDtypeStruct((1, LANES), f32)],
        scratch_shapes=[pltpu.VMEM((1, LANES), f32)],
        compiler_params=_cparams("arbitrary"),
        name="outproj_router",
    )(a, b, x2, wo, g2, wr_hi, wr_lo, br)


def _moe_kernel(vb_ref, ve_ref, lo_ref, hi_ref, nv_ref, xs_ref, wgu_ref, bgu_ref, wd_ref, bd_ref,
                o_ref, wgu_bf, wd_bf):
    v = pl.program_id(0)
    prev = jnp.maximum(v - 1, 0)
    live = v < nv_ref[0]
    new_expert = (v == 0) | (ve_ref[v] != ve_ref[prev])
    new_block = (v == 0) | (vb_ref[v] != vb_ref[prev])

    @pl.when(live & new_expert)
    def _():
        wgu_bf[...] = wgu_ref[0].astype(bf16)
        wd_bf[...] = wd_ref[0].astype(bf16)

    @pl.when(live)
    def _():
        h = jnp.dot(xs_ref[...], wgu_bf[...], preferred_element_type=f32) + bgu_ref[0]
        x_glu = jnp.minimum(h[:, :D_FF], SWIGLU_LIMIT)
        x_lin = jnp.clip(h[:, D_FF:], -SWIGLU_LIMIT, SWIGLU_LIMIT)
        act = x_glu * jax.nn.sigmoid(SWIGLU_ALPHA * x_glu) * (x_lin + 1.0)
        y = jnp.dot(act.astype(bf16), wd_bf[...], preferred_element_type=f32) + bd_ref[0]
        row = vb_ref[v] * MOE_BLOCK + lax.broadcasted_iota(jnp.int32, (MOE_BLOCK, 1), 0)
        mine = (row >= lo_ref[v]) & (row < hi_ref[v])

        @pl.when(new_block)
        def _():
            o_ref[...] = jnp.where(mine, y, 0.0).astype(o_ref.dtype)

        @pl.when(jnp.logical_not(new_block))
        def _():
            o_ref[...] = jnp.where(mine, y.astype(o_ref.dtype), o_ref[...])


def _moe_experts(vb, ve, lo, hi, nv, xs, wgu, bgu, wd, bd):
    a, d = xs.shape
    nvis = vb.shape[0]
    return pl.pallas_call(
        _moe_kernel,
        grid_spec=pltpu.PrefetchScalarGridSpec(
            num_scalar_prefetch=5,
            grid=(nvis,),
            in_specs=[pl.BlockSpec((MOE_BLOCK, d), lambda v, vb, ve, *_: (vb[v], 0)),
                      pl.BlockSpec((1, d, 2 * D_FF), lambda v, vb, ve, *_: (ve[v], 0, 0)),
                      pl.BlockSpec((1, 1, 2 * D_FF), lambda v, vb, ve, *_: (ve[v], 0, 0)),
                      pl.BlockSpec((1, D_FF, d), lambda v, vb, ve, *_: (ve[v], 0, 0)),
                      pl.BlockSpec((1, 1, d), lambda v, vb, ve, *_: (ve[v], 0, 0))],
            out_specs=pl.BlockSpec((MOE_BLOCK, d), lambda v, vb, ve, *_: (vb[v], 0)),
            scratch_shapes=[pltpu.VMEM((d, 2 * D_FF), bf16), pltpu.VMEM((D_FF, d), bf16)],
        ),
        out_shape=jax.ShapeDtypeStruct((a, d), bf16),
        compiler_params=_cparams("arbitrary"),
        name="moe_experts",
    )(vb, ve, lo, hi, nv, xs, wgu, bgu, wd, bd)


def _visit_table(counts, n_rows):
    end = jnp.cumsum(counts)
    start = end - counts
    nblk = n_rows // MOE_BLOCK
    first_blk = start // MOE_BLOCK
    last_blk = jnp.where(counts > 0, (end - 1) // MOE_BLOCK, first_blk - 1)
    per_e = jnp.maximum(last_blk - first_blk + 1, 0)
    v_end = jnp.cumsum(per_e)
    v_start = v_end - per_e
    nvis = nblk + N_EXPERTS - 1
    v = jnp.arange(nvis, dtype=jnp.int32)
    n_live = v_end[-1]
    vc = jnp.minimum(v, n_live - 1)
    ve = jnp.searchsorted(v_end, vc, side='right').astype(jnp.int32)
    vb = (first_blk[ve] + vc - v_start[ve]).astype(jnp.int32)
    return vb, ve, start[ve].astype(jnp.int32), end[ve].astype(jnp.int32), n_live.astype(jnp.int32).reshape(1), start


def _rope_tables(positions):
    half = ROPE_DIM // 2
    inv_freq = jnp.exp(-math.log(ROPE_THETA) * jnp.arange(0, ROPE_DIM, 2, dtype=f32) / ROPE_DIM)
    ang = positions.astype(f32)[..., None] * inv_freq
    cos, sin = jnp.cos(ang), jnp.sin(ang)
    rest = HEAD_DIM - ROPE_DIM
    b, s = positions.shape
    cos_f = jnp.concatenate([cos, cos, jnp.ones((b, s, rest), f32)], axis=-1)
    sin_f = jnp.concatenate([-sin, sin, jnp.zeros((b, s, rest), f32)], axis=-1)
    perm = jnp.zeros((HEAD_DIM, HEAD_DIM), f32)
    idx = jnp.arange(half)
    perm = perm.at[idx + half, idx].set(1.0).at[idx, idx + half].set(1.0)
    return cos_f, sin_f, perm.astype(bf16), cos.transpose(0, 2, 1), sin.transpose(0, 2, 1)


def _layer(x, positions, norm1_g, w_in, q_norm_g, k_norm_g, cmp_pos_k, cmp_w1_k, cmp_w2_k,
           cmp_pos_v, cmp_w1_v, cmp_w2_v, gla_w_gate, gla_b_gate, gla_norm_g, w_out, norm2_g,
           w_router, b_router, w_gate_up, b_gate_up, w_down, b_down):
    b, s, d = x.shape
    m = b * s
    hk, g, dh = NSA_KV_HEADS, NSA_GROUP, HEAD_DIM
    x2 = x.reshape(m, d)

    offs = [0]
    for sz in IN_SIZES:
        offs.append(offs[-1] + sz)
    seg = lambda i: w_in[:, offs[i]:offs[i + 1]]
    w_main = jnp.concatenate([seg(i) for i in (0, 1, 2, 3, 4, 5, 6, 8, 9, 10, 12)], axis=1).astype(bf16)
    padw = lambda t: jnp.concatenate([t, jnp.zeros((d, LANES - t.shape[1]), f32)], axis=1)
    ng = g * 3
    w_small = jnp.concatenate([padw(seg(7)[:, :ng]), padw(seg(7)[:, ng:]), padw(seg(11))], axis=1).astype(bf16)
    proj, proj_s = _inproj(x2, norm1_g.reshape(1, d), w_main, w_small)
    proj = proj.reshape(b, s, -1)
    proj_s = proj_s.reshape(b, s, 3 * LANES)

    col = lambda i: proj[:, :, MAIN_OFFS[i]:MAIN_OFFS[i + 1]]
    nkc, nvc, nks, nvs, nkw, nvw = [col(i) for i in range(1, 7)]

    cos_f, sin_f, perm, cos_t, sin_t = _rope_tables(positions)
    heads = lambda t: t.reshape(b, s, hk, dh).transpose(0, 2, 1, 3)
    heads_t = lambda t: t.reshape(b, s, hk, dh).transpose(0, 2, 3, 1)
    halves = lambda t: (t.reshape(b, s // CMP_STRIDE, CMP_STRIDE, hk, dh)
                        .transpose(0, 3, 1, 2, 4).reshape(b, hk, s // CMP_STRIDE, CMP_STRIDE * dh))
    kc, vc, ks_rot, kw_rot = _kvprep(
        halves(nkc), halves(nvc), heads(nks), heads(nkw), cos_f, sin_f, k_norm_g, perm,
        cmp_pos_k.reshape(1, -1), cmp_w1_k.astype(bf16), cmp_w2_k.astype(bf16),
        cmp_pos_v.reshape(1, -1), cmp_w1_v.astype(bf16), cmp_w2_v.astype(bf16))
    ncp = -(-kc.shape[2] // LANES) * LANES
    if ncp != kc.shape[2]:
        padn = ((0, 0), (0, 0), (0, ncp - kc.shape[2]), (0, 0))
        kc, vc = jnp.pad(kc, padn), jnp.pad(vc, padn)
    nsa_out = _nsa_attention(proj, cos_t, sin_t, q_norm_g.reshape(dh, 1), kc, vc.transpose(0, 1, 3, 2),
                             ks_rot, heads_t(nvs), kw_rot, heads_t(nvw), proj_s).reshape(m, NSA_WIDTH)

    gla_out = _gla(proj, proj_s, gla_w_gate, gla_b_gate.reshape(1, -1),
                   gla_norm_g.reshape(1, GLA_VAL_DIM)).reshape(m, GLA_WIDTH)

    padr = lambda t: jnp.concatenate([t, jnp.zeros(t.shape[:-1] + (LANES - t.shape[-1],), t.dtype)], axis=-1)
    wr_hi = w_router.astype(bf16)
    wr_lo = (w_router - wr_hi.astype(f32)).astype(bf16)
    x1, h2, rt, cnt = _outproj_router(nsa_out, gla_out, x2, w_out.astype(bf16), norm2_g.reshape(1, d),
                                      padr(wr_hi), padr(wr_lo), padr(b_router.reshape(1, -1)))

    top_idx = rt[:, :TOP_K].astype(jnp.int32)
    gate = rt[:, TOP_K:2 * TOP_K]
    rank = rt[:, 2 * TOP_K:3 * TOP_K].astype(jnp.int32)
    counts = cnt[0, :N_EXPERTS].astype(jnp.int32)
    a = m * TOP_K
    vb, ve, lo, hi, nv, start = _visit_table(counts, a)
    onehot = top_idx[:, :, None] == jnp.arange(N_EXPERTS, dtype=jnp.int32)
    pos = jnp.sum(jnp.where(onehot, start, 0), axis=-1) + rank
    sorted_tok = (jnp.argsort(top_idx.reshape(-1)) // TOP_K).astype(jnp.int32)
    xs = jnp.take(h2, sorted_tok, axis=0)
    ys = _moe_experts(vb, ve, lo, hi, nv, xs, w_gate_up, b_gate_up.reshape(N_EXPERTS, 1, -1),
                      w_down, b_down.reshape(N_EXPERTS, 1, -1))
    moe = jnp.sum(jnp.take(ys, pos, axis=0).astype(f32) * gate[:, :, None], axis=1)
    return (x1 + moe).reshape(b, s, d)


def kernel(x, positions, norm1_g, w_in, nsa_q_norm_g, nsa_k_norm_g, cmp_pos_k, cmp_w1_k, cmp_w2_k,
           cmp_pos_v, cmp_w1_v, cmp_w2_v, gla_w_gate, gla_b_gate, gla_norm_g, w_out, norm2_g,
           w_router, b_router, w_gate_up, b_gate_up, w_down, b_down):
    for l in range(norm1_g.shape[0]):
        x = _layer(x, positions, norm1_g[l], w_in[l], nsa_q_norm_g[l], nsa_k_norm_g[l],
                   cmp_pos_k[l], cmp_w1_k[l], cmp_w2_k[l], cmp_pos_v[l], cmp_w1_v[l], cmp_w2_v[l],
                   gla_w_gate[l], gla_b_gate[l], gla_norm_g[l], w_out[l], norm2_g[l],
                   w_router[l], b_router[l], w_gate_up[l], b_gate_up[l], w_down[l], b_down[l])
    return x
```

```python
import functools
import math

import jax
import jax.numpy as jnp
from jax import lax
from jax.experimental import pallas as pl
from jax.experimental.pallas import tpu as pltpu

f32 = jnp.float32
bf16 = jnp.bfloat16

NSA_HEADS = 8
NSA_KV_HEADS = 2
NSA_GROUP = NSA_HEADS // NSA_KV_HEADS
HEAD_DIM = 64
CMP_BLOCK = 32
CMP_STRIDE = 16
CMP_HIDDEN = 256
SEL_BLOCK = 64
SEL_TOPK = 8
WINDOW = 512
FORCE_BONUS = 1e4
GLA_HEADS = 4
GLA_KEY_DIM = 64
GLA_VAL_DIM = 128
GLA_CHUNK = 64
GLA_GATE_RANK = 16
GLA_TAU = 16.0
ROPE_THETA = 500000.0
ROPE_DIM = HEAD_DIM // 4
N_EXPERTS = 32
TOP_K = 4
D_FF = 1024
SWIGLU_LIMIT = 7.0
SWIGLU_ALPHA = 1.702
EPS = 1e-6
NEG_INF = -1e30

NSA_WIDTH = NSA_HEADS * HEAD_DIM
NSA_KV_WIDTH = NSA_KV_HEADS * HEAD_DIM
GLA_KEY_WIDTH = GLA_HEADS * GLA_KEY_DIM
GLA_WIDTH = GLA_HEADS * GLA_VAL_DIM
IN_SIZES = (NSA_WIDTH,) + (NSA_KV_WIDTH,) * 6 + (
    NSA_HEADS * 3, GLA_KEY_WIDTH, GLA_KEY_WIDTH, GLA_WIDTH, GLA_GATE_RANK, GLA_WIDTH)
MAIN_SIZES = (NSA_WIDTH,) + (NSA_KV_WIDTH,) * 6 + (GLA_KEY_WIDTH, GLA_KEY_WIDTH, GLA_WIDTH, GLA_WIDTH)
MAIN_OFFS = tuple(sum(MAIN_SIZES[:i]) for i in range(len(MAIN_SIZES) + 1))

LANES = 128
VMEM_LIMIT = 48 * 1024 * 1024
MOE_BLOCK = 256

_NT = (((1,), (1,)), ((), ()))
_TN = (((0,), (0,)), ((), ()))


def _cparams(*sem):
    return pltpu.CompilerParams(dimension_semantics=sem, vmem_limit_bytes=VMEM_LIMIT)


def _rms(t, g):
    return t * lax.rsqrt(jnp.mean(t * t, axis=-1, keepdims=True) + EPS) * g


def _split_bf16(t):
    hi = t.astype(bf16)
    lo = (t - hi.astype(f32)).astype(bf16)
    return hi, lo


def _rope(t, cos_f, sin_f, perm):
    hi, lo = _split_bf16(t)
    rot = (jnp.dot(hi, perm, preferred_element_type=f32) +
           jnp.dot(lo, perm, preferred_element_type=f32))
    return t * cos_f + rot * sin_f


def _inproj_kernel(x_ref, g_ref, w_ref, ws_ref, o_ref, os_ref):
    x = x_ref[...]
    h = _rms(x, g_ref[...]).astype(bf16)
    n = o_ref.shape[1]
    step = 512
    for c in range(0, n, step):
        e = min(c + step, n)
        o_ref[:, c:e] = jnp.dot(h, w_ref[:, c:e], preferred_element_type=f32).astype(bf16)
    os_ref[...] = jnp.dot(h, ws_ref[...], preferred_element_type=f32)


def _inproj(x2, g, w_main, w_small, tm=512):
    m, d = x2.shape
    n = w_main.shape[1]
    ns = w_small.shape[1]
    return pl.pallas_call(
        _inproj_kernel,
        grid=(m // tm,),
        in_specs=[pl.BlockSpec((tm, d), lambda i: (i, 0)),
                  pl.BlockSpec((1, d), lambda i: (0, 0)),
                  pl.BlockSpec((d, n), lambda i: (0, 0)),
                  pl.BlockSpec((d, ns), lambda i: (0, 0))],
        out_specs=[pl.BlockSpec((tm, n), lambda i: (i, 0)),
                   pl.BlockSpec((tm, ns), lambda i: (i, 0))],
        out_shape=[jax.ShapeDtypeStruct((m, n), bf16),
                   jax.ShapeDtypeStruct((m, ns), f32)],
        compiler_params=_cparams("parallel"),
        name="inproj",
    )(x2, g, w_main, w_small)


def _kvprep_kernel(kch_ref, vch_ref, ks_ref, kw_ref, cos_ref, sin_ref, kg_ref, perm_ref,
                   posk_ref, w1k_ref, w2k_ref, posv_ref, w1v_ref, w2v_ref,
                   kc_ref, vc_ref, kso_ref, kwo_ref):
    half = CMP_STRIDE * HEAD_DIM

    def compress(h_ref, pos_ref, w1_ref, w2_ref):
        hb = h_ref[0, 0].astype(f32)
        n = hb.shape[0]
        top = (hb + pos_ref[:, :half]).astype(bf16)
        bot = (hb + pos_ref[:, half:]).astype(bf16)
        a = jnp.dot(top, w1_ref[:half, :], preferred_element_type=f32)
        b = jnp.dot(bot, w1_ref[half:, :], preferred_element_type=f32)
        pre = a + pltpu.roll(b, n - 1, 0)
        hid = pre * jax.nn.sigmoid(pre)
        return jnp.dot(hid.astype(bf16), w2_ref[...], preferred_element_type=f32)

    kc = compress(kch_ref, posk_ref, w1k_ref, w2k_ref)
    kc_ref[0, 0] = _rms(kc, kg_ref[0:1, :]).astype(bf16)
    vc_ref[0, 0] = compress(vch_ref, posv_ref, w1v_ref, w2v_ref).astype(bf16)
    cos_f = cos_ref[0]
    sin_f = sin_ref[0]
    perm = perm_ref[...]
    ks = _rms(ks_ref[0, 0].astype(f32), kg_ref[1:2, :])
    kso_ref[0, 0] = _rope(ks, cos_f, sin_f, perm).astype(bf16)
    kw = _rms(kw_ref[0, 0].astype(f32), kg_ref[2:3, :])
    kwo_ref[0, 0] = _rope(kw, cos_f, sin_f, perm).astype(bf16)


def _kvprep(kch, vch, ks, kw, cos_f, sin_f, kg, perm, posk, w1k, w2k, posv, w1v, w2v):
    b, hk, nh, hw = kch.shape
    s = ks.shape[2]
    per_head = lambda r, c: pl.BlockSpec((1, 1, r, c), lambda i, j: (i, j, 0, 0))
    whole = lambda a: pl.BlockSpec(a.shape, lambda i, j: (0,) * a.ndim)
    tab = pl.BlockSpec((1, s, HEAD_DIM), lambda i, j: (i, 0, 0))
    return pl.pallas_call(
        _kvprep_kernel,
        grid=(b, hk),
        in_specs=[per_head(nh, hw), per_head(nh, hw), per_head(s, HEAD_DIM), per_head(s, HEAD_DIM),
                  tab, tab, whole(kg), whole(perm),
                  whole(posk), whole(w1k), whole(w2k), whole(posv), whole(w1v), whole(w2v)],
        out_specs=[per_head(nh, HEAD_DIM), per_head(nh, HEAD_DIM),
                   per_head(s, HEAD_DIM), per_head(s, HEAD_DIM)],
        out_shape=[jax.ShapeDtypeStruct((b, hk, nh, HEAD_DIM), bf16),
                   jax.ShapeDtypeStruct((b, hk, nh, HEAD_DIM), bf16),
                   jax.ShapeDtypeStruct((b, hk, s, HEAD_DIM), bf16),
                   jax.ShapeDtypeStruct((b, hk, s, HEAD_DIM), bf16)],
        compiler_params=_cparams("parallel", "parallel"),
        name="nsa_kvprep",
    )(kch, vch, ks, kw, cos_f, sin_f, kg, perm, posk, w1k, w2k, posv, w1v, w2v)


def _nsa_kernel(q_ref, cos_ref, sin_ref, qg_ref, kc_ref, vct_ref, ks_ref, vst_ref, kw_ref, vwt_ref,
                gate_ref, o_ref, sel_ref, *, tq, ck):
    g = NSA_GROUP
    qi = pl.program_id(2)
    t0 = pl.multiple_of(qi * tq, tq)
    scale = HEAD_DIM ** -0.5 * math.log2(math.e)
    per_head = lambda t: jnp.concatenate([t] * g, axis=1)

    qt = q_ref[0].astype(f32).T
    qt = jnp.concatenate([qt[i * HEAD_DIM:(i + 1) * HEAD_DIM] for i in range(g)], axis=1)
    qn = qt * lax.rsqrt(jnp.mean(qt * qt, axis=0, keepdims=True) + EPS) * qg_ref[...]
    half = ROPE_DIM // 2
    cos8, sin8 = per_head(cos_ref[0]), per_head(sin_ref[0])
    x1, x2 = qn[:half], qn[half:ROPE_DIM]
    q_rot = jnp.concatenate([x1 * cos8 - x2 * sin8, x2 * cos8 + x1 * sin8, qn[ROPE_DIM:]], axis=0)
    q_cmp = (qn * scale).astype(bf16)
    q_rot = (q_rot * scale).astype(bf16)
    tq_row = t0 + lax.broadcasted_iota(jnp.int32, (1, tq), 1)

    kc = kc_ref[0, 0]
    ncp = kc.shape[0]
    n_col = lax.broadcasted_iota(jnp.int32, (ncp, 1), 0)
    ok_c = (n_col * CMP_STRIDE + (CMP_BLOCK - 1)) <= tq_row
    s_c = jnp.dot(kc, q_cmp, preferred_element_type=f32) + per_head(jnp.where(ok_c, 0.0, NEG_INF))
    m_c = jnp.max(s_c, axis=0, keepdims=True)
    p_c = jnp.exp2(s_c - m_c) * per_head(jnp.where(ok_c, 1.0, 0.0))
    den = jnp.sum(p_c, axis=0, keepdims=True)
    p_c = p_c * (1.0 / jnp.where(den > 0, den, 1.0))
    o_cmp = jnp.dot(vct_ref[0, 0], p_c.astype(bf16), preferred_element_type=f32)

    pg = p_c[:, :tq]
    for i in range(1, g):
        pg = pg + p_c[:, i * tq:(i + 1) * tq]
    nj = sel_ref.shape[0]
    jj = lax.broadcasted_iota(jnp.int32, (nj, ncp), 0)
    nn = lax.broadcasted_iota(jnp.int32, (nj, ncp), 1)
    overlap = (nn * CMP_STRIDE < (jj + 1) * SEL_BLOCK) & (nn * CMP_STRIDE + CMP_BLOCK > jj * SEL_BLOCK)
    overlap = jnp.where(overlap, 1.0, 0.0).astype(bf16)
    pg_hi, pg_lo = _split_bf16(pg)
    imp = (jnp.dot(overlap, pg_hi, preferred_element_type=f32) +
           jnp.dot(overlap, pg_lo, preferred_element_type=f32))
    j_col = lax.broadcasted_iota(jnp.int32, (nj, 1), 0)
    j_f = j_col.astype(f32)
    cur = tq_row // SEL_BLOCK
    valid = j_col <= cur
    forced = (j_col == 0) | (j_col == cur) | (j_col == cur - 1)
    score = jnp.where(valid, imp + jnp.where(forced, FORCE_BONUS, 0.0), NEG_INF)
    sel = jnp.zeros((nj, tq), f32)
    for _ in range(SEL_TOPK):
        m = jnp.max(score, axis=0, keepdims=True)
        first = jnp.min(jnp.where(score == m, j_f, float(nj)), axis=0, keepdims=True)
        hit = j_f == first
        sel = jnp.where(hit, 1.0, sel)
        score = jnp.where(hit, -jnp.inf, score)
    sel_ref[...] = jnp.where(valid, sel, 0.0)

    wl = WINDOW + tq
    w0 = pl.multiple_of(jnp.maximum(t0 - WINDOW, 0), LANES)
    kpos_w = w0 + lax.broadcasted_iota(jnp.int32, (wl, 1), 0)
    ok_w = (kpos_w <= tq_row) & (kpos_w > tq_row - WINDOW)
    s_w = jnp.dot(kw_ref[0, 0, pl.ds(w0, wl), :], q_rot, preferred_element_type=f32)
    s_w = s_w + per_head(jnp.where(ok_w, 0.0, NEG_INF))
    p_w = jnp.exp2(s_w - jnp.max(s_w, axis=0, keepdims=True))
    l_w = jnp.sum(p_w, axis=0, keepdims=True)
    o_win = jnp.dot(vwt_ref[0, 0, :, pl.ds(w0, wl)], p_w.astype(bf16), preferred_element_type=f32)
    o_win = o_win * (1.0 / l_w)

    bpc = ck // SEL_BLOCK

    def sel_body(c, carry):
        m_prev, l_prev, acc = carry
        k0 = pl.multiple_of(c * ck, ck)
        kpos = k0 + lax.broadcasted_iota(jnp.int32, (ck, 1), 0)
        picked = jnp.concatenate(
            [jnp.broadcast_to(sel_ref[pl.ds(c * bpc + i, 1), :], (SEL_BLOCK, tq)) for i in range(bpc)], axis=0)
        ok = (picked > 0.5) & (kpos <= tq_row)
        s = jnp.dot(ks_ref[0, 0, pl.ds(k0, ck), :], q_rot, preferred_element_type=f32)
        s = s + per_head(jnp.where(ok, 0.0, NEG_INF))
        m_new = jnp.maximum(m_prev, jnp.max(s, axis=0, keepdims=True))
        alpha = jnp.exp2(m_prev - m_new)
        p = jnp.exp2(s - m_new)
        l_new = alpha * l_prev + jnp.sum(p, axis=0, keepdims=True)
        acc = alpha * acc + jnp.dot(vst_ref[0, 0, :, pl.ds(k0, ck)], p.astype(bf16),
                                    preferred_element_type=f32)
        return m_new, l_new, acc

    init = (jnp.full((1, g * tq), NEG_INF, f32), jnp.zeros((1, g * tq), f32),
            jnp.zeros((HEAD_DIM, g * tq), f32))
    _, l_s, acc_s = lax.fori_loop(0, (t0 + tq + ck - 1) // ck, sel_body, init)
    o_sel = acc_s * (1.0 / l_s)

    gt = gate_ref[0].T
    gate = lambda j: jax.nn.sigmoid(jnp.concatenate([gt[i * 3 + j:i * 3 + j + 1] for i in range(g)], axis=1))
    ot = gate(0) * o_cmp + gate(1) * o_sel + gate(2) * o_win
    o2 = jnp.concatenate([ot[:, i * tq:(i + 1) * tq] for i in range(g)], axis=0)
    o_ref[0] = o2.T.astype(o_ref.dtype)


def _nsa_attention(proj, cos_t, sin_t, qg, kc, vct, ks, vst, kw, vwt, gate_logits, tq=256, ck=512):
    b, s, _ = proj.shape
    hk, g, dh = NSA_KV_HEADS, NSA_GROUP, HEAD_DIM
    ncp = kc.shape[2]
    tab = pl.BlockSpec((1, ROPE_DIM // 2, tq), lambda i, j, t: (i, 0, t))
    kv = lambda n: pl.BlockSpec((1, 1, n, dh), lambda i, j, t: (i, j, 0, 0))
    kvt = lambda n: pl.BlockSpec((1, 1, dh, n), lambda i, j, t: (i, j, 0, 0))
    return pl.pallas_call(
        functools.partial(_nsa_kernel, tq=tq, ck=ck),
        grid=(b, hk, s // tq),
        in_specs=[pl.BlockSpec((1, tq, g * dh), lambda i, j, t: (i, t, j)),
                  tab, tab, pl.BlockSpec((dh, 1), lambda i, j, t: (0, 0)),
                  kv(ncp), kvt(ncp), kv(s), kvt(s), kv(s), kvt(s),
                  pl.BlockSpec((1, tq, LANES), lambda i, j, t: (i, t, j))],
        out_specs=pl.BlockSpec((1, tq, g * dh), lambda i, j, t: (i, t, j)),
        out_shape=jax.ShapeDtypeStruct((b, s, NSA_WIDTH), bf16),
        scratch_shapes=[pltpu.VMEM((s // SEL_BLOCK, tq), f32)],
        compiler_params=_cparams("parallel", "parallel", "arbitrary"),
        name="nsa_attention",
    )(proj, cos_t, sin_t, qg, kc, vct, ks, vst, kw, vwt, gate_logits)


def _gla_kernel(q_ref, k_ref, v_ref, lr_ref, wg_ref, bg_ref, r_ref, ng_ref, o_ref, *, grp):
    c = GLA_CHUNK
    s = q_ref.shape[1]
    dk, dv = GLA_KEY_DIM, GLA_VAL_DIM
    rows = grp * c
    ri = lax.broadcasted_iota(jnp.int32, (rows, rows), 0)
    ci = lax.broadcasted_iota(jnp.int32, (rows, rows), 1)
    causal = (ci <= ri) & (ci // c == ri // c)
    tri = jnp.where(causal, 1.0, 0.0).astype(bf16)
    lane = lax.broadcasted_iota(jnp.int32, (1, 2 * dk), 1)
    head_mask = [lane < dk, lane >= dk]
    wg = wg_ref[...]
    bg = bg_ref[...]
    ng = ng_ref[...]

    def body(n, states):
        r0 = pl.multiple_of(n * rows, rows)
        z = jnp.dot(lr_ref[0, pl.ds(r0, rows), :GLA_GATE_RANK], wg, preferred_element_type=f32,
                    precision=lax.Precision.HIGHEST) + bg
        log_a = -(jnp.maximum(-z, 0.0) + jnp.log(1.0 + jnp.exp(-jnp.abs(z)))) / GLA_TAU
        la_hi = log_a.astype(bf16)
        la_mid, la_lo = _split_bf16(log_a - la_hi.astype(f32))
        bcum = (jnp.dot(tri, la_hi, preferred_element_type=f32) + jnp.dot(tri, la_mid, preferred_element_type=f32) +
                jnp.dot(tri, la_lo, preferred_element_type=f32))
        qf = q_ref[0, pl.ds(r0, rows), :].astype(f32) * (dk ** -0.5)
        kf = k_ref[0, pl.ds(r0, rows), :].astype(f32)
        q_dec = qf * jnp.exp(bcum)
        k_dec = (kf * jnp.exp(-bcum)).astype(bf16)
        b_last = [bcum[(i + 1) * c - 1:(i + 1) * c, :] for i in range(grp)]
        k_state = jnp.concatenate(
            [kf[i * c:(i + 1) * c] * jnp.exp(b_last[i] - bcum[i * c:(i + 1) * c]) for i in range(grp)],
            axis=0).astype(bf16)
        new_states = []
        for h in range(2):
            q_h = jnp.where(head_mask[h], q_dec, 0.0).astype(bf16)
            v_h = v_ref[0, pl.ds(r0, rows), h * dv:(h + 1) * dv]
            attn = lax.dot_general(q_h, k_dec, _NT, preferred_element_type=f32)
            attn = jnp.where(causal, attn, 0.0).astype(bf16)
            o_intra = jnp.dot(attn, v_h, preferred_element_type=f32)
            st = states[h]
            o_inter = []
            for i in range(grp):
                o_inter.append(lax.dot_general(q_h[i * c:(i + 1) * c], st.astype(bf16), _NT,
                                               preferred_element_type=f32))
                u_t = lax.dot_general(v_h[i * c:(i + 1) * c], k_state[i * c:(i + 1) * c], _TN,
                                      preferred_element_type=f32)
                st = st * jnp.exp(b_last[i]) + u_t
            new_states.append(st)
            o = _rms(o_intra + jnp.concatenate(o_inter, axis=0), ng)
            rr = r_ref[0, pl.ds(r0, rows), h * dv:(h + 1) * dv].astype(f32)
            o_ref[0, pl.ds(r0, rows), h * dv:(h + 1) * dv] = (o * (rr * jax.nn.sigmoid(rr))).astype(o_ref.dtype)
        return tuple(new_states)

    zero = jnp.zeros((dv, 2 * dk), f32)
    lax.fori_loop(0, s // rows, body, (zero, zero))


def _gla(proj, proj_s, wg, bg, ng, grp=4):
    b, s, _ = proj.shape
    dk2, dv2 = 2 * GLA_KEY_DIM, 2 * GLA_VAL_DIM
    oq, ok_, ov, orr = (MAIN_OFFS[7] // dk2, MAIN_OFFS[8] // dk2, MAIN_OFFS[9] // dv2, MAIN_OFFS[10] // dv2)
    col = lambda w, o: pl.BlockSpec((1, s, w), lambda i, j: (i, 0, o + j))
    return pl.pallas_call(
        functools.partial(_gla_kernel, grp=grp),
        grid=(b, GLA_HEADS // 2),
        in_specs=[col(dk2, oq), col(dk2, ok_), col(dv2, ov),
                  pl.BlockSpec((1, s, LANES), lambda i, j: (i, 0, 2)),
                  pl.BlockSpec((GLA_GATE_RANK, dk2), lambda i, j: (0, j)),
                  pl.BlockSpec((1, dk2), lambda i, j: (0, j)),
                  col(dv2, orr),
                  pl.BlockSpec((1, GLA_VAL_DIM), lambda i, j: (0, 0))],
        out_specs=pl.BlockSpec((1, s, dv2), lambda i, j: (i, 0, j)),
        out_shape=jax.ShapeDtypeStruct((b, s, GLA_WIDTH), bf16),
        compiler_params=_cparams("parallel", "parallel"),
        name="gla",
    )(proj, proj, proj, proj_s, wg, bg, proj, ng)


def _outproj_kernel(a_ref, b_ref, x_ref, wo_ref, g2_ref, wrh_ref, wrl_ref, br_ref,
                    x1_ref, h2_ref, rt_ref, cnt_ref, carry_ref):
    @pl.when(pl.program_id(0) == 0)
    def _():
        carry_ref[...] = jnp.zeros_like(carry_ref)

    na = a_ref.shape[1]
    tm = a_ref.shape[0]
    y = (jnp.dot(a_ref[...], wo_ref[:na, :], preferred_element_type=f32) +
         jnp.dot(b_ref[...], wo_ref[na:, :], preferred_element_type=f32))
    x1 = x_ref[...] + y
    x1_ref[...] = x1
    h2 = _rms(x1, g2_ref[...])
    h2_ref[...] = h2.astype(bf16)
    hi, lo = _split_bf16(h2)
    logits = (jnp.dot(hi, wrh_ref[...], preferred_element_type=f32) +
              jnp.dot(lo, wrh_ref[...], preferred_element_type=f32) +
              jnp.dot(hi, wrl_ref[...], preferred_element_type=f32)) + br_ref[...]
    lane = lax.broadcasted_iota(jnp.int32, (1, LANES), 1)
    score = jnp.where(lane < N_EXPERTS, logits, -jnp.inf)
    top = jnp.max(score, axis=-1, keepdims=True)
    hits, firsts, weights = [], [], []
    for _ in range(TOP_K):
        m = jnp.max(score, axis=-1, keepdims=True)
        first = jnp.min(jnp.where(score == m, lane, LANES), axis=-1, keepdims=True)
        hit = lane == first
        hits.append(hit)
        firsts.append(first.astype(f32))
        weights.append(jnp.exp(m - top))
        score = jnp.where(hit, -jnp.inf, score)
    tot = weights[0] + weights[1] + weights[2] + weights[3]
    onehot = jnp.where(hits[0] | hits[1] | hits[2] | hits[3], 1.0, 0.0)
    ri = lax.broadcasted_iota(jnp.int32, (tm, tm), 0)
    ci = lax.broadcasted_iota(jnp.int32, (tm, tm), 1)
    before = jnp.where(ci < ri, 1.0, 0.0).astype(bf16)
    rank = carry_ref[...] + jnp.dot(before, onehot.astype(bf16), preferred_element_type=f32)
    carry_ref[...] = carry_ref[...] + jnp.sum(onehot, axis=0, keepdims=True)
    cnt_ref[...] = carry_ref[...]
    packed = jnp.zeros((tm, LANES), f32)
    for k in range(TOP_K):
        rank_k = jnp.sum(jnp.where(hits[k], rank, 0.0), axis=-1, keepdims=True)
        packed = jnp.where(lane == k, firsts[k], packed)
        packed = jnp.where(lane == TOP_K + k, weights[k] / tot, packed)
        packed = jnp.where(lane == 2 * TOP_K + k, rank_k, packed)
    rt_ref[...] = packed


def _outproj_router(a, b, x2, wo, g2, wr_hi, wr_lo, br, tm=512):
    m, d = x2.shape
    na, nb = a.shape[1], b.shape[1]
    row = lambda n: pl.BlockSpec((tm, n), lambda i: (i, 0))
    whole = lambda t: pl.BlockSpec(t.shape, lambda i: (0,) * t.ndim)
    return pl.pallas_call(
        _outproj_kernel,
        grid=(m // tm,),
        in_specs=[row(na), row(nb), row(d), whole(wo), whole(g2), whole(wr_hi), whole(wr_lo), whole(br)],
        out_specs=[row(d), row(d), row(LANES), pl.BlockSpec((1, LANES), lambda i: (0, 0))],
        out_shape=[jax.ShapeDtypeStruct((m, d), f32),
                   jax.ShapeDtypeStruct((m, d), bf16),
                   jax.ShapeDtypeStruct((m, LANES), f32),
                   jax.ShapeDtypeStruct((1, LANES), f32)],
        scratch_shapes=[pltpu.VMEM((1, LANES), f32)],
        compiler_params=_cparams("arbitrary"),
        name="outproj_router",
    )(a, b, x2, wo, g2, wr_hi, wr_lo, br)


def _moe_kernel(vb_ref, ve_ref, lo_ref, hi_ref, nv_ref, xs_ref, wgu_ref, bgu_ref, wd_ref, bd_ref,
                o_ref, wgu_bf, wd_bf):
    v = pl.program_id(0)
    prev = jnp.maximum(v - 1, 0)
    live = v < nv_ref[0]
    new_expert = (v == 0) | (ve_ref[v] != ve_ref[prev])
    new_block = (v == 0) | (vb_ref[v] != vb_ref[prev])

    @pl.when(live & new_expert)
    def _():
        wgu_bf[...] = wgu_ref[0].astype(bf16)
        wd_bf[...] = wd_ref[0].astype(bf16)

    @pl.when(live)
    def _():
        h = jnp.dot(xs_ref[...], wgu_bf[...], preferred_element_type=f32) + bgu_ref[0]
        x_glu = jnp.minimum(h[:, :D_FF], SWIGLU_LIMIT)
        x_lin = jnp.clip(h[:, D_FF:], -SWIGLU_LIMIT, SWIGLU_LIMIT)
        act = x_glu * jax.nn.sigmoid(SWIGLU_ALPHA * x_glu) * (x_lin + 1.0)
        y = jnp.dot(act.astype(bf16), wd_bf[...], preferred_element_type=f32) + bd_ref[0]
        row = vb_ref[v] * MOE_BLOCK + lax.broadcasted_iota(jnp.int32, (MOE_BLOCK, 1), 0)
        mine = (row >= lo_ref[v]) & (row < hi_ref[v])

        @pl.when(new_block)
        def _():
            o_ref[...] = jnp.where(mine, y, 0.0).astype(o_ref.dtype)

        @pl.when(jnp.logical_not(new_block))
        def _():
            o_ref[...] = jnp.where(mine, y.astype(o_ref.dtype), o_ref[...])


def _moe_experts(vb, ve, lo, hi, nv, xs, wgu, bgu, wd, bd):
    a, d = xs.shape
    nvis = vb.shape[0]
    return pl.pallas_call(
        _moe_kernel,
        grid_spec=pltpu.PrefetchScalarGridSpec(
            num_scalar_prefetch=5,
            grid=(nvis,),
            in_specs=[pl.BlockSpec((MOE_BLOCK, d), lambda v, vb, ve, *_: (vb[v], 0)),
                      pl.BlockSpec((1, d, 2 * D_FF), lambda v, vb, ve, *_: (ve[v], 0, 0)),
                      pl.BlockSpec((1, 1, 2 * D_FF), lambda v, vb, ve, *_: (ve[v], 0, 0)),
                      pl.BlockSpec((1, D_FF, d), lambda v, vb, ve, *_: (ve[v], 0, 0)),
                      pl.BlockSpec((1, 1, d), lambda v, vb, ve, *_: (ve[v], 0, 0))],
            out_specs=pl.BlockSpec((MOE_BLOCK, d), lambda v, vb, ve, *_: (vb[v], 0)),
            scratch_shapes=[pltpu.VMEM((d, 2 * D_FF), bf16), pltpu.VMEM((D_FF, d), bf16)],
        ),
        out_shape=jax.ShapeDtypeStruct((a, d), bf16),
        compiler_params=_cparams("arbitrary"),
        name="moe_experts",
    )(vb, ve, lo, hi, nv, xs, wgu, bgu, wd, bd)


def _visit_table(counts, n_rows):
    end = jnp.cumsum(counts)
    start = end - counts
    nblk = n_rows // MOE_BLOCK
    first_blk = start // MOE_BLOCK
    last_blk = jnp.where(counts > 0, (end - 1) // MOE_BLOCK, first_blk - 1)
    per_e = jnp.maximum(last_blk - first_blk + 1, 0)
    v_end = jnp.cumsum(per_e)
    v_start = v_end - per_e
    nvis = nblk + N_EXPERTS - 1
    v = jnp.arange(nvis, dtype=jnp.int32)
    n_live = v_end[-1]
    vc = jnp.minimum(v, n_live - 1)
    ve = jnp.sum((v_end[None, :] <= vc[:, None]).astype(jnp.int32), axis=1)
    vb = (first_blk[ve] + vc - v_start[ve]).astype(jnp.int32)
    return vb, ve, start[ve].astype(jnp.int32), end[ve].astype(jnp.int32), n_live.astype(jnp.int32).reshape(1), start


def _rope_tables(positions):
    half = ROPE_DIM // 2
    inv_freq = jnp.exp(-math.log(ROPE_THETA) * jnp.arange(0, ROPE_DIM, 2, dtype=f32) / ROPE_DIM)
    ang = positions.astype(f32)[..., None] * inv_freq
    cos, sin = jnp.cos(ang), jnp.sin(ang)
    rest = HEAD_DIM - ROPE_DIM
    b, s = positions.shape
    cos_f = jnp.concatenate([cos, cos, jnp.ones((b, s, rest), f32)], axis=-1)
    sin_f = jnp.concatenate([-sin, sin, jnp.zeros((b, s, rest), f32)], axis=-1)
    perm = jnp.zeros((HEAD_DIM, HEAD_DIM), f32)
    idx = jnp.arange(half)
    perm = perm.at[idx + half, idx].set(1.0).at[idx, idx + half].set(1.0)
    return cos_f, sin_f, perm.astype(bf16), cos.transpose(0, 2, 1), sin.transpose(0, 2, 1)


def _layer(x, positions, norm1_g, w_in, q_norm_g, k_norm_g, cmp_pos_k, cmp_w1_k, cmp_w2_k,
           cmp_pos_v, cmp_w1_v, cmp_w2_v, gla_w_gate, gla_b_gate, gla_norm_g, w_out, norm2_g,
           w_router, b_router, w_gate_up, b_gate_up, w_down, b_down):
    b, s, d = x.shape
    m = b * s
    hk, g, dh = NSA_KV_HEADS, NSA_GROUP, HEAD_DIM
    x2 = x.reshape(m, d)

    offs = [0]
    for sz in IN_SIZES:
        offs.append(offs[-1] + sz)
    seg = lambda i: w_in[:, offs[i]:offs[i + 1]]
    w_main = jnp.concatenate([seg(i) for i in (0, 1, 2, 3, 4, 5, 6, 8, 9, 10, 12)], axis=1).astype(bf16)
    padw = lambda t: jnp.concatenate([t, jnp.zeros((d, LANES - t.shape[1]), f32)], axis=1)
    ng = g * 3
    w_small = jnp.concatenate([padw(seg(7)[:, :ng]), padw(seg(7)[:, ng:]), padw(seg(11))], axis=1).astype(bf16)
    proj, proj_s = _inproj(x2, norm1_g.reshape(1, d), w_main, w_small)
    proj = proj.reshape(b, s, -1)
    proj_s = proj_s.reshape(b, s, 3 * LANES)

    col = lambda i: proj[:, :, MAIN_OFFS[i]:MAIN_OFFS[i + 1]]
    nkc, nvc, nks, nvs, nkw, nvw = [col(i) for i in range(1, 7)]

    cos_f, sin_f, perm, cos_t, sin_t = _rope_tables(positions)
    heads = lambda t: t.reshape(b, s, hk, dh).transpose(0, 2, 1, 3)
    heads_t = lambda t: t.reshape(b, s, hk, dh).transpose(0, 2, 3, 1)
    halves = lambda t: (t.reshape(b, s // CMP_STRIDE, CMP_STRIDE, hk, dh)
                        .transpose(0, 3, 1, 2, 4).reshape(b, hk, s // CMP_STRIDE, CMP_STRIDE * dh))
    kc, vc, ks_rot, kw_rot = _kvprep(
        halves(nkc), halves(nvc), heads(nks), heads(nkw), cos_f, sin_f, k_norm_g, perm,
        cmp_pos_k.reshape(1, -1), cmp_w1_k.astype(bf16), cmp_w2_k.astype(bf16),
        cmp_pos_v.reshape(1, -1), cmp_w1_v.astype(bf16), cmp_w2_v.astype(bf16))
    ncp = -(-kc.shape[2] // LANES) * LANES
    if ncp != kc.shape[2]:
        padn = ((0, 0), (0, 0), (0, ncp - kc.shape[2]), (0, 0))
        kc, vc = jnp.pad(kc, padn), jnp.pad(vc, padn)
    nsa_out = _nsa_attention(proj, cos_t, sin_t, q_norm_g.reshape(dh, 1), kc, vc.transpose(0, 1, 3, 2),
                             ks_rot, heads_t(nvs), kw_rot, heads_t(nvw), proj_s).reshape(m, NSA_WIDTH)

    gla_out = _gla(proj, proj_s, gla_w_gate, gla_b_gate.reshape(1, -1),
                   gla_norm_g.reshape(1, GLA_VAL_DIM)).reshape(m, GLA_WIDTH)

    padr = lambda t: jnp.concatenate([t, jnp.zeros(t.shape[:-1] + (LANES - t.shape[-1],), t.dtype)], axis=-1)
    wr_hi = w_router.astype(bf16)
    wr_lo = (w_router - wr_hi.astype(f32)).astype(bf16)
    x1, h2, rt, cnt = _outproj_router(nsa_out, gla_out, x2, w_out.astype(bf16), norm2_g.reshape(1, d),
                                      padr(wr_hi), padr(wr_lo), padr(b_router.reshape(1, -1)))

    top_idx = rt[:, :TOP_K].astype(jnp.int32)
    gate = rt[:, TOP_K:2 * TOP_K]
    rank = rt[:, 2 * TOP_K:3 * TOP_K].astype(jnp.int32)
    counts = cnt[0, :N_EXPERTS].astype(jnp.int32)
    a = m * TOP_K
    vb, ve, lo, hi, nv, start = _visit_table(counts, a)
    onehot = top_idx[:, :, None] == jnp.arange(N_EXPERTS, dtype=jnp.int32)
    pos = jnp.sum(jnp.where(onehot, start, 0), axis=-1) + rank
    sorted_tok = (jnp.argsort(top_idx.reshape(-1)) // TOP_K).astype(jnp.int32)
    rows_of = lambda t, idx: t.at[idx].get(mode='promise_in_bounds')
    xs = rows_of(h2, sorted_tok)
    ys = _moe_experts(vb, ve, lo, hi, nv, xs, w_gate_up, b_gate_up.reshape(N_EXPERTS, 1, -1),
                      w_down, b_down.reshape(N_EXPERTS, 1, -1))
    out = x1
    for k in range(TOP_K):
        out = out + rows_of(ys, pos[:, k]).astype(f32) * gate[:, k:k + 1]
    return out.reshape(b, s, d)


def kernel(x, positions, norm1_g, w_in, nsa_q_norm_g, nsa_k_norm_g, cmp_pos_k, cmp_w1_k, cmp_w2_k,
           cmp_pos_v, cmp_w1_v, cmp_w2_v, gla_w_gate, gla_b_gate, gla_norm_g, w_out, norm2_g,
           w_router, b_router, w_gate_up, b_gate_up, w_down, b_down):
    for l in range(norm1_g.shape[0]):
        x = _layer(x, positions, norm1_g[l], w_in[l], nsa_q_norm_g[l], nsa_k_norm_g[l],
                   cmp_pos_k[l], cmp_w1_k[l], cmp_w2_k[l], cmp_pos_v[l], cmp_w1_v[l], cmp_w2_v[l],
                   gla_w_gate[l], gla_b_gate[l], gla_norm_g[l], w_out[l], norm2_g[l],
                   w_router[l], b_router[l], w_gate_up[l], b_gate_up[l], w_down[l], b_down[l])
    return x
```

```python
import functools
import math

import jax
import jax.numpy as jnp
from jax import lax
from jax.experimental import pallas as pl
from jax.experimental.pallas import tpu as pltpu

f32 = jnp.float32
bf16 = jnp.bfloat16

NSA_HEADS = 8
NSA_KV_HEADS = 2
NSA_GROUP = NSA_HEADS // NSA_KV_HEADS
HEAD_DIM = 64
CMP_BLOCK = 32
CMP_STRIDE = 16
CMP_HIDDEN = 256
SEL_BLOCK = 64
SEL_TOPK = 8
WINDOW = 512
FORCE_BONUS = 1e4
GLA_HEADS = 4
GLA_KEY_DIM = 64
GLA_VAL_DIM = 128
GLA_CHUNK = 64
GLA_GATE_RANK = 16
GLA_TAU = 16.0
ROPE_THETA = 500000.0
ROPE_DIM = HEAD_DIM // 4
N_EXPERTS = 32
TOP_K = 4
D_FF = 1024
SWIGLU_LIMIT = 7.0
SWIGLU_ALPHA = 1.702
EPS = 1e-6
NEG_INF = -1e30

NSA_WIDTH = NSA_HEADS * HEAD_DIM
NSA_KV_WIDTH = NSA_KV_HEADS * HEAD_DIM
GLA_KEY_WIDTH = GLA_HEADS * GLA_KEY_DIM
GLA_WIDTH = GLA_HEADS * GLA_VAL_DIM
IN_SIZES = (NSA_WIDTH,) + (NSA_KV_WIDTH,) * 6 + (
    NSA_HEADS * 3, GLA_KEY_WIDTH, GLA_KEY_WIDTH, GLA_WIDTH, GLA_GATE_RANK, GLA_WIDTH)
MAIN_SIZES = (NSA_WIDTH,) + (NSA_KV_WIDTH,) * 6 + (GLA_KEY_WIDTH, GLA_KEY_WIDTH, GLA_WIDTH, GLA_WIDTH)
MAIN_OFFS = tuple(sum(MAIN_SIZES[:i]) for i in range(len(MAIN_SIZES) + 1))

LANES = 128
VMEM_LIMIT = 48 * 1024 * 1024
MOE_BLOCK = 1024
MOE_SUB = 256

_NT = (((1,), (1,)), ((), ()))
_TN = (((0,), (0,)), ((), ()))


def _cparams(*sem):
    return pltpu.CompilerParams(dimension_semantics=sem, vmem_limit_bytes=VMEM_LIMIT)


def _rms(t, g):
    return t * lax.rsqrt(jnp.mean(t * t, axis=-1, keepdims=True) + EPS) * g


def _split_bf16(t):
    hi = t.astype(bf16)
    lo = (t - hi.astype(f32)).astype(bf16)
    return hi, lo


def _rope(t, cos_f, sin_f, perm):
    hi, lo = _split_bf16(t)
    rot = (jnp.dot(hi, perm, preferred_element_type=f32) +
           jnp.dot(lo, perm, preferred_element_type=f32))
    return t * cos_f + rot * sin_f


def _inproj_kernel(x_ref, g_ref, w_ref, ws_ref, o_ref, os_ref):
    x = x_ref[...]
    h = _rms(x, g_ref[...]).astype(bf16)
    n = o_ref.shape[1]
    step = 512
    for c in range(0, n, step):
        e = min(c + step, n)
        o_ref[:, c:e] = jnp.dot(h, w_ref[:, c:e], preferred_element_type=f32).astype(bf16)
    os_ref[...] = jnp.dot(h, ws_ref[...], preferred_element_type=f32)


def _inproj(x2, g, w_main, w_small, tm=512):
    m, d = x2.shape
    n = w_main.shape[1]
    ns = w_small.shape[1]
    return pl.pallas_call(
        _inproj_kernel,
        grid=(m // tm,),
        in_specs=[pl.BlockSpec((tm, d), lambda i: (i, 0)),
                  pl.BlockSpec((1, d), lambda i: (0, 0)),
                  pl.BlockSpec((d, n), lambda i: (0, 0)),
                  pl.BlockSpec((d, ns), lambda i: (0, 0))],
        out_specs=[pl.BlockSpec((tm, n), lambda i: (i, 0)),
                   pl.BlockSpec((tm, ns), lambda i: (i, 0))],
        out_shape=[jax.ShapeDtypeStruct((m, n), bf16),
                   jax.ShapeDtypeStruct((m, ns), f32)],
        compiler_params=_cparams("parallel"),
        name="inproj",
    )(x2, g, w_main, w_small)


def _kvprep_kernel(kc_ref, vc_ref, ks_ref, vs_ref, kw_ref, vw_ref, cos_ref, sin_ref, kg_ref, ones_ref,
                   perm_ref, pick_ref, posk_ref, w1k_ref, w2k_ref, posv_ref, w1v_ref, w2v_ref,
                   kco_ref, vcto_ref, kso_ref, vsto_ref, kwo_ref, vwto_ref, tmp_ref):
    ones_bd = ones_ref[...]
    nh_out = kco_ref.shape[2]

    def rms_heads(t, g):
        sq_hi, sq_lo = _split_bf16(t * t)
        ss = (jnp.dot(sq_hi, ones_bd, preferred_element_type=f32) +
              jnp.dot(sq_lo, ones_bd, preferred_element_type=f32))
        return t * lax.rsqrt(ss * (1.0 / HEAD_DIM) + EPS) * g

    def compress(src_ref, pos_ref, w1_ref, w2_ref):
        tmp_ref[...] = src_ref[0].astype(f32)
        nh = tmp_ref.shape[0] // CMP_STRIDE
        a = jnp.zeros((nh, NSA_KV_HEADS * CMP_HIDDEN), f32)
        b = jnp.zeros((nh, NSA_KV_HEADS * CMP_HIDDEN), f32)
        for p in range(CMP_STRIDE):
            rows = tmp_ref[pl.ds(p, nh, stride=CMP_STRIDE), :]
            a = a + jnp.dot((rows + pos_ref[p:p + 1, :]).astype(bf16), w1_ref[p], preferred_element_type=f32)
            q = CMP_STRIDE + p
            b = b + jnp.dot((rows + pos_ref[q:q + 1, :]).astype(bf16), w1_ref[q], preferred_element_type=f32)
        pre = a + pltpu.roll(b, nh - 1, 0)
        hid = pre * jax.nn.sigmoid(pre)
        return jnp.dot(hid.astype(bf16), w2_ref[...], preferred_element_type=f32)

    kc = rms_heads(compress(kc_ref, posk_ref, w1k_ref, w2k_ref), kg_ref[0:1, :]).astype(bf16)
    tmp_ref[:nh_out, :] = compress(vc_ref, posv_ref, w1v_ref, w2v_ref)
    vcto_ref[0] = tmp_ref[:nh_out, :].T.astype(bf16)
    cos_f = cos_ref[0]
    sin_f = sin_ref[0]
    perm = perm_ref[...]
    s, w = ks_ref.shape[1], ks_ref.shape[2]
    ks = _rope(rms_heads(ks_ref[0].astype(f32), kg_ref[1:2, :]), cos_f, sin_f, perm).astype(bf16)
    kw = _rope(rms_heads(kw_ref[0].astype(f32), kg_ref[2:3, :]), cos_f, sin_f, perm).astype(bf16)
    row = lax.broadcasted_iota(jnp.int32, (s, LANES), 0)
    lane = lax.broadcasted_iota(jnp.int32, (s, LANES), 1)
    block_onehot = jnp.where(row // SEL_BLOCK + HEAD_DIM == lane, 1.0, 0.0)
    pad_flag = jnp.where(lax.broadcasted_iota(jnp.int32, (WINDOW, LANES), 1) == HEAD_DIM, 1.0, 0.0)
    for h in range(NSA_KV_HEADS):
        pick = pick_ref[h]
        kco_ref[0, h] = jnp.dot(kc, pick, preferred_element_type=f32).astype(bf16)
        kso_ref[0, h] = (jnp.dot(ks, pick, preferred_element_type=f32) + block_onehot).astype(bf16)
        kwo_ref[0, h, :WINDOW, :] = pad_flag.astype(bf16)
        kwo_ref[0, h, WINDOW:, :] = jnp.dot(kw, pick, preferred_element_type=f32).astype(bf16)
    vsto_ref[0] = vs_ref[0].astype(f32).T.astype(bf16)
    vwto_ref[0] = jnp.concatenate([jnp.zeros((w, WINDOW), f32), vw_ref[0].astype(f32).T], axis=1).astype(bf16)


def _kvprep(proj, cos2, sin2, kg2, ones_bd, perm2, pick, posk, w1k, w2k, posv, w1v, w2v):
    b, s, _ = proj.shape
    w = NSA_KV_WIDTH
    assert w == LANES
    nh = s // CMP_STRIDE
    col = lambda i: pl.BlockSpec((1, s, w), lambda n: (n, 0, MAIN_OFFS[i] // w))
    whole = lambda a: pl.BlockSpec(a.shape, lambda n: (0,) * a.ndim)
    tab = pl.BlockSpec((1, s, w), lambda n: (n, 0, 0))
    out = lambda *shp: (pl.BlockSpec((1,) + shp, lambda n: (n,) + (0,) * len(shp)),
                        jax.ShapeDtypeStruct((b,) + shp, bf16))
    hk = NSA_KV_HEADS
    outs = [out(hk, nh, LANES), out(w, nh), out(hk, s, LANES), out(w, s), out(hk, WINDOW + s, LANES),
            out(w, WINDOW + s)]
    return pl.pallas_call(
        _kvprep_kernel,
        grid=(b,),
        in_specs=[col(1), col(2), col(3), col(4), col(5), col(6), tab, tab, whole(kg2), whole(ones_bd),
                  whole(perm2), whole(pick), whole(posk), whole(w1k), whole(w2k), whole(posv), whole(w1v),
                  whole(w2v)],
        out_specs=[o[0] for o in outs],
        out_shape=[o[1] for o in outs],
        scratch_shapes=[pltpu.VMEM((s, w), f32)],
        compiler_params=_cparams("parallel"),
        name="nsa_kvprep",
    )(proj, proj, proj, proj, proj, proj, cos2, sin2, kg2, ones_bd, perm2, pick, posk, w1k, w2k, posv, w1v, w2v)


def _nsa_kernel(q_ref, cos_ref, sin_ref, qg_ref, kc_ref, vct_ref, ks_ref, vst_ref, kw_ref, vwt_ref,
                gate_ref, o_ref, *, tq, ck):
    g = NSA_GROUP
    qi = pl.program_id(2)
    t0 = pl.multiple_of(qi * tq, tq)
    scale = HEAD_DIM ** -0.5 * math.log2(math.e)
    per_head = lambda t: jnp.concatenate([t] * g, axis=1)

    qt = q_ref[0].astype(f32).T
    qt = jnp.concatenate([qt[i * HEAD_DIM:(i + 1) * HEAD_DIM] for i in range(g)], axis=1)
    qn = qt * lax.rsqrt(jnp.mean(qt * qt, axis=0, keepdims=True) + EPS) * qg_ref[...]
    half = ROPE_DIM // 2
    cos8, sin8 = per_head(cos_ref[0]), per_head(sin_ref[0])
    x1, x2 = qn[:half], qn[half:ROPE_DIM]
    q_rot = jnp.concatenate([x1 * cos8 - x2 * sin8, x2 * cos8 + x1 * sin8, qn[ROPE_DIM:]], axis=0)
    n_extra = kc_ref.shape[3] - HEAD_DIM
    q_cmp = jnp.concatenate([(qn * scale).astype(bf16), jnp.zeros((n_extra, g * tq), bf16)], axis=0)
    q_rot = (q_rot * scale).astype(bf16)
    tq_row = t0 + lax.broadcasted_iota(jnp.int32, (1, tq), 1)

    kc = kc_ref[0, 0]
    ncp = kc.shape[0]
    n_col = lax.broadcasted_iota(jnp.int32, (ncp, 1), 0)
    ok_c = (n_col * CMP_STRIDE + (CMP_BLOCK - 1)) <= tq_row
    s_c = jnp.dot(kc, q_cmp, preferred_element_type=f32) + per_head(jnp.where(ok_c, 0.0, NEG_INF))
    m_c = jnp.max(s_c, axis=0, keepdims=True)
    p_c = jnp.exp2(s_c - m_c) * per_head(jnp.where(ok_c, 1.0, 0.0))
    den = jnp.sum(p_c, axis=0, keepdims=True)
    p_c = p_c * (1.0 / jnp.where(den > 0, den, 1.0))
    o_cmp = jnp.dot(vct_ref[0], p_c.astype(bf16), preferred_element_type=f32)

    pg = p_c[:, :tq]
    for i in range(1, g):
        pg = pg + p_c[:, i * tq:(i + 1) * tq]
    nj = ks_ref.shape[2] // SEL_BLOCK
    jj = lax.broadcasted_iota(jnp.int32, (nj, ncp), 0)
    nn = lax.broadcasted_iota(jnp.int32, (nj, ncp), 1)
    overlap = (nn * CMP_STRIDE < (jj + 1) * SEL_BLOCK) & (nn * CMP_STRIDE + CMP_BLOCK > jj * SEL_BLOCK)
    overlap = jnp.where(overlap, 1.0, 0.0).astype(bf16)
    pg_hi, pg_lo = _split_bf16(pg)
    imp = (jnp.dot(overlap, pg_hi, preferred_element_type=f32) +
           jnp.dot(overlap, pg_lo, preferred_element_type=f32))
    j_col = lax.broadcasted_iota(jnp.int32, (nj, 1), 0)
    j_f = j_col.astype(f32)
    cur = tq_row // SEL_BLOCK
    valid = j_col <= cur
    forced = (j_col == 0) | (j_col == cur) | (j_col == cur - 1)
    score = jnp.where(valid, imp + jnp.where(forced, FORCE_BONUS, 0.0), NEG_INF)
    sel = jnp.zeros((nj, tq), f32)
    for _ in range(SEL_TOPK):
        m = jnp.max(score, axis=0, keepdims=True)
        first = jnp.min(jnp.where(score == m, j_f, float(nj)), axis=0, keepdims=True)
        hit = j_f == first
        sel = jnp.where(hit, 1.0, sel)
        score = jnp.where(hit, -jnp.inf, score)
    sel = jnp.where(valid, sel, 0.0)

    def weighted_values(vt, p):
        lhs = jnp.concatenate([vt, jnp.ones((16, vt.shape[1]), bf16)], axis=0)
        r = jnp.dot(lhs, p.astype(bf16), preferred_element_type=f32)
        return r[:HEAD_DIM], r[HEAD_DIM:HEAD_DIM + 1]

    wl = WINDOW + tq
    flag_row = jnp.where(lax.broadcasted_iota(jnp.int32, (n_extra, 1), 0) == 0, NEG_INF, 0.0)
    q_win = jnp.concatenate([q_rot, jnp.broadcast_to(flag_row, (n_extra, g * tq)).astype(bf16)], axis=0)
    s_w = jnp.dot(kw_ref[0, 0, pl.ds(t0, wl), :], q_win, preferred_element_type=f32)
    step = lax.broadcasted_iota(jnp.int32, (tq, 1), 0)
    lo_ok = (t0 - WINDOW + step) > (tq_row - WINDOW)
    hi_ok = (t0 + step) <= tq_row
    s_lo = s_w[:tq] + per_head(jnp.where(lo_ok, 0.0, NEG_INF))
    s_mid = s_w[tq:wl - tq]
    s_hi = s_w[wl - tq:] + per_head(jnp.where(hi_ok, 0.0, NEG_INF))
    m_w = jnp.maximum(jnp.maximum(jnp.max(s_lo, axis=0, keepdims=True), jnp.max(s_mid, axis=0, keepdims=True)),
                      jnp.max(s_hi, axis=0, keepdims=True))
    p_w = jnp.concatenate([jnp.exp2(s_lo - m_w), jnp.exp2(s_mid - m_w), jnp.exp2(s_hi - m_w)], axis=0)
    o_win, l_w = weighted_values(vwt_ref[0, :, pl.ds(t0, wl)], p_w)
    o_win = o_win * (1.0 / l_w)

    sel_bias = per_head(jnp.where(sel > 0.5, 0.0, NEG_INF)).astype(bf16)
    q_sel = jnp.concatenate([q_rot, sel_bias, jnp.zeros((n_extra - nj, g * tq), bf16)], axis=0)

    def sel_step(carry, k0, diagonal):
        m_prev, l_prev, acc = carry
        s = jnp.dot(ks_ref[0, 0, pl.ds(k0, ck), :], q_sel, preferred_element_type=f32)
        if diagonal:
            kpos = k0 + lax.broadcasted_iota(jnp.int32, (ck, 1), 0)
            s = s + per_head(jnp.where(kpos <= tq_row, 0.0, NEG_INF))
        m_new = jnp.maximum(m_prev, jnp.max(s, axis=0, keepdims=True))
        alpha = jnp.exp2(m_prev - m_new)
        pv, p_sum = weighted_values(vst_ref[0, :, pl.ds(k0, ck)], jnp.exp2(s - m_new))
        return m_new, alpha * l_prev + p_sum, alpha * acc + pv

    init = (jnp.full((1, g * tq), NEG_INF, f32), jnp.zeros((1, g * tq), f32),
            jnp.zeros((HEAD_DIM, g * tq), f32))
    last = (t0 + tq + ck - 1) // ck - 1
    carry = lax.fori_loop(0, last, lambda c, cr: sel_step(cr, pl.multiple_of(c * ck, ck), False), init)
    _, l_s, acc_s = sel_step(carry, pl.multiple_of(last * ck, ck), True)
    o_sel = acc_s * (1.0 / l_s)

    gt = gate_ref[0].T
    gate = lambda j: jax.nn.sigmoid(jnp.concatenate([gt[i * 3 + j:i * 3 + j + 1] for i in range(g)], axis=1))
    ot = gate(0) * o_cmp + gate(1) * o_sel + gate(2) * o_win
    o2 = jnp.concatenate([ot[:, i * tq:(i + 1) * tq] for i in range(g)], axis=0)
    o_ref[0] = o2.T.astype(o_ref.dtype)


def _nsa_attention(proj, cos_t, sin_t, qg, kc, vct, ks, vst, kw, vwt, gate_logits, tq=256, ck=512):
    b, s, _ = proj.shape
    hk, g, dh = NSA_KV_HEADS, NSA_GROUP, HEAD_DIM
    assert WINDOW >= tq and ck % tq == 0 and s % ck == 0
    tab = pl.BlockSpec((1, ROPE_DIM // 2, tq), lambda i, j, t: (i, 0, t))
    assert s // SEL_BLOCK <= kc.shape[3] - dh
    keys = lambda a: pl.BlockSpec((1, 1) + a.shape[2:], lambda i, j, t: (i, j, 0, 0))
    vals = lambda a: pl.BlockSpec((1, dh, a.shape[2]), lambda i, j, t: (i, j, 0))
    return pl.pallas_call(
        functools.partial(_nsa_kernel, tq=tq, ck=ck),
        grid=(b, hk, s // tq),
        in_specs=[pl.BlockSpec((1, tq, g * dh), lambda i, j, t: (i, t, j)),
                  tab, tab, pl.BlockSpec((dh, 1), lambda i, j, t: (0, 0)),
                  keys(kc), vals(vct), keys(ks), vals(vst), keys(kw), vals(vwt),
                  pl.BlockSpec((1, tq, LANES), lambda i, j, t: (i, t, j))],
        out_specs=pl.BlockSpec((1, tq, g * dh), lambda i, j, t: (i, t, j)),
        out_shape=jax.ShapeDtypeStruct((b, s, NSA_WIDTH), bf16),
        compiler_params=_cparams("parallel", "parallel", "arbitrary"),
        name="nsa_attention",
    )(proj, cos_t, sin_t, qg, kc, vct, ks, vst, kw, vwt, gate_logits)


def _gla_kernel(q_ref, k_ref, v_ref, lr_ref, wg_ref, bg_ref, r_ref, ng_ref, o_ref, *, grp):
    c = GLA_CHUNK
    s = q_ref.shape[1]
    dk, dv = GLA_KEY_DIM, GLA_VAL_DIM
    rows = grp * c
    ri = lax.broadcasted_iota(jnp.int32, (rows, rows), 0)
    ci = lax.broadcasted_iota(jnp.int32, (rows, rows), 1)
    causal = (ci <= ri) & (ci // c == ri // c)
    tri = jnp.where(causal, 1.0, 0.0).astype(bf16)
    lane = lax.broadcasted_iota(jnp.int32, (1, 2 * dk), 1)
    head_mask = [lane < dk, lane >= dk]
    wg = wg_ref[...]
    bg = bg_ref[...]
    ng = ng_ref[...]

    def body(n, states):
        r0 = pl.multiple_of(n * rows, rows)
        z = jnp.dot(lr_ref[0, pl.ds(r0, rows), :GLA_GATE_RANK], wg, preferred_element_type=f32,
                    precision=lax.Precision.HIGHEST) + bg
        log_a = -(jnp.maximum(-z, 0.0) + jnp.log(1.0 + jnp.exp(-jnp.abs(z)))) / GLA_TAU
        la_hi = log_a.astype(bf16)
        la_mid, la_lo = _split_bf16(log_a - la_hi.astype(f32))
        bcum = (jnp.dot(tri, la_hi, preferred_element_type=f32) + jnp.dot(tri, la_mid, preferred_element_type=f32) +
                jnp.dot(tri, la_lo, preferred_element_type=f32))
        qf = q_ref[0, pl.ds(r0, rows), :].astype(f32) * (dk ** -0.5)
        kf = k_ref[0, pl.ds(r0, rows), :].astype(f32)
        q_dec = qf * jnp.exp(bcum)
        k_dec = (kf * jnp.exp(-bcum)).astype(bf16)
        b_last = [bcum[(i + 1) * c - 1:(i + 1) * c, :] for i in range(grp)]
        k_state = jnp.concatenate(
            [kf[i * c:(i + 1) * c] * jnp.exp(b_last[i] - bcum[i * c:(i + 1) * c]) for i in range(grp)],
            axis=0).astype(bf16)
        new_states = []
        for h in range(2):
            q_h = jnp.where(head_mask[h], q_dec, 0.0).astype(bf16)
            v_h = v_ref[0, pl.ds(r0, rows), h * dv:(h + 1) * dv]
            attn = lax.dot_general(q_h, k_dec, _NT, preferred_element_type=f32)
            attn = jnp.where(causal, attn, 0.0).astype(bf16)
            o_intra = jnp.dot(attn, v_h, preferred_element_type=f32)
            st = states[h]
            o_inter = []
            for i in range(grp):
                o_inter.append(lax.dot_general(q_h[i * c:(i + 1) * c], st.astype(bf16), _NT,
                                               preferred_element_type=f32))
                u_t = lax.dot_general(v_h[i * c:(i + 1) * c], k_state[i * c:(i + 1) * c], _TN,
                                      preferred_element_type=f32)
                st = st * jnp.exp(b_last[i]) + u_t
            new_states.append(st)
            o = _rms(o_intra + jnp.concatenate(o_inter, axis=0), ng)
            rr = r_ref[0, pl.ds(r0, rows), h * dv:(h + 1) * dv].astype(f32)
            o_ref[0, pl.ds(r0, rows), h * dv:(h + 1) * dv] = (o * (rr * jax.nn.sigmoid(rr))).astype(o_ref.dtype)
        return tuple(new_states)

    zero = jnp.zeros((dv, 2 * dk), f32)
    lax.fori_loop(0, s // rows, body, (zero, zero))


def _gla(proj, proj_s, wg, bg, ng, grp=4):
    b, s, _ = proj.shape
    dk2, dv2 = 2 * GLA_KEY_DIM, 2 * GLA_VAL_DIM
    oq, ok_, ov, orr = (MAIN_OFFS[7] // dk2, MAIN_OFFS[8] // dk2, MAIN_OFFS[9] // dv2, MAIN_OFFS[10] // dv2)
    col = lambda w, o: pl.BlockSpec((1, s, w), lambda i, j: (i, 0, o + j))
    return pl.pallas_call(
        functools.partial(_gla_kernel, grp=grp),
        grid=(b, GLA_HEADS // 2),
        in_specs=[col(dk2, oq), col(dk2, ok_), col(dv2, ov),
                  pl.BlockSpec((1, s, LANES), lambda i, j: (i, 0, 2)),
                  pl.BlockSpec((GLA_GATE_RANK, dk2), lambda i, j: (0, j)),
                  pl.BlockSpec((1, dk2), lambda i, j: (0, j)),
                  col(dv2, orr),
                  pl.BlockSpec((1, GLA_VAL_DIM), lambda i, j: (0, 0))],
        out_specs=pl.BlockSpec((1, s, dv2), lambda i, j: (i, 0, j)),
        out_shape=jax.ShapeDtypeStruct((b, s, GLA_WIDTH), bf16),
        compiler_params=_cparams("parallel", "parallel"),
        name="gla",
    )(proj, proj, proj, proj_s, wg, bg, proj, ng)


def _outproj_kernel(a_ref, b_ref, x_ref, wo_ref, g2_ref, wrh_ref, wrl_ref, br_ref,
                    x1_ref, h2_ref, rt_ref, cnt_ref, carry_ref):
    @pl.when(pl.program_id(0) == 0)
    def _():
        carry_ref[...] = jnp.zeros_like(carry_ref)

    na = a_ref.shape[1]
    tm = a_ref.shape[0]
    y = (jnp.dot(a_ref[...], wo_ref[:na, :], preferred_element_type=f32) +
         jnp.dot(b_ref[...], wo_ref[na:, :], preferred_element_type=f32))
    x1 = x_ref[...] + y
    x1_ref[...] = x1
    h2 = _rms(x1, g2_ref[...])
    h2_ref[...] = h2.astype(bf16)
    hi, lo = _split_bf16(h2)
    logits = (jnp.dot(hi, wrh_ref[...], preferred_element_type=f32) +
              jnp.dot(lo, wrh_ref[...], preferred_element_type=f32) +
              jnp.dot(hi, wrl_ref[...], preferred_element_type=f32)) + br_ref[...]
    lane = lax.broadcasted_iota(jnp.int32, (1, LANES), 1)
    score = jnp.where(lane < N_EXPERTS, logits, -jnp.inf)
    top = jnp.max(score, axis=-1, keepdims=True)
    hits, firsts, weights = [], [], []
    for _ in range(TOP_K):
        m = jnp.max(score, axis=-1, keepdims=True)
        first = jnp.min(jnp.where(score == m, lane, LANES), axis=-1, keepdims=True)
        hit = lane == first
        hits.append(hit)
        firsts.append(first.astype(f32))
        weights.append(jnp.exp(m - top))
        score = jnp.where(hit, -jnp.inf, score)
    tot = weights[0] + weights[1] + weights[2] + weights[3]
    onehot = jnp.where(hits[0] | hits[1] | hits[2] | hits[3], 1.0, 0.0)
    ri = lax.broadcasted_iota(jnp.int32, (tm, tm), 0)
    ci = lax.broadcasted_iota(jnp.int32, (tm, tm), 1)
    before = jnp.where(ci < ri, 1.0, 0.0).astype(bf16)
    rank = carry_ref[...] + jnp.dot(before, onehot.astype(bf16), preferred_element_type=f32)
    carry_ref[...] = carry_ref[...] + jnp.sum(onehot, axis=0, keepdims=True)
    cnt_ref[...] = carry_ref[...]
    packed = jnp.zeros((tm, LANES), f32)
    for k in range(TOP_K):
        rank_k = jnp.sum(jnp.where(hits[k], rank, 0.0), axis=-1, keepdims=True)
        packed = jnp.where(lane == k, firsts[k], packed)
        packed = jnp.where(lane == TOP_K + k, weights[k] / tot, packed)
        packed = jnp.where(lane == 2 * TOP_K + k, rank_k, packed)
    rt_ref[...] = packed


def _outproj_router(a, b, x2, wo, g2, wr_hi, wr_lo, br, tm=512):
    m, d = x2.shape
    na, nb = a.shape[1], b.shape[1]
    row = lambda n: pl.BlockSpec((tm, n), lambda i: (i, 0))
    whole = lambda t: pl.BlockSpec(t.shape, lambda i: (0,) * t.ndim)
    return pl.pallas_call(
        _outproj_kernel,
        grid=(m // tm,),
        in_specs=[row(na), row(nb), row(d), whole(wo), whole(g2), whole(wr_hi), whole(wr_lo), whole(br)],
        out_specs=[row(d), row(d), row(LANES), pl.BlockSpec((1, LANES), lambda i: (0, 0))],
        out_shape=[jax.ShapeDtypeStruct((m, d), f32),
                   jax.ShapeDtypeStruct((m, d), bf16),
                   jax.ShapeDtypeStruct((m, LANES), f32),
                   jax.ShapeDtypeStruct((1, LANES), f32)],
        scratch_shapes=[pltpu.VMEM((1, LANES), f32)],
        compiler_params=_cparams("arbitrary"),
        name="outproj_router",
    )(a, b, x2, wo, g2, wr_hi, wr_lo, br)


def _moe_kernel(vb_ref, ve_ref, lo_ref, hi_ref, nv_ref, xs_ref, wgu_ref, bgu_ref, wd_ref, bd_ref,
                o_ref, wgu_bf, wd_bf):
    v = pl.program_id(0)
    prev = jnp.maximum(v - 1, 0)
    live = v < nv_ref[0]
    new_expert = (v == 0) | (ve_ref[v] != ve_ref[prev])
    new_block = (v == 0) | (vb_ref[v] != vb_ref[prev])

    @pl.when(live & new_expert)
    def _():
        wgu_bf[...] = wgu_ref[0].astype(bf16)
        wd_bf[...] = wd_ref[0].astype(bf16)

    @pl.when(live & new_block)
    def _():
        o_ref[...] = jnp.zeros_like(o_ref)

    @pl.when(live)
    def _():
        blk0 = vb_ref[v] * MOE_BLOCK
        lo, hi = lo_ref[v], hi_ref[v]
        first = jnp.maximum(lo - blk0, 0) // MOE_SUB
        last = (jnp.minimum(hi - blk0, MOE_BLOCK) + MOE_SUB - 1) // MOE_SUB

        def sub_block(j, carry):
            r0 = pl.multiple_of(j * MOE_SUB, MOE_SUB)
            h = jnp.dot(xs_ref[pl.ds(r0, MOE_SUB), :], wgu_bf[...], preferred_element_type=f32) + bgu_ref[0]
            x_glu = jnp.minimum(h[:, :D_FF], SWIGLU_LIMIT)
            x_lin = jnp.clip(h[:, D_FF:], -SWIGLU_LIMIT, SWIGLU_LIMIT)
            act = x_glu * jax.nn.sigmoid(SWIGLU_ALPHA * x_glu) * (x_lin + 1.0)
            y = jnp.dot(act.astype(bf16), wd_bf[...], preferred_element_type=f32) + bd_ref[0]
            row = blk0 + r0 + lax.broadcasted_iota(jnp.int32, (MOE_SUB, 1), 0)
            mine = (row >= lo) & (row < hi)
            o_ref[pl.ds(r0, MOE_SUB), :] = jnp.where(mine, y.astype(o_ref.dtype), o_ref[pl.ds(r0, MOE_SUB), :])
            return carry

        lax.fori_loop(first, last, sub_block, 0)


def _moe_experts(vb, ve, lo, hi, nv, xs, wgu, bgu, wd, bd):
    a, d = xs.shape
    nvis = vb.shape[0]
    return pl.pallas_call(
        _moe_kernel,
        grid_spec=pltpu.PrefetchScalarGridSpec(
            num_scalar_prefetch=5,
            grid=(nvis,),
            in_specs=[pl.BlockSpec((MOE_BLOCK, d), lambda v, vb, ve, *_: (vb[v], 0)),
                      pl.BlockSpec((1, d, 2 * D_FF), lambda v, vb, ve, *_: (ve[v], 0, 0)),
                      pl.BlockSpec((1, 1, 2 * D_FF), lambda v, vb, ve, *_: (ve[v], 0, 0)),
                      pl.BlockSpec((1, D_FF, d), lambda v, vb, ve, *_: (ve[v], 0, 0)),
                      pl.BlockSpec((1, 1, d), lambda v, vb, ve, *_: (ve[v], 0, 0))],
            out_specs=pl.BlockSpec((MOE_BLOCK, d), lambda v, vb, ve, *_: (vb[v], 0)),
            scratch_shapes=[pltpu.VMEM((d, 2 * D_FF), bf16), pltpu.VMEM((D_FF, d), bf16)],
        ),
        out_shape=jax.ShapeDtypeStruct((a, d), bf16),
        compiler_params=_cparams("arbitrary"),
        name="moe_experts",
    )(vb, ve, lo, hi, nv, xs, wgu, bgu, wd, bd)


def _visit_table(counts, n_rows):
    end = jnp.cumsum(counts)
    start = end - counts
    nblk = n_rows // MOE_BLOCK
    first_blk = start // MOE_BLOCK
    last_blk = jnp.where(counts > 0, (end - 1) // MOE_BLOCK, first_blk - 1)
    per_e = jnp.maximum(last_blk - first_blk + 1, 0)
    v_end = jnp.cumsum(per_e)
    v_start = v_end - per_e
    nvis = nblk + N_EXPERTS - 1
    v = jnp.arange(nvis, dtype=jnp.int32)
    n_live = v_end[-1]
    vc = jnp.minimum(v, n_live - 1)
    ve = jnp.sum((v_end[None, :] <= vc[:, None]).astype(jnp.int32), axis=1)
    vb = (first_blk[ve] + vc - v_start[ve]).astype(jnp.int32)
    return vb, ve, start[ve].astype(jnp.int32), end[ve].astype(jnp.int32), n_live.astype(jnp.int32).reshape(1), start


def _rope_tables(positions):
    half = ROPE_DIM // 2
    inv_freq = jnp.exp(-math.log(ROPE_THETA) * jnp.arange(0, ROPE_DIM, 2, dtype=f32) / ROPE_DIM)
    ang = positions.astype(f32)[..., None] * inv_freq
    cos, sin = jnp.cos(ang), jnp.sin(ang)
    rest = HEAD_DIM - ROPE_DIM
    b, s = positions.shape
    cos_f = jnp.concatenate([cos, cos, jnp.ones((b, s, rest), f32)], axis=-1)
    sin_f = jnp.concatenate([-sin, sin, jnp.zeros((b, s, rest), f32)], axis=-1)
    perm = jnp.zeros((HEAD_DIM, HEAD_DIM), f32)
    idx = jnp.arange(half)
    perm = perm.at[idx + half, idx].set(1.0).at[idx, idx + half].set(1.0)
    return cos_f, sin_f, perm.astype(bf16), cos.transpose(0, 2, 1), sin.transpose(0, 2, 1)


def _layer(x, positions, norm1_g, w_in, q_norm_g, k_norm_g, cmp_pos_k, cmp_w1_k, cmp_w2_k,
           cmp_pos_v, cmp_w1_v, cmp_w2_v, gla_w_gate, gla_b_gate, gla_norm_g, w_out, norm2_g,
           w_router, b_router, w_gate_up, b_gate_up, w_down, b_down):
    b, s, d = x.shape
    m = b * s
    hk, g, dh = NSA_KV_HEADS, NSA_GROUP, HEAD_DIM
    x2 = x.reshape(m, d)

    offs = [0]
    for sz in IN_SIZES:
        offs.append(offs[-1] + sz)
    seg = lambda i: w_in[:, offs[i]:offs[i + 1]]
    w_main = jnp.concatenate([seg(i) for i in (0, 1, 2, 3, 4, 5, 6, 8, 9, 10, 12)], axis=1).astype(bf16)
    padw = lambda t: jnp.concatenate([t, jnp.zeros((d, LANES - t.shape[1]), f32)], axis=1)
    ng = g * 3
    w_small = jnp.concatenate([padw(seg(7)[:, :ng]), padw(seg(7)[:, ng:]), padw(seg(11))], axis=1).astype(bf16)
    proj, proj_s = _inproj(x2, norm1_g.reshape(1, d), w_main, w_small)
    proj = proj.reshape(b, s, -1)
    proj_s = proj_s.reshape(b, s, 3 * LANES)

    cos_f, sin_f, perm, cos_t, sin_t = _rope_tables(positions)
    per_kv = lambda t: jnp.concatenate([t] * hk, axis=-1)
    eye = jnp.eye(hk, dtype=f32)
    bdiag = lambda t: jnp.kron(eye, t)
    w1_bd = lambda w1: jax.vmap(bdiag)(w1.reshape(CMP_BLOCK, dh, CMP_HIDDEN)).astype(bf16)
    kc, vct, ks_rot, vst, kw_rot, vwt = _kvprep(
        proj, per_kv(cos_f), per_kv(sin_f), per_kv(k_norm_g), bdiag(jnp.ones((dh, dh), f32)).astype(bf16),
        bdiag(perm.astype(f32)).astype(bf16),
        jnp.stack([jnp.eye(hk * dh, LANES, k=-h * dh, dtype=f32) * (jnp.arange(LANES) < dh)
                   for h in range(hk)]).astype(bf16),
        per_kv(cmp_pos_k), w1_bd(cmp_w1_k), bdiag(cmp_w2_k).astype(bf16),
        per_kv(cmp_pos_v), w1_bd(cmp_w1_v), bdiag(cmp_w2_v).astype(bf16))
    nsa_out = _nsa_attention(proj, cos_t, sin_t, q_norm_g.reshape(dh, 1), kc, vct,
                             ks_rot, vst, kw_rot, vwt, proj_s).reshape(m, NSA_WIDTH)

    gla_out = _gla(proj, proj_s, gla_w_gate, gla_b_gate.reshape(1, -1),
                   gla_norm_g.reshape(1, GLA_VAL_DIM)).reshape(m, GLA_WIDTH)

    padr = lambda t: jnp.concatenate([t, jnp.zeros(t.shape[:-1] + (LANES - t.shape[-1],), t.dtype)], axis=-1)
    wr_hi = w_router.astype(bf16)
    wr_lo = (w_router - wr_hi.astype(f32)).astype(bf16)
    x1, h2, rt, cnt = _outproj_router(nsa_out, gla_out, x2, w_out.astype(bf16), norm2_g.reshape(1, d),
                                      padr(wr_hi), padr(wr_lo), padr(b_router.reshape(1, -1)))

    top_idx = rt[:, :TOP_K].astype(jnp.int32)
    gate = rt[:, TOP_K:2 * TOP_K]
    rank = rt[:, 2 * TOP_K:3 * TOP_K].astype(jnp.int32)
    counts = cnt[0, :N_EXPERTS].astype(jnp.int32)
    a = m * TOP_K
    vb, ve, lo, hi, nv, start = _visit_table(counts, a)
    onehot = top_idx[:, :, None] == jnp.arange(N_EXPERTS, dtype=jnp.int32)
    pos = jnp.sum(jnp.where(onehot, start, 0), axis=-1) + rank
    sorted_tok = (jnp.argsort(top_idx.reshape(-1)) // TOP_K).astype(jnp.int32)
    rows_of = lambda t, idx: t.at[idx].get(mode='promise_in_bounds')
    xs = rows_of(h2, sorted_tok)
    ys = _moe_experts(vb, ve, lo, hi, nv, xs, w_gate_up, b_gate_up.reshape(N_EXPERTS, 1, -1),
                      w_down, b_down.reshape(N_EXPERTS, 1, -1))
    out = x1
    for k in range(TOP_K):
        out = out + rows_of(ys, pos[:, k]).astype(f32) * gate[:, k:k + 1]
    return out.reshape(b, s, d)


def kernel(x, positions, norm1_g, w_in, nsa_q_norm_g, nsa_k_norm_g, cmp_pos_k, cmp_w1_k, cmp_w2_k,
           cmp_pos_v, cmp_w1_v, cmp_w2_v, gla_w_gate, gla_b_gate, gla_norm_g, w_out, norm2_g,
           w_router, b_router, w_gate_up, b_gate_up, w_down, b_down):
    for l in range(norm1_g.shape[0]):
        x = _layer(x, positions, norm1_g[l], w_in[l], nsa_q_norm_g[l], nsa_k_norm_g[l],
                   cmp_pos_k[l], cmp_w1_k[l], cmp_w2_k[l], cmp_pos_v[l], cmp_w1_v[l], cmp_w2_v[l],
                   gla_w_gate[l], gla_b_gate[l], gla_norm_g[l], w_out[l], norm2_g[l],
                   w_router[l], b_router[l], w_gate_up[l], b_gate_up[l], w_down[l], b_down[l])
    return x
```

```python
import functools
import math

import jax
import jax.numpy as jnp
from jax import lax
from jax.experimental import pallas as pl
from jax.experimental.pallas import tpu as pltpu

f32 = jnp.float32
bf16 = jnp.bfloat16

NSA_HEADS = 8
NSA_KV_HEADS = 2
NSA_GROUP = NSA_HEADS // NSA_KV_HEADS
HEAD_DIM = 64
CMP_BLOCK = 32
CMP_STRIDE = 16
CMP_HIDDEN = 256
SEL_BLOCK = 64
SEL_TOPK = 8
WINDOW = 512
FORCE_BONUS = 1e4
GLA_HEADS = 4
GLA_KEY_DIM = 64
GLA_VAL_DIM = 128
GLA_CHUNK = 64
GLA_GATE_RANK = 16
GLA_TAU = 16.0
ROPE_THETA = 500000.0
ROPE_DIM = HEAD_DIM // 4
N_EXPERTS = 32
TOP_K = 4
D_FF = 1024
SWIGLU_LIMIT = 7.0
SWIGLU_ALPHA = 1.702
EPS = 1e-6
NEG_INF = -1e30

NSA_WIDTH = NSA_HEADS * HEAD_DIM
NSA_KV_WIDTH = NSA_KV_HEADS * HEAD_DIM
GLA_KEY_WIDTH = GLA_HEADS * GLA_KEY_DIM
GLA_WIDTH = GLA_HEADS * GLA_VAL_DIM
IN_SIZES = (NSA_WIDTH,) + (NSA_KV_WIDTH,) * 6 + (
    NSA_HEADS * 3, GLA_KEY_WIDTH, GLA_KEY_WIDTH, GLA_WIDTH, GLA_GATE_RANK, GLA_WIDTH)
MAIN_SIZES = (NSA_WIDTH,) + (NSA_KV_WIDTH,) * 6 + (GLA_KEY_WIDTH, GLA_KEY_WIDTH, GLA_WIDTH, GLA_WIDTH)
MAIN_OFFS = tuple(sum(MAIN_SIZES[:i]) for i in range(len(MAIN_SIZES) + 1))

LANES = 128
VMEM_LIMIT = 48 * 1024 * 1024
MOE_BLOCK = 1024
MOE_SUB = 256
ROUTE_ROWS = 16

_NT = (((1,), (1,)), ((), ()))
_TN = (((0,), (0,)), ((), ()))


def _cparams(*sem):
    return pltpu.CompilerParams(dimension_semantics=sem, vmem_limit_bytes=VMEM_LIMIT)


def _rms(t, g):
    return t * lax.rsqrt(jnp.mean(t * t, axis=-1, keepdims=True) + EPS) * g


def _split_bf16(t):
    hi = t.astype(bf16)
    lo = (t - hi.astype(f32)).astype(bf16)
    return hi, lo


def _rope(t, cos_f, sin_f, perm):
    hi, lo = _split_bf16(t)
    rot = (jnp.dot(hi, perm, preferred_element_type=f32) +
           jnp.dot(lo, perm, preferred_element_type=f32))
    return t * cos_f + rot * sin_f


def _inproj_kernel(x_ref, g_ref, w_ref, ws_ref, o_ref, os_ref):
    x = x_ref[...]
    h = _rms(x, g_ref[...]).astype(bf16)
    n = o_ref.shape[1]
    step = 512
    for c in range(0, n, step):
        e = min(c + step, n)
        o_ref[:, c:e] = jnp.dot(h, w_ref[:, c:e], preferred_element_type=f32).astype(bf16)
    os_ref[...] = jnp.dot(h, ws_ref[...], preferred_element_type=f32)


def _inproj(x2, g, w_main, w_small, tm=512):
    m, d = x2.shape
    n = w_main.shape[1]
    ns = w_small.shape[1]
    return pl.pallas_call(
        _inproj_kernel,
        grid=(m // tm,),
        in_specs=[pl.BlockSpec((tm, d), lambda i: (i, 0)),
                  pl.BlockSpec((1, d), lambda i: (0, 0)),
                  pl.BlockSpec((d, n), lambda i: (0, 0)),
                  pl.BlockSpec((d, ns), lambda i: (0, 0))],
        out_specs=[pl.BlockSpec((tm, n), lambda i: (i, 0)),
                   pl.BlockSpec((tm, ns), lambda i: (i, 0))],
        out_shape=[jax.ShapeDtypeStruct((m, n), bf16),
                   jax.ShapeDtypeStruct((m, ns), f32)],
        compiler_params=_cparams("parallel"),
        name="inproj",
    )(x2, g, w_main, w_small)


def _kvprep_kernel(kc_ref, vc_ref, ks_ref, vs_ref, kw_ref, vw_ref, cs_ref, spread_ref, kg_ref, ones_ref,
                   perm_ref, pick_ref, posk_ref, w1k_ref, w2k_ref, posv_ref, w1v_ref, w2v_ref,
                   kco_ref, vcto_ref, kso_ref, vsto_ref, kwo_ref, vwto_ref, tmp_ref):
    ones_bd = ones_ref[...]
    nh_out = kco_ref.shape[2]

    def rms_heads(t, g):
        sq_hi, sq_lo = _split_bf16(t * t)
        ss = (jnp.dot(sq_hi, ones_bd, preferred_element_type=f32) +
              jnp.dot(sq_lo, ones_bd, preferred_element_type=f32))
        return t * lax.rsqrt(ss * (1.0 / HEAD_DIM) + EPS) * g

    def compress(src_ref, pos_ref, w1_ref, w2_ref):
        tmp_ref[...] = src_ref[0].astype(f32)
        nh = tmp_ref.shape[0] // CMP_STRIDE
        a = jnp.zeros((nh, NSA_KV_HEADS * CMP_HIDDEN), f32)
        b = jnp.zeros((nh, NSA_KV_HEADS * CMP_HIDDEN), f32)
        for p in range(CMP_STRIDE):
            rows = tmp_ref[pl.ds(p, nh, stride=CMP_STRIDE), :]
            a = a + jnp.dot((rows + pos_ref[p:p + 1, :]).astype(bf16), w1_ref[p], preferred_element_type=f32)
            q = CMP_STRIDE + p
            b = b + jnp.dot((rows + pos_ref[q:q + 1, :]).astype(bf16), w1_ref[q], preferred_element_type=f32)
        pre = a + pltpu.roll(b, nh - 1, 0)
        hid = pre * jax.nn.sigmoid(pre)
        return jnp.dot(hid.astype(bf16), w2_ref[...], preferred_element_type=f32)

    kc = rms_heads(compress(kc_ref, posk_ref, w1k_ref, w2k_ref), kg_ref[0:1, :]).astype(bf16)
    tmp_ref[:nh_out, :] = compress(vc_ref, posv_ref, w1v_ref, w2v_ref)
    vcto_ref[0] = tmp_ref[:nh_out, :].T.astype(bf16)
    s, w = ks_ref.shape[1], ks_ref.shape[2]
    cs = jnp.concatenate([cs_ref[0], jnp.zeros((LANES - cs_ref.shape[1], s), f32)], axis=0).T
    cs_hi = cs.astype(bf16)
    cs_mid, cs_lo = _split_bf16(cs - cs_hi.astype(f32))
    spread = spread_ref[...]
    tables = (jnp.dot(cs_hi, spread, preferred_element_type=f32) + jnp.dot(cs_mid, spread, preferred_element_type=f32) +
              jnp.dot(cs_lo, spread, preferred_element_type=f32))
    lane_row = lax.broadcasted_iota(jnp.int32, (1, LANES), 1)
    cos_f = tables[:, :LANES] + jnp.where(lane_row % HEAD_DIM >= ROPE_DIM, 1.0, 0.0)
    sin_f = tables[:, LANES:]
    perm = perm_ref[...]
    ks = _rope(rms_heads(ks_ref[0].astype(f32), kg_ref[1:2, :]), cos_f, sin_f, perm).astype(bf16)
    kw = _rope(rms_heads(kw_ref[0].astype(f32), kg_ref[2:3, :]), cos_f, sin_f, perm).astype(bf16)
    row = lax.broadcasted_iota(jnp.int32, (s, LANES), 0)
    lane = lax.broadcasted_iota(jnp.int32, (s, LANES), 1)
    block_onehot = jnp.where(row // SEL_BLOCK + HEAD_DIM == lane, 1.0, 0.0)
    pad_flag = jnp.where(lax.broadcasted_iota(jnp.int32, (WINDOW, LANES), 1) == HEAD_DIM, 1.0, 0.0)
    for h in range(NSA_KV_HEADS):
        pick = pick_ref[h]
        kco_ref[0, h] = jnp.dot(kc, pick, preferred_element_type=f32).astype(bf16)
        kso_ref[0, h] = (jnp.dot(ks, pick, preferred_element_type=f32) + block_onehot).astype(bf16)
        kwo_ref[0, h, :WINDOW, :] = pad_flag.astype(bf16)
        kwo_ref[0, h, WINDOW:, :] = jnp.dot(kw, pick, preferred_element_type=f32).astype(bf16)
    vsto_ref[0] = vs_ref[0].astype(f32).T.astype(bf16)
    vwto_ref[0] = jnp.concatenate([jnp.zeros((w, WINDOW), f32), vw_ref[0].astype(f32).T], axis=1).astype(bf16)


def _kvprep(proj, cs_t, spread, kg2, ones_bd, perm2, pick, posk, w1k, w2k, posv, w1v, w2v):
    b, s, _ = proj.shape
    w = NSA_KV_WIDTH
    assert w == LANES
    nh = s // CMP_STRIDE
    col = lambda i: pl.BlockSpec((1, s, w), lambda n: (n, 0, MAIN_OFFS[i] // w))
    whole = lambda a: pl.BlockSpec(a.shape, lambda n: (0,) * a.ndim)
    tab = pl.BlockSpec((1, cs_t.shape[1], s), lambda n: (n, 0, 0))
    out = lambda *shp: (pl.BlockSpec((1,) + shp, lambda n: (n,) + (0,) * len(shp)),
                        jax.ShapeDtypeStruct((b,) + shp, bf16))
    hk = NSA_KV_HEADS
    outs = [out(hk, nh, LANES), out(w, nh), out(hk, s, LANES), out(w, s), out(hk, WINDOW + s, LANES),
            out(w, WINDOW + s)]
    return pl.pallas_call(
        _kvprep_kernel,
        grid=(b,),
        in_specs=[col(1), col(2), col(3), col(4), col(5), col(6), tab, whole(spread), whole(kg2), whole(ones_bd),
                  whole(perm2), whole(pick), whole(posk), whole(w1k), whole(w2k), whole(posv), whole(w1v),
                  whole(w2v)],
        out_specs=[o[0] for o in outs],
        out_shape=[o[1] for o in outs],
        scratch_shapes=[pltpu.VMEM((s, w), f32)],
        compiler_params=_cparams("parallel"),
        name="nsa_kvprep",
    )(proj, proj, proj, proj, proj, proj, cs_t, spread, kg2, ones_bd, perm2, pick, posk, w1k, w2k, posv, w1v, w2v)


def _nsa_kernel(q_ref, cos_ref, sin_ref, qg_ref, kc_ref, vct_ref, ks_ref, vst_ref, kw_ref, vwt_ref,
                gate_ref, o_ref, *, tq, ck):
    g = NSA_GROUP
    qi = pl.program_id(2)
    t0 = pl.multiple_of(qi * tq, tq)
    scale = HEAD_DIM ** -0.5 * math.log2(math.e)
    per_head = lambda t: jnp.concatenate([t] * g, axis=1)

    qt = q_ref[0].astype(f32).T
    qt = jnp.concatenate([qt[i * HEAD_DIM:(i + 1) * HEAD_DIM] for i in range(g)], axis=1)
    qn = qt * lax.rsqrt(jnp.mean(qt * qt, axis=0, keepdims=True) + EPS) * qg_ref[...]
    half = ROPE_DIM // 2
    cos8, sin8 = per_head(cos_ref[0]), per_head(sin_ref[0])
    x1, x2 = qn[:half], qn[half:ROPE_DIM]
    q_rot = jnp.concatenate([x1 * cos8 - x2 * sin8, x2 * cos8 + x1 * sin8, qn[ROPE_DIM:]], axis=0)
    n_extra = kc_ref.shape[3] - HEAD_DIM
    q_cmp = jnp.concatenate([(qn * scale).astype(bf16), jnp.zeros((n_extra, g * tq), bf16)], axis=0)
    q_rot = (q_rot * scale).astype(bf16)
    tq_row = t0 + lax.broadcasted_iota(jnp.int32, (1, tq), 1)

    kc = kc_ref[0, 0]
    ncp = kc.shape[0]
    n_col = lax.broadcasted_iota(jnp.int32, (ncp, 1), 0)
    ok_c = (n_col * CMP_STRIDE + (CMP_BLOCK - 1)) <= tq_row
    s_c = jnp.dot(kc, q_cmp, preferred_element_type=f32) + per_head(jnp.where(ok_c, 0.0, NEG_INF))
    m_c = jnp.max(s_c, axis=0, keepdims=True)
    p_c = jnp.exp2(s_c - m_c) * per_head(jnp.where(ok_c, 1.0, 0.0))
    den = jnp.sum(p_c, axis=0, keepdims=True)
    p_c = p_c * (1.0 / jnp.where(den > 0, den, 1.0))
    o_cmp = jnp.dot(vct_ref[0], p_c.astype(bf16), preferred_element_type=f32)

    pg = p_c[:, :tq]
    for i in range(1, g):
        pg = pg + p_c[:, i * tq:(i + 1) * tq]
    nj = ks_ref.shape[2] // SEL_BLOCK
    jj = lax.broadcasted_iota(jnp.int32, (nj, ncp), 0)
    nn = lax.broadcasted_iota(jnp.int32, (nj, ncp), 1)
    overlap = (nn * CMP_STRIDE < (jj + 1) * SEL_BLOCK) & (nn * CMP_STRIDE + CMP_BLOCK > jj * SEL_BLOCK)
    overlap = jnp.where(overlap, 1.0, 0.0).astype(bf16)
    pg_hi, pg_lo = _split_bf16(pg)
    imp = (jnp.dot(overlap, pg_hi, preferred_element_type=f32) +
           jnp.dot(overlap, pg_lo, preferred_element_type=f32))
    j_col = lax.broadcasted_iota(jnp.int32, (nj, 1), 0)
    j_f = j_col.astype(f32)
    cur = tq_row // SEL_BLOCK
    valid = j_col <= cur
    forced = (j_col == 0) | (j_col == cur) | (j_col == cur - 1)
    score = jnp.where(valid, imp + jnp.where(forced, FORCE_BONUS, 0.0), NEG_INF)
    sel = jnp.zeros((nj, tq), f32)
    for _ in range(SEL_TOPK):
        m = jnp.max(score, axis=0, keepdims=True)
        first = jnp.min(jnp.where(score == m, j_f, float(nj)), axis=0, keepdims=True)
        hit = j_f == first
        sel = jnp.where(hit, 1.0, sel)
        score = jnp.where(hit, -jnp.inf, score)
    sel = jnp.where(valid, sel, 0.0)

    def weighted_values(vt, p):
        lhs = jnp.concatenate([vt, jnp.ones((16, vt.shape[1]), bf16)], axis=0)
        r = jnp.dot(lhs, p.astype(bf16), preferred_element_type=f32)
        return r[:HEAD_DIM], r[HEAD_DIM:HEAD_DIM + 1]

    wl = WINDOW + tq
    flag_row = jnp.where(lax.broadcasted_iota(jnp.int32, (n_extra, 1), 0) == 0, NEG_INF, 0.0)
    q_win = jnp.concatenate([q_rot, jnp.broadcast_to(flag_row, (n_extra, g * tq)).astype(bf16)], axis=0)
    s_w = jnp.dot(kw_ref[0, 0, pl.ds(t0, wl), :], q_win, preferred_element_type=f32)
    step = lax.broadcasted_iota(jnp.int32, (tq, 1), 0)
    lo_ok = (t0 - WINDOW + step) > (tq_row - WINDOW)
    hi_ok = (t0 + step) <= tq_row
    parts = [s_w[:tq] + per_head(jnp.where(lo_ok, 0.0, NEG_INF))]
    if wl > 2 * tq:
        parts.append(s_w[tq:wl - tq])
    parts.append(s_w[wl - tq:] + per_head(jnp.where(hi_ok, 0.0, NEG_INF)))
    m_w = functools.reduce(jnp.maximum, [jnp.max(t, axis=0, keepdims=True) for t in parts])
    p_w = jnp.concatenate([jnp.exp2(t - m_w) for t in parts], axis=0)
    o_win, l_w = weighted_values(vwt_ref[0, :, pl.ds(t0, wl)], p_w)
    o_win = o_win * (1.0 / l_w)

    sel_bias = per_head(jnp.where(sel > 0.5, 0.0, NEG_INF)).astype(bf16)
    q_sel = jnp.concatenate([q_rot, sel_bias, jnp.zeros((n_extra - nj, g * tq), bf16)], axis=0)

    def sel_step(carry, k0, diagonal):
        m_prev, l_prev, acc = carry
        s = jnp.dot(ks_ref[0, 0, pl.ds(k0, ck), :], q_sel, preferred_element_type=f32)
        if diagonal:
            kpos = k0 + lax.broadcasted_iota(jnp.int32, (ck, 1), 0)
            s = s + per_head(jnp.where(kpos <= tq_row, 0.0, NEG_INF))
        m_new = jnp.maximum(m_prev, jnp.max(s, axis=0, keepdims=True))
        alpha = jnp.exp2(m_prev - m_new)
        pv, p_sum = weighted_values(vst_ref[0, :, pl.ds(k0, ck)], jnp.exp2(s - m_new))
        return m_new, alpha * l_prev + p_sum, alpha * acc + pv

    init = (jnp.full((1, g * tq), NEG_INF, f32), jnp.zeros((1, g * tq), f32),
            jnp.zeros((HEAD_DIM, g * tq), f32))
    last = (t0 + tq + ck - 1) // ck - 1
    carry = lax.fori_loop(0, last, lambda c, cr: sel_step(cr, pl.multiple_of(c * ck, ck), False), init)
    _, l_s, acc_s = sel_step(carry, pl.multiple_of(last * ck, ck), True)
    o_sel = acc_s * (1.0 / l_s)

    gt = gate_ref[0].T
    gate = lambda j: jax.nn.sigmoid(jnp.concatenate([gt[i * 3 + j:i * 3 + j + 1] for i in range(g)], axis=1))
    ot = gate(0) * o_cmp + gate(1) * o_sel + gate(2) * o_win
    o2 = jnp.concatenate([ot[:, i * tq:(i + 1) * tq] for i in range(g)], axis=0)
    o_ref[0] = o2.T.astype(o_ref.dtype)


def _nsa_attention(proj, cos_t, sin_t, qg, kc, vct, ks, vst, kw, vwt, gate_logits, tq=256, ck=512):
    b, s, _ = proj.shape
    hk, g, dh = NSA_KV_HEADS, NSA_GROUP, HEAD_DIM
    assert WINDOW >= tq and ck % tq == 0 and s % ck == 0
    tab = pl.BlockSpec((1, ROPE_DIM // 2, tq), lambda i, j, t: (i, 0, t))
    assert s // SEL_BLOCK <= kc.shape[3] - dh
    keys = lambda a: pl.BlockSpec((1, 1) + a.shape[2:], lambda i, j, t: (i, j, 0, 0))
    vals = lambda a: pl.BlockSpec((1, dh, a.shape[2]), lambda i, j, t: (i, j, 0))
    return pl.pallas_call(
        functools.partial(_nsa_kernel, tq=tq, ck=ck),
        grid=(b, hk, s // tq),
        in_specs=[pl.BlockSpec((1, tq, g * dh), lambda i, j, t: (i, t, j)),
                  tab, tab, pl.BlockSpec((dh, 1), lambda i, j, t: (0, 0)),
                  keys(kc), vals(vct), keys(ks), vals(vst), keys(kw), vals(vwt),
                  pl.BlockSpec((1, tq, LANES), lambda i, j, t: (i, t, j))],
        out_specs=pl.BlockSpec((1, tq, g * dh), lambda i, j, t: (i, t, j)),
        out_shape=jax.ShapeDtypeStruct((b, s, NSA_WIDTH), bf16),
        compiler_params=_cparams("parallel", "parallel", "arbitrary"),
        name="nsa_attention",
    )(proj, cos_t, sin_t, qg, kc, vct, ks, vst, kw, vwt, gate_logits)


def _gla_kernel(q_ref, k_ref, v_ref, lr_ref, wg_ref, bg_ref, r_ref, ng_ref, o_ref, *, grp):
    c = GLA_CHUNK
    s = q_ref.shape[1]
    dk, dv = GLA_KEY_DIM, GLA_VAL_DIM
    rows = grp * c
    ri = lax.broadcasted_iota(jnp.int32, (rows, rows), 0)
    ci = lax.broadcasted_iota(jnp.int32, (rows, rows), 1)
    causal = (ci <= ri) & (ci // c == ri // c)
    tri = jnp.where(causal, 1.0, 0.0).astype(bf16)
    lane = lax.broadcasted_iota(jnp.int32, (1, 2 * dk), 1)
    head_mask = [lane < dk, lane >= dk]
    wg = wg_ref[...]
    bg = bg_ref[...]
    ng = ng_ref[...]

    def body(n, states):
        r0 = pl.multiple_of(n * rows, rows)
        z = jnp.dot(lr_ref[0, pl.ds(r0, rows), :GLA_GATE_RANK], wg, preferred_element_type=f32,
                    precision=lax.Precision.HIGHEST) + bg
        log_a = -(jnp.maximum(-z, 0.0) + jnp.log(1.0 + jnp.exp(-jnp.abs(z)))) / GLA_TAU
        la_hi = log_a.astype(bf16)
        la_mid, la_lo = _split_bf16(log_a - la_hi.astype(f32))
        bcum = (jnp.dot(tri, la_hi, preferred_element_type=f32) + jnp.dot(tri, la_mid, preferred_element_type=f32) +
                jnp.dot(tri, la_lo, preferred_element_type=f32))
        qf = q_ref[0, pl.ds(r0, rows), :].astype(f32) * (dk ** -0.5)
        kf = k_ref[0, pl.ds(r0, rows), :].astype(f32)
        q_dec = qf * jnp.exp(bcum)
        k_dec = (kf * jnp.exp(-bcum)).astype(bf16)
        b_last = [bcum[(i + 1) * c - 1:(i + 1) * c, :] for i in range(grp)]
        k_state = jnp.concatenate(
            [kf[i * c:(i + 1) * c] * jnp.exp(b_last[i] - bcum[i * c:(i + 1) * c]) for i in range(grp)],
            axis=0).astype(bf16)
        new_states = []
        for h in range(2):
            q_h = jnp.where(head_mask[h], q_dec, 0.0).astype(bf16)
            v_h = v_ref[0, pl.ds(r0, rows), h * dv:(h + 1) * dv]
            attn = lax.dot_general(q_h, k_dec, _NT, preferred_element_type=f32)
            attn = jnp.where(causal, attn, 0.0).astype(bf16)
            o_intra = jnp.dot(attn, v_h, preferred_element_type=f32)
            st = states[h]
            o_inter = []
            for i in range(grp):
                o_inter.append(lax.dot_general(q_h[i * c:(i + 1) * c], st.astype(bf16), _NT,
                                               preferred_element_type=f32))
                u_t = lax.dot_general(v_h[i * c:(i + 1) * c], k_state[i * c:(i + 1) * c], _TN,
                                      preferred_element_type=f32)
                st = st * jnp.exp(b_last[i]) + u_t
            new_states.append(st)
            o = _rms(o_intra + jnp.concatenate(o_inter, axis=0), ng)
            rr = r_ref[0, pl.ds(r0, rows), h * dv:(h + 1) * dv].astype(f32)
            o_ref[0, pl.ds(r0, rows), h * dv:(h + 1) * dv] = (o * (rr * jax.nn.sigmoid(rr))).astype(o_ref.dtype)
        return tuple(new_states)

    zero = jnp.zeros((dv, 2 * dk), f32)
    lax.fori_loop(0, s // rows, body, (zero, zero))


def _gla(proj, proj_s, wg, bg, ng, grp=4):
    b, s, _ = proj.shape
    dk2, dv2 = 2 * GLA_KEY_DIM, 2 * GLA_VAL_DIM
    oq, ok_, ov, orr = (MAIN_OFFS[7] // dk2, MAIN_OFFS[8] // dk2, MAIN_OFFS[9] // dv2, MAIN_OFFS[10] // dv2)
    col = lambda w, o: pl.BlockSpec((1, s, w), lambda i, j: (i, 0, o + j))
    return pl.pallas_call(
        functools.partial(_gla_kernel, grp=grp),
        grid=(b, GLA_HEADS // 2),
        in_specs=[col(dk2, oq), col(dk2, ok_), col(dv2, ov),
                  pl.BlockSpec((1, s, LANES), lambda i, j: (i, 0, 2)),
                  pl.BlockSpec((GLA_GATE_RANK, dk2), lambda i, j: (0, j)),
                  pl.BlockSpec((1, dk2), lambda i, j: (0, j)),
                  col(dv2, orr),
                  pl.BlockSpec((1, GLA_VAL_DIM), lambda i, j: (0, 0))],
        out_specs=pl.BlockSpec((1, s, dv2), lambda i, j: (i, 0, j)),
        out_shape=jax.ShapeDtypeStruct((b, s, GLA_WIDTH), bf16),
        compiler_params=_cparams("parallel", "parallel"),
        name="gla",
    )(proj, proj, proj, proj_s, wg, bg, proj, ng)


def _outproj_kernel(a_ref, b_ref, x_ref, wo_ref, g2_ref, wrh_ref, wrl_ref, br_ref,
                    x1_ref, h2_ref, rt_ref, cnt_ref, carry_ref):
    @pl.when(pl.program_id(0) == 0)
    def _():
        carry_ref[...] = jnp.zeros_like(carry_ref)

    na = a_ref.shape[1]
    tm = a_ref.shape[0]
    y = (jnp.dot(a_ref[...], wo_ref[:na, :], preferred_element_type=f32) +
         jnp.dot(b_ref[...], wo_ref[na:, :], preferred_element_type=f32))
    x1 = x_ref[...] + y
    x1_ref[...] = x1
    h2 = _rms(x1, g2_ref[...])
    h2_ref[...] = h2.astype(bf16)
    hi, lo = _split_bf16(h2)
    logits = (jnp.dot(hi, wrh_ref[...], preferred_element_type=f32) +
              jnp.dot(lo, wrh_ref[...], preferred_element_type=f32) +
              jnp.dot(hi, wrl_ref[...], preferred_element_type=f32)) + br_ref[...]
    score = logits.T[:N_EXPERTS]
    e_col = lax.broadcasted_iota(jnp.int32, (N_EXPERTS, 1), 0).astype(f32)
    top = jnp.max(score, axis=0, keepdims=True)
    hits, firsts, weights = [], [], []
    for _ in range(TOP_K):
        m = jnp.max(score, axis=0, keepdims=True)
        first = jnp.min(jnp.where(score == m, e_col, float(N_EXPERTS)), axis=0, keepdims=True)
        hit = e_col == first
        hits.append(hit)
        firsts.append(first)
        weights.append(jnp.exp(m - top))
        score = jnp.where(hit, -jnp.inf, score)
    tot = weights[0] + weights[1] + weights[2] + weights[3]
    onehot = jnp.where(hits[0] | hits[1] | hits[2] | hits[3], 1.0, 0.0)
    ri = lax.broadcasted_iota(jnp.int32, (tm, tm), 0)
    ci = lax.broadcasted_iota(jnp.int32, (tm, tm), 1)
    earlier = jnp.where(ri < ci, 1.0, 0.0).astype(bf16)
    rank = carry_ref[...] + jnp.dot(onehot.astype(bf16), earlier, preferred_element_type=f32)
    carry_ref[...] = carry_ref[...] + jnp.sum(onehot, axis=1, keepdims=True)
    cnt_ref[...] = jnp.broadcast_to(carry_ref[...], cnt_ref.shape)
    ranks = [jnp.sum(jnp.where(hits[k], rank, 0.0), axis=0, keepdims=True) for k in range(TOP_K)]
    gates = [weights[k] / tot for k in range(TOP_K)]
    pad = jnp.zeros((rt_ref.shape[0] - 3 * TOP_K, tm), f32)
    rt_ref[...] = jnp.concatenate(firsts + gates + ranks + [pad], axis=0)


def _outproj_router(a, b, x2, wo, g2, wr_hi, wr_lo, br, tm=512):
    m, d = x2.shape
    na, nb = a.shape[1], b.shape[1]
    row = lambda n: pl.BlockSpec((tm, n), lambda i: (i, 0))
    whole = lambda t: pl.BlockSpec(t.shape, lambda i: (0,) * t.ndim)
    return pl.pallas_call(
        _outproj_kernel,
        grid=(m // tm,),
        in_specs=[row(na), row(nb), row(d), whole(wo), whole(g2), whole(wr_hi), whole(wr_lo), whole(br)],
        out_specs=[row(d), row(d), pl.BlockSpec((ROUTE_ROWS, tm), lambda i: (0, i)),
                   pl.BlockSpec((N_EXPERTS, LANES), lambda i: (0, 0))],
        out_shape=[jax.ShapeDtypeStruct((m, d), f32),
                   jax.ShapeDtypeStruct((m, d), bf16),
                   jax.ShapeDtypeStruct((ROUTE_ROWS, m), f32),
                   jax.ShapeDtypeStruct((N_EXPERTS, LANES), f32)],
        scratch_shapes=[pltpu.VMEM((N_EXPERTS, 1), f32)],
        compiler_params=_cparams("arbitrary"),
        name="outproj_router",
    )(a, b, x2, wo, g2, wr_hi, wr_lo, br)


def _moe_kernel(vb_ref, ve_ref, lo_ref, hi_ref, nv_ref, xs_ref, wgu_ref, bgu_ref, wd_ref, bd_ref,
                o_ref, wgu_bf, wd_bf):
    v = pl.program_id(0)
    prev = jnp.maximum(v - 1, 0)
    live = v < nv_ref[0]
    new_expert = (v == 0) | (ve_ref[v] != ve_ref[prev])
    new_block = (v == 0) | (vb_ref[v] != vb_ref[prev])

    @pl.when(live & new_expert)
    def _():
        wgu_bf[...] = wgu_ref[0].astype(bf16)
        wd_bf[...] = wd_ref[0].astype(bf16)

    @pl.when(live & new_block)
    def _():
        o_ref[...] = jnp.zeros_like(o_ref)

    @pl.when(live)
    def _():
        blk0 = vb_ref[v] * MOE_BLOCK
        lo, hi = lo_ref[v], hi_ref[v]
        first = jnp.maximum(lo - blk0, 0) // MOE_SUB
        last = (jnp.minimum(hi - blk0, MOE_BLOCK) + MOE_SUB - 1) // MOE_SUB

        def sub_block(j, carry):
            r0 = pl.multiple_of(j * MOE_SUB, MOE_SUB)
            h = jnp.dot(xs_ref[pl.ds(r0, MOE_SUB), :], wgu_bf[...], preferred_element_type=f32) + bgu_ref[0]
            x_glu = jnp.minimum(h[:, :D_FF], SWIGLU_LIMIT)
            x_lin = jnp.clip(h[:, D_FF:], -SWIGLU_LIMIT, SWIGLU_LIMIT)
            act = x_glu * jax.nn.sigmoid(SWIGLU_ALPHA * x_glu) * (x_lin + 1.0)
            y = jnp.dot(act.astype(bf16), wd_bf[...], preferred_element_type=f32) + bd_ref[0]
            row = blk0 + r0 + lax.broadcasted_iota(jnp.int32, (MOE_SUB, 1), 0)
            mine = (row >= lo) & (row < hi)
            o_ref[pl.ds(r0, MOE_SUB), :] = jnp.where(mine, y.astype(o_ref.dtype), o_ref[pl.ds(r0, MOE_SUB), :])
            return carry

        lax.fori_loop(first, last, sub_block, 0)


def _moe_experts(vb, ve, lo, hi, nv, xs, wgu, bgu, wd, bd):
    a, d = xs.shape
    nvis = vb.shape[0]
    return pl.pallas_call(
        _moe_kernel,
        grid_spec=pltpu.PrefetchScalarGridSpec(
            num_scalar_prefetch=5,
            grid=(nvis,),
            in_specs=[pl.BlockSpec((MOE_BLOCK, d), lambda v, vb, ve, *_: (vb[v], 0)),
                      pl.BlockSpec((1, d, 2 * D_FF), lambda v, vb, ve, *_: (ve[v], 0, 0)),
                      pl.BlockSpec((1, 1, 2 * D_FF), lambda v, vb, ve, *_: (ve[v], 0, 0)),
                      pl.BlockSpec((1, D_FF, d), lambda v, vb, ve, *_: (ve[v], 0, 0)),
                      pl.BlockSpec((1, 1, d), lambda v, vb, ve, *_: (ve[v], 0, 0))],
            out_specs=pl.BlockSpec((MOE_BLOCK, d), lambda v, vb, ve, *_: (vb[v], 0)),
            scratch_shapes=[pltpu.VMEM((d, 2 * D_FF), bf16), pltpu.VMEM((D_FF, d), bf16)],
        ),
        out_shape=jax.ShapeDtypeStruct((a, d), bf16),
        compiler_params=_cparams("arbitrary"),
        name="moe_experts",
    )(vb, ve, lo, hi, nv, xs, wgu, bgu, wd, bd)


def _visit_table(counts, n_rows):
    end = jnp.cumsum(counts)
    start = end - counts
    nblk = n_rows // MOE_BLOCK
    first_blk = start // MOE_BLOCK
    last_blk = jnp.where(counts > 0, (end - 1) // MOE_BLOCK, first_blk - 1)
    per_e = jnp.maximum(last_blk - first_blk + 1, 0)
    v_end = jnp.cumsum(per_e)
    v_start = v_end - per_e
    nvis = nblk + N_EXPERTS - 1
    v = jnp.arange(nvis, dtype=jnp.int32)
    n_live = v_end[-1]
    vc = jnp.minimum(v, n_live - 1)
    ve = jnp.sum((v_end[None, :] <= vc[:, None]).astype(jnp.int32), axis=1)
    vb = (first_blk[ve] + vc - v_start[ve]).astype(jnp.int32)
    return vb, ve, start[ve].astype(jnp.int32), end[ve].astype(jnp.int32), n_live.astype(jnp.int32).reshape(1), start


def _rope_tables(positions):
    half = ROPE_DIM // 2
    inv_freq = jnp.exp(-math.log(ROPE_THETA) * jnp.arange(0, ROPE_DIM, 2, dtype=f32) / ROPE_DIM)
    ang = positions.astype(f32)[:, None, :] * inv_freq[None, :, None]
    cos_t, sin_t = jnp.cos(ang), jnp.sin(ang)
    d = jnp.arange(LANES) % HEAD_DIM
    f = jnp.arange(LANES)[:, None]
    first, second = d[None, :] == f % half, d[None, :] == f % half + half
    to_cos = jnp.where((f < half) & (first | second), 1.0, 0.0)
    to_sin = jnp.where((f >= half) & (f < 2 * half), jnp.where(second, 1.0, 0.0) - jnp.where(first, 1.0, 0.0), 0.0)
    spread = jnp.concatenate([to_cos, to_sin], axis=1).astype(bf16)
    idx = jnp.arange(half)
    perm = jnp.zeros((HEAD_DIM, HEAD_DIM), f32).at[idx + half, idx].set(1.0).at[idx, idx + half].set(1.0)
    return cos_t, sin_t, spread, perm


def _layer(x, positions, norm1_g, w_in, q_norm_g, k_norm_g, cmp_pos_k, cmp_w1_k, cmp_w2_k,
           cmp_pos_v, cmp_w1_v, cmp_w2_v, gla_w_gate, gla_b_gate, gla_norm_g, w_out, norm2_g,
           w_router, b_router, w_gate_up, b_gate_up, w_down, b_down):
    b, s, d = x.shape
    m = b * s
    hk, g, dh = NSA_KV_HEADS, NSA_GROUP, HEAD_DIM
    x2 = x.reshape(m, d)

    offs = [0]
    for sz in IN_SIZES:
        offs.append(offs[-1] + sz)
    seg = lambda i: w_in[:, offs[i]:offs[i + 1]]
    w_main = jnp.concatenate([seg(i) for i in (0, 1, 2, 3, 4, 5, 6, 8, 9, 10, 12)], axis=1).astype(bf16)
    padw = lambda t: jnp.concatenate([t, jnp.zeros((d, LANES - t.shape[1]), f32)], axis=1)
    ng = g * 3
    w_small = jnp.concatenate([padw(seg(7)[:, :ng]), padw(seg(7)[:, ng:]), padw(seg(11))], axis=1).astype(bf16)
    proj, proj_s = _inproj(x2, norm1_g.reshape(1, d), w_main, w_small)
    proj = proj.reshape(b, s, -1)
    proj_s = proj_s.reshape(b, s, 3 * LANES)

    cos_t, sin_t, spread, perm = _rope_tables(positions)
    per_kv = lambda t: jnp.concatenate([t] * hk, axis=-1)
    eye = jnp.eye(hk, dtype=f32)
    bdiag = lambda t: jnp.kron(eye, t)
    w1_bd = lambda w1: jax.vmap(bdiag)(w1.reshape(CMP_BLOCK, dh, CMP_HIDDEN)).astype(bf16)
    kc, vct, ks_rot, vst, kw_rot, vwt = _kvprep(
        proj, jnp.concatenate([cos_t, sin_t], axis=1), spread, per_kv(k_norm_g),
        bdiag(jnp.ones((dh, dh), f32)).astype(bf16), bdiag(perm).astype(bf16),
        jnp.stack([jnp.eye(hk * dh, LANES, k=-h * dh, dtype=f32) * (jnp.arange(LANES) < dh)
                   for h in range(hk)]).astype(bf16),
        per_kv(cmp_pos_k), w1_bd(cmp_w1_k), bdiag(cmp_w2_k).astype(bf16),
        per_kv(cmp_pos_v), w1_bd(cmp_w1_v), bdiag(cmp_w2_v).astype(bf16))
    nsa_out = _nsa_attention(proj, cos_t, sin_t, q_norm_g.reshape(dh, 1), kc, vct,
                             ks_rot, vst, kw_rot, vwt, proj_s).reshape(m, NSA_WIDTH)

    gla_out = _gla(proj, proj_s, gla_w_gate, gla_b_gate.reshape(1, -1),
                   gla_norm_g.reshape(1, GLA_VAL_DIM)).reshape(m, GLA_WIDTH)

    padr = lambda t: jnp.concatenate([t, jnp.zeros(t.shape[:-1] + (LANES - t.shape[-1],), t.dtype)], axis=-1)
    wr_hi = w_router.astype(bf16)
    wr_lo = (w_router - wr_hi.astype(f32)).astype(bf16)
    x1, h2, rt, cnt = _outproj_router(nsa_out, gla_out, x2, w_out.astype(bf16), norm2_g.reshape(1, d),
                                      padr(wr_hi), padr(wr_lo), padr(b_router.reshape(1, -1)))

    top_idx = rt[:TOP_K].astype(jnp.int32)
    gate = rt[TOP_K:2 * TOP_K]
    rank = rt[2 * TOP_K:3 * TOP_K].astype(jnp.int32)
    counts = cnt[:, 0].astype(jnp.int32)
    a = m * TOP_K
    vb, ve, lo, hi, nv, start = _visit_table(counts, a)
    pos = rank
    for e in range(N_EXPERTS):
        pos = pos + jnp.where(top_idx == e, start[e], 0)
    tok = jnp.broadcast_to(jnp.arange(m, dtype=jnp.int32), (TOP_K, m))
    _, sorted_tok = lax.sort(((top_idx * m + tok).reshape(-1), tok.reshape(-1)), num_keys=1)
    rows_of = lambda t, idx: t.at[idx].get(mode='promise_in_bounds')
    xs = rows_of(h2, sorted_tok)
    ys = _moe_experts(vb, ve, lo, hi, nv, xs, w_gate_up, b_gate_up.reshape(N_EXPERTS, 1, -1),
                      w_down, b_down.reshape(N_EXPERTS, 1, -1))
    out = x1
    for k in range(TOP_K):
        out = out + rows_of(ys, pos[k]).astype(f32) * gate[k][:, None]
    return out.reshape(b, s, d)


def kernel(x, positions, norm1_g, w_in, nsa_q_norm_g, nsa_k_norm_g, cmp_pos_k, cmp_w1_k, cmp_w2_k,
           cmp_pos_v, cmp_w1_v, cmp_w2_v, gla_w_gate, gla_b_gate, gla_norm_g, w_out, norm2_g,
           w_router, b_router, w_gate_up, b_gate_up, w_down, b_down):
    for l in range(norm1_g.shape[0]):
        x = _layer(x, positions, norm1_g[l], w_in[l], nsa_q_norm_g[l], nsa_k_norm_g[l],
                   cmp_pos_k[l], cmp_w1_k[l], cmp_w2_k[l], cmp_pos_v[l], cmp_w1_v[l], cmp_w2_v[l],
                   gla_w_gate[l], gla_b_gate[l], gla_norm_g[l], w_out[l], norm2_g[l],
                   w_router[l], b_router[l], w_gate_up[l], b_gate_up[l], w_down[l], b_down[l])
    return x
```

```python
import functools
import math

import jax
import jax.numpy as jnp
from jax import lax
from jax.experimental import pallas as pl
from jax.experimental.pallas import tpu as pltpu

f32 = jnp.float32
bf16 = jnp.bfloat16

NSA_HEADS = 8
NSA_KV_HEADS = 2
NSA_GROUP = NSA_HEADS // NSA_KV_HEADS
HEAD_DIM = 64
CMP_BLOCK = 32
CMP_STRIDE = 16
CMP_HIDDEN = 256
SEL_BLOCK = 64
SEL_TOPK = 8
WINDOW = 512
FORCE_BONUS = 1e4
GLA_HEADS = 4
GLA_KEY_DIM = 64
GLA_VAL_DIM = 128
GLA_CHUNK = 64
GLA_GATE_RANK = 16
GLA_TAU = 16.0
ROPE_THETA = 500000.0
ROPE_DIM = HEAD_DIM // 4
N_EXPERTS = 32
TOP_K = 4
D_FF = 1024
SWIGLU_LIMIT = 7.0
SWIGLU_ALPHA = 1.702
EPS = 1e-6
NEG_INF = -1e30

NSA_WIDTH = NSA_HEADS * HEAD_DIM
NSA_KV_WIDTH = NSA_KV_HEADS * HEAD_DIM
GLA_KEY_WIDTH = GLA_HEADS * GLA_KEY_DIM
GLA_WIDTH = GLA_HEADS * GLA_VAL_DIM
IN_SIZES = (NSA_WIDTH,) + (NSA_KV_WIDTH,) * 6 + (
    NSA_HEADS * 3, GLA_KEY_WIDTH, GLA_KEY_WIDTH, GLA_WIDTH, GLA_GATE_RANK, GLA_WIDTH)
MAIN_SIZES = (NSA_WIDTH,) + (NSA_KV_WIDTH,) * 6 + (GLA_KEY_WIDTH, GLA_KEY_WIDTH, GLA_WIDTH, GLA_WIDTH)
MAIN_OFFS = tuple(sum(MAIN_SIZES[:i]) for i in range(len(MAIN_SIZES) + 1))

LANES = 128
VMEM_LIMIT = 48 * 1024 * 1024
MOE_BLOCK = 1024
MOE_SUB = 256
NSA_SAFE_EXPONENT = 120.0
GLA_UNROLL = 4
ROUTE_ROWS = 16

_NT = (((1,), (1,)), ((), ()))
_TN = (((0,), (0,)), ((), ()))


def _cparams(*sem):
    return pltpu.CompilerParams(dimension_semantics=sem, vmem_limit_bytes=VMEM_LIMIT)


def _rms(t, g):
    return t * lax.rsqrt(jnp.mean(t * t, axis=-1, keepdims=True) + EPS) * g


def _split_bf16(t):
    hi = t.astype(bf16)
    lo = (t - hi.astype(f32)).astype(bf16)
    return hi, lo


def _rope(t, cos_f, sin_f, perm):
    hi, lo = _split_bf16(t)
    rot = (jnp.dot(hi, perm, preferred_element_type=f32) +
           jnp.dot(lo, perm, preferred_element_type=f32))
    return t * cos_f + rot * sin_f


def _inproj_kernel(x_ref, g_ref, w_ref, ws_ref, o_ref, os_ref):
    x = x_ref[...]
    h = _rms(x, g_ref[...]).astype(bf16)
    n = o_ref.shape[1]
    step = 512
    for c in range(0, n, step):
        e = min(c + step, n)
        o_ref[:, c:e] = jnp.dot(h, w_ref[:, c:e], preferred_element_type=f32).astype(bf16)
    os_ref[...] = jnp.dot(h, ws_ref[...], preferred_element_type=f32)


def _inproj(x2, g, w_main, w_small, tm=512):
    m, d = x2.shape
    n = w_main.shape[1]
    ns = w_small.shape[1]
    return pl.pallas_call(
        _inproj_kernel,
        grid=(m // tm,),
        in_specs=[pl.BlockSpec((tm, d), lambda i: (i, 0)),
                  pl.BlockSpec((1, d), lambda i: (0, 0)),
                  pl.BlockSpec((d, n), lambda i: (0, 0)),
                  pl.BlockSpec((d, ns), lambda i: (0, 0))],
        out_specs=[pl.BlockSpec((tm, n), lambda i: (i, 0)),
                   pl.BlockSpec((tm, ns), lambda i: (i, 0))],
        out_shape=[jax.ShapeDtypeStruct((m, n), bf16),
                   jax.ShapeDtypeStruct((m, ns), f32)],
        compiler_params=_cparams("parallel"),
        name="inproj",
    )(x2, g, w_main, w_small)


def _kvprep_kernel(kc_ref, vc_ref, ks_ref, vs_ref, kw_ref, vw_ref, cs_ref, spread_ref, kg_ref, ones_ref,
                   perm_ref, pick_ref, posk_ref, w1k_ref, w2k_ref, posv_ref, w1v_ref, w2v_ref,
                   kco_ref, vcto_ref, kso_ref, vsto_ref, kwo_ref, vwto_ref, tmp_ref):
    ones_bd = ones_ref[...]
    nh_out = kco_ref.shape[2]

    def rms_heads(t, g):
        sq_hi, sq_lo = _split_bf16(t * t)
        ss = (jnp.dot(sq_hi, ones_bd, preferred_element_type=f32) +
              jnp.dot(sq_lo, ones_bd, preferred_element_type=f32))
        return t * lax.rsqrt(ss * (1.0 / HEAD_DIM) + EPS) * g

    def compress(src_ref, pos_ref, w1_ref, w2_ref):
        tmp_ref[...] = src_ref[0].astype(f32)
        nh = tmp_ref.shape[0] // CMP_STRIDE
        a = jnp.zeros((nh, NSA_KV_HEADS * CMP_HIDDEN), f32)
        b = jnp.zeros((nh, NSA_KV_HEADS * CMP_HIDDEN), f32)
        for p in range(CMP_STRIDE):
            rows = tmp_ref[pl.ds(p, nh, stride=CMP_STRIDE), :]
            a = a + jnp.dot((rows + pos_ref[p:p + 1, :]).astype(bf16), w1_ref[p], preferred_element_type=f32)
            q = CMP_STRIDE + p
            b = b + jnp.dot((rows + pos_ref[q:q + 1, :]).astype(bf16), w1_ref[q], preferred_element_type=f32)
        pre = a + pltpu.roll(b, nh - 1, 0)
        hid = pre * jax.nn.sigmoid(pre)
        return jnp.dot(hid.astype(bf16), w2_ref[...], preferred_element_type=f32)

    kc = rms_heads(compress(kc_ref, posk_ref, w1k_ref, w2k_ref), kg_ref[0:1, :]).astype(bf16)
    tmp_ref[:nh_out, :] = compress(vc_ref, posv_ref, w1v_ref, w2v_ref)
    vcto_ref[0] = tmp_ref[:nh_out, :].T.astype(bf16)
    s, w = ks_ref.shape[1], ks_ref.shape[2]
    cs = jnp.concatenate([cs_ref[0], jnp.zeros((LANES - cs_ref.shape[1], s), f32)], axis=0).T
    cs_hi = cs.astype(bf16)
    cs_mid, cs_lo = _split_bf16(cs - cs_hi.astype(f32))
    spread = spread_ref[...]
    tables = (jnp.dot(cs_hi, spread, preferred_element_type=f32) + jnp.dot(cs_mid, spread, preferred_element_type=f32) +
              jnp.dot(cs_lo, spread, preferred_element_type=f32))
    lane_row = lax.broadcasted_iota(jnp.int32, (1, LANES), 1)
    cos_f = tables[:, :LANES] + jnp.where(lane_row % HEAD_DIM >= ROPE_DIM, 1.0, 0.0)
    sin_f = tables[:, LANES:]
    perm = perm_ref[...]
    ks = _rope(rms_heads(ks_ref[0].astype(f32), kg_ref[1:2, :]), cos_f, sin_f, perm).astype(bf16)
    kw = _rope(rms_heads(kw_ref[0].astype(f32), kg_ref[2:3, :]), cos_f, sin_f, perm).astype(bf16)
    row = lax.broadcasted_iota(jnp.int32, (s, LANES), 0)
    lane = lax.broadcasted_iota(jnp.int32, (s, LANES), 1)
    last_lane_one = lambda n: jnp.where(lax.broadcasted_iota(jnp.int32, (n, LANES), 1) == LANES - 1, 1.0, 0.0)
    ones_lane = last_lane_one(s)
    block_onehot = jnp.where(row // SEL_BLOCK + HEAD_DIM == lane, 1.0, 0.0) + ones_lane
    pad_lane = lax.broadcasted_iota(jnp.int32, (WINDOW, LANES), 1)
    pad_flag = jnp.where((pad_lane == HEAD_DIM) | (pad_lane == LANES - 1), 1.0, 0.0)
    for h in range(NSA_KV_HEADS):
        pick = pick_ref[h]
        kco_ref[0, h] = (jnp.dot(kc, pick, preferred_element_type=f32) + last_lane_one(nh_out)).astype(bf16)
        kso_ref[0, h] = (jnp.dot(ks, pick, preferred_element_type=f32) + block_onehot).astype(bf16)
        kwo_ref[0, h, :WINDOW, :] = pad_flag.astype(bf16)
        kwo_ref[0, h, WINDOW:, :] = (jnp.dot(kw, pick, preferred_element_type=f32) + ones_lane).astype(bf16)
    vsto_ref[0] = vs_ref[0].astype(f32).T.astype(bf16)
    vwto_ref[0] = jnp.concatenate([jnp.zeros((w, WINDOW), f32), vw_ref[0].astype(f32).T], axis=1).astype(bf16)


def _kvprep(proj, cs_t, spread, kg2, ones_bd, perm2, pick, posk, w1k, w2k, posv, w1v, w2v):
    b, s, _ = proj.shape
    w = NSA_KV_WIDTH
    assert w == LANES
    nh = s // CMP_STRIDE
    col = lambda i: pl.BlockSpec((1, s, w), lambda n: (n, 0, MAIN_OFFS[i] // w))
    whole = lambda a: pl.BlockSpec(a.shape, lambda n: (0,) * a.ndim)
    tab = pl.BlockSpec((1, cs_t.shape[1], s), lambda n: (n, 0, 0))
    out = lambda *shp: (pl.BlockSpec((1,) + shp, lambda n: (n,) + (0,) * len(shp)),
                        jax.ShapeDtypeStruct((b,) + shp, bf16))
    hk = NSA_KV_HEADS
    outs = [out(hk, nh, LANES), out(w, nh), out(hk, s, LANES), out(w, s), out(hk, WINDOW + s, LANES),
            out(w, WINDOW + s)]
    return pl.pallas_call(
        _kvprep_kernel,
        grid=(b,),
        in_specs=[col(1), col(2), col(3), col(4), col(5), col(6), tab, whole(spread), whole(kg2), whole(ones_bd),
                  whole(perm2), whole(pick), whole(posk), whole(w1k), whole(w2k), whole(posv), whole(w1v),
                  whole(w2v)],
        out_specs=[o[0] for o in outs],
        out_shape=[o[1] for o in outs],
        scratch_shapes=[pltpu.VMEM((s, w), f32)],
        compiler_params=_cparams("parallel"),
        name="nsa_kvprep",
    )(proj, proj, proj, proj, proj, proj, cs_t, spread, kg2, ones_bd, perm2, pick, posk, w1k, w2k, posv, w1v, w2v)


def _nsa_kernel(q_ref, cos_ref, sin_ref, qg_ref, mb_ref, kc_ref, vct_ref, ks_ref, vst_ref, kw_ref, vwt_ref,
                gate_ref, o_ref, *, tq, ck, bounded):
    g = NSA_GROUP
    qi = pl.program_id(2)
    t0 = pl.multiple_of(qi * tq, tq)
    scale = HEAD_DIM ** -0.5 * math.log2(math.e)
    per_head = lambda t: jnp.concatenate([t] * g, axis=1)

    qt = q_ref[0].astype(f32).T
    qt = jnp.concatenate([qt[i * HEAD_DIM:(i + 1) * HEAD_DIM] for i in range(g)], axis=1)
    qn = qt * lax.rsqrt(jnp.mean(qt * qt, axis=0, keepdims=True) + EPS) * qg_ref[...]
    half = ROPE_DIM // 2
    cos8, sin8 = per_head(cos_ref[0]), per_head(sin_ref[0])
    x1, x2 = qn[:half], qn[half:ROPE_DIM]
    q_rot = jnp.concatenate([x1 * cos8 - x2 * sin8, x2 * cos8 + x1 * sin8, qn[ROPE_DIM:]], axis=0)
    n_extra = kc_ref.shape[3] - HEAD_DIM

    def with_features(qb, feats, branch):
        used = sum(f.shape[0] for f in feats)
        last = jnp.broadcast_to(-mb_ref[0:1, branch:branch + 1] if bounded else 0.0, (1, g * tq))
        fill = jnp.zeros((n_extra - used - 1, g * tq), f32)
        return jnp.concatenate([qb] + feats + [fill, last], axis=0).astype(bf16)

    q_cmp = with_features(qn * scale, [], 0)
    q_rot = q_rot * scale
    tq_row = t0 + lax.broadcasted_iota(jnp.int32, (1, tq), 1)

    kc = kc_ref[0, 0]
    ncp = kc.shape[0]
    n_col = lax.broadcasted_iota(jnp.int32, (ncp, 1), 0)
    ok_c = (n_col * CMP_STRIDE + (CMP_BLOCK - 1)) <= tq_row
    s_c = jnp.dot(kc, q_cmp, preferred_element_type=f32) + per_head(jnp.where(ok_c, 0.0, NEG_INF))
    if bounded:
        p_c = jnp.exp2(s_c)
    else:
        m_c = jnp.max(s_c, axis=0, keepdims=True)
        p_c = jnp.exp2(s_c - m_c) * per_head(jnp.where(ok_c, 1.0, 0.0))
    den = jnp.sum(p_c, axis=0, keepdims=True)
    p_c = p_c * (1.0 / jnp.where(den > 0, den, 1.0))
    o_cmp = jnp.dot(vct_ref[0], p_c.astype(bf16), preferred_element_type=f32)

    pg = p_c[:, :tq]
    for i in range(1, g):
        pg = pg + p_c[:, i * tq:(i + 1) * tq]
    nj = ks_ref.shape[2] // SEL_BLOCK
    jj = lax.broadcasted_iota(jnp.int32, (nj, ncp), 0)
    nn = lax.broadcasted_iota(jnp.int32, (nj, ncp), 1)
    overlap = (nn * CMP_STRIDE < (jj + 1) * SEL_BLOCK) & (nn * CMP_STRIDE + CMP_BLOCK > jj * SEL_BLOCK)
    overlap = jnp.where(overlap, 1.0, 0.0).astype(bf16)
    pg_hi, pg_lo = _split_bf16(pg)
    imp = (jnp.dot(overlap, pg_hi, preferred_element_type=f32) +
           jnp.dot(overlap, pg_lo, preferred_element_type=f32))
    j_col = lax.broadcasted_iota(jnp.int32, (nj, 1), 0)
    j_f = j_col.astype(f32)
    cur = tq_row // SEL_BLOCK
    valid = j_col <= cur
    forced = (j_col == 0) | (j_col == cur) | (j_col == cur - 1)
    score = jnp.where(valid, imp + jnp.where(forced, FORCE_BONUS, 0.0), NEG_INF)
    sel = jnp.zeros((nj, tq), f32)
    for _ in range(SEL_TOPK):
        m = jnp.max(score, axis=0, keepdims=True)
        first = jnp.min(jnp.where(score == m, j_f, float(nj)), axis=0, keepdims=True)
        hit = j_f == first
        sel = jnp.where(hit, 1.0, sel)
        score = jnp.where(hit, -jnp.inf, score)
    sel = jnp.where(valid, sel, 0.0)

    def weighted_values(vt, p):
        lhs = jnp.concatenate([vt, jnp.ones((16, vt.shape[1]), bf16)], axis=0)
        r = jnp.dot(lhs, p.astype(bf16), preferred_element_type=f32)
        return r[:HEAD_DIM], r[HEAD_DIM:HEAD_DIM + 1]

    wl = WINDOW + tq
    q_win = with_features(q_rot, [jnp.full((1, g * tq), NEG_INF, f32)], 2)
    s_w = jnp.dot(kw_ref[0, 0, pl.ds(t0, wl), :], q_win, preferred_element_type=f32)
    step = lax.broadcasted_iota(jnp.int32, (tq, 1), 0)
    lo_ok = (t0 - WINDOW + step) > (tq_row - WINDOW)
    hi_ok = (t0 + step) <= tq_row
    parts = [s_w[:tq] + per_head(jnp.where(lo_ok, 0.0, NEG_INF))]
    if wl > 2 * tq:
        parts.append(s_w[tq:wl - tq])
    parts.append(s_w[wl - tq:] + per_head(jnp.where(hi_ok, 0.0, NEG_INF)))
    if bounded:
        p_w = jnp.concatenate([jnp.exp2(t) for t in parts], axis=0)
    else:
        m_w = functools.reduce(jnp.maximum, [jnp.max(t, axis=0, keepdims=True) for t in parts])
        p_w = jnp.concatenate([jnp.exp2(t - m_w) for t in parts], axis=0)
    o_win, l_w = weighted_values(vwt_ref[0, :, pl.ds(t0, wl)], p_w)
    o_win = o_win * (1.0 / l_w)

    q_sel = with_features(q_rot, [per_head(jnp.where(sel > 0.5, 0.0, NEG_INF))], 1)

    def sel_step(carry, k0, diagonal):
        m_prev, l_prev, acc = carry
        s = jnp.dot(ks_ref[0, 0, pl.ds(k0, ck), :], q_sel, preferred_element_type=f32)
        if diagonal:
            kpos = k0 + lax.broadcasted_iota(jnp.int32, (ck, 1), 0)
            s = s + per_head(jnp.where(kpos <= tq_row, 0.0, NEG_INF))
        if bounded:
            pv, p_sum = weighted_values(vst_ref[0, :, pl.ds(k0, ck)], jnp.exp2(s))
            return m_prev, l_prev + p_sum, acc + pv
        m_new = jnp.maximum(m_prev, jnp.max(s, axis=0, keepdims=True))
        alpha = jnp.exp2(m_prev - m_new)
        pv, p_sum = weighted_values(vst_ref[0, :, pl.ds(k0, ck)], jnp.exp2(s - m_new))
        return m_new, alpha * l_prev + p_sum, alpha * acc + pv

    init = (jnp.full((1, g * tq), NEG_INF, f32), jnp.zeros((1, g * tq), f32),
            jnp.zeros((HEAD_DIM, g * tq), f32))
    last = (t0 + tq + ck - 1) // ck - 1
    carry = lax.fori_loop(0, last, lambda c, cr: sel_step(cr, pl.multiple_of(c * ck, ck), False), init)
    _, l_s, acc_s = sel_step(carry, pl.multiple_of(last * ck, ck), True)
    o_sel = acc_s * (1.0 / l_s)

    gt = gate_ref[0].T
    gate = lambda j: jax.nn.sigmoid(jnp.concatenate([gt[i * 3 + j:i * 3 + j + 1] for i in range(g)], axis=1))
    ot = gate(0) * o_cmp + gate(1) * o_sel + gate(2) * o_win
    o2 = jnp.concatenate([ot[:, i * tq:(i + 1) * tq] for i in range(g)], axis=0)
    o_ref[0] = o2.T.astype(o_ref.dtype)


def _nsa_attention(proj, cos_t, sin_t, qg, bounds, kc, vct, ks, vst, kw, vwt, gate_logits, *, bounded,
                   tq=256, ck=512):
    b, s, _ = proj.shape
    hk, g, dh = NSA_KV_HEADS, NSA_GROUP, HEAD_DIM
    assert WINDOW >= tq and ck % tq == 0 and s % ck == 0
    tab = pl.BlockSpec((1, ROPE_DIM // 2, tq), lambda i, j, t: (i, 0, t))
    assert s // SEL_BLOCK <= kc.shape[3] - dh
    keys = lambda a: pl.BlockSpec((1, 1) + a.shape[2:], lambda i, j, t: (i, j, 0, 0))
    vals = lambda a: pl.BlockSpec((1, dh, a.shape[2]), lambda i, j, t: (i, j, 0))
    return pl.pallas_call(
        functools.partial(_nsa_kernel, tq=tq, ck=ck, bounded=bounded),
        grid=(b, hk, s // tq),
        in_specs=[pl.BlockSpec((1, tq, g * dh), lambda i, j, t: (i, t, j)),
                  tab, tab, pl.BlockSpec((dh, 1), lambda i, j, t: (0, 0)),
                  pl.BlockSpec(bounds.shape, lambda i, j, t: (0, 0)),
                  keys(kc), vals(vct), keys(ks), vals(vst), keys(kw), vals(vwt),
                  pl.BlockSpec((1, tq, LANES), lambda i, j, t: (i, t, j))],
        out_specs=pl.BlockSpec((1, tq, g * dh), lambda i, j, t: (i, t, j)),
        out_shape=jax.ShapeDtypeStruct((b, s, NSA_WIDTH), bf16),
        compiler_params=_cparams("parallel", "parallel", "arbitrary"),
        name="nsa_attention_bounded" if bounded else "nsa_attention",
    )(proj, cos_t, sin_t, qg, bounds, kc, vct, ks, vst, kw, vwt, gate_logits)


def _gla_kernel(q_ref, k_ref, v_ref, lr_ref, wg_ref, bg_ref, r_ref, ng_ref, o_ref, *, grp):
    c = GLA_CHUNK
    s = q_ref.shape[1]
    dk, dv = GLA_KEY_DIM, GLA_VAL_DIM
    rows = grp * c
    ri = lax.broadcasted_iota(jnp.int32, (rows, rows), 0)
    ci = lax.broadcasted_iota(jnp.int32, (rows, rows), 1)
    causal = (ci <= ri) & (ci // c == ri // c)
    tri = jnp.where(causal, 1.0, 0.0).astype(bf16)
    lane = lax.broadcasted_iota(jnp.int32, (1, 2 * dk), 1)
    head_mask = [lane < dk, lane >= dk]
    chunk_of_row = lax.broadcasted_iota(jnp.int32, (rows, 1), 0) // c
    wg = wg_ref[...]
    bg = bg_ref[...]
    ng = ng_ref[...]

    def body(n, states):
        r0 = pl.multiple_of(n * rows, rows)
        z = jnp.dot(lr_ref[0, pl.ds(r0, rows), :GLA_GATE_RANK], wg, preferred_element_type=f32,
                    precision=lax.Precision.HIGHEST) + bg
        log_a = -(jnp.maximum(-z, 0.0) + jnp.log(1.0 + jnp.exp(-jnp.abs(z)))) / GLA_TAU
        la_hi = log_a.astype(bf16)
        la_mid, la_lo = _split_bf16(log_a - la_hi.astype(f32))
        bcum = (jnp.dot(tri, la_hi, preferred_element_type=f32) + jnp.dot(tri, la_mid, preferred_element_type=f32) +
                jnp.dot(tri, la_lo, preferred_element_type=f32))
        qf = q_ref[0, pl.ds(r0, rows), :].astype(f32) * (dk ** -0.5)
        kf = k_ref[0, pl.ds(r0, rows), :].astype(f32)
        q_dec = qf * jnp.exp(bcum)
        k_dec = (kf * jnp.exp(-bcum)).astype(bf16)
        b_last = [bcum[(i + 1) * c - 1:(i + 1) * c, :] for i in range(grp)]
        k_state = jnp.concatenate(
            [kf[i * c:(i + 1) * c] * jnp.exp(b_last[i] - bcum[i * c:(i + 1) * c]) for i in range(grp)],
            axis=0).astype(bf16)
        by_chunk = lambda t: jnp.concatenate(
            [jnp.where(chunk_of_row == i, t, jnp.zeros_like(t)) for i in range(grp)], axis=1)
        k_state_bc = by_chunk(k_state)
        new_states = []
        for h in range(2):
            q_h = jnp.where(head_mask[h], q_dec, 0.0).astype(bf16)
            v_h = v_ref[0, pl.ds(r0, rows), h * dv:(h + 1) * dv]
            attn = lax.dot_general(q_h, k_dec, _NT, preferred_element_type=f32)
            attn = jnp.where(causal, attn, 0.0).astype(bf16)
            o_intra = jnp.dot(attn, v_h, preferred_element_type=f32)
            u_all = lax.dot_general(v_h, k_state_bc, _TN, preferred_element_type=f32)
            st = states[h]
            entering = []
            for i in range(grp):
                entering.append(st)
                st = st * jnp.exp(b_last[i]) + u_all[:, i * 2 * dk:(i + 1) * 2 * dk]
            new_states.append(st)
            o_inter = lax.dot_general(by_chunk(q_h), jnp.concatenate(entering, axis=1).astype(bf16), _NT,
                                      preferred_element_type=f32)
            o = _rms(o_intra + o_inter, ng)
            rr = r_ref[0, pl.ds(r0, rows), h * dv:(h + 1) * dv].astype(f32)
            o_ref[0, pl.ds(r0, rows), h * dv:(h + 1) * dv] = (o * (rr * jax.nn.sigmoid(rr))).astype(o_ref.dtype)
        return tuple(new_states)

    zero = jnp.zeros((dv, 2 * dk), f32)
    lax.fori_loop(0, s // rows, body, (zero, zero), unroll=GLA_UNROLL)


def _gla(proj, proj_s, wg, bg, ng, grp=4):
    b, s, _ = proj.shape
    dk2, dv2 = 2 * GLA_KEY_DIM, 2 * GLA_VAL_DIM
    oq, ok_, ov, orr = (MAIN_OFFS[7] // dk2, MAIN_OFFS[8] // dk2, MAIN_OFFS[9] // dv2, MAIN_OFFS[10] // dv2)
    col = lambda w, o: pl.BlockSpec((1, s, w), lambda i, j: (i, 0, o + j))
    return pl.pallas_call(
        functools.partial(_gla_kernel, grp=grp),
        grid=(b, GLA_HEADS // 2),
        in_specs=[col(dk2, oq), col(dk2, ok_), col(dv2, ov),
                  pl.BlockSpec((1, s, LANES), lambda i, j: (i, 0, 2)),
                  pl.BlockSpec((GLA_GATE_RANK, dk2), lambda i, j: (0, j)),
                  pl.BlockSpec((1, dk2), lambda i, j: (0, j)),
                  col(dv2, orr),
                  pl.BlockSpec((1, GLA_VAL_DIM), lambda i, j: (0, 0))],
        out_specs=pl.BlockSpec((1, s, dv2), lambda i, j: (i, 0, j)),
        out_shape=jax.ShapeDtypeStruct((b, s, GLA_WIDTH), bf16),
        compiler_params=_cparams("parallel", "parallel"),
        name="gla",
    )(proj, proj, proj, proj_s, wg, bg, proj, ng)


def _outproj_kernel(a_ref, b_ref, x_ref, wo_ref, g2_ref, wrh_ref, wrl_ref, br_ref,
                    x1_ref, h2_ref, rt_ref, cnt_ref, carry_ref):
    @pl.when(pl.program_id(0) == 0)
    def _():
        carry_ref[...] = jnp.zeros_like(carry_ref)

    na = a_ref.shape[1]
    tm = a_ref.shape[0]
    y = (jnp.dot(a_ref[...], wo_ref[:na, :], preferred_element_type=f32) +
         jnp.dot(b_ref[...], wo_ref[na:, :], preferred_element_type=f32))
    x1 = x_ref[...] + y
    x1_ref[...] = x1
    h2 = _rms(x1, g2_ref[...])
    h2_ref[...] = h2.astype(bf16)
    hi, lo = _split_bf16(h2)
    logits = (jnp.dot(hi, wrh_ref[...], preferred_element_type=f32) +
              jnp.dot(lo, wrh_ref[...], preferred_element_type=f32) +
              jnp.dot(hi, wrl_ref[...], preferred_element_type=f32)) + br_ref[...]
    score = logits.T[:N_EXPERTS]
    e_col = lax.broadcasted_iota(jnp.int32, (N_EXPERTS, 1), 0).astype(f32)
    top = jnp.max(score, axis=0, keepdims=True)
    hits, firsts, weights = [], [], []
    for _ in range(TOP_K):
        m = jnp.max(score, axis=0, keepdims=True)
        first = jnp.min(jnp.where(score == m, e_col, float(N_EXPERTS)), axis=0, keepdims=True)
        hit = e_col == first
        hits.append(hit)
        firsts.append(first)
        weights.append(jnp.exp(m - top))
        score = jnp.where(hit, -jnp.inf, score)
    tot = weights[0] + weights[1] + weights[2] + weights[3]
    onehot = jnp.where(hits[0] | hits[1] | hits[2] | hits[3], 1.0, 0.0)
    ri = lax.broadcasted_iota(jnp.int32, (tm, tm), 0)
    ci = lax.broadcasted_iota(jnp.int32, (tm, tm), 1)
    earlier = jnp.where(ri < ci, 1.0, 0.0).astype(bf16)
    rank = carry_ref[...] + jnp.dot(onehot.astype(bf16), earlier, preferred_element_type=f32)
    carry_ref[...] = carry_ref[...] + jnp.sum(onehot, axis=1, keepdims=True)
    cnt_ref[...] = jnp.broadcast_to(carry_ref[...], cnt_ref.shape)
    ranks = [jnp.sum(jnp.where(hits[k], rank, 0.0), axis=0, keepdims=True) for k in range(TOP_K)]
    gates = [weights[k] / tot for k in range(TOP_K)]
    pad = jnp.zeros((rt_ref.shape[0] - 3 * TOP_K, tm), f32)
    rt_ref[...] = jnp.concatenate(firsts + gates + ranks + [pad], axis=0)


def _outproj_router(a, b, x2, wo, g2, wr_hi, wr_lo, br, tm=512):
    m, d = x2.shape
    na, nb = a.shape[1], b.shape[1]
    row = lambda n: pl.BlockSpec((tm, n), lambda i: (i, 0))
    whole = lambda t: pl.BlockSpec(t.shape, lambda i: (0,) * t.ndim)
    return pl.pallas_call(
        _outproj_kernel,
        grid=(m // tm,),
        in_specs=[row(na), row(nb), row(d), whole(wo), whole(g2), whole(wr_hi), whole(wr_lo), whole(br)],
        out_specs=[row(d), row(d), pl.BlockSpec((ROUTE_ROWS, tm), lambda i: (0, i)),
                   pl.BlockSpec((N_EXPERTS, LANES), lambda i: (0, 0))],
        out_shape=[jax.ShapeDtypeStruct((m, d), f32),
                   jax.ShapeDtypeStruct((m, d), bf16),
                   jax.ShapeDtypeStruct((ROUTE_ROWS, m), f32),
                   jax.ShapeDtypeStruct((N_EXPERTS, LANES), f32)],
        scratch_shapes=[pltpu.VMEM((N_EXPERTS, 1), f32)],
        compiler_params=_cparams("arbitrary"),
        name="outproj_router",
    )(a, b, x2, wo, g2, wr_hi, wr_lo, br)


def _moe_kernel(vb_ref, ve_ref, lo_ref, hi_ref, nv_ref, xs_ref, wgu_ref, bgu_ref, wd_ref, bd_ref,
                o_ref, wgu_bf, wd_bf):
    v = pl.program_id(0)
    prev = jnp.maximum(v - 1, 0)
    live = v < nv_ref[0]
    new_expert = (v == 0) | (ve_ref[v] != ve_ref[prev])
    new_block = (v == 0) | (vb_ref[v] != vb_ref[prev])

    @pl.when(live & new_expert)
    def _():
        wgu_bf[...] = wgu_ref[0].astype(bf16)
        wd_bf[...] = wd_ref[0].astype(bf16)

    @pl.when(live & new_block)
    def _():
        o_ref[...] = jnp.zeros_like(o_ref)

    @pl.when(live)
    def _():
        blk0 = vb_ref[v] * MOE_BLOCK
        lo, hi = lo_ref[v], hi_ref[v]
        first = jnp.maximum(lo - blk0, 0) // MOE_SUB
        last = (jnp.minimum(hi - blk0, MOE_BLOCK) + MOE_SUB - 1) // MOE_SUB

        def sub_block(j, carry):
            r0 = pl.multiple_of(j * MOE_SUB, MOE_SUB)
            h = jnp.dot(xs_ref[pl.ds(r0, MOE_SUB), :], wgu_bf[...], preferred_element_type=f32) + bgu_ref[0]
            x_glu = jnp.minimum(h[:, :D_FF], SWIGLU_LIMIT)
            x_lin = jnp.clip(h[:, D_FF:], -SWIGLU_LIMIT, SWIGLU_LIMIT)
            act = x_glu * jax.nn.sigmoid(SWIGLU_ALPHA * x_glu) * (x_lin + 1.0)
            y = jnp.dot(act.astype(bf16), wd_bf[...], preferred_element_type=f32) + bd_ref[0]
            row = blk0 + r0 + lax.broadcasted_iota(jnp.int32, (MOE_SUB, 1), 0)
            mine = (row >= lo) & (row < hi)
            o_ref[pl.ds(r0, MOE_SUB), :] = jnp.where(mine, y.astype(o_ref.dtype), o_ref[pl.ds(r0, MOE_SUB), :])
            return carry

        lax.fori_loop(first, last, sub_block, 0)


def _moe_experts(vb, ve, lo, hi, nv, xs, wgu, bgu, wd, bd):
    a, d = xs.shape
    nvis = vb.shape[0]
    return pl.pallas_call(
        _moe_kernel,
        grid_spec=pltpu.PrefetchScalarGridSpec(
            num_scalar_prefetch=5,
            grid=(nvis,),
            in_specs=[pl.BlockSpec((MOE_BLOCK, d), lambda v, vb, ve, *_: (vb[v], 0)),
                      pl.BlockSpec((1, d, 2 * D_FF), lambda v, vb, ve, *_: (ve[v], 0, 0)),
                      pl.BlockSpec((1, 1, 2 * D_FF), lambda v, vb, ve, *_: (ve[v], 0, 0)),
                      pl.BlockSpec((1, D_FF, d), lambda v, vb, ve, *_: (ve[v], 0, 0)),
                      pl.BlockSpec((1, 1, d), lambda v, vb, ve, *_: (ve[v], 0, 0))],
            out_specs=pl.BlockSpec((MOE_BLOCK, d), lambda v, vb, ve, *_: (vb[v], 0)),
            scratch_shapes=[pltpu.VMEM((d, 2 * D_FF), bf16), pltpu.VMEM((D_FF, d), bf16)],
        ),
        out_shape=jax.ShapeDtypeStruct((a, d), bf16),
        compiler_params=_cparams("arbitrary"),
        name="moe_experts",
    )(vb, ve, lo, hi, nv, xs, wgu, bgu, wd, bd)


def _visit_table(counts, n_rows):
    end = jnp.cumsum(counts)
    start = end - counts
    nblk = n_rows // MOE_BLOCK
    first_blk = start // MOE_BLOCK
    last_blk = jnp.where(counts > 0, (end - 1) // MOE_BLOCK, first_blk - 1)
    per_e = jnp.maximum(last_blk - first_blk + 1, 0)
    v_end = jnp.cumsum(per_e)
    v_start = v_end - per_e
    nvis = nblk + N_EXPERTS - 1
    v = jnp.arange(nvis, dtype=jnp.int32)
    n_live = v_end[-1]
    vc = jnp.minimum(v, n_live - 1)
    ve = jnp.sum((v_end[None, :] <= vc[:, None]).astype(jnp.int32), axis=1)
    vb = (first_blk[ve] + vc - v_start[ve]).astype(jnp.int32)
    return vb, ve, start[ve].astype(jnp.int32), end[ve].astype(jnp.int32), n_live.astype(jnp.int32).reshape(1), start


def _rope_tables(positions):
    half = ROPE_DIM // 2
    inv_freq = jnp.exp(-math.log(ROPE_THETA) * jnp.arange(0, ROPE_DIM, 2, dtype=f32) / ROPE_DIM)
    ang = positions.astype(f32)[:, None, :] * inv_freq[None, :, None]
    cos_t, sin_t = jnp.cos(ang), jnp.sin(ang)
    d = jnp.arange(LANES) % HEAD_DIM
    f = jnp.arange(LANES)[:, None]
    first, second = d[None, :] == f % half, d[None, :] == f % half + half
    to_cos = jnp.where((f < half) & (first | second), 1.0, 0.0)
    to_sin = jnp.where((f >= half) & (f < 2 * half), jnp.where(second, 1.0, 0.0) - jnp.where(first, 1.0, 0.0), 0.0)
    spread = jnp.concatenate([to_cos, to_sin], axis=1).astype(bf16)
    idx = jnp.arange(half)
    perm = jnp.zeros((HEAD_DIM, HEAD_DIM), f32).at[idx + half, idx].set(1.0).at[idx, idx + half].set(1.0)
    return cos_t, sin_t, spread, perm


def _layer(x, positions, norm1_g, w_in, q_norm_g, k_norm_g, cmp_pos_k, cmp_w1_k, cmp_w2_k,
           cmp_pos_v, cmp_w1_v, cmp_w2_v, gla_w_gate, gla_b_gate, gla_norm_g, w_out, norm2_g,
           w_router, b_router, w_gate_up, b_gate_up, w_down, b_down):
    b, s, d = x.shape
    m = b * s
    hk, g, dh = NSA_KV_HEADS, NSA_GROUP, HEAD_DIM
    x2 = x.reshape(m, d)

    offs = [0]
    for sz in IN_SIZES:
        offs.append(offs[-1] + sz)
    seg = lambda i: w_in[:, offs[i]:offs[i + 1]]
    w_main = jnp.concatenate([seg(i) for i in (0, 1, 2, 3, 4, 5, 6, 8, 9, 10, 12)], axis=1).astype(bf16)
    padw = lambda t: jnp.concatenate([t, jnp.zeros((d, LANES - t.shape[1]), f32)], axis=1)
    ng = g * 3
    w_small = jnp.concatenate([padw(seg(7)[:, :ng]), padw(seg(7)[:, ng:]), padw(seg(11))], axis=1).astype(bf16)
    proj, proj_s = _inproj(x2, norm1_g.reshape(1, d), w_main, w_small)
    proj = proj.reshape(b, s, -1)
    proj_s = proj_s.reshape(b, s, 3 * LANES)

    cos_t, sin_t, spread, perm = _rope_tables(positions)
    per_kv = lambda t: jnp.concatenate([t] * hk, axis=-1)
    eye = jnp.eye(hk, dtype=f32)
    bdiag = lambda t: jnp.kron(eye, t)
    w1_bd = lambda w1: jax.vmap(bdiag)(w1.reshape(CMP_BLOCK, dh, CMP_HIDDEN)).astype(bf16)
    kc, vct, ks_rot, vst, kw_rot, vwt = _kvprep(
        proj, jnp.concatenate([cos_t, sin_t], axis=1), spread, per_kv(k_norm_g),
        bdiag(jnp.ones((dh, dh), f32)).astype(bf16), bdiag(perm).astype(bf16),
        jnp.stack([jnp.eye(hk * dh, LANES, k=-h * dh, dtype=f32) * (jnp.arange(LANES) < dh)
                   for h in range(hk)]).astype(bf16),
        per_kv(cmp_pos_k), w1_bd(cmp_w1_k), bdiag(cmp_w2_k).astype(bf16),
        per_kv(cmp_pos_v), w1_bd(cmp_w1_v), bdiag(cmp_w2_v).astype(bf16))
    score_bound = (1.05 * dh * dh ** -0.5 * math.log2(math.e) * jnp.max(jnp.abs(q_norm_g)) *
                   jnp.max(jnp.abs(k_norm_g), axis=1))
    bounds = jnp.zeros((8, LANES), f32).at[0, :3].set(score_bound)
    nsa_args = (proj, cos_t, sin_t, q_norm_g.reshape(dh, 1), bounds, kc, vct, ks_rot, vst, kw_rot, vwt, proj_s)
    nsa_out = lax.cond(2.0 * jnp.max(score_bound) < NSA_SAFE_EXPONENT,
                       functools.partial(_nsa_attention, bounded=True),
                       functools.partial(_nsa_attention, bounded=False), *nsa_args).reshape(m, NSA_WIDTH)

    gla_out = _gla(proj, proj_s, gla_w_gate, gla_b_gate.reshape(1, -1),
                   gla_norm_g.reshape(1, GLA_VAL_DIM)).reshape(m, GLA_WIDTH)

    padr = lambda t: jnp.concatenate([t, jnp.zeros(t.shape[:-1] + (LANES - t.shape[-1],), t.dtype)], axis=-1)
    wr_hi = w_router.astype(bf16)
    wr_lo = (w_router - wr_hi.astype(f32)).astype(bf16)
    x1, h2, rt, cnt = _outproj_router(nsa_out, gla_out, x2, w_out.astype(bf16), norm2_g.reshape(1, d),
                                      padr(wr_hi), padr(wr_lo), padr(b_router.reshape(1, -1)))

    top_idx = rt[:TOP_K].astype(jnp.int32)
    gate = rt[TOP_K:2 * TOP_K]
    rank = rt[2 * TOP_K:3 * TOP_K].astype(jnp.int32)
    counts = cnt[:, 0].astype(jnp.int32)
    a = m * TOP_K
    vb, ve, lo, hi, nv, start = _visit_table(counts, a)
    pos = rank
    for e in range(N_EXPERTS):
        pos = pos + jnp.where(top_idx == e, start[e], 0)
    tok = jnp.broadcast_to(jnp.arange(m, dtype=jnp.int32), (TOP_K, m))
    sorted_tok = jnp.zeros((a,), jnp.int32).at[pos.reshape(-1)].add(
        tok.reshape(-1), unique_indices=True, mode='promise_in_bounds')
    rows_of = lambda t, idx: t.at[idx].get(mode='promise_in_bounds')
    xs = rows_of(h2, sorted_tok)
    ys = _moe_experts(vb, ve, lo, hi, nv, xs, w_gate_up, b_gate_up.reshape(N_EXPERTS, 1, -1),
                      w_down, b_down.reshape(N_EXPERTS, 1, -1))
    out = x1
    for k in range(TOP_K):
        out = out + rows_of(ys, pos[k]).astype(f32) * gate[k][:, None]
    return out.reshape(b, s, d)


def kernel(x, positions, norm1_g, w_in, nsa_q_norm_g, nsa_k_norm_g, cmp_pos_k, cmp_w1_k, cmp_w2_k,
           cmp_pos_v, cmp_w1_v, cmp_w2_v, gla_w_gate, gla_b_gate, gla_norm_g, w_out, norm2_g,
           w_router, b_router, w_gate_up, b_gate_up, w_down, b_down):
    for l in range(norm1_g.shape[0]):
        x = _layer(x, positions, norm1_g[l], w_in[l], nsa_q_norm_g[l], nsa_k_norm_g[l],
                   cmp_pos_k[l], cmp_w1_k[l], cmp_w2_k[l], cmp_pos_v[l], cmp_w1_v[l], cmp_w2_v[l],
                   gla_w_gate[l], gla_b_gate[l], gla_norm_g[l], w_out[l], norm2_g[l],
                   w_router[l], b_router[l], w_gate_up[l], b_gate_up[l], w_down[l], b_down[l])
    return x
```

```python
import functools
import math

import jax
import jax.numpy as jnp
from jax import lax
from jax.experimental import pallas as pl
from jax.experimental.pallas import tpu as pltpu

f32 = jnp.float32
bf16 = jnp.bfloat16

NSA_HEADS = 8
NSA_KV_HEADS = 2
NSA_GROUP = NSA_HEADS // NSA_KV_HEADS
HEAD_DIM = 64
CMP_BLOCK = 32
CMP_STRIDE = 16
CMP_HIDDEN = 256
SEL_BLOCK = 64
SEL_TOPK = 8
WINDOW = 512
FORCE_BONUS = 1e4
GLA_HEADS = 4
GLA_KEY_DIM = 64
GLA_VAL_DIM = 128
GLA_CHUNK = 64
GLA_GATE_RANK = 16
GLA_TAU = 16.0
ROPE_THETA = 500000.0
ROPE_DIM = HEAD_DIM // 4
N_EXPERTS = 32
TOP_K = 4
D_FF = 1024
SWIGLU_LIMIT = 7.0
SWIGLU_ALPHA = 1.702
EPS = 1e-6
NEG_INF = -1e30

NSA_WIDTH = NSA_HEADS * HEAD_DIM
NSA_KV_WIDTH = NSA_KV_HEADS * HEAD_DIM
GLA_KEY_WIDTH = GLA_HEADS * GLA_KEY_DIM
GLA_WIDTH = GLA_HEADS * GLA_VAL_DIM
IN_SIZES = (NSA_WIDTH,) + (NSA_KV_WIDTH,) * 6 + (
    NSA_HEADS * 3, GLA_KEY_WIDTH, GLA_KEY_WIDTH, GLA_WIDTH, GLA_GATE_RANK, GLA_WIDTH)
MAIN_SIZES = (NSA_WIDTH,) + (NSA_KV_WIDTH,) * 6 + (GLA_KEY_WIDTH, GLA_KEY_WIDTH, GLA_WIDTH, GLA_WIDTH)
MAIN_IN_SEG = (0, 1, 2, 3, 4, 5, 6, 8, 9, 10, 12)
MAIN_LAYOUT = (0, 9, 10, 1, 2, 3, 4, 5, 6, 7, 8)
MAIN_OFFS = tuple(sum(MAIN_SIZES[j] for j in MAIN_LAYOUT[:MAIN_LAYOUT.index(i)]) for i in range(len(MAIN_SIZES)))

LANES = 128
VMEM_LIMIT = 48 * 1024 * 1024
MOE_BLOCK = 1024
MOE_SUB = 256
MOE_SLICES = 4
NSA_SAFE_EXPONENT = 120.0
GLA_UNROLL = 4
ROUTE_ROWS = 16

_NT = (((1,), (1,)), ((), ()))
_TN = (((0,), (0,)), ((), ()))


def _cparams(*sem):
    return pltpu.CompilerParams(dimension_semantics=sem, vmem_limit_bytes=VMEM_LIMIT)


def _rms(t, g):
    return t * lax.rsqrt(jnp.mean(t * t, axis=-1, keepdims=True) + EPS) * g


def _split_bf16(t):
    hi = t.astype(bf16)
    lo = (t - hi.astype(f32)).astype(bf16)
    return hi, lo


def _rope(t, cos_f, sin_f, perm):
    hi, lo = _split_bf16(t)
    rot = (jnp.dot(hi, perm, preferred_element_type=f32) +
           jnp.dot(lo, perm, preferred_element_type=f32))
    return t * cos_f + rot * sin_f


def _inproj_kernel(x_ref, g_ref, w_ref, ws_ref, o_ref, os_ref):
    x = x_ref[...]
    h = _rms(x, g_ref[...]).astype(bf16)
    n = o_ref.shape[1]
    step = 512
    for c in range(0, n, step):
        e = min(c + step, n)
        o_ref[:, c:e] = jnp.dot(h, w_ref[:, c:e], preferred_element_type=f32).astype(bf16)
    os_ref[...] = jnp.dot(h, ws_ref[...], preferred_element_type=f32)


def _inproj(x2, g, w_main, w_small, tm=512):
    m, d = x2.shape
    n = w_main.shape[1]
    ns = w_small.shape[1]
    return pl.pallas_call(
        _inproj_kernel,
        grid=(m // tm,),
        in_specs=[pl.BlockSpec((tm, d), lambda i: (i, 0)),
                  pl.BlockSpec((1, d), lambda i: (0, 0)),
                  pl.BlockSpec((d, n), lambda i: (0, 0)),
                  pl.BlockSpec((d, ns), lambda i: (0, 0))],
        out_specs=[pl.BlockSpec((tm, n), lambda i: (i, 0)),
                   pl.BlockSpec((tm, ns), lambda i: (i, 0))],
        out_shape=[jax.ShapeDtypeStruct((m, n), bf16),
                   jax.ShapeDtypeStruct((m, ns), f32)],
        compiler_params=_cparams("parallel"),
        name="inproj",
    )(x2, g, w_main, w_small)


def _kvprep_kernel(kc_ref, vc_ref, ks_ref, vs_ref, kw_ref, vw_ref, cs_ref, spread_ref, kg_ref, ones_ref,
                   perm_ref, pick_ref, posk_ref, w1k_ref, w2k_ref, posv_ref, w1v_ref, w2v_ref,
                   kco_ref, vcto_ref, kso_ref, vsto_ref, kwo_ref, vwto_ref, tmp_ref):
    ones_bd = ones_ref[...]
    nh_out = kco_ref.shape[2]

    def rms_heads(t, g):
        sq_hi, sq_lo = _split_bf16(t * t)
        ss = (jnp.dot(sq_hi, ones_bd, preferred_element_type=f32) +
              jnp.dot(sq_lo, ones_bd, preferred_element_type=f32))
        return t * lax.rsqrt(ss * (1.0 / HEAD_DIM) + EPS) * g

    def compress(src_ref, pos_ref, w1_ref, w2_ref):
        tmp_ref[...] = src_ref[0].astype(f32)
        nh = tmp_ref.shape[0] // CMP_STRIDE
        a = jnp.zeros((nh, NSA_KV_HEADS * CMP_HIDDEN), f32)
        b = jnp.zeros((nh, NSA_KV_HEADS * CMP_HIDDEN), f32)
        for p in range(CMP_STRIDE):
            rows = tmp_ref[pl.ds(p, nh, stride=CMP_STRIDE), :]
            a = a + jnp.dot((rows + pos_ref[p:p + 1, :]).astype(bf16), w1_ref[p], preferred_element_type=f32)
            q = CMP_STRIDE + p
            b = b + jnp.dot((rows + pos_ref[q:q + 1, :]).astype(bf16), w1_ref[q], preferred_element_type=f32)
        pre = a + pltpu.roll(b, nh - 1, 0)
        hid = pre * jax.nn.sigmoid(pre)
        return jnp.dot(hid.astype(bf16), w2_ref[...], preferred_element_type=f32)

    kc = rms_heads(compress(kc_ref, posk_ref, w1k_ref, w2k_ref), kg_ref[0:1, :]).astype(bf16)
    tmp_ref[:nh_out, :] = compress(vc_ref, posv_ref, w1v_ref, w2v_ref)
    vcto_ref[0] = tmp_ref[:nh_out, :].T.astype(bf16)
    s, w = ks_ref.shape[1], ks_ref.shape[2]
    cs = jnp.concatenate([cs_ref[0], jnp.zeros((LANES - cs_ref.shape[1], s), f32)], axis=0).T
    cs_hi = cs.astype(bf16)
    cs_mid, cs_lo = _split_bf16(cs - cs_hi.astype(f32))
    spread = spread_ref[...]
    tables = (jnp.dot(cs_hi, spread, preferred_element_type=f32) + jnp.dot(cs_mid, spread, preferred_element_type=f32) +
              jnp.dot(cs_lo, spread, preferred_element_type=f32))
    lane_row = lax.broadcasted_iota(jnp.int32, (1, LANES), 1)
    cos_f = tables[:, :LANES] + jnp.where(lane_row % HEAD_DIM >= ROPE_DIM, 1.0, 0.0)
    sin_f = tables[:, LANES:]
    perm = perm_ref[...]
    ks = _rope(rms_heads(ks_ref[0].astype(f32), kg_ref[1:2, :]), cos_f, sin_f, perm).astype(bf16)
    kw = _rope(rms_heads(kw_ref[0].astype(f32), kg_ref[2:3, :]), cos_f, sin_f, perm).astype(bf16)
    row = lax.broadcasted_iota(jnp.int32, (s, LANES), 0)
    lane = lax.broadcasted_iota(jnp.int32, (s, LANES), 1)
    last_lane_one = lambda n: jnp.where(lax.broadcasted_iota(jnp.int32, (n, LANES), 1) == LANES - 1, 1.0, 0.0)
    ones_lane = last_lane_one(s)
    block_onehot = jnp.where(row // SEL_BLOCK + HEAD_DIM == lane, 1.0, 0.0) + ones_lane
    pad_lane = lax.broadcasted_iota(jnp.int32, (WINDOW, LANES), 1)
    pad_flag = jnp.where((pad_lane == HEAD_DIM) | (pad_lane == LANES - 1), 1.0, 0.0)
    for h in range(NSA_KV_HEADS):
        pick = pick_ref[h]
        kco_ref[0, h] = (jnp.dot(kc, pick, preferred_element_type=f32) + last_lane_one(nh_out)).astype(bf16)
        kso_ref[0, h] = (jnp.dot(ks, pick, preferred_element_type=f32) + block_onehot).astype(bf16)
        kwo_ref[0, h, :WINDOW, :] = pad_flag.astype(bf16)
        kwo_ref[0, h, WINDOW:, :] = (jnp.dot(kw, pick, preferred_element_type=f32) + ones_lane).astype(bf16)
    vsto_ref[0] = vs_ref[0].astype(f32).T.astype(bf16)
    vwto_ref[0] = jnp.concatenate([jnp.zeros((w, WINDOW), f32), vw_ref[0].astype(f32).T], axis=1).astype(bf16)


def _kvprep(proj, cs_t, spread, kg2, ones_bd, perm2, pick, posk, w1k, w2k, posv, w1v, w2v):
    b, s, _ = proj.shape
    w = NSA_KV_WIDTH
    assert w == LANES
    nh = s // CMP_STRIDE
    col = lambda i: pl.BlockSpec((1, s, w), lambda n: (n, 0, MAIN_OFFS[i] // w))
    whole = lambda a: pl.BlockSpec(a.shape, lambda n: (0,) * a.ndim)
    tab = pl.BlockSpec((1, cs_t.shape[1], s), lambda n: (n, 0, 0))
    out = lambda *shp: (pl.BlockSpec((1,) + shp, lambda n: (n,) + (0,) * len(shp)),
                        jax.ShapeDtypeStruct((b,) + shp, bf16))
    hk = NSA_KV_HEADS
    outs = [out(hk, nh, LANES), out(w, nh), out(hk, s, LANES), out(w, s), out(hk, WINDOW + s, LANES),
            out(w, WINDOW + s)]
    return pl.pallas_call(
        _kvprep_kernel,
        grid=(b,),
        in_specs=[col(1), col(2), col(3), col(4), col(5), col(6), tab, whole(spread), whole(kg2), whole(ones_bd),
                  whole(perm2), whole(pick), whole(posk), whole(w1k), whole(w2k), whole(posv), whole(w1v),
                  whole(w2v)],
        out_specs=[o[0] for o in outs],
        out_shape=[o[1] for o in outs],
        scratch_shapes=[pltpu.VMEM((s, w), f32)],
        compiler_params=_cparams("parallel"),
        name="nsa_kvprep",
    )(proj, proj, proj, proj, proj, proj, cs_t, spread, kg2, ones_bd, perm2, pick, posk, w1k, w2k, posv, w1v, w2v)


def _nsa_kernel(q_ref, cos_ref, sin_ref, qg_ref, mb_ref, kc_ref, vct_ref, ks_ref, vst_ref, kw_ref, vwt_ref,
                gate_ref, o_ref, *, tq, ck, bounded):
    g = NSA_GROUP
    qi = pl.program_id(2)
    t0 = pl.multiple_of(qi * tq, tq)
    scale = HEAD_DIM ** -0.5 * math.log2(math.e)
    per_head = lambda t: jnp.concatenate([t] * g, axis=1)

    qt = q_ref[0].astype(f32).T
    qt = jnp.concatenate([qt[i * HEAD_DIM:(i + 1) * HEAD_DIM] for i in range(g)], axis=1)
    qn = qt * lax.rsqrt(jnp.mean(qt * qt, axis=0, keepdims=True) + EPS) * qg_ref[...]
    half = ROPE_DIM // 2
    cos8, sin8 = per_head(cos_ref[0]), per_head(sin_ref[0])
    x1, x2 = qn[:half], qn[half:ROPE_DIM]
    q_rot = jnp.concatenate([x1 * cos8 - x2 * sin8, x2 * cos8 + x1 * sin8, qn[ROPE_DIM:]], axis=0)
    n_extra = kc_ref.shape[3] - HEAD_DIM

    def with_features(qb, feats, branch):
        used = sum(f.shape[0] for f in feats)
        last = jnp.broadcast_to(-mb_ref[0:1, branch:branch + 1] if bounded else 0.0, (1, g * tq))
        fill = jnp.zeros((n_extra - used - 1, g * tq), f32)
        return jnp.concatenate([qb] + feats + [fill, last], axis=0).astype(bf16)

    q_cmp = with_features(qn * scale, [], 0)
    q_rot = q_rot * scale
    tq_row = t0 + lax.broadcasted_iota(jnp.int32, (1, tq), 1)

    kc = kc_ref[0, 0]
    ncp = kc.shape[0]
    n_col = lax.broadcasted_iota(jnp.int32, (ncp, 1), 0)
    ok_c = (n_col * CMP_STRIDE + (CMP_BLOCK - 1)) <= tq_row
    s_c = jnp.dot(kc, q_cmp, preferred_element_type=f32) + per_head(jnp.where(ok_c, 0.0, NEG_INF))
    if bounded:
        p_c = jnp.exp2(s_c)
    else:
        m_c = jnp.max(s_c, axis=0, keepdims=True)
        p_c = jnp.exp2(s_c - m_c) * per_head(jnp.where(ok_c, 1.0, 0.0))
    den = jnp.sum(p_c, axis=0, keepdims=True)
    p_c = p_c * (1.0 / jnp.where(den > 0, den, 1.0))
    o_cmp = jnp.dot(vct_ref[0], p_c.astype(bf16), preferred_element_type=f32)

    pg = p_c[:, :tq]
    for i in range(1, g):
        pg = pg + p_c[:, i * tq:(i + 1) * tq]
    nj = ks_ref.shape[2] // SEL_BLOCK
    jj = lax.broadcasted_iota(jnp.int32, (nj, ncp), 0)
    nn = lax.broadcasted_iota(jnp.int32, (nj, ncp), 1)
    overlap = (nn * CMP_STRIDE < (jj + 1) * SEL_BLOCK) & (nn * CMP_STRIDE + CMP_BLOCK > jj * SEL_BLOCK)
    overlap = jnp.where(overlap, 1.0, 0.0).astype(bf16)
    pg_hi, pg_lo = _split_bf16(pg)
    imp = (jnp.dot(overlap, pg_hi, preferred_element_type=f32) +
           jnp.dot(overlap, pg_lo, preferred_element_type=f32))
    j_col = lax.broadcasted_iota(jnp.int32, (nj, 1), 0)
    j_f = j_col.astype(f32)
    cur = tq_row // SEL_BLOCK
    valid = j_col <= cur
    forced = (j_col == 0) | (j_col == cur) | (j_col == cur - 1)
    score = jnp.where(valid, imp + jnp.where(forced, FORCE_BONUS, 0.0), NEG_INF)
    sel = jnp.zeros((nj, tq), f32)
    for _ in range(SEL_TOPK):
        m = jnp.max(score, axis=0, keepdims=True)
        first = jnp.min(jnp.where(score == m, j_f, float(nj)), axis=0, keepdims=True)
        hit = j_f == first
        sel = jnp.where(hit, 1.0, sel)
        score = jnp.where(hit, -jnp.inf, score)
    sel = jnp.where(valid, sel, 0.0)

    def weighted_values(vt, p):
        lhs = jnp.concatenate([vt, jnp.ones((16, vt.shape[1]), bf16)], axis=0)
        r = jnp.dot(lhs, p.astype(bf16), preferred_element_type=f32)
        return r[:HEAD_DIM], r[HEAD_DIM:HEAD_DIM + 1]

    wl = WINDOW + tq
    q_win = with_features(q_rot, [jnp.full((1, g * tq), NEG_INF, f32)], 2)
    s_w = jnp.dot(kw_ref[0, 0, pl.ds(t0, wl), :], q_win, preferred_element_type=f32)
    step = lax.broadcasted_iota(jnp.int32, (tq, 1), 0)
    lo_ok = (t0 - WINDOW + step) > (tq_row - WINDOW)
    hi_ok = (t0 + step) <= tq_row
    parts = [s_w[:tq] + per_head(jnp.where(lo_ok, 0.0, NEG_INF))]
    if wl > 2 * tq:
        parts.append(s_w[tq:wl - tq])
    parts.append(s_w[wl - tq:] + per_head(jnp.where(hi_ok, 0.0, NEG_INF)))
    if bounded:
        p_w = jnp.concatenate([jnp.exp2(t) for t in parts], axis=0)
    else:
        m_w = functools.reduce(jnp.maximum, [jnp.max(t, axis=0, keepdims=True) for t in parts])
        p_w = jnp.concatenate([jnp.exp2(t - m_w) for t in parts], axis=0)
    o_win, l_w = weighted_values(vwt_ref[0, :, pl.ds(t0, wl)], p_w)
    o_win = o_win * (1.0 / l_w)

    q_sel = with_features(q_rot, [per_head(jnp.where(sel > 0.5, 0.0, NEG_INF))], 1)

    def sel_step(carry, k0, diagonal):
        m_prev, l_prev, acc = carry
        s = jnp.dot(ks_ref[0, 0, pl.ds(k0, ck), :], q_sel, preferred_element_type=f32)
        if diagonal:
            kpos = k0 + lax.broadcasted_iota(jnp.int32, (ck, 1), 0)
            s = s + per_head(jnp.where(kpos <= tq_row, 0.0, NEG_INF))
        if bounded:
            pv, p_sum = weighted_values(vst_ref[0, :, pl.ds(k0, ck)], jnp.exp2(s))
            return m_prev, l_prev + p_sum, acc + pv
        m_new = jnp.maximum(m_prev, jnp.max(s, axis=0, keepdims=True))
        alpha = jnp.exp2(m_prev - m_new)
        pv, p_sum = weighted_values(vst_ref[0, :, pl.ds(k0, ck)], jnp.exp2(s - m_new))
        return m_new, alpha * l_prev + p_sum, alpha * acc + pv

    init = (jnp.full((1, g * tq), NEG_INF, f32), jnp.zeros((1, g * tq), f32),
            jnp.zeros((HEAD_DIM, g * tq), f32))
    last = (t0 + tq + ck - 1) // ck - 1
    carry = lax.fori_loop(0, last, lambda c, cr: sel_step(cr, pl.multiple_of(c * ck, ck), False), init)
    _, l_s, acc_s = sel_step(carry, pl.multiple_of(last * ck, ck), True)
    o_sel = acc_s * (1.0 / l_s)

    gt = gate_ref[0].T
    gate = lambda j: jax.nn.sigmoid(jnp.concatenate([gt[i * 3 + j:i * 3 + j + 1] for i in range(g)], axis=1))
    ot = gate(0) * o_cmp + gate(1) * o_sel + gate(2) * o_win
    o2 = jnp.concatenate([ot[:, i * tq:(i + 1) * tq] for i in range(g)], axis=0)
    o_ref[0] = o2.T.astype(o_ref.dtype)


def _nsa_attention(proj, cos_t, sin_t, qg, bounds, kc, vct, ks, vst, kw, vwt, gate_logits, *, bounded,
                   tq=256, ck=512):
    b, s, _ = proj.shape
    hk, g, dh = NSA_KV_HEADS, NSA_GROUP, HEAD_DIM
    assert WINDOW >= tq and ck % tq == 0 and s % ck == 0
    tab = pl.BlockSpec((1, ROPE_DIM // 2, tq), lambda i, j, t: (i, 0, t))
    assert s // SEL_BLOCK <= kc.shape[3] - dh
    keys = lambda a: pl.BlockSpec((1, 1) + a.shape[2:], lambda i, j, t: (i, j, 0, 0))
    vals = lambda a: pl.BlockSpec((1, dh, a.shape[2]), lambda i, j, t: (i, j, 0))
    return pl.pallas_call(
        functools.partial(_nsa_kernel, tq=tq, ck=ck, bounded=bounded),
        grid=(b, hk, s // tq),
        in_specs=[pl.BlockSpec((1, tq, g * dh), lambda i, j, t: (i, t, j)),
                  tab, tab, pl.BlockSpec((dh, 1), lambda i, j, t: (0, 0)),
                  pl.BlockSpec(bounds.shape, lambda i, j, t: (0, 0)),
                  keys(kc), vals(vct), keys(ks), vals(vst), keys(kw), vals(vwt),
                  pl.BlockSpec((1, tq, LANES), lambda i, j, t: (i, t, j))],
        out_specs=pl.BlockSpec((1, tq, g * dh), lambda i, j, t: (i, t, j)),
        out_shape=jax.ShapeDtypeStruct((b, s, NSA_WIDTH), bf16),
        compiler_params=_cparams("parallel", "parallel", "arbitrary"),
        name="nsa_attention_bounded" if bounded else "nsa_attention",
    )(proj, cos_t, sin_t, qg, bounds, kc, vct, ks, vst, kw, vwt, gate_logits)


def _gla_kernel(q_ref, k_ref, v_ref, lr_ref, wg_ref, bg_ref, r_ref, ng_ref, o_ref, *, grp, npair):
    c = GLA_CHUNK
    s = q_ref.shape[1]
    dk, dv = GLA_KEY_DIM, GLA_VAL_DIM
    rows = grp * c
    ri = lax.broadcasted_iota(jnp.int32, (rows, rows), 0)
    ci = lax.broadcasted_iota(jnp.int32, (rows, rows), 1)
    causal = (ci <= ri) & (ci // c == ri // c)
    tri = jnp.where(causal, 1.0, 0.0).astype(bf16)
    lane = lax.broadcasted_iota(jnp.int32, (1, 2 * dk), 1)
    head_mask = [lane < dk, lane >= dk]
    chunk_of_row = lax.broadcasted_iota(jnp.int32, (rows, 1), 0) // c
    wg = wg_ref[...]
    bg = bg_ref[...]
    ng = ng_ref[...]

    def body(n, states):
        r0 = pl.multiple_of(n * rows, rows)
        z = jnp.dot(lr_ref[0, pl.ds(r0, rows), :GLA_GATE_RANK], wg, preferred_element_type=f32,
                    precision=lax.Precision.HIGHEST) + bg
        log_a = -(jnp.maximum(-z, 0.0) + jnp.log(1.0 + jnp.exp(-jnp.abs(z)))) / GLA_TAU
        la_hi = log_a.astype(bf16)
        la_mid, la_lo = _split_bf16(log_a - la_hi.astype(f32))
        bcum = (jnp.dot(tri, la_hi, preferred_element_type=f32) + jnp.dot(tri, la_mid, preferred_element_type=f32) +
                jnp.dot(tri, la_lo, preferred_element_type=f32))
        qf = q_ref[0, pl.ds(r0, rows), :].astype(f32) * (dk ** -0.5)
        kf = k_ref[0, pl.ds(r0, rows), :].astype(f32)
        q_dec = qf * jnp.exp(bcum)
        k_dec = (kf * jnp.exp(-bcum)).astype(bf16)
        b_last = [bcum[(i + 1) * c - 1:(i + 1) * c, :] for i in range(grp)]
        k_state = jnp.concatenate(
            [kf[i * c:(i + 1) * c] * jnp.exp(b_last[i] - bcum[i * c:(i + 1) * c]) for i in range(grp)],
            axis=0).astype(bf16)
        by_chunk = lambda t: jnp.concatenate(
            [jnp.where(chunk_of_row == i, t, jnp.zeros_like(t)) for i in range(grp)], axis=1)
        new_states = []
        for hh in range(2 * npair):
            pr, h = divmod(hh, 2)
            pair = slice(pr * 2 * dk, (pr + 1) * 2 * dk)
            q_h = jnp.where(head_mask[h], q_dec[:, pair], 0.0).astype(bf16)
            v_h = v_ref[0, pl.ds(r0, rows), hh * dv:(hh + 1) * dv]
            attn = lax.dot_general(q_h, k_dec[:, pair], _NT, preferred_element_type=f32)
            attn = jnp.where(causal, attn, 0.0).astype(bf16)
            o_intra = jnp.dot(attn, v_h, preferred_element_type=f32)
            u_all = lax.dot_general(v_h, by_chunk(k_state[:, pair]), _TN, preferred_element_type=f32)
            st = states[hh]
            entering = []
            for i in range(grp):
                entering.append(st)
                st = st * jnp.exp(b_last[i][:, pair]) + u_all[:, i * 2 * dk:(i + 1) * 2 * dk]
            new_states.append(st)
            o_inter = lax.dot_general(by_chunk(q_h), jnp.concatenate(entering, axis=1).astype(bf16), _NT,
                                      preferred_element_type=f32)
            o = _rms(o_intra + o_inter, ng)
            rr = r_ref[0, pl.ds(r0, rows), hh * dv:(hh + 1) * dv].astype(f32)
            o_ref[0, pl.ds(r0, rows), hh * dv:(hh + 1) * dv] = (o * (rr * jax.nn.sigmoid(rr))).astype(o_ref.dtype)
        return tuple(new_states)

    zero = jnp.zeros((dv, 2 * dk), f32)
    lax.fori_loop(0, s // rows, body, (zero,) * (2 * npair), unroll=GLA_UNROLL)


def _gla(proj, proj_s, wg, bg, ng, grp=4, npair=GLA_HEADS // 2):
    b, s, _ = proj.shape
    dk2, dv2 = 2 * npair * GLA_KEY_DIM, 2 * npair * GLA_VAL_DIM
    oq, ok_, ov, orr = (MAIN_OFFS[7] // dk2, MAIN_OFFS[8] // dk2, MAIN_OFFS[9] // dv2, MAIN_OFFS[10] // dv2)
    assert all(MAIN_OFFS[i] % w == 0 for i, w in ((7, dk2), (8, dk2), (9, dv2), (10, dv2)))
    col = lambda w, o: pl.BlockSpec((1, s, w), lambda i, j: (i, 0, o + j))
    return pl.pallas_call(
        functools.partial(_gla_kernel, grp=grp, npair=npair),
        grid=(b, GLA_HEADS // (2 * npair)),
        in_specs=[col(dk2, oq), col(dk2, ok_), col(dv2, ov),
                  pl.BlockSpec((1, s, LANES), lambda i, j: (i, 0, 2)),
                  pl.BlockSpec((GLA_GATE_RANK, dk2), lambda i, j: (0, j)),
                  pl.BlockSpec((1, dk2), lambda i, j: (0, j)),
                  col(dv2, orr),
                  pl.BlockSpec((1, GLA_VAL_DIM), lambda i, j: (0, 0))],
        out_specs=pl.BlockSpec((1, s, dv2), lambda i, j: (i, 0, j)),
        out_shape=jax.ShapeDtypeStruct((b, s, GLA_WIDTH), bf16),
        compiler_params=_cparams("parallel", "parallel"),
        name="gla",
    )(proj, proj, proj, proj_s, wg, bg, proj, ng)


def _outproj_kernel(a_ref, b_ref, x_ref, wo_ref, g2_ref, wrh_ref, wrl_ref, br_ref,
                    x1_ref, h2_ref, rt_ref, cnt_ref, carry_ref):
    @pl.when(pl.program_id(0) == 0)
    def _():
        carry_ref[...] = jnp.zeros_like(carry_ref)

    na = a_ref.shape[1]
    tm = a_ref.shape[0]
    y = (jnp.dot(a_ref[...], wo_ref[:na, :], preferred_element_type=f32) +
         jnp.dot(b_ref[...], wo_ref[na:, :], preferred_element_type=f32))
    x1 = x_ref[...] + y
    x1_ref[...] = x1
    h2 = _rms(x1, g2_ref[...])
    h2_ref[...] = h2.astype(bf16)
    hi, lo = _split_bf16(h2)
    logits = (jnp.dot(hi, wrh_ref[...], preferred_element_type=f32) +
              jnp.dot(lo, wrh_ref[...], preferred_element_type=f32) +
              jnp.dot(hi, wrl_ref[...], preferred_element_type=f32)) + br_ref[...]
    score = logits.T[:N_EXPERTS]
    e_col = lax.broadcasted_iota(jnp.int32, (N_EXPERTS, 1), 0).astype(f32)
    top = jnp.max(score, axis=0, keepdims=True)
    hits, firsts, weights = [], [], []
    for _ in range(TOP_K):
        m = jnp.max(score, axis=0, keepdims=True)
        first = jnp.min(jnp.where(score == m, e_col, float(N_EXPERTS)), axis=0, keepdims=True)
        hit = e_col == first
        hits.append(hit)
        firsts.append(first)
        weights.append(jnp.exp(m - top))
        score = jnp.where(hit, -jnp.inf, score)
    tot = weights[0] + weights[1] + weights[2] + weights[3]
    onehot = jnp.where(hits[0] | hits[1] | hits[2] | hits[3], 1.0, 0.0)
    ri = lax.broadcasted_iota(jnp.int32, (tm, tm), 0)
    ci = lax.broadcasted_iota(jnp.int32, (tm, tm), 1)
    earlier = jnp.where(ri < ci, 1.0, 0.0).astype(bf16)
    rank = carry_ref[...] + jnp.dot(onehot.astype(bf16), earlier, preferred_element_type=f32)
    carry_ref[...] = carry_ref[...] + jnp.sum(onehot, axis=1, keepdims=True)
    cnt_ref[...] = jnp.broadcast_to(carry_ref[...], cnt_ref.shape)
    ranks = [jnp.sum(jnp.where(hits[k], rank, 0.0), axis=0, keepdims=True) for k in range(TOP_K)]
    gates = [weights[k] / tot for k in range(TOP_K)]
    pad = jnp.zeros((rt_ref.shape[0] - 3 * TOP_K, tm), f32)
    rt_ref[...] = jnp.concatenate(firsts + gates + ranks + [pad], axis=0)


def _outproj_router(a, b, x2, wo, g2, wr_hi, wr_lo, br, tm=512):
    m, d = x2.shape
    na, nb = a.shape[1], b.shape[1]
    row = lambda n: pl.BlockSpec((tm, n), lambda i: (i, 0))
    whole = lambda t: pl.BlockSpec(t.shape, lambda i: (0,) * t.ndim)
    return pl.pallas_call(
        _outproj_kernel,
        grid=(m // tm,),
        in_specs=[row(na), row(nb), row(d), whole(wo), whole(g2), whole(wr_hi), whole(wr_lo), whole(br)],
        out_specs=[row(d), row(d), pl.BlockSpec((ROUTE_ROWS, tm), lambda i: (0, i)),
                   pl.BlockSpec((N_EXPERTS, LANES), lambda i: (0, 0))],
        out_shape=[jax.ShapeDtypeStruct((m, d), f32),
                   jax.ShapeDtypeStruct((m, d), bf16),
                   jax.ShapeDtypeStruct((ROUTE_ROWS, m), f32),
                   jax.ShapeDtypeStruct((N_EXPERTS, LANES), f32)],
        scratch_shapes=[pltpu.VMEM((N_EXPERTS, 1), f32)],
        compiler_params=_cparams("arbitrary"),
        name="outproj_router",
    )(a, b, x2, wo, g2, wr_hi, wr_lo, br)


def _moe_kernel(vb_ref, ve_ref, lo_ref, hi_ref, nv_ref, xs_ref, wgu_ref, bgu_ref, wd_ref, bd_ref, *rest):
    o_ref, wgu_bf, wd_bf = rest[-3:]
    v = pl.program_id(0)
    prev = jnp.maximum(v - 1, 0)
    live = v < nv_ref[0]
    new_expert = (v == 0) | (ve_ref[v] != ve_ref[prev])
    new_block = (v == 0) | (vb_ref[v] != vb_ref[prev])

    @pl.when(live & new_expert)
    def _():
        wgu_bf[...] = wgu_ref[0].astype(bf16)
        wd_bf[...] = wd_ref[0].astype(bf16)

    @pl.when(live & new_block)
    def _():
        o_ref[...] = jnp.zeros_like(o_ref)

    @pl.when(live)
    def _():
        blk0 = vb_ref[v] * MOE_BLOCK
        lo, hi = lo_ref[v], hi_ref[v]
        first = jnp.maximum(lo - blk0, 0) // MOE_SUB
        last = (jnp.minimum(hi - blk0, MOE_BLOCK) + MOE_SUB - 1) // MOE_SUB

        def sub_block(j, carry):
            r0 = pl.multiple_of(j * MOE_SUB, MOE_SUB)
            h = jnp.dot(xs_ref[pl.ds(r0, MOE_SUB), :], wgu_bf[...], preferred_element_type=f32) + bgu_ref[0]
            x_glu = jnp.minimum(h[:, :D_FF], SWIGLU_LIMIT)
            x_lin = jnp.clip(h[:, D_FF:], -SWIGLU_LIMIT, SWIGLU_LIMIT)
            act = x_glu * jax.nn.sigmoid(SWIGLU_ALPHA * x_glu) * (x_lin + 1.0)
            y = jnp.dot(act.astype(bf16), wd_bf[...], preferred_element_type=f32) + bd_ref[0]
            row = blk0 + r0 + lax.broadcasted_iota(jnp.int32, (MOE_SUB, 1), 0)
            mine = (row >= lo) & (row < hi)
            o_ref[pl.ds(r0, MOE_SUB), :] = jnp.where(mine, y.astype(o_ref.dtype), o_ref[pl.ds(r0, MOE_SUB), :])
            return carry

        lax.fori_loop(first, last, sub_block, 0)


def _moe_experts(table, blk_lo, xs, wgu, bgu, wd, bd, n_rows, ys_so_far=None):
    d = xs.shape[1]
    nvis = table[0].shape[0]
    in_specs = [pl.BlockSpec((MOE_BLOCK, d), lambda v, vb, ve, *_: (vb[v] - blk_lo, 0)),
                pl.BlockSpec((1, d, 2 * D_FF), lambda v, vb, ve, *_: (ve[v], 0, 0)),
                pl.BlockSpec((1, 1, 2 * D_FF), lambda v, vb, ve, *_: (ve[v], 0, 0)),
                pl.BlockSpec((1, D_FF, d), lambda v, vb, ve, *_: (ve[v], 0, 0)),
                pl.BlockSpec((1, 1, d), lambda v, vb, ve, *_: (ve[v], 0, 0))]
    operands = table + (xs, wgu, bgu, wd, bd)
    aliases = {}
    if ys_so_far is not None:
        in_specs.append(pl.BlockSpec(memory_space=pl.ANY))
        aliases = {len(operands): 0}
        operands += (ys_so_far,)
    return pl.pallas_call(
        _moe_kernel,
        grid_spec=pltpu.PrefetchScalarGridSpec(
            num_scalar_prefetch=5,
            grid=(nvis,),
            in_specs=in_specs,
            out_specs=pl.BlockSpec((MOE_BLOCK, d), lambda v, vb, ve, *_: (vb[v], 0)),
            scratch_shapes=[pltpu.VMEM((d, 2 * D_FF), bf16), pltpu.VMEM((D_FF, d), bf16)],
        ),
        out_shape=jax.ShapeDtypeStruct((n_rows, d), bf16),
        input_output_aliases=aliases,
        compiler_params=_cparams("arbitrary"),
        name="moe_experts",
    )(*operands)


def _visit_table(start, end, blk_lo, blk_hi):
    first_blk = jnp.maximum(start // MOE_BLOCK, blk_lo)
    last_blk = jnp.minimum(jnp.where(end > start, (end - 1) // MOE_BLOCK, -1), blk_hi - 1)
    per_e = jnp.maximum(last_blk - first_blk + 1, 0)
    v_end = jnp.cumsum(per_e)
    v_start = v_end - per_e
    nvis = blk_hi - blk_lo + N_EXPERTS - 1
    v = jnp.arange(nvis, dtype=jnp.int32)
    n_live = v_end[-1]
    vc = jnp.minimum(v, n_live - 1)
    ve = jnp.sum((v_end[None, :] <= vc[:, None]).astype(jnp.int32), axis=1)
    vb = (first_blk[ve] + vc - v_start[ve]).astype(jnp.int32)
    return vb, ve, start[ve].astype(jnp.int32), end[ve].astype(jnp.int32), n_live.astype(jnp.int32).reshape(1)


def _rope_tables(positions):
    half = ROPE_DIM // 2
    inv_freq = jnp.exp(-math.log(ROPE_THETA) * jnp.arange(0, ROPE_DIM, 2, dtype=f32) / ROPE_DIM)
    ang = positions.astype(f32)[:, None, :] * inv_freq[None, :, None]
    cos_t, sin_t = jnp.cos(ang), jnp.sin(ang)
    d = jnp.arange(LANES) % HEAD_DIM
    f = jnp.arange(LANES)[:, None]
    first, second = d[None, :] == f % half, d[None, :] == f % half + half
    to_cos = jnp.where((f < half) & (first | second), 1.0, 0.0)
    to_sin = jnp.where((f >= half) & (f < 2 * half), jnp.where(second, 1.0, 0.0) - jnp.where(first, 1.0, 0.0), 0.0)
    spread = jnp.concatenate([to_cos, to_sin], axis=1).astype(bf16)
    idx = jnp.arange(half)
    perm = jnp.zeros((HEAD_DIM, HEAD_DIM), f32).at[idx + half, idx].set(1.0).at[idx, idx + half].set(1.0)
    return cos_t, sin_t, spread, perm


def _layer(x, positions, norm1_g, w_in, q_norm_g, k_norm_g, cmp_pos_k, cmp_w1_k, cmp_w2_k,
           cmp_pos_v, cmp_w1_v, cmp_w2_v, gla_w_gate, gla_b_gate, gla_norm_g, w_out, norm2_g,
           w_router, b_router, w_gate_up, b_gate_up, w_down, b_down):
    b, s, d = x.shape
    m = b * s
    hk, g, dh = NSA_KV_HEADS, NSA_GROUP, HEAD_DIM
    x2 = x.reshape(m, d)

    offs = [0]
    for sz in IN_SIZES:
        offs.append(offs[-1] + sz)
    seg = lambda i: w_in[:, offs[i]:offs[i + 1]]
    w_main = jnp.concatenate([seg(MAIN_IN_SEG[i]) for i in MAIN_LAYOUT], axis=1).astype(bf16)
    padw = lambda t: jnp.concatenate([t, jnp.zeros((d, LANES - t.shape[1]), f32)], axis=1)
    ng = g * 3
    w_small = jnp.concatenate([padw(seg(7)[:, :ng]), padw(seg(7)[:, ng:]), padw(seg(11))], axis=1).astype(bf16)
    proj, proj_s = _inproj(x2, norm1_g.reshape(1, d), w_main, w_small)
    proj = proj.reshape(b, s, -1)
    proj_s = proj_s.reshape(b, s, 3 * LANES)

    cos_t, sin_t, spread, perm = _rope_tables(positions)
    per_kv = lambda t: jnp.concatenate([t] * hk, axis=-1)
    eye = jnp.eye(hk, dtype=f32)
    bdiag = lambda t: jnp.kron(eye, t)
    w1_bd = lambda w1: jax.vmap(bdiag)(w1.reshape(CMP_BLOCK, dh, CMP_HIDDEN)).astype(bf16)
    kc, vct, ks_rot, vst, kw_rot, vwt = _kvprep(
        proj, jnp.concatenate([cos_t, sin_t], axis=1), spread, per_kv(k_norm_g),
        bdiag(jnp.ones((dh, dh), f32)).astype(bf16), bdiag(perm).astype(bf16),
        jnp.stack([jnp.eye(hk * dh, LANES, k=-h * dh, dtype=f32) * (jnp.arange(LANES) < dh)
                   for h in range(hk)]).astype(bf16),
        per_kv(cmp_pos_k), w1_bd(cmp_w1_k), bdiag(cmp_w2_k).astype(bf16),
        per_kv(cmp_pos_v), w1_bd(cmp_w1_v), bdiag(cmp_w2_v).astype(bf16))
    score_bound = (1.05 * dh * dh ** -0.5 * math.log2(math.e) * jnp.max(jnp.abs(q_norm_g)) *
                   jnp.max(jnp.abs(k_norm_g), axis=1))
    bounds = jnp.zeros((8, LANES), f32).at[0, :3].set(score_bound)
    nsa_args = (proj, cos_t, sin_t, q_norm_g.reshape(dh, 1), bounds, kc, vct, ks_rot, vst, kw_rot, vwt, proj_s)
    nsa_out = lax.cond(2.0 * jnp.max(score_bound) < NSA_SAFE_EXPONENT,
                       functools.partial(_nsa_attention, bounded=True),
                       functools.partial(_nsa_attention, bounded=False), *nsa_args).reshape(m, NSA_WIDTH)

    gla_out = _gla(proj, proj_s, gla_w_gate, gla_b_gate.reshape(1, -1),
                   gla_norm_g.reshape(1, GLA_VAL_DIM)).reshape(m, GLA_WIDTH)

    padr = lambda t: jnp.concatenate([t, jnp.zeros(t.shape[:-1] + (LANES - t.shape[-1],), t.dtype)], axis=-1)
    wr_hi = w_router.astype(bf16)
    wr_lo = (w_router - wr_hi.astype(f32)).astype(bf16)
    x1, h2, rt, cnt = _outproj_router(nsa_out, gla_out, x2, w_out.astype(bf16), norm2_g.reshape(1, d),
                                      padr(wr_hi), padr(wr_lo), padr(b_router.reshape(1, -1)))

    top_idx = rt[:TOP_K].astype(jnp.int32)
    gate = rt[TOP_K:2 * TOP_K]
    rank = rt[2 * TOP_K:3 * TOP_K].astype(jnp.int32)
    counts = cnt[:, 0].astype(jnp.int32)
    a = m * TOP_K
    end = jnp.cumsum(counts)
    start = end - counts
    pos = rank
    for e in range(N_EXPERTS):
        pos = pos + jnp.where(top_idx == e, start[e], 0)
    tok = jnp.broadcast_to(jnp.arange(m, dtype=jnp.int32), (TOP_K, m))
    sorted_tok = jnp.zeros((a,), jnp.int32).at[pos.reshape(-1)].add(
        tok.reshape(-1), unique_indices=True, mode='promise_in_bounds')
    rows_of = lambda t, idx: t.at[idx].get(mode='promise_in_bounds')
    nblk = a // MOE_BLOCK
    n_slices = MOE_SLICES if nblk % MOE_SLICES == 0 else 1
    per = nblk // n_slices
    ys = None
    for i in range(n_slices):
        xs = rows_of(h2, sorted_tok[i * per * MOE_BLOCK:(i + 1) * per * MOE_BLOCK])
        ys = _moe_experts(_visit_table(start, end, i * per, (i + 1) * per), i * per, xs, w_gate_up,
                          b_gate_up.reshape(N_EXPERTS, 1, -1), w_down, b_down.reshape(N_EXPERTS, 1, -1), a, ys)
    out = x1
    for k in range(TOP_K):
        out = out + rows_of(ys, pos[k]).astype(f32) * gate[k][:, None]
    return out.reshape(b, s, d)


def kernel(x, positions, norm1_g, w_in, nsa_q_norm_g, nsa_k_norm_g, cmp_pos_k, cmp_w1_k, cmp_w2_k,
           cmp_pos_v, cmp_w1_v, cmp_w2_v, gla_w_gate, gla_b_gate, gla_norm_g, w_out, norm2_g,
           w_router, b_router, w_gate_up, b_gate_up, w_down, b_down):
    for l in range(norm1_g.shape[0]):
        x = _layer(x, positions, norm1_g[l], w_in[l], nsa_q_norm_g[l], nsa_k_norm_g[l],
                   cmp_pos_k[l], cmp_w1_k[l], cmp_w2_k[l], cmp_pos_v[l], cmp_w1_v[l], cmp_w2_v[l],
                   gla_w_gate[l], gla_b_gate[l], gla_norm_g[l], w_out[l], norm2_g[l],
                   w_router[l], b_router[l], w_gate_up[l], b_gate_up[l], w_down[l], b_down[l])
    return x
```

```python
import functools
import math

import jax
import jax.numpy as jnp
from jax import lax
from jax.experimental import pallas as pl
from jax.experimental.pallas import tpu as pltpu

f32 = jnp.float32
bf16 = jnp.bfloat16

NSA_HEADS = 8
NSA_KV_HEADS = 2
NSA_GROUP = NSA_HEADS // NSA_KV_HEADS
HEAD_DIM = 64
CMP_BLOCK = 32
CMP_STRIDE = 16
CMP_HIDDEN = 256
SEL_BLOCK = 64
SEL_TOPK = 8
WINDOW = 512
FORCE_BONUS = 1e4
GLA_HEADS = 4
GLA_KEY_DIM = 64
GLA_VAL_DIM = 128
GLA_CHUNK = 64
GLA_GATE_RANK = 16
GLA_TAU = 16.0
ROPE_THETA = 500000.0
ROPE_DIM = HEAD_DIM // 4
N_EXPERTS = 32
TOP_K = 4
D_FF = 1024
SWIGLU_LIMIT = 7.0
SWIGLU_ALPHA = 1.702
EPS = 1e-6
NEG_INF = -1e30

NSA_WIDTH = NSA_HEADS * HEAD_DIM
NSA_KV_WIDTH = NSA_KV_HEADS * HEAD_DIM
GLA_KEY_WIDTH = GLA_HEADS * GLA_KEY_DIM
GLA_WIDTH = GLA_HEADS * GLA_VAL_DIM
IN_SIZES = (NSA_WIDTH,) + (NSA_KV_WIDTH,) * 6 + (
    NSA_HEADS * 3, GLA_KEY_WIDTH, GLA_KEY_WIDTH, GLA_WIDTH, GLA_GATE_RANK, GLA_WIDTH)
MAIN_SIZES = (NSA_WIDTH,) + (NSA_KV_WIDTH,) * 6 + (GLA_KEY_WIDTH, GLA_KEY_WIDTH, GLA_WIDTH, GLA_WIDTH)
MAIN_IN_SEG = (0, 1, 2, 3, 4, 5, 6, 8, 9, 10, 12)
MAIN_LAYOUT = (0, 9, 10, 1, 2, 3, 4, 5, 6, 7, 8)
MAIN_OFFS = tuple(sum(MAIN_SIZES[j] for j in MAIN_LAYOUT[:MAIN_LAYOUT.index(i)]) for i in range(len(MAIN_SIZES)))

LANES = 128
VMEM_LIMIT = 48 * 1024 * 1024
MOE_BLOCK = 1024
MOE_SUB = 256
MOE_SLICE_CUTS = (1, 4, 8, 16)
NSA_SAFE_EXPONENT = 120.0
GLA_UNROLL = 4
ROUTE_ROWS = 16

_NT = (((1,), (1,)), ((), ()))
_TN = (((0,), (0,)), ((), ()))


def _cparams(*sem):
    return pltpu.CompilerParams(dimension_semantics=sem, vmem_limit_bytes=VMEM_LIMIT)


def _rms(t, g):
    return t * lax.rsqrt(jnp.mean(t * t, axis=-1, keepdims=True) + EPS) * g


def _split_bf16(t):
    hi = t.astype(bf16)
    lo = (t - hi.astype(f32)).astype(bf16)
    return hi, lo


def _rope(t, cos_f, sin_f, perm):
    hi, lo = _split_bf16(t)
    rot = (jnp.dot(hi, perm, preferred_element_type=f32) +
           jnp.dot(lo, perm, preferred_element_type=f32))
    return t * cos_f + rot * sin_f


def _inproj_kernel(x_ref, g_ref, w_ref, ws_ref, o_ref, os_ref):
    x = x_ref[...]
    h = _rms(x, g_ref[...]).astype(bf16)
    n = o_ref.shape[1]
    step = 512
    for c in range(0, n, step):
        e = min(c + step, n)
        o_ref[:, c:e] = jnp.dot(h, w_ref[:, c:e], preferred_element_type=f32).astype(bf16)
    os_ref[...] = jnp.dot(h, ws_ref[...], preferred_element_type=f32)


def _inproj(x2, g, w_main, w_small, tm=512):
    m, d = x2.shape
    n = w_main.shape[1]
    ns = w_small.shape[1]
    return pl.pallas_call(
        _inproj_kernel,
        grid=(m // tm,),
        in_specs=[pl.BlockSpec((tm, d), lambda i: (i, 0)),
                  pl.BlockSpec((1, d), lambda i: (0, 0)),
                  pl.BlockSpec((d, n), lambda i: (0, 0)),
                  pl.BlockSpec((d, ns), lambda i: (0, 0))],
        out_specs=[pl.BlockSpec((tm, n), lambda i: (i, 0)),
                   pl.BlockSpec((tm, ns), lambda i: (i, 0))],
        out_shape=[jax.ShapeDtypeStruct((m, n), bf16),
                   jax.ShapeDtypeStruct((m, ns), f32)],
        compiler_params=_cparams("parallel"),
        name="inproj",
    )(x2, g, w_main, w_small)


def _kvprep_kernel(kc_ref, vc_ref, ks_ref, vs_ref, kw_ref, vw_ref, cs_ref, spread_ref, kg_ref, ones_ref,
                   perm_ref, pick_ref, posk_ref, w1k_ref, w2k_ref, posv_ref, w1v_ref, w2v_ref,
                   kco_ref, vcto_ref, kso_ref, vsto_ref, kwo_ref, vwto_ref, tmp_ref):
    ones_bd = ones_ref[...]
    nh_out = kco_ref.shape[2]

    def rms_heads(t, g):
        sq_hi, sq_lo = _split_bf16(t * t)
        ss = (jnp.dot(sq_hi, ones_bd, preferred_element_type=f32) +
              jnp.dot(sq_lo, ones_bd, preferred_element_type=f32))
        return t * lax.rsqrt(ss * (1.0 / HEAD_DIM) + EPS) * g

    def compress(src_ref, pos_ref, w1_ref, w2_ref):
        tmp_ref[...] = src_ref[0].astype(f32)
        nh = tmp_ref.shape[0] // CMP_STRIDE
        a = jnp.zeros((nh, NSA_KV_HEADS * CMP_HIDDEN), f32)
        b = jnp.zeros((nh, NSA_KV_HEADS * CMP_HIDDEN), f32)
        for p in range(CMP_STRIDE):
            rows = tmp_ref[pl.ds(p, nh, stride=CMP_STRIDE), :]
            a = a + jnp.dot((rows + pos_ref[p:p + 1, :]).astype(bf16), w1_ref[p], preferred_element_type=f32)
            q = CMP_STRIDE + p
            b = b + jnp.dot((rows + pos_ref[q:q + 1, :]).astype(bf16), w1_ref[q], preferred_element_type=f32)
        pre = a + pltpu.roll(b, nh - 1, 0)
        hid = pre * jax.nn.sigmoid(pre)
        return jnp.dot(hid.astype(bf16), w2_ref[...], preferred_element_type=f32)

    kc = rms_heads(compress(kc_ref, posk_ref, w1k_ref, w2k_ref), kg_ref[0:1, :]).astype(bf16)
    tmp_ref[:nh_out, :] = compress(vc_ref, posv_ref, w1v_ref, w2v_ref)
    vcto_ref[0] = tmp_ref[:nh_out, :].T.astype(bf16)
    s, w = ks_ref.shape[1], ks_ref.shape[2]
    cs = jnp.concatenate([cs_ref[0], jnp.zeros((LANES - cs_ref.shape[1], s), f32)], axis=0).T
    cs_hi = cs.astype(bf16)
    cs_mid, cs_lo = _split_bf16(cs - cs_hi.astype(f32))
    spread = spread_ref[...]
    tables = (jnp.dot(cs_hi, spread, preferred_element_type=f32) + jnp.dot(cs_mid, spread, preferred_element_type=f32) +
              jnp.dot(cs_lo, spread, preferred_element_type=f32))
    lane_row = lax.broadcasted_iota(jnp.int32, (1, LANES), 1)
    cos_f = tables[:, :LANES] + jnp.where(lane_row % HEAD_DIM >= ROPE_DIM, 1.0, 0.0)
    sin_f = tables[:, LANES:]
    perm = perm_ref[...]
    ks = _rope(rms_heads(ks_ref[0].astype(f32), kg_ref[1:2, :]), cos_f, sin_f, perm).astype(bf16)
    kw = _rope(rms_heads(kw_ref[0].astype(f32), kg_ref[2:3, :]), cos_f, sin_f, perm).astype(bf16)
    row = lax.broadcasted_iota(jnp.int32, (s, LANES), 0)
    lane = lax.broadcasted_iota(jnp.int32, (s, LANES), 1)
    last_lane_one = lambda n: jnp.where(lax.broadcasted_iota(jnp.int32, (n, LANES), 1) == LANES - 1, 1.0, 0.0)
    ones_lane = last_lane_one(s)
    block_onehot = jnp.where(row // SEL_BLOCK + HEAD_DIM == lane, 1.0, 0.0) + ones_lane
    pad_lane = lax.broadcasted_iota(jnp.int32, (WINDOW, LANES), 1)
    pad_flag = jnp.where((pad_lane == HEAD_DIM) | (pad_lane == LANES - 1), 1.0, 0.0)
    for h in range(NSA_KV_HEADS):
        pick = pick_ref[h]
        kco_ref[0, h] = (jnp.dot(kc, pick, preferred_element_type=f32) + last_lane_one(nh_out)).astype(bf16)
        kso_ref[0, h] = (jnp.dot(ks, pick, preferred_element_type=f32) + block_onehot).astype(bf16)
        kwo_ref[0, h, :WINDOW, :] = pad_flag.astype(bf16)
        kwo_ref[0, h, WINDOW:, :] = (jnp.dot(kw, pick, preferred_element_type=f32) + ones_lane).astype(bf16)
    vsto_ref[0] = vs_ref[0].astype(f32).T.astype(bf16)
    vwto_ref[0] = jnp.concatenate([jnp.zeros((w, WINDOW), f32), vw_ref[0].astype(f32).T], axis=1).astype(bf16)


def _kvprep(proj, cs_t, spread, kg2, ones_bd, perm2, pick, posk, w1k, w2k, posv, w1v, w2v):
    b, s, _ = proj.shape
    w = NSA_KV_WIDTH
    assert w == LANES
    nh = s // CMP_STRIDE
    col = lambda i: pl.BlockSpec((1, s, w), lambda n: (n, 0, MAIN_OFFS[i] // w))
    whole = lambda a: pl.BlockSpec(a.shape, lambda n: (0,) * a.ndim)
    tab = pl.BlockSpec((1, cs_t.shape[1], s), lambda n: (n, 0, 0))
    out = lambda *shp: (pl.BlockSpec((1,) + shp, lambda n: (n,) + (0,) * len(shp)),
                        jax.ShapeDtypeStruct((b,) + shp, bf16))
    hk = NSA_KV_HEADS
    outs = [out(hk, nh, LANES), out(w, nh), out(hk, s, LANES), out(w, s), out(hk, WINDOW + s, LANES),
            out(w, WINDOW + s)]
    return pl.pallas_call(
        _kvprep_kernel,
        grid=(b,),
        in_specs=[col(1), col(2), col(3), col(4), col(5), col(6), tab, whole(spread), whole(kg2), whole(ones_bd),
                  whole(perm2), whole(pick), whole(posk), whole(w1k), whole(w2k), whole(posv), whole(w1v),
                  whole(w2v)],
        out_specs=[o[0] for o in outs],
        out_shape=[o[1] for o in outs],
        scratch_shapes=[pltpu.VMEM((s, w), f32)],
        compiler_params=_cparams("parallel"),
        name="nsa_kvprep",
    )(proj, proj, proj, proj, proj, proj, cs_t, spread, kg2, ones_bd, perm2, pick, posk, w1k, w2k, posv, w1v, w2v)


def _nsa_kernel(q_ref, cos_ref, sin_ref, qg_ref, mb_ref, kc_ref, vct_ref, ks_ref, vst_ref, kw_ref, vwt_ref,
                gate_ref, o_ref, *, tq, ck, bounded):
    g = NSA_GROUP
    qi = pl.program_id(2)
    t0 = pl.multiple_of(qi * tq, tq)
    scale = HEAD_DIM ** -0.5 * math.log2(math.e)
    per_head = lambda t: jnp.concatenate([t] * g, axis=1)

    qt = q_ref[0].astype(f32).T
    qt = jnp.concatenate([qt[i * HEAD_DIM:(i + 1) * HEAD_DIM] for i in range(g)], axis=1)
    qn = qt * lax.rsqrt(jnp.mean(qt * qt, axis=0, keepdims=True) + EPS) * qg_ref[...]
    half = ROPE_DIM // 2
    cos8, sin8 = per_head(cos_ref[0]), per_head(sin_ref[0])
    x1, x2 = qn[:half], qn[half:ROPE_DIM]
    q_rot = jnp.concatenate([x1 * cos8 - x2 * sin8, x2 * cos8 + x1 * sin8, qn[ROPE_DIM:]], axis=0)
    n_extra = kc_ref.shape[3] - HEAD_DIM

    def with_features(qb, feats, branch):
        used = sum(f.shape[0] for f in feats)
        last = jnp.broadcast_to(-mb_ref[0:1, branch:branch + 1] if bounded else 0.0, (1, g * tq))
        fill = jnp.zeros((n_extra - used - 1, g * tq), f32)
        return jnp.concatenate([qb] + feats + [fill, last], axis=0).astype(bf16)

    q_cmp = with_features(qn * scale, [], 0)
    q_rot = q_rot * scale
    tq_row = t0 + lax.broadcasted_iota(jnp.int32, (1, tq), 1)

    kc = kc_ref[0, 0]
    ncp = kc.shape[0]
    n_col = lax.broadcasted_iota(jnp.int32, (ncp, 1), 0)
    ok_c = (n_col * CMP_STRIDE + (CMP_BLOCK - 1)) <= tq_row
    s_c = jnp.dot(kc, q_cmp, preferred_element_type=f32) + per_head(jnp.where(ok_c, 0.0, NEG_INF))
    if bounded:
        p_c = jnp.exp2(s_c)
    else:
        m_c = jnp.max(s_c, axis=0, keepdims=True)
        p_c = jnp.exp2(s_c - m_c) * per_head(jnp.where(ok_c, 1.0, 0.0))
    den = jnp.sum(p_c, axis=0, keepdims=True)
    p_c = p_c * (1.0 / jnp.where(den > 0, den, 1.0))
    o_cmp = jnp.dot(vct_ref[0], p_c.astype(bf16), preferred_element_type=f32)

    pg = p_c[:, :tq]
    for i in range(1, g):
        pg = pg + p_c[:, i * tq:(i + 1) * tq]
    nj = ks_ref.shape[2] // SEL_BLOCK
    jj = lax.broadcasted_iota(jnp.int32, (nj, ncp), 0)
    nn = lax.broadcasted_iota(jnp.int32, (nj, ncp), 1)
    overlap = (nn * CMP_STRIDE < (jj + 1) * SEL_BLOCK) & (nn * CMP_STRIDE + CMP_BLOCK > jj * SEL_BLOCK)
    overlap = jnp.where(overlap, 1.0, 0.0).astype(bf16)
    pg_hi, pg_lo = _split_bf16(pg)
    imp = (jnp.dot(overlap, pg_hi, preferred_element_type=f32) +
           jnp.dot(overlap, pg_lo, preferred_element_type=f32))
    j_col = lax.broadcasted_iota(jnp.int32, (nj, 1), 0)
    j_f = j_col.astype(f32)
    cur = tq_row // SEL_BLOCK
    valid = j_col <= cur
    forced = (j_col == 0) | (j_col == cur) | (j_col == cur - 1)
    score = jnp.where(valid, imp + jnp.where(forced, FORCE_BONUS, 0.0), NEG_INF)
    sel = jnp.zeros((nj, tq), f32)
    for _ in range(SEL_TOPK):
        m = jnp.max(score, axis=0, keepdims=True)
        first = jnp.min(jnp.where(score == m, j_f, float(nj)), axis=0, keepdims=True)
        hit = j_f == first
        sel = jnp.where(hit, 1.0, sel)
        score = jnp.where(hit, -jnp.inf, score)
    sel = jnp.where(valid, sel, 0.0)

    def weighted_values(vt, p):
        lhs = jnp.concatenate([vt, jnp.ones((16, vt.shape[1]), bf16)], axis=0)
        r = jnp.dot(lhs, p.astype(bf16), preferred_element_type=f32)
        return r[:HEAD_DIM], r[HEAD_DIM:HEAD_DIM + 1]

    wl = WINDOW + tq
    q_win = with_features(q_rot, [jnp.full((1, g * tq), NEG_INF, f32)], 2)
    s_w = jnp.dot(kw_ref[0, 0, pl.ds(t0, wl), :], q_win, preferred_element_type=f32)
    step = lax.broadcasted_iota(jnp.int32, (tq, 1), 0)
    lo_ok = (t0 - WINDOW + step) > (tq_row - WINDOW)
    hi_ok = (t0 + step) <= tq_row
    parts = [s_w[:tq] + per_head(jnp.where(lo_ok, 0.0, NEG_INF))]
    if wl > 2 * tq:
        parts.append(s_w[tq:wl - tq])
    parts.append(s_w[wl - tq:] + per_head(jnp.where(hi_ok, 0.0, NEG_INF)))
    if bounded:
        p_w = jnp.concatenate([jnp.exp2(t) for t in parts], axis=0)
    else:
        m_w = functools.reduce(jnp.maximum, [jnp.max(t, axis=0, keepdims=True) for t in parts])
        p_w = jnp.concatenate([jnp.exp2(t - m_w) for t in parts], axis=0)
    o_win, l_w = weighted_values(vwt_ref[0, :, pl.ds(t0, wl)], p_w)
    o_win = o_win * (1.0 / l_w)

    q_sel = with_features(q_rot, [per_head(jnp.where(sel > 0.5, 0.0, NEG_INF))], 1)

    def sel_step(carry, k0, diagonal):
        m_prev, l_prev, acc = carry
        s = jnp.dot(ks_ref[0, 0, pl.ds(k0, ck), :], q_sel, preferred_element_type=f32)
        if diagonal:
            kpos = k0 + lax.broadcasted_iota(jnp.int32, (ck, 1), 0)
            s = s + per_head(jnp.where(kpos <= tq_row, 0.0, NEG_INF))
        if bounded:
            pv, p_sum = weighted_values(vst_ref[0, :, pl.ds(k0, ck)], jnp.exp2(s))
            return m_prev, l_prev + p_sum, acc + pv
        m_new = jnp.maximum(m_prev, jnp.max(s, axis=0, keepdims=True))
        alpha = jnp.exp2(m_prev - m_new)
        pv, p_sum = weighted_values(vst_ref[0, :, pl.ds(k0, ck)], jnp.exp2(s - m_new))
        return m_new, alpha * l_prev + p_sum, alpha * acc + pv

    init = (jnp.full((1, g * tq), NEG_INF, f32), jnp.zeros((1, g * tq), f32),
            jnp.zeros((HEAD_DIM, g * tq), f32))
    last = (t0 + tq + ck - 1) // ck - 1
    carry = lax.fori_loop(0, last, lambda c, cr: sel_step(cr, pl.multiple_of(c * ck, ck), False), init)
    _, l_s, acc_s = sel_step(carry, pl.multiple_of(last * ck, ck), True)
    o_sel = acc_s * (1.0 / l_s)

    gt = gate_ref[0].T
    gate = lambda j: jax.nn.sigmoid(jnp.concatenate([gt[i * 3 + j:i * 3 + j + 1] for i in range(g)], axis=1))
    ot = gate(0) * o_cmp + gate(1) * o_sel + gate(2) * o_win
    o2 = jnp.concatenate([ot[:, i * tq:(i + 1) * tq] for i in range(g)], axis=0)
    o_ref[0] = o2.T.astype(o_ref.dtype)


def _nsa_attention(proj, cos_t, sin_t, qg, bounds, kc, vct, ks, vst, kw, vwt, gate_logits, *, bounded,
                   tq=256, ck=512):
    b, s, _ = proj.shape
    hk, g, dh = NSA_KV_HEADS, NSA_GROUP, HEAD_DIM
    assert WINDOW >= tq and ck % tq == 0 and s % ck == 0
    tab = pl.BlockSpec((1, ROPE_DIM // 2, tq), lambda i, j, t: (i, 0, t))
    assert s // SEL_BLOCK <= kc.shape[3] - dh
    keys = lambda a: pl.BlockSpec((1, 1) + a.shape[2:], lambda i, j, t: (i, j, 0, 0))
    vals = lambda a: pl.BlockSpec((1, dh, a.shape[2]), lambda i, j, t: (i, j, 0))
    return pl.pallas_call(
        functools.partial(_nsa_kernel, tq=tq, ck=ck, bounded=bounded),
        grid=(b, hk, s // tq),
        in_specs=[pl.BlockSpec((1, tq, g * dh), lambda i, j, t: (i, t, j)),
                  tab, tab, pl.BlockSpec((dh, 1), lambda i, j, t: (0, 0)),
                  pl.BlockSpec(bounds.shape, lambda i, j, t: (0, 0)),
                  keys(kc), vals(vct), keys(ks), vals(vst), keys(kw), vals(vwt),
                  pl.BlockSpec((1, tq, LANES), lambda i, j, t: (i, t, j))],
        out_specs=pl.BlockSpec((1, tq, g * dh), lambda i, j, t: (i, t, j)),
        out_shape=jax.ShapeDtypeStruct((b, s, NSA_WIDTH), bf16),
        compiler_params=_cparams("parallel", "parallel", "arbitrary"),
        name="nsa_attention_bounded" if bounded else "nsa_attention",
    )(proj, cos_t, sin_t, qg, bounds, kc, vct, ks, vst, kw, vwt, gate_logits)


def _gla_kernel(q_ref, k_ref, v_ref, lr_ref, wg_ref, bg_ref, r_ref, ng_ref, o_ref, *, grp, npair):
    c = GLA_CHUNK
    s = q_ref.shape[1]
    dk, dv = GLA_KEY_DIM, GLA_VAL_DIM
    rows = grp * c
    ri = lax.broadcasted_iota(jnp.int32, (rows, rows), 0)
    ci = lax.broadcasted_iota(jnp.int32, (rows, rows), 1)
    causal = (ci <= ri) & (ci // c == ri // c)
    tri = jnp.where(causal, 1.0, 0.0).astype(bf16)
    lane = lax.broadcasted_iota(jnp.int32, (1, 2 * dk), 1)
    head_mask = [lane < dk, lane >= dk]
    chunk_of_row = lax.broadcasted_iota(jnp.int32, (rows, 1), 0) // c
    wg = wg_ref[...]
    bg = bg_ref[...]
    ng = ng_ref[...]

    def body(n, states):
        r0 = pl.multiple_of(n * rows, rows)
        z = jnp.dot(lr_ref[0, pl.ds(r0, rows), :GLA_GATE_RANK], wg, preferred_element_type=f32,
                    precision=lax.Precision.HIGHEST) + bg
        log_a = -(jnp.maximum(-z, 0.0) + jnp.log(1.0 + jnp.exp(-jnp.abs(z)))) / GLA_TAU
        la_hi = log_a.astype(bf16)
        la_mid, la_lo = _split_bf16(log_a - la_hi.astype(f32))
        bcum = (jnp.dot(tri, la_hi, preferred_element_type=f32) + jnp.dot(tri, la_mid, preferred_element_type=f32) +
                jnp.dot(tri, la_lo, preferred_element_type=f32))
        qf = q_ref[0, pl.ds(r0, rows), :].astype(f32) * (dk ** -0.5)
        kf = k_ref[0, pl.ds(r0, rows), :].astype(f32)
        q_dec = qf * jnp.exp(bcum)
        k_dec = (kf * jnp.exp(-bcum)).astype(bf16)
        b_last = [bcum[(i + 1) * c - 1:(i + 1) * c, :] for i in range(grp)]
        k_state = jnp.concatenate(
            [kf[i * c:(i + 1) * c] * jnp.exp(b_last[i] - bcum[i * c:(i + 1) * c]) for i in range(grp)],
            axis=0).astype(bf16)
        by_chunk = lambda t: jnp.concatenate(
            [jnp.where(chunk_of_row == i, t, jnp.zeros_like(t)) for i in range(grp)], axis=1)
        new_states = []
        for hh in range(2 * npair):
            pr, h = divmod(hh, 2)
            pair = slice(pr * 2 * dk, (pr + 1) * 2 * dk)
            q_h = jnp.where(head_mask[h], q_dec[:, pair], 0.0).astype(bf16)
            v_h = v_ref[0, pl.ds(r0, rows), hh * dv:(hh + 1) * dv]
            attn = lax.dot_general(q_h, k_dec[:, pair], _NT, preferred_element_type=f32)
            attn = jnp.where(causal, attn, 0.0).astype(bf16)
            o_intra = jnp.dot(attn, v_h, preferred_element_type=f32)
            u_all = lax.dot_general(v_h, by_chunk(k_state[:, pair]), _TN, preferred_element_type=f32)
            st = states[hh]
            entering = []
            for i in range(grp):
                entering.append(st)
                st = st * jnp.exp(b_last[i][:, pair]) + u_all[:, i * 2 * dk:(i + 1) * 2 * dk]
            new_states.append(st)
            o_inter = lax.dot_general(by_chunk(q_h), jnp.concatenate(entering, axis=1).astype(bf16), _NT,
                                      preferred_element_type=f32)
            o = _rms(o_intra + o_inter, ng)
            rr = r_ref[0, pl.ds(r0, rows), hh * dv:(hh + 1) * dv].astype(f32)
            o_ref[0, pl.ds(r0, rows), hh * dv:(hh + 1) * dv] = (o * (rr * jax.nn.sigmoid(rr))).astype(o_ref.dtype)
        return tuple(new_states)

    zero = jnp.zeros((dv, 2 * dk), f32)
    lax.fori_loop(0, s // rows, body, (zero,) * (2 * npair), unroll=GLA_UNROLL)


def _gla(proj, proj_s, wg, bg, ng, grp=4, npair=GLA_HEADS // 2):
    b, s, _ = proj.shape
    dk2, dv2 = 2 * npair * GLA_KEY_DIM, 2 * npair * GLA_VAL_DIM
    oq, ok_, ov, orr = (MAIN_OFFS[7] // dk2, MAIN_OFFS[8] // dk2, MAIN_OFFS[9] // dv2, MAIN_OFFS[10] // dv2)
    assert all(MAIN_OFFS[i] % w == 0 for i, w in ((7, dk2), (8, dk2), (9, dv2), (10, dv2)))
    col = lambda w, o: pl.BlockSpec((1, s, w), lambda i, j: (i, 0, o + j))
    return pl.pallas_call(
        functools.partial(_gla_kernel, grp=grp, npair=npair),
        grid=(b, GLA_HEADS // (2 * npair)),
        in_specs=[col(dk2, oq), col(dk2, ok_), col(dv2, ov),
                  pl.BlockSpec((1, s, LANES), lambda i, j: (i, 0, 2)),
                  pl.BlockSpec((GLA_GATE_RANK, dk2), lambda i, j: (0, j)),
                  pl.BlockSpec((1, dk2), lambda i, j: (0, j)),
                  col(dv2, orr),
                  pl.BlockSpec((1, GLA_VAL_DIM), lambda i, j: (0, 0))],
        out_specs=pl.BlockSpec((1, s, dv2), lambda i, j: (i, 0, j)),
        out_shape=jax.ShapeDtypeStruct((b, s, GLA_WIDTH), bf16),
        compiler_params=_cparams("parallel", "parallel"),
        name="gla",
    )(proj, proj, proj, proj_s, wg, bg, proj, ng)


def _outproj_kernel(a_ref, b_ref, x_ref, wo_ref, g2_ref, wrh_ref, wrl_ref, br_ref,
                    x1_ref, h2_ref, rt_ref, cnt_ref, carry_ref):
    @pl.when(pl.program_id(0) == 0)
    def _():
        carry_ref[...] = jnp.zeros_like(carry_ref)

    na = a_ref.shape[1]
    tm = a_ref.shape[0]
    y = (jnp.dot(a_ref[...], wo_ref[:na, :], preferred_element_type=f32) +
         jnp.dot(b_ref[...], wo_ref[na:, :], preferred_element_type=f32))
    x1 = x_ref[...] + y
    x1_ref[...] = x1
    h2 = _rms(x1, g2_ref[...])
    h2_ref[...] = h2.astype(bf16)
    hi, lo = _split_bf16(h2)
    logits = (jnp.dot(hi, wrh_ref[...], preferred_element_type=f32) +
              jnp.dot(lo, wrh_ref[...], preferred_element_type=f32) +
              jnp.dot(hi, wrl_ref[...], preferred_element_type=f32)) + br_ref[...]
    score = logits.T[:N_EXPERTS]
    e_col = lax.broadcasted_iota(jnp.int32, (N_EXPERTS, 1), 0).astype(f32)
    top = jnp.max(score, axis=0, keepdims=True)
    hits, firsts, weights = [], [], []
    for _ in range(TOP_K):
        m = jnp.max(score, axis=0, keepdims=True)
        first = jnp.min(jnp.where(score == m, e_col, float(N_EXPERTS)), axis=0, keepdims=True)
        hit = e_col == first
        hits.append(hit)
        firsts.append(first)
        weights.append(jnp.exp(m - top))
        score = jnp.where(hit, -jnp.inf, score)
    tot = weights[0] + weights[1] + weights[2] + weights[3]
    onehot = jnp.where(hits[0] | hits[1] | hits[2] | hits[3], 1.0, 0.0)
    ri = lax.broadcasted_iota(jnp.int32, (tm, tm), 0)
    ci = lax.broadcasted_iota(jnp.int32, (tm, tm), 1)
    earlier = jnp.where(ri < ci, 1.0, 0.0).astype(bf16)
    rank = carry_ref[...] + jnp.dot(onehot.astype(bf16), earlier, preferred_element_type=f32)
    carry_ref[...] = carry_ref[...] + jnp.sum(onehot, axis=1, keepdims=True)
    cnt_ref[...] = jnp.broadcast_to(carry_ref[...], cnt_ref.shape)
    ranks = [jnp.sum(jnp.where(hits[k], rank, 0.0), axis=0, keepdims=True) for k in range(TOP_K)]
    gates = [weights[k] / tot for k in range(TOP_K)]
    pad = jnp.zeros((rt_ref.shape[0] - 3 * TOP_K, tm), f32)
    rt_ref[...] = jnp.concatenate(firsts + gates + ranks + [pad], axis=0)


def _outproj_router(a, b, x2, wo, g2, wr_hi, wr_lo, br, tm=512):
    m, d = x2.shape
    na, nb = a.shape[1], b.shape[1]
    row = lambda n: pl.BlockSpec((tm, n), lambda i: (i, 0))
    whole = lambda t: pl.BlockSpec(t.shape, lambda i: (0,) * t.ndim)
    return pl.pallas_call(
        _outproj_kernel,
        grid=(m // tm,),
        in_specs=[row(na), row(nb), row(d), whole(wo), whole(g2), whole(wr_hi), whole(wr_lo), whole(br)],
        out_specs=[row(d), row(d), pl.BlockSpec((ROUTE_ROWS, tm), lambda i: (0, i)),
                   pl.BlockSpec((N_EXPERTS, LANES), lambda i: (0, 0))],
        out_shape=[jax.ShapeDtypeStruct((m, d), f32),
                   jax.ShapeDtypeStruct((m, d), bf16),
                   jax.ShapeDtypeStruct((ROUTE_ROWS, m), f32),
                   jax.ShapeDtypeStruct((N_EXPERTS, LANES), f32)],
        scratch_shapes=[pltpu.VMEM((N_EXPERTS, 1), f32)],
        compiler_params=_cparams("arbitrary"),
        name="outproj_router",
    )(a, b, x2, wo, g2, wr_hi, wr_lo, br)


def _moe_kernel(vb_ref, ve_ref, lo_ref, hi_ref, nv_ref, xs_ref, wgu_ref, bgu_ref, wd_ref, bd_ref, *rest):
    o_ref, wgu_bf, wd_bf = rest[-3:]
    v = pl.program_id(0)
    prev = jnp.maximum(v - 1, 0)
    live = v < nv_ref[0]
    new_expert = (v == 0) | (ve_ref[v] != ve_ref[prev])
    new_block = (v == 0) | (vb_ref[v] != vb_ref[prev])

    @pl.when(live & new_expert)
    def _():
        wgu_bf[...] = wgu_ref[0].astype(bf16)
        wd_bf[...] = wd_ref[0].astype(bf16)

    @pl.when(live & new_block)
    def _():
        o_ref[...] = jnp.zeros_like(o_ref)

    @pl.when(live)
    def _():
        blk0 = vb_ref[v] * MOE_BLOCK
        lo, hi = lo_ref[v], hi_ref[v]
        first = jnp.maximum(lo - blk0, 0) // MOE_SUB
        last = (jnp.minimum(hi - blk0, MOE_BLOCK) + MOE_SUB - 1) // MOE_SUB

        def sub_block(j, carry):
            r0 = pl.multiple_of(j * MOE_SUB, MOE_SUB)
            h = jnp.dot(xs_ref[pl.ds(r0, MOE_SUB), :], wgu_bf[...], preferred_element_type=f32) + bgu_ref[0]
            x_glu = jnp.minimum(h[:, :D_FF], SWIGLU_LIMIT)
            x_lin = jnp.clip(h[:, D_FF:], -SWIGLU_LIMIT, SWIGLU_LIMIT)
            act = x_glu * jax.nn.sigmoid(SWIGLU_ALPHA * x_glu) * (x_lin + 1.0)
            y = jnp.dot(act.astype(bf16), wd_bf[...], preferred_element_type=f32) + bd_ref[0]
            row = blk0 + r0 + lax.broadcasted_iota(jnp.int32, (MOE_SUB, 1), 0)
            mine = (row >= lo) & (row < hi)
            o_ref[pl.ds(r0, MOE_SUB), :] = jnp.where(mine, y.astype(o_ref.dtype), o_ref[pl.ds(r0, MOE_SUB), :])
            return carry

        lax.fori_loop(first, last, sub_block, 0)


def _moe_experts(table, blk_lo, xs, wgu, bgu, wd, bd, n_rows, ys_so_far=None):
    d = xs.shape[1]
    nvis = table[0].shape[0]
    in_specs = [pl.BlockSpec((MOE_BLOCK, d), lambda v, vb, ve, *_: (vb[v] - blk_lo, 0)),
                pl.BlockSpec((1, d, 2 * D_FF), lambda v, vb, ve, *_: (ve[v], 0, 0)),
                pl.BlockSpec((1, 1, 2 * D_FF), lambda v, vb, ve, *_: (ve[v], 0, 0)),
                pl.BlockSpec((1, D_FF, d), lambda v, vb, ve, *_: (ve[v], 0, 0)),
                pl.BlockSpec((1, 1, d), lambda v, vb, ve, *_: (ve[v], 0, 0))]
    operands = table + (xs, wgu, bgu, wd, bd)
    aliases = {}
    if ys_so_far is not None:
        in_specs.append(pl.BlockSpec(memory_space=pl.ANY))
        aliases = {len(operands): 0}
        operands += (ys_so_far,)
    return pl.pallas_call(
        _moe_kernel,
        grid_spec=pltpu.PrefetchScalarGridSpec(
            num_scalar_prefetch=5,
            grid=(nvis,),
            in_specs=in_specs,
            out_specs=pl.BlockSpec((MOE_BLOCK, d), lambda v, vb, ve, *_: (vb[v], 0)),
            scratch_shapes=[pltpu.VMEM((d, 2 * D_FF), bf16), pltpu.VMEM((D_FF, d), bf16)],
        ),
        out_shape=jax.ShapeDtypeStruct((n_rows, d), bf16),
        input_output_aliases=aliases,
        compiler_params=_cparams("arbitrary"),
        name="moe_experts",
    )(*operands)


def _visit_table(start, end, blk_lo, blk_hi):
    first_blk = jnp.maximum(start // MOE_BLOCK, blk_lo)
    last_blk = jnp.minimum(jnp.where(end > start, (end - 1) // MOE_BLOCK, -1), blk_hi - 1)
    per_e = jnp.maximum(last_blk - first_blk + 1, 0)
    v_end = jnp.cumsum(per_e)
    v_start = v_end - per_e
    nvis = blk_hi - blk_lo + N_EXPERTS - 1
    v = jnp.arange(nvis, dtype=jnp.int32)
    n_live = v_end[-1]
    vc = jnp.minimum(v, n_live - 1)
    ve = jnp.sum((v_end[None, :] <= vc[:, None]).astype(jnp.int32), axis=1)
    vb = (first_blk[ve] + vc - v_start[ve]).astype(jnp.int32)
    return vb, ve, start[ve].astype(jnp.int32), end[ve].astype(jnp.int32), n_live.astype(jnp.int32).reshape(1)


def _rope_tables(positions):
    half = ROPE_DIM // 2
    inv_freq = jnp.exp(-math.log(ROPE_THETA) * jnp.arange(0, ROPE_DIM, 2, dtype=f32) / ROPE_DIM)
    ang = positions.astype(f32)[:, None, :] * inv_freq[None, :, None]
    cos_t, sin_t = jnp.cos(ang), jnp.sin(ang)
    d = jnp.arange(LANES) % HEAD_DIM
    f = jnp.arange(LANES)[:, None]
    first, second = d[None, :] == f % half, d[None, :] == f % half + half
    to_cos = jnp.where((f < half) & (first | second), 1.0, 0.0)
    to_sin = jnp.where((f >= half) & (f < 2 * half), jnp.where(second, 1.0, 0.0) - jnp.where(first, 1.0, 0.0), 0.0)
    spread = jnp.concatenate([to_cos, to_sin], axis=1).astype(bf16)
    idx = jnp.arange(half)
    perm = jnp.zeros((HEAD_DIM, HEAD_DIM), f32).at[idx + half, idx].set(1.0).at[idx, idx + half].set(1.0)
    return cos_t, sin_t, spread, perm


def _layer(x, positions, norm1_g, w_in, q_norm_g, k_norm_g, cmp_pos_k, cmp_w1_k, cmp_w2_k,
           cmp_pos_v, cmp_w1_v, cmp_w2_v, gla_w_gate, gla_b_gate, gla_norm_g, w_out, norm2_g,
           w_router, b_router, w_gate_up, b_gate_up, w_down, b_down):
    b, s, d = x.shape
    m = b * s
    hk, g, dh = NSA_KV_HEADS, NSA_GROUP, HEAD_DIM
    x2 = x.reshape(m, d)

    offs = [0]
    for sz in IN_SIZES:
        offs.append(offs[-1] + sz)
    seg = lambda i: w_in[:, offs[i]:offs[i + 1]]
    w_main = jnp.concatenate([seg(MAIN_IN_SEG[i]) for i in MAIN_LAYOUT], axis=1).astype(bf16)
    padw = lambda t: jnp.concatenate([t, jnp.zeros((d, LANES - t.shape[1]), f32)], axis=1)
    ng = g * 3
    w_small = jnp.concatenate([padw(seg(7)[:, :ng]), padw(seg(7)[:, ng:]), padw(seg(11))], axis=1).astype(bf16)
    proj, proj_s = _inproj(x2, norm1_g.reshape(1, d), w_main, w_small)
    proj = proj.reshape(b, s, -1)
    proj_s = proj_s.reshape(b, s, 3 * LANES)

    cos_t, sin_t, spread, perm = _rope_tables(positions)
    per_kv = lambda t: jnp.concatenate([t] * hk, axis=-1)

    def bdiag(t):
        r, c = t.shape[-2:]
        lead = [(0, 0)] * (t.ndim - 2)
        return jnp.concatenate([jnp.pad(t, lead + [(0, 0), (i * c, (hk - 1 - i) * c)]) for i in range(hk)], axis=-2)

    w1_bd = lambda w1: bdiag(w1.reshape(CMP_BLOCK, dh, CMP_HIDDEN)).astype(bf16)
    kc, vct, ks_rot, vst, kw_rot, vwt = _kvprep(
        proj, jnp.concatenate([cos_t, sin_t], axis=1), spread, per_kv(k_norm_g),
        bdiag(jnp.ones((dh, dh), f32)).astype(bf16), bdiag(perm).astype(bf16),
        jnp.stack([jnp.eye(hk * dh, LANES, k=-h * dh, dtype=f32) * (jnp.arange(LANES) < dh)
                   for h in range(hk)]).astype(bf16),
        per_kv(cmp_pos_k), w1_bd(cmp_w1_k), bdiag(cmp_w2_k).astype(bf16),
        per_kv(cmp_pos_v), w1_bd(cmp_w1_v), bdiag(cmp_w2_v).astype(bf16))
    score_bound = (1.05 * dh * dh ** -0.5 * math.log2(math.e) * jnp.max(jnp.abs(q_norm_g)) *
                   jnp.max(jnp.abs(k_norm_g), axis=1))
    bounds = jnp.zeros((8, LANES), f32).at[0, :3].set(score_bound)
    nsa_args = (proj, cos_t, sin_t, q_norm_g.reshape(dh, 1), bounds, kc, vct, ks_rot, vst, kw_rot, vwt, proj_s)
    nsa_out = lax.cond(2.0 * jnp.max(score_bound) < NSA_SAFE_EXPONENT,
                       functools.partial(_nsa_attention, bounded=True),
                       functools.partial(_nsa_attention, bounded=False), *nsa_args).reshape(m, NSA_WIDTH)

    gla_out = _gla(proj, proj_s, gla_w_gate, gla_b_gate.reshape(1, -1),
                   gla_norm_g.reshape(1, GLA_VAL_DIM)).reshape(m, GLA_WIDTH)

    padr = lambda t: jnp.concatenate([t, jnp.zeros(t.shape[:-1] + (LANES - t.shape[-1],), t.dtype)], axis=-1)
    wr_hi = w_router.astype(bf16)
    wr_lo = (w_router - wr_hi.astype(f32)).astype(bf16)
    x1, h2, rt, cnt = _outproj_router(nsa_out, gla_out, x2, w_out.astype(bf16), norm2_g.reshape(1, d),
                                      padr(wr_hi), padr(wr_lo), padr(b_router.reshape(1, -1)))

    a = m * TOP_K
    top_idx = rt[:TOP_K].reshape(a).astype(jnp.int32)
    gate = rt[TOP_K:2 * TOP_K]
    rank = rt[2 * TOP_K:3 * TOP_K].reshape(a).astype(jnp.int32)
    counts = cnt[:, 0].astype(jnp.int32)
    end = jnp.cumsum(counts)
    start = end - counts
    pos = rank
    for e in range(N_EXPERTS):
        pos = pos + jnp.where(top_idx == e, start[e], 0)
    tok = jnp.arange(a, dtype=jnp.int32) % m
    sorted_tok = jnp.zeros((a,), jnp.int32).at[pos].add(tok, unique_indices=True, mode='promise_in_bounds')
    rows_of = lambda t, idx: t.at[idx].get(mode='promise_in_bounds')
    nblk = a // MOE_BLOCK
    cuts = [nblk * c // MOE_SLICE_CUTS[-1] for c in MOE_SLICE_CUTS] if nblk % MOE_SLICE_CUTS[-1] == 0 else [nblk]
    ys, lo_blk = None, 0
    for hi_blk in cuts:
        xs = rows_of(h2, sorted_tok[lo_blk * MOE_BLOCK:hi_blk * MOE_BLOCK])
        ys = _moe_experts(_visit_table(start, end, lo_blk, hi_blk), lo_blk, xs, w_gate_up,
                          b_gate_up.reshape(N_EXPERTS, 1, -1), w_down, b_down.reshape(N_EXPERTS, 1, -1), a, ys)
        lo_blk = hi_blk
    out = x1
    for k in range(TOP_K):
        out = out + rows_of(ys, pos[k * m:(k + 1) * m]).astype(f32) * gate[k][:, None]
    return out.reshape(b, s, d)


def kernel(x, positions, norm1_g, w_in, nsa_q_norm_g, nsa_k_norm_g, cmp_pos_k, cmp_w1_k, cmp_w2_k,
           cmp_pos_v, cmp_w1_v, cmp_w2_v, gla_w_gate, gla_b_gate, gla_norm_g, w_out, norm2_g,
           w_router, b_router, w_gate_up, b_gate_up, w_down, b_down):
    for l in range(norm1_g.shape[0]):
        x = _layer(x, positions, norm1_g[l], w_in[l], nsa_q_norm_g[l], nsa_k_norm_g[l],
                   cmp_pos_k[l], cmp_w1_k[l], cmp_w2_k[l], cmp_pos_v[l], cmp_w1_v[l], cmp_w2_v[l],
                   gla_w_gate[l], gla_b_gate[l], gla_norm_g[l], w_out[l], norm2_g[l],
                   w_router[l], b_router[l], w_gate_up[l], b_gate_up[l], w_down[l], b_down[l])
    return x
```

```python
import functools
import math

import jax
import jax.numpy as jnp
from jax import lax
from jax.experimental import pallas as pl
from jax.experimental.pallas import tpu as pltpu

f32 = jnp.float32
bf16 = jnp.bfloat16

NSA_HEADS = 8
NSA_KV_HEADS = 2
NSA_GROUP = NSA_HEADS // NSA_KV_HEADS
HEAD_DIM = 64
CMP_BLOCK = 32
CMP_STRIDE = 16
CMP_HIDDEN = 256
SEL_BLOCK = 64
SEL_TOPK = 8
WINDOW = 512
FORCE_BONUS = 1e4
GLA_HEADS = 4
GLA_KEY_DIM = 64
GLA_VAL_DIM = 128
GLA_CHUNK = 64
GLA_GATE_RANK = 16
GLA_TAU = 16.0
ROPE_THETA = 500000.0
ROPE_DIM = HEAD_DIM // 4
N_EXPERTS = 32
TOP_K = 4
D_FF = 1024
SWIGLU_LIMIT = 7.0
SWIGLU_ALPHA = 1.702
EPS = 1e-6
NEG_INF = -1e30

NSA_WIDTH = NSA_HEADS * HEAD_DIM
NSA_KV_WIDTH = NSA_KV_HEADS * HEAD_DIM
GLA_KEY_WIDTH = GLA_HEADS * GLA_KEY_DIM
GLA_WIDTH = GLA_HEADS * GLA_VAL_DIM
IN_SIZES = (NSA_WIDTH,) + (NSA_KV_WIDTH,) * 6 + (
    NSA_HEADS * 3, GLA_KEY_WIDTH, GLA_KEY_WIDTH, GLA_WIDTH, GLA_GATE_RANK, GLA_WIDTH)
MAIN_SIZES = (NSA_WIDTH,) + (NSA_KV_WIDTH,) * 6 + (GLA_KEY_WIDTH, GLA_KEY_WIDTH, GLA_WIDTH, GLA_WIDTH)
MAIN_IN_SEG = (0, 1, 2, 3, 4, 5, 6, 8, 9, 10, 12)
MAIN_LAYOUT = (0, 9, 10, 1, 2, 3, 4, 5, 6, 7, 8)
MAIN_OFFS = tuple(sum(MAIN_SIZES[j] for j in MAIN_LAYOUT[:MAIN_LAYOUT.index(i)]) for i in range(len(MAIN_SIZES)))

LANES = 128
VMEM_LIMIT = 48 * 1024 * 1024
MOE_BLOCK = 1024
MOE_SUB = 256
MOE_SLICE_CUTS = (1, 4, 8, 16)
NSA_SAFE_EXPONENT = 120.0
GLA_UNROLL = 4
ROUTE_ROWS = 16

_NT = (((1,), (1,)), ((), ()))
_TN = (((0,), (0,)), ((), ()))


def _cparams(*sem):
    return pltpu.CompilerParams(dimension_semantics=sem, vmem_limit_bytes=VMEM_LIMIT)


def _rms(t, g):
    return t * lax.rsqrt(jnp.mean(t * t, axis=-1, keepdims=True) + EPS) * g


def _split_bf16(t):
    hi = t.astype(bf16)
    lo = (t - hi.astype(f32)).astype(bf16)
    return hi, lo


def _rope(t, cos_f, sin_f, perm):
    hi, lo = _split_bf16(t)
    rot = (jnp.dot(hi, perm, preferred_element_type=f32) +
           jnp.dot(lo, perm, preferred_element_type=f32))
    return t * cos_f + rot * sin_f


def _inproj_kernel(x_ref, g_ref, w_ref, ws_ref, o_ref, os_ref):
    x = x_ref[...]
    h = _rms(x, g_ref[...]).astype(bf16)
    n = o_ref.shape[1]
    step = 512
    for c in range(0, n, step):
        e = min(c + step, n)
        o_ref[:, c:e] = jnp.dot(h, w_ref[:, c:e], preferred_element_type=f32).astype(bf16)
    os_ref[...] = jnp.dot(h, ws_ref[...], preferred_element_type=f32)


def _inproj(x2, g, w_main, w_small, tm=512):
    m, d = x2.shape
    n = w_main.shape[1]
    ns = w_small.shape[1]
    return pl.pallas_call(
        _inproj_kernel,
        grid=(m // tm,),
        in_specs=[pl.BlockSpec((tm, d), lambda i: (i, 0)),
                  pl.BlockSpec((1, d), lambda i: (0, 0)),
                  pl.BlockSpec((d, n), lambda i: (0, 0)),
                  pl.BlockSpec((d, ns), lambda i: (0, 0))],
        out_specs=[pl.BlockSpec((tm, n), lambda i: (i, 0)),
                   pl.BlockSpec((tm, ns), lambda i: (i, 0))],
        out_shape=[jax.ShapeDtypeStruct((m, n), bf16),
                   jax.ShapeDtypeStruct((m, ns), f32)],
        compiler_params=_cparams("parallel"),
        name="inproj",
    )(x2, g, w_main, w_small)


def _kvprep_kernel(kc_ref, vc_ref, ks_ref, vs_ref, kw_ref, vw_ref, cs_ref, spread_ref, kg_ref, ones_ref,
                   perm_ref, pick_ref, posk_ref, w1k_ref, w2k_ref, posv_ref, w1v_ref, w2v_ref,
                   kco_ref, vcto_ref, kso_ref, vsto_ref, kwo_ref, vwto_ref, tmp_ref):
    ones_bd = ones_ref[...]
    nh_out = kco_ref.shape[2]

    def rms_heads(t, g):
        sq_hi, sq_lo = _split_bf16(t * t)
        ss = (jnp.dot(sq_hi, ones_bd, preferred_element_type=f32) +
              jnp.dot(sq_lo, ones_bd, preferred_element_type=f32))
        return t * lax.rsqrt(ss * (1.0 / HEAD_DIM) + EPS) * g

    def compress(src_ref, pos_ref, w1_ref, w2_ref):
        tmp_ref[...] = src_ref[0].astype(f32)
        nh = tmp_ref.shape[0] // CMP_STRIDE
        a = jnp.zeros((nh, NSA_KV_HEADS * CMP_HIDDEN), f32)
        b = jnp.zeros((nh, NSA_KV_HEADS * CMP_HIDDEN), f32)
        for p in range(CMP_STRIDE):
            rows = tmp_ref[pl.ds(p, nh, stride=CMP_STRIDE), :]
            a = a + jnp.dot((rows + pos_ref[p:p + 1, :]).astype(bf16), w1_ref[p], preferred_element_type=f32)
            q = CMP_STRIDE + p
            b = b + jnp.dot((rows + pos_ref[q:q + 1, :]).astype(bf16), w1_ref[q], preferred_element_type=f32)
        pre = a + pltpu.roll(b, nh - 1, 0)
        hid = pre * jax.nn.sigmoid(pre)
        return jnp.dot(hid.astype(bf16), w2_ref[...], preferred_element_type=f32)

    kc = rms_heads(compress(kc_ref, posk_ref, w1k_ref, w2k_ref), kg_ref[0:1, :]).astype(bf16)
    tmp_ref[:nh_out, :] = compress(vc_ref, posv_ref, w1v_ref, w2v_ref)
    vcto_ref[0] = tmp_ref[:nh_out, :].T.astype(bf16)
    s, w = ks_ref.shape[1], ks_ref.shape[2]
    cs = jnp.concatenate([cs_ref[0], jnp.zeros((LANES - cs_ref.shape[1], s), f32)], axis=0).T
    cs_hi = cs.astype(bf16)
    cs_mid, cs_lo = _split_bf16(cs - cs_hi.astype(f32))
    spread = spread_ref[...]
    tables = (jnp.dot(cs_hi, spread, preferred_element_type=f32) + jnp.dot(cs_mid, spread, preferred_element_type=f32) +
              jnp.dot(cs_lo, spread, preferred_element_type=f32))
    lane_row = lax.broadcasted_iota(jnp.int32, (1, LANES), 1)
    cos_f = tables[:, :LANES] + jnp.where(lane_row % HEAD_DIM >= ROPE_DIM, 1.0, 0.0)
    sin_f = tables[:, LANES:]
    perm = perm_ref[...]
    ks = _rope(rms_heads(ks_ref[0].astype(f32), kg_ref[1:2, :]), cos_f, sin_f, perm).astype(bf16)
    kw = _rope(rms_heads(kw_ref[0].astype(f32), kg_ref[2:3, :]), cos_f, sin_f, perm).astype(bf16)
    row = lax.broadcasted_iota(jnp.int32, (s, LANES), 0)
    lane = lax.broadcasted_iota(jnp.int32, (s, LANES), 1)
    last_lane_one = lambda n: jnp.where(lax.broadcasted_iota(jnp.int32, (n, LANES), 1) == LANES - 1, 1.0, 0.0)
    ones_lane = last_lane_one(s)
    block_onehot = jnp.where(row // SEL_BLOCK + HEAD_DIM == lane, 1.0, 0.0) + ones_lane
    pad_lane = lax.broadcasted_iota(jnp.int32, (WINDOW, LANES), 1)
    pad_flag = jnp.where((pad_lane == HEAD_DIM) | (pad_lane == LANES - 1), 1.0, 0.0)
    for h in range(NSA_KV_HEADS):
        pick = pick_ref[h]
        kco_ref[0, h] = (jnp.dot(kc, pick, preferred_element_type=f32) + last_lane_one(nh_out)).astype(bf16)
        kso_ref[0, h] = (jnp.dot(ks, pick, preferred_element_type=f32) + block_onehot).astype(bf16)
        kwo_ref[0, h, :WINDOW, :] = pad_flag.astype(bf16)
        kwo_ref[0, h, WINDOW:, :] = (jnp.dot(kw, pick, preferred_element_type=f32) + ones_lane).astype(bf16)
    vsto_ref[0] = vs_ref[0].astype(f32).T.astype(bf16)
    vwto_ref[0] = jnp.concatenate([jnp.zeros((w, WINDOW), f32), vw_ref[0].astype(f32).T], axis=1).astype(bf16)


def _kvprep(proj, cs_t, spread, kg2, ones_bd, perm2, pick, posk, w1k, w2k, posv, w1v, w2v):
    b, s, _ = proj.shape
    w = NSA_KV_WIDTH
    assert w == LANES
    nh = s // CMP_STRIDE
    col = lambda i: pl.BlockSpec((1, s, w), lambda n: (n, 0, MAIN_OFFS[i] // w))
    whole = lambda a: pl.BlockSpec(a.shape, lambda n: (0,) * a.ndim)
    tab = pl.BlockSpec((1, cs_t.shape[1], s), lambda n: (n, 0, 0))
    out = lambda *shp: (pl.BlockSpec((1,) + shp, lambda n: (n,) + (0,) * len(shp)),
                        jax.ShapeDtypeStruct((b,) + shp, bf16))
    hk = NSA_KV_HEADS
    outs = [out(hk, nh, LANES), out(w, nh), out(hk, s, LANES), out(w, s), out(hk, WINDOW + s, LANES),
            out(w, WINDOW + s)]
    return pl.pallas_call(
        _kvprep_kernel,
        grid=(b,),
        in_specs=[col(1), col(2), col(3), col(4), col(5), col(6), tab, whole(spread), whole(kg2), whole(ones_bd),
                  whole(perm2), whole(pick), whole(posk), whole(w1k), whole(w2k), whole(posv), whole(w1v),
                  whole(w2v)],
        out_specs=[o[0] for o in outs],
        out_shape=[o[1] for o in outs],
        scratch_shapes=[pltpu.VMEM((s, w), f32)],
        compiler_params=_cparams("parallel"),
        name="nsa_kvprep",
    )(proj, proj, proj, proj, proj, proj, cs_t, spread, kg2, ones_bd, perm2, pick, posk, w1k, w2k, posv, w1v, w2v)


def _nsa_kernel(q_ref, cos_ref, sin_ref, qg_ref, mb_ref, kc_ref, vct_ref, ks_ref, vst_ref, kw_ref, vwt_ref,
                gate_ref, o_ref, *, tq, ck, bounded):
    g = NSA_GROUP
    qi = pl.program_id(2)
    t0 = pl.multiple_of(qi * tq, tq)
    scale = HEAD_DIM ** -0.5 * math.log2(math.e)
    per_head = lambda t: jnp.concatenate([t] * g, axis=1)

    qt = q_ref[0].astype(f32).T
    qt = jnp.concatenate([qt[i * HEAD_DIM:(i + 1) * HEAD_DIM] for i in range(g)], axis=1)
    qn = qt * lax.rsqrt(jnp.mean(qt * qt, axis=0, keepdims=True) + EPS) * qg_ref[...]
    half = ROPE_DIM // 2
    cos8, sin8 = per_head(cos_ref[0]), per_head(sin_ref[0])
    x1, x2 = qn[:half], qn[half:ROPE_DIM]
    q_rot = jnp.concatenate([x1 * cos8 - x2 * sin8, x2 * cos8 + x1 * sin8, qn[ROPE_DIM:]], axis=0)
    n_extra = kc_ref.shape[3] - HEAD_DIM

    def with_features(qb, feats, branch):
        used = sum(f.shape[0] for f in feats)
        last = jnp.broadcast_to(-mb_ref[0:1, branch:branch + 1] if bounded else 0.0, (1, g * tq))
        fill = jnp.zeros((n_extra - used - 1, g * tq), f32)
        return jnp.concatenate([qb] + feats + [fill, last], axis=0).astype(bf16)

    q_cmp = with_features(qn * scale, [], 0)
    q_rot = q_rot * scale
    tq_row = t0 + lax.broadcasted_iota(jnp.int32, (1, tq), 1)

    kc = kc_ref[0, 0]
    ncp = kc.shape[0]
    n_col = lax.broadcasted_iota(jnp.int32, (ncp, 1), 0)
    ok_c = (n_col * CMP_STRIDE + (CMP_BLOCK - 1)) <= tq_row
    s_c = jnp.dot(kc, q_cmp, preferred_element_type=f32) + per_head(jnp.where(ok_c, 0.0, NEG_INF))
    if bounded:
        p_c = jnp.exp2(s_c)
    else:
        m_c = jnp.max(s_c, axis=0, keepdims=True)
        p_c = jnp.exp2(s_c - m_c) * per_head(jnp.where(ok_c, 1.0, 0.0))
    den = jnp.sum(p_c, axis=0, keepdims=True)
    p_c = p_c * (1.0 / jnp.where(den > 0, den, 1.0))
    o_cmp = jnp.dot(vct_ref[0], p_c.astype(bf16), preferred_element_type=f32)

    pg = p_c[:, :tq]
    for i in range(1, g):
        pg = pg + p_c[:, i * tq:(i + 1) * tq]
    nj = ks_ref.shape[2] // SEL_BLOCK
    jj = lax.broadcasted_iota(jnp.int32, (nj, ncp), 0)
    nn = lax.broadcasted_iota(jnp.int32, (nj, ncp), 1)
    overlap = (nn * CMP_STRIDE < (jj + 1) * SEL_BLOCK) & (nn * CMP_STRIDE + CMP_BLOCK > jj * SEL_BLOCK)
    overlap = jnp.where(overlap, 1.0, 0.0).astype(bf16)
    pg_hi, pg_lo = _split_bf16(pg)
    imp = (jnp.dot(overlap, pg_hi, preferred_element_type=f32) +
           jnp.dot(overlap, pg_lo, preferred_element_type=f32))
    j_col = lax.broadcasted_iota(jnp.int32, (nj, 1), 0)
    j_f = j_col.astype(f32)
    cur = tq_row // SEL_BLOCK
    valid = j_col <= cur
    forced = (j_col == 0) | (j_col == cur) | (j_col == cur - 1)
    score = jnp.where(valid, imp + jnp.where(forced, FORCE_BONUS, 0.0), NEG_INF)
    sel = jnp.zeros((nj, tq), f32)
    for _ in range(SEL_TOPK):
        m = jnp.max(score, axis=0, keepdims=True)
        first = jnp.min(jnp.where(score == m, j_f, float(nj)), axis=0, keepdims=True)
        hit = j_f == first
        sel = jnp.where(hit, 1.0, sel)
        score = jnp.where(hit, -jnp.inf, score)
    sel = jnp.where(valid, sel, 0.0)

    def weighted_values(vt, p):
        lhs = jnp.concatenate([vt, jnp.ones((16, vt.shape[1]), bf16)], axis=0)
        r = jnp.dot(lhs, p.astype(bf16), preferred_element_type=f32)
        return r[:HEAD_DIM], r[HEAD_DIM:HEAD_DIM + 1]

    wl = WINDOW + tq
    q_win = with_features(q_rot, [jnp.full((1, g * tq), NEG_INF, f32)], 2)
    s_w = jnp.dot(kw_ref[0, 0, pl.ds(t0, wl), :], q_win, preferred_element_type=f32)
    step = lax.broadcasted_iota(jnp.int32, (tq, 1), 0)
    lo_ok = (t0 - WINDOW + step) > (tq_row - WINDOW)
    hi_ok = (t0 + step) <= tq_row
    parts = [s_w[:tq] + per_head(jnp.where(lo_ok, 0.0, NEG_INF))]
    if wl > 2 * tq:
        parts.append(s_w[tq:wl - tq])
    parts.append(s_w[wl - tq:] + per_head(jnp.where(hi_ok, 0.0, NEG_INF)))
    if bounded:
        p_w = jnp.concatenate([jnp.exp2(t) for t in parts], axis=0)
    else:
        m_w = functools.reduce(jnp.maximum, [jnp.max(t, axis=0, keepdims=True) for t in parts])
        p_w = jnp.concatenate([jnp.exp2(t - m_w) for t in parts], axis=0)
    o_win, l_w = weighted_values(vwt_ref[0, :, pl.ds(t0, wl)], p_w)
    o_win = o_win * (1.0 / l_w)

    q_sel = with_features(q_rot, [per_head(jnp.where(sel > 0.5, 0.0, NEG_INF))], 1)

    def sel_step(carry, k0, diagonal):
        m_prev, l_prev, acc = carry
        s = jnp.dot(ks_ref[0, 0, pl.ds(k0, ck), :], q_sel, preferred_element_type=f32)
        if diagonal:
            kpos = k0 + lax.broadcasted_iota(jnp.int32, (ck, 1), 0)
            s = s + per_head(jnp.where(kpos <= tq_row, 0.0, NEG_INF))
        if bounded:
            pv, p_sum = weighted_values(vst_ref[0, :, pl.ds(k0, ck)], jnp.exp2(s))
            return m_prev, l_prev + p_sum, acc + pv
        m_new = jnp.maximum(m_prev, jnp.max(s, axis=0, keepdims=True))
        alpha = jnp.exp2(m_prev - m_new)
        pv, p_sum = weighted_values(vst_ref[0, :, pl.ds(k0, ck)], jnp.exp2(s - m_new))
        return m_new, alpha * l_prev + p_sum, alpha * acc + pv

    init = (jnp.full((1, g * tq), NEG_INF, f32), jnp.zeros((1, g * tq), f32),
            jnp.zeros((HEAD_DIM, g * tq), f32))
    last = (t0 + tq + ck - 1) // ck - 1
    carry = lax.fori_loop(0, last, lambda c, cr: sel_step(cr, pl.multiple_of(c * ck, ck), False), init)
    _, l_s, acc_s = sel_step(carry, pl.multiple_of(last * ck, ck), True)
    o_sel = acc_s * (1.0 / l_s)

    gt = gate_ref[0].T
    gate = lambda j: jax.nn.sigmoid(jnp.concatenate([gt[i * 3 + j:i * 3 + j + 1] for i in range(g)], axis=1))
    ot = gate(0) * o_cmp + gate(1) * o_sel + gate(2) * o_win
    o2 = jnp.concatenate([ot[:, i * tq:(i + 1) * tq] for i in range(g)], axis=0)
    o_ref[0] = o2.T.astype(o_ref.dtype)


def _nsa_attention(proj, cos_t, sin_t, qg, bounds, kc, vct, ks, vst, kw, vwt, gate_logits, *, bounded,
                   tq=256, ck=512):
    b, s, _ = proj.shape
    hk, g, dh = NSA_KV_HEADS, NSA_GROUP, HEAD_DIM
    assert WINDOW >= tq and ck % tq == 0 and s % ck == 0
    tab = pl.BlockSpec((1, ROPE_DIM // 2, tq), lambda i, j, t: (i, 0, t))
    assert s // SEL_BLOCK <= kc.shape[3] - dh
    keys = lambda a: pl.BlockSpec((1, 1) + a.shape[2:], lambda i, j, t: (i, j, 0, 0))
    vals = lambda a: pl.BlockSpec((1, dh, a.shape[2]), lambda i, j, t: (i, j, 0))
    return pl.pallas_call(
        functools.partial(_nsa_kernel, tq=tq, ck=ck, bounded=bounded),
        grid=(b, hk, s // tq),
        in_specs=[pl.BlockSpec((1, tq, g * dh), lambda i, j, t: (i, t, j)),
                  tab, tab, pl.BlockSpec((dh, 1), lambda i, j, t: (0, 0)),
                  pl.BlockSpec(bounds.shape, lambda i, j, t: (0, 0)),
                  keys(kc), vals(vct), keys(ks), vals(vst), keys(kw), vals(vwt),
                  pl.BlockSpec((1, tq, LANES), lambda i, j, t: (i, t, j))],
        out_specs=pl.BlockSpec((1, tq, g * dh), lambda i, j, t: (i, t, j)),
        out_shape=jax.ShapeDtypeStruct((b, s, NSA_WIDTH), bf16),
        compiler_params=_cparams("parallel", "parallel", "arbitrary"),
        name="nsa_attention_bounded" if bounded else "nsa_attention",
    )(proj, cos_t, sin_t, qg, bounds, kc, vct, ks, vst, kw, vwt, gate_logits)


def _gla_kernel(q_ref, k_ref, v_ref, lr_ref, wg_ref, bg_ref, r_ref, ng_ref, o_ref, *, grp, npair):
    c = GLA_CHUNK
    s = q_ref.shape[1]
    dk, dv = GLA_KEY_DIM, GLA_VAL_DIM
    rows = grp * c
    ri = lax.broadcasted_iota(jnp.int32, (rows, rows), 0)
    ci = lax.broadcasted_iota(jnp.int32, (rows, rows), 1)
    causal = (ci <= ri) & (ci // c == ri // c)
    tri = jnp.where(causal, 1.0, 0.0).astype(bf16)
    lane = lax.broadcasted_iota(jnp.int32, (1, 2 * dk), 1)
    head_mask = [lane < dk, lane >= dk]
    chunk_of_row = lax.broadcasted_iota(jnp.int32, (rows, 1), 0) // c
    wg = wg_ref[...]
    bg = bg_ref[...]
    ng = ng_ref[...]

    def body(n, states):
        r0 = pl.multiple_of(n * rows, rows)
        z = jnp.dot(lr_ref[0, pl.ds(r0, rows), :GLA_GATE_RANK], wg, preferred_element_type=f32,
                    precision=lax.Precision.HIGHEST) + bg
        log_a = -(jnp.maximum(-z, 0.0) + jnp.log(1.0 + jnp.exp(-jnp.abs(z)))) / GLA_TAU
        la_hi = log_a.astype(bf16)
        la_mid, la_lo = _split_bf16(log_a - la_hi.astype(f32))
        bcum = (jnp.dot(tri, la_hi, preferred_element_type=f32) + jnp.dot(tri, la_mid, preferred_element_type=f32) +
                jnp.dot(tri, la_lo, preferred_element_type=f32))
        qf = q_ref[0, pl.ds(r0, rows), :].astype(f32) * (dk ** -0.5)
        kf = k_ref[0, pl.ds(r0, rows), :].astype(f32)
        q_dec = qf * jnp.exp(bcum)
        k_dec = (kf * jnp.exp(-bcum)).astype(bf16)
        b_last = [bcum[(i + 1) * c - 1:(i + 1) * c, :] for i in range(grp)]
        k_state = jnp.concatenate(
            [kf[i * c:(i + 1) * c] * jnp.exp(b_last[i] - bcum[i * c:(i + 1) * c]) for i in range(grp)],
            axis=0).astype(bf16)
        by_chunk = lambda t: jnp.concatenate(
            [jnp.where(chunk_of_row == i, t, jnp.zeros_like(t)) for i in range(grp)], axis=1)
        new_states = []
        for hh in range(2 * npair):
            pr, h = divmod(hh, 2)
            pair = slice(pr * 2 * dk, (pr + 1) * 2 * dk)
            q_h = jnp.where(head_mask[h], q_dec[:, pair], 0.0).astype(bf16)
            v_h = v_ref[0, pl.ds(r0, rows), hh * dv:(hh + 1) * dv]
            attn = lax.dot_general(q_h, k_dec[:, pair], _NT, preferred_element_type=f32)
            attn = jnp.where(causal, attn, 0.0).astype(bf16)
            o_intra = jnp.dot(attn, v_h, preferred_element_type=f32)
            u_all = lax.dot_general(v_h, by_chunk(k_state[:, pair]), _TN, preferred_element_type=f32)
            st = states[hh]
            entering = []
            for i in range(grp):
                entering.append(st)
                st = st * jnp.exp(b_last[i][:, pair]) + u_all[:, i * 2 * dk:(i + 1) * 2 * dk]
            new_states.append(st)
            o_inter = lax.dot_general(by_chunk(q_h), jnp.concatenate(entering, axis=1).astype(bf16), _NT,
                                      preferred_element_type=f32)
            o = _rms(o_intra + o_inter, ng)
            rr = r_ref[0, pl.ds(r0, rows), hh * dv:(hh + 1) * dv].astype(f32)
            o_ref[0, pl.ds(r0, rows), hh * dv:(hh + 1) * dv] = (o * (rr * jax.nn.sigmoid(rr))).astype(o_ref.dtype)
        return tuple(new_states)

    zero = jnp.zeros((dv, 2 * dk), f32)
    lax.fori_loop(0, s // rows, body, (zero,) * (2 * npair), unroll=GLA_UNROLL)


def _gla(proj, proj_s, wg, bg, ng, grp=4, npair=GLA_HEADS // 2):
    b, s, _ = proj.shape
    dk2, dv2 = 2 * npair * GLA_KEY_DIM, 2 * npair * GLA_VAL_DIM
    oq, ok_, ov, orr = (MAIN_OFFS[7] // dk2, MAIN_OFFS[8] // dk2, MAIN_OFFS[9] // dv2, MAIN_OFFS[10] // dv2)
    assert all(MAIN_OFFS[i] % w == 0 for i, w in ((7, dk2), (8, dk2), (9, dv2), (10, dv2)))
    col = lambda w, o: pl.BlockSpec((1, s, w), lambda i, j: (i, 0, o + j))
    return pl.pallas_call(
        functools.partial(_gla_kernel, grp=grp, npair=npair),
        grid=(b, GLA_HEADS // (2 * npair)),
        in_specs=[col(dk2, oq), col(dk2, ok_), col(dv2, ov),
                  pl.BlockSpec((1, s, LANES), lambda i, j: (i, 0, 2)),
                  pl.BlockSpec((GLA_GATE_RANK, dk2), lambda i, j: (0, j)),
                  pl.BlockSpec((1, dk2), lambda i, j: (0, j)),
                  col(dv2, orr),
                  pl.BlockSpec((1, GLA_VAL_DIM), lambda i, j: (0, 0))],
        out_specs=pl.BlockSpec((1, s, dv2), lambda i, j: (i, 0, j)),
        out_shape=jax.ShapeDtypeStruct((b, s, GLA_WIDTH), bf16),
        compiler_params=_cparams("parallel", "parallel"),
        name="gla",
    )(proj, proj, proj, proj_s, wg, bg, proj, ng)


def _outproj_kernel(a_ref, b_ref, x_ref, wo_ref, g2_ref, wrh_ref, wrl_ref, br_ref,
                    x1_ref, h2_ref, rt_ref, cnt_ref, carry_ref):
    @pl.when(pl.program_id(0) == 0)
    def _():
        carry_ref[...] = jnp.zeros_like(carry_ref)

    na = a_ref.shape[1]
    tm = a_ref.shape[0]
    y = (jnp.dot(a_ref[...], wo_ref[:na, :], preferred_element_type=f32) +
         jnp.dot(b_ref[...], wo_ref[na:, :], preferred_element_type=f32))
    x1 = x_ref[...] + y
    x1_ref[...] = x1
    h2 = _rms(x1, g2_ref[...])
    h2_ref[...] = h2.astype(bf16)
    hi, lo = _split_bf16(h2)
    logits = (jnp.dot(hi, wrh_ref[...], preferred_element_type=f32) +
              jnp.dot(lo, wrh_ref[...], preferred_element_type=f32) +
              jnp.dot(hi, wrl_ref[...], preferred_element_type=f32)) + br_ref[...]
    score = logits.T[:N_EXPERTS]
    e_col = lax.broadcasted_iota(jnp.int32, (N_EXPERTS, 1), 0).astype(f32)
    top = jnp.max(score, axis=0, keepdims=True)
    hits, firsts, weights = [], [], []
    for _ in range(TOP_K):
        m = jnp.max(score, axis=0, keepdims=True)
        first = jnp.min(jnp.where(score == m, e_col, float(N_EXPERTS)), axis=0, keepdims=True)
        hit = e_col == first
        hits.append(hit)
        firsts.append(first)
        weights.append(jnp.exp(m - top))
        score = jnp.where(hit, -jnp.inf, score)
    tot = weights[0] + weights[1] + weights[2] + weights[3]
    onehot = jnp.where(hits[0] | hits[1] | hits[2] | hits[3], 1.0, 0.0)
    ri = lax.broadcasted_iota(jnp.int32, (tm, tm), 0)
    ci = lax.broadcasted_iota(jnp.int32, (tm, tm), 1)
    earlier = jnp.where(ri < ci, 1.0, 0.0).astype(bf16)
    rank = carry_ref[...] + jnp.dot(onehot.astype(bf16), earlier, preferred_element_type=f32)
    carry_ref[...] = carry_ref[...] + jnp.sum(onehot, axis=1, keepdims=True)
    cnt_ref[...] = jnp.broadcast_to(carry_ref[...], cnt_ref.shape)
    ranks = [jnp.sum(jnp.where(hits[k], rank, 0.0), axis=0, keepdims=True) for k in range(TOP_K)]
    gates = [weights[k] / tot for k in range(TOP_K)]
    pad = jnp.zeros((rt_ref.shape[0] - 3 * TOP_K, tm), f32)
    rt_ref[...] = jnp.concatenate(firsts + gates + ranks + [pad], axis=0)


def _outproj_router(a, b, x2, wo, g2, wr_hi, wr_lo, br, tm=512):
    m, d = x2.shape
    na, nb = a.shape[1], b.shape[1]
    row = lambda n: pl.BlockSpec((tm, n), lambda i: (i, 0))
    whole = lambda t: pl.BlockSpec(t.shape, lambda i: (0,) * t.ndim)
    return pl.pallas_call(
        _outproj_kernel,
        grid=(m // tm,),
        in_specs=[row(na), row(nb), row(d), whole(wo), whole(g2), whole(wr_hi), whole(wr_lo), whole(br)],
        out_specs=[row(d), row(d), pl.BlockSpec((ROUTE_ROWS, tm), lambda i: (0, i)),
                   pl.BlockSpec((N_EXPERTS, LANES), lambda i: (0, 0))],
        out_shape=[jax.ShapeDtypeStruct((m, d), f32),
                   jax.ShapeDtypeStruct((m, d), bf16),
                   jax.ShapeDtypeStruct((ROUTE_ROWS, m), f32),
                   jax.ShapeDtypeStruct((N_EXPERTS, LANES), f32)],
        scratch_shapes=[pltpu.VMEM((N_EXPERTS, 1), f32)],
        compiler_params=_cparams("arbitrary"),
        name="outproj_router",
    )(a, b, x2, wo, g2, wr_hi, wr_lo, br)


def _moe_kernel(vb_ref, ve_ref, lo_ref, hi_ref, nv_ref, xs_ref, wgu_ref, bgu_ref, wd_ref, bd_ref, *rest,
                slice_id, table_off):
    o_ref, wgu_bf, wd_bf = rest[-3:]
    step = pl.program_id(0)
    v = table_off + step
    prev = jnp.maximum(v - 1, table_off)
    live = step < nv_ref[slice_id]
    new_expert = (step == 0) | (ve_ref[v] != ve_ref[prev])
    new_block = (step == 0) | (vb_ref[v] != vb_ref[prev])

    @pl.when(live & new_expert)
    def _():
        wgu_bf[...] = wgu_ref[0].astype(bf16)
        wd_bf[...] = wd_ref[0].astype(bf16)

    @pl.when(live & new_block)
    def _():
        o_ref[...] = jnp.zeros_like(o_ref)

    @pl.when(live)
    def _():
        blk0 = vb_ref[v] * MOE_BLOCK
        lo, hi = lo_ref[v], hi_ref[v]
        first = jnp.maximum(lo - blk0, 0) // MOE_SUB
        last = (jnp.minimum(hi - blk0, MOE_BLOCK) + MOE_SUB - 1) // MOE_SUB

        def sub_block(j, carry):
            r0 = pl.multiple_of(j * MOE_SUB, MOE_SUB)
            h = jnp.dot(xs_ref[pl.ds(r0, MOE_SUB), :], wgu_bf[...], preferred_element_type=f32) + bgu_ref[0]
            x_glu = jnp.minimum(h[:, :D_FF], SWIGLU_LIMIT)
            x_lin = jnp.clip(h[:, D_FF:], -SWIGLU_LIMIT, SWIGLU_LIMIT)
            act = x_glu * jax.nn.sigmoid(SWIGLU_ALPHA * x_glu) * (x_lin + 1.0)
            y = jnp.dot(act.astype(bf16), wd_bf[...], preferred_element_type=f32) + bd_ref[0]
            row = blk0 + r0 + lax.broadcasted_iota(jnp.int32, (MOE_SUB, 1), 0)
            mine = (row >= lo) & (row < hi)
            o_ref[pl.ds(r0, MOE_SUB), :] = jnp.where(mine, y.astype(o_ref.dtype), o_ref[pl.ds(r0, MOE_SUB), :])
            return carry

        lax.fori_loop(first, last, sub_block, 0)


def _moe_experts(table, slice_id, blk_lo, xs, wgu, bgu, wd, bd, n_rows, ys_so_far=None):
    d = xs.shape[1]
    nvis = xs.shape[0] // MOE_BLOCK + N_EXPERTS - 1
    off = slice_id * (table[0].shape[0] // table[4].shape[0])
    in_specs = [pl.BlockSpec((MOE_BLOCK, d), lambda v, vb, ve, *_: (vb[off + v] - blk_lo, 0)),
                pl.BlockSpec((1, d, 2 * D_FF), lambda v, vb, ve, *_: (ve[off + v], 0, 0)),
                pl.BlockSpec((1, 1, 2 * D_FF), lambda v, vb, ve, *_: (ve[off + v], 0, 0)),
                pl.BlockSpec((1, D_FF, d), lambda v, vb, ve, *_: (ve[off + v], 0, 0)),
                pl.BlockSpec((1, 1, d), lambda v, vb, ve, *_: (ve[off + v], 0, 0))]
    operands = table + (xs, wgu, bgu, wd, bd)
    aliases = {}
    if ys_so_far is not None:
        in_specs.append(pl.BlockSpec(memory_space=pl.ANY))
        aliases = {len(operands): 0}
        operands += (ys_so_far,)
    return pl.pallas_call(
        functools.partial(_moe_kernel, slice_id=slice_id, table_off=off),
        grid_spec=pltpu.PrefetchScalarGridSpec(
            num_scalar_prefetch=5,
            grid=(nvis,),
            in_specs=in_specs,
            out_specs=pl.BlockSpec((MOE_BLOCK, d), lambda v, vb, ve, *_: (vb[off + v], 0)),
            scratch_shapes=[pltpu.VMEM((d, 2 * D_FF), bf16), pltpu.VMEM((D_FF, d), bf16)],
        ),
        out_shape=jax.ShapeDtypeStruct((n_rows, d), bf16),
        input_output_aliases=aliases,
        compiler_params=_cparams("arbitrary"),
        name="moe_experts",
    )(*operands)


def _positions_kernel(start_ref, rt_ref, pos_ref):
    idx = rt_ref[:TOP_K, :].astype(jnp.int32)
    pos = rt_ref[2 * TOP_K:3 * TOP_K, :].astype(jnp.int32)
    for e in range(N_EXPERTS):
        pos = pos + jnp.where(idx == e, start_ref[e], 0)
    pos_ref[...] = pos


def _positions(start, rt):
    m = rt.shape[1]
    return pl.pallas_call(
        _positions_kernel,
        grid_spec=pltpu.PrefetchScalarGridSpec(
            num_scalar_prefetch=1, grid=(1,),
            in_specs=[pl.BlockSpec(rt.shape, lambda i, st: (0, 0))],
            out_specs=pl.BlockSpec((TOP_K, m), lambda i, st: (0, 0))),
        out_shape=jax.ShapeDtypeStruct((TOP_K, m), jnp.int32),
        compiler_params=_cparams("arbitrary"),
        name="route_positions",
    )(start, rt)


def _visit_tables(start, end, cuts):
    blk_lo = jnp.array([0] + list(cuts[:-1]), jnp.int32)[:, None]
    blk_hi = jnp.array(list(cuts), jnp.int32)[:, None]
    room = max(h - l for l, h in zip([0] + list(cuts[:-1]), cuts)) + N_EXPERTS - 1
    first_blk = jnp.maximum((start // MOE_BLOCK)[None, :], blk_lo)
    last_blk = jnp.minimum(jnp.where(end > start, (end - 1) // MOE_BLOCK, -1)[None, :], blk_hi - 1)
    per_e = jnp.maximum(last_blk - first_blk + 1, 0)
    v_end = jnp.cumsum(per_e, axis=1)
    v_start = v_end - per_e
    n_live = v_end[:, -1]
    vc = jnp.minimum(jnp.arange(room, dtype=jnp.int32)[None, :], n_live[:, None] - 1)
    ve = jnp.sum((v_end[:, None, :] <= vc[:, :, None]).astype(jnp.int32), axis=2)
    of_ve = lambda t: jnp.sum(jnp.where(ve[:, :, None] == jnp.arange(N_EXPERTS), t[:, None, :], 0), axis=2)
    vb = of_ve(first_blk) + vc - of_ve(v_start)
    flat = lambda t: t.reshape(-1).astype(jnp.int32)
    return (flat(vb), flat(ve), flat(of_ve(jnp.broadcast_to(start, per_e.shape))),
            flat(of_ve(jnp.broadcast_to(end, per_e.shape))), n_live.astype(jnp.int32))


def _rope_tables(positions):
    half = ROPE_DIM // 2
    inv_freq = jnp.exp(-math.log(ROPE_THETA) * jnp.arange(0, ROPE_DIM, 2, dtype=f32) / ROPE_DIM)
    ang = positions.astype(f32)[:, None, :] * inv_freq[None, :, None]
    cos_t, sin_t = jnp.cos(ang), jnp.sin(ang)
    d = jnp.arange(LANES) % HEAD_DIM
    f = jnp.arange(LANES)[:, None]
    first, second = d[None, :] == f % half, d[None, :] == f % half + half
    to_cos = jnp.where((f < half) & (first | second), 1.0, 0.0)
    to_sin = jnp.where((f >= half) & (f < 2 * half), jnp.where(second, 1.0, 0.0) - jnp.where(first, 1.0, 0.0), 0.0)
    spread = jnp.concatenate([to_cos, to_sin], axis=1).astype(bf16)
    idx = jnp.arange(half)
    perm = jnp.zeros((HEAD_DIM, HEAD_DIM), f32).at[idx + half, idx].set(1.0).at[idx, idx + half].set(1.0)
    return cos_t, sin_t, spread, perm


def _layer(x, positions, norm1_g, w_in, q_norm_g, k_norm_g, cmp_pos_k, cmp_w1_k, cmp_w2_k,
           cmp_pos_v, cmp_w1_v, cmp_w2_v, gla_w_gate, gla_b_gate, gla_norm_g, w_out, norm2_g,
           w_router, b_router, w_gate_up, b_gate_up, w_down, b_down):
    b, s, d = x.shape
    m = b * s
    hk, g, dh = NSA_KV_HEADS, NSA_GROUP, HEAD_DIM
    x2 = x.reshape(m, d)

    offs = [0]
    for sz in IN_SIZES:
        offs.append(offs[-1] + sz)
    seg = lambda i: w_in[:, offs[i]:offs[i + 1]]
    w_main = jnp.concatenate([seg(MAIN_IN_SEG[i]) for i in MAIN_LAYOUT], axis=1).astype(bf16)
    padw = lambda t: jnp.concatenate([t, jnp.zeros((d, LANES - t.shape[1]), f32)], axis=1)
    ng = g * 3
    w_small = jnp.concatenate([padw(seg(7)[:, :ng]), padw(seg(7)[:, ng:]), padw(seg(11))], axis=1).astype(bf16)
    proj, proj_s = _inproj(x2, norm1_g.reshape(1, d), w_main, w_small)
    proj = proj.reshape(b, s, -1)
    proj_s = proj_s.reshape(b, s, 3 * LANES)

    cos_t, sin_t, spread, perm = _rope_tables(positions)
    per_kv = lambda t: jnp.concatenate([t] * hk, axis=-1)

    def bdiag(t):
        r, c = t.shape[-2:]
        lead = [(0, 0)] * (t.ndim - 2)
        return jnp.concatenate([jnp.pad(t, lead + [(0, 0), (i * c, (hk - 1 - i) * c)]) for i in range(hk)], axis=-2)

    w1_bd = lambda w1: bdiag(w1.reshape(CMP_BLOCK, dh, CMP_HIDDEN)).astype(bf16)
    kc, vct, ks_rot, vst, kw_rot, vwt = _kvprep(
        proj, jnp.concatenate([cos_t, sin_t], axis=1), spread, per_kv(k_norm_g),
        bdiag(jnp.ones((dh, dh), f32)).astype(bf16), bdiag(perm).astype(bf16),
        jnp.stack([jnp.eye(hk * dh, LANES, k=-h * dh, dtype=f32) * (jnp.arange(LANES) < dh)
                   for h in range(hk)]).astype(bf16),
        per_kv(cmp_pos_k), w1_bd(cmp_w1_k), bdiag(cmp_w2_k).astype(bf16),
        per_kv(cmp_pos_v), w1_bd(cmp_w1_v), bdiag(cmp_w2_v).astype(bf16))
    score_bound = (1.05 * dh * dh ** -0.5 * math.log2(math.e) * jnp.max(jnp.abs(q_norm_g)) *
                   jnp.max(jnp.abs(k_norm_g), axis=1))
    bounds = jnp.zeros((8, LANES), f32).at[0, :3].set(score_bound)
    nsa_args = (proj, cos_t, sin_t, q_norm_g.reshape(dh, 1), bounds, kc, vct, ks_rot, vst, kw_rot, vwt, proj_s)
    nsa_out = lax.cond(2.0 * jnp.max(score_bound) < NSA_SAFE_EXPONENT,
                       functools.partial(_nsa_attention, bounded=True),
                       functools.partial(_nsa_attention, bounded=False), *nsa_args).reshape(m, NSA_WIDTH)

    gla_out = _gla(proj, proj_s, gla_w_gate, gla_b_gate.reshape(1, -1),
                   gla_norm_g.reshape(1, GLA_VAL_DIM)).reshape(m, GLA_WIDTH)

    padr = lambda t: jnp.concatenate([t, jnp.zeros(t.shape[:-1] + (LANES - t.shape[-1],), t.dtype)], axis=-1)
    wr_hi = w_router.astype(bf16)
    wr_lo = (w_router - wr_hi.astype(f32)).astype(bf16)
    x1, h2, rt, cnt = _outproj_router(nsa_out, gla_out, x2, w_out.astype(bf16), norm2_g.reshape(1, d),
                                      padr(wr_hi), padr(wr_lo), padr(b_router.reshape(1, -1)))

    a = m * TOP_K
    gate = rt[TOP_K:2 * TOP_K]
    counts = cnt[:, 0].astype(jnp.int32)
    end = jnp.cumsum(counts)
    start = end - counts
    pos = _positions(start, rt).reshape(a)
    tok = jnp.arange(a, dtype=jnp.int32) % m
    sorted_tok = jnp.zeros((a,), jnp.int32).at[pos].add(tok, unique_indices=True, mode='promise_in_bounds')
    rows_of = lambda t, idx: t.at[idx].get(mode='promise_in_bounds')
    nblk = a // MOE_BLOCK
    cuts = [nblk * c // MOE_SLICE_CUTS[-1] for c in MOE_SLICE_CUTS] if nblk % MOE_SLICE_CUTS[-1] == 0 else [nblk]
    table = _visit_tables(start, end, cuts)
    ys, lo_blk = None, 0
    for i, hi_blk in enumerate(cuts):
        xs = rows_of(h2, sorted_tok[lo_blk * MOE_BLOCK:hi_blk * MOE_BLOCK])
        ys = _moe_experts(table, i, lo_blk, xs, w_gate_up, b_gate_up.reshape(N_EXPERTS, 1, -1),
                          w_down, b_down.reshape(N_EXPERTS, 1, -1), a, ys)
        lo_blk = hi_blk
    out = x1
    for k in range(TOP_K):
        out = out + rows_of(ys, pos[k * m:(k + 1) * m]).astype(f32) * gate[k][:, None]
    return out.reshape(b, s, d)


def kernel(x, positions, norm1_g, w_in, nsa_q_norm_g, nsa_k_norm_g, cmp_pos_k, cmp_w1_k, cmp_w2_k,
           cmp_pos_v, cmp_w1_v, cmp_w2_v, gla_w_gate, gla_b_gate, gla_norm_g, w_out, norm2_g,
           w_router, b_router, w_gate_up, b_gate_up, w_down, b_down):
    for l in range(norm1_g.shape[0]):
        x = _layer(x, positions, norm1_g[l], w_in[l], nsa_q_norm_g[l], nsa_k_norm_g[l],
                   cmp_pos_k[l], cmp_w1_k[l], cmp_w2_k[l], cmp_pos_v[l], cmp_w1_v[l], cmp_w2_v[l],
                   gla_w_gate[l], gla_b_gate[l], gla_norm_g[l], w_out[l], norm2_g[l],
                   w_router[l], b_router[l], w_gate_up[l], b_gate_up[l], w_down[l], b_down[l])
    return x
```

```python
import functools
import math

import jax
import jax.numpy as jnp
from jax import lax
from jax.experimental import pallas as pl
from jax.experimental.pallas import tpu as pltpu

f32 = jnp.float32
bf16 = jnp.bfloat16

NSA_HEADS = 8
NSA_KV_HEADS = 2
NSA_GROUP = NSA_HEADS // NSA_KV_HEADS
HEAD_DIM = 64
CMP_BLOCK = 32
CMP_STRIDE = 16
CMP_HIDDEN = 256
SEL_BLOCK = 64
SEL_TOPK = 8
WINDOW = 512
FORCE_BONUS = 1e4
GLA_HEADS = 4
GLA_KEY_DIM = 64
GLA_VAL_DIM = 128
GLA_CHUNK = 64
GLA_GATE_RANK = 16
GLA_TAU = 16.0
ROPE_THETA = 500000.0
ROPE_DIM = HEAD_DIM // 4
N_EXPERTS = 32
TOP_K = 4
D_FF = 1024
SWIGLU_LIMIT = 7.0
SWIGLU_ALPHA = 1.702
EPS = 1e-6
NEG_INF = -1e30

NSA_WIDTH = NSA_HEADS * HEAD_DIM
NSA_KV_WIDTH = NSA_KV_HEADS * HEAD_DIM
GLA_KEY_WIDTH = GLA_HEADS * GLA_KEY_DIM
GLA_WIDTH = GLA_HEADS * GLA_VAL_DIM
IN_SIZES = (NSA_WIDTH,) + (NSA_KV_WIDTH,) * 6 + (
    NSA_HEADS * 3, GLA_KEY_WIDTH, GLA_KEY_WIDTH, GLA_WIDTH, GLA_GATE_RANK, GLA_WIDTH)
MAIN_SIZES = (NSA_WIDTH,) + (NSA_KV_WIDTH,) * 6 + (GLA_KEY_WIDTH, GLA_KEY_WIDTH, GLA_WIDTH, GLA_WIDTH)
MAIN_IN_SEG = (0, 1, 2, 3, 4, 5, 6, 8, 9, 10, 12)
MAIN_LAYOUT = (0, 9, 10, 1, 2, 3, 4, 5, 6, 7, 8)
MAIN_OFFS = tuple(sum(MAIN_SIZES[j] for j in MAIN_LAYOUT[:MAIN_LAYOUT.index(i)]) for i in range(len(MAIN_SIZES)))

LANES = 128
VMEM_LIMIT = 48 * 1024 * 1024
MOE_BLOCK = 1024
MOE_SUB = 256
MOE_SLICE_CUTS = (1, 4, 8, 16)
NSA_SAFE_EXPONENT = 120.0
GLA_UNROLL = 4
ROUTE_ROWS = 16

_NT = (((1,), (1,)), ((), ()))
_TN = (((0,), (0,)), ((), ()))


def _cparams(*sem):
    return pltpu.CompilerParams(dimension_semantics=sem, vmem_limit_bytes=VMEM_LIMIT)


def _rms(t, g):
    return t * lax.rsqrt(jnp.mean(t * t, axis=-1, keepdims=True) + EPS) * g


def _split_bf16(t):
    hi = t.astype(bf16)
    lo = (t - hi.astype(f32)).astype(bf16)
    return hi, lo


def _rope(t, cos_f, sin_f, perm):
    hi, lo = _split_bf16(t)
    rot = (jnp.dot(hi, perm, preferred_element_type=f32) +
           jnp.dot(lo, perm, preferred_element_type=f32))
    return t * cos_f + rot * sin_f


def _inproj_kernel(x_ref, g_ref, w_ref, ws_ref, o_ref, os_ref):
    x = x_ref[...]
    h = _rms(x, g_ref[...]).astype(bf16)
    n = o_ref.shape[1]
    step = 512
    for c in range(0, n, step):
        e = min(c + step, n)
        o_ref[:, c:e] = jnp.dot(h, w_ref[:, c:e], preferred_element_type=f32).astype(bf16)
    os_ref[...] = jnp.dot(h, ws_ref[...], preferred_element_type=f32)


def _inproj(x2, g, w_main, w_small, tm=512):
    m, d = x2.shape
    n = w_main.shape[1]
    ns = w_small.shape[1]
    return pl.pallas_call(
        _inproj_kernel,
        grid=(m // tm,),
        in_specs=[pl.BlockSpec((tm, d), lambda i: (i, 0)),
                  pl.BlockSpec((1, d), lambda i: (0, 0)),
                  pl.BlockSpec((d, n), lambda i: (0, 0)),
                  pl.BlockSpec((d, ns), lambda i: (0, 0))],
        out_specs=[pl.BlockSpec((tm, n), lambda i: (i, 0)),
                   pl.BlockSpec((tm, ns), lambda i: (i, 0))],
        out_shape=[jax.ShapeDtypeStruct((m, n), bf16),
                   jax.ShapeDtypeStruct((m, ns), f32)],
        compiler_params=_cparams("parallel"),
        name="inproj",
    )(x2, g, w_main, w_small)


def _kvprep_kernel(kc_ref, vc_ref, ks_ref, vs_ref, kw_ref, vw_ref, cs_ref, spread_ref, kg_ref, ones_ref,
                   perm_ref, pick_ref, posk_ref, w1k_ref, w2k_ref, posv_ref, w1v_ref, w2v_ref,
                   kco_ref, vcto_ref, kso_ref, vsto_ref, kwo_ref, vwto_ref, tmp_ref):
    ones_bd = ones_ref[...]
    nh_out = kco_ref.shape[2]

    def rms_heads(t, g):
        sq_hi, sq_lo = _split_bf16(t * t)
        ss = (jnp.dot(sq_hi, ones_bd, preferred_element_type=f32) +
              jnp.dot(sq_lo, ones_bd, preferred_element_type=f32))
        return t * lax.rsqrt(ss * (1.0 / HEAD_DIM) + EPS) * g

    def compress(src_ref, pos_ref, w1_ref, w2_ref):
        tmp_ref[...] = src_ref[0].astype(f32)
        nh = tmp_ref.shape[0] // CMP_STRIDE
        a = jnp.zeros((nh, NSA_KV_HEADS * CMP_HIDDEN), f32)
        b = jnp.zeros((nh, NSA_KV_HEADS * CMP_HIDDEN), f32)
        for p in range(CMP_STRIDE):
            rows = tmp_ref[pl.ds(p, nh, stride=CMP_STRIDE), :]
            a = a + jnp.dot((rows + pos_ref[p:p + 1, :]).astype(bf16), w1_ref[p], preferred_element_type=f32)
            q = CMP_STRIDE + p
            b = b + jnp.dot((rows + pos_ref[q:q + 1, :]).astype(bf16), w1_ref[q], preferred_element_type=f32)
        pre = a + pltpu.roll(b, nh - 1, 0)
        hid = pre * jax.nn.sigmoid(pre)
        return jnp.dot(hid.astype(bf16), w2_ref[...], preferred_element_type=f32)

    kc = rms_heads(compress(kc_ref, posk_ref, w1k_ref, w2k_ref), kg_ref[0:1, :]).astype(bf16)
    tmp_ref[:nh_out, :] = compress(vc_ref, posv_ref, w1v_ref, w2v_ref)
    vcto_ref[0] = tmp_ref[:nh_out, :].T.astype(bf16)
    s, w = ks_ref.shape[1], ks_ref.shape[2]
    cs = jnp.concatenate([cs_ref[0], jnp.zeros((LANES - cs_ref.shape[1], s), f32)], axis=0).T
    cs_hi = cs.astype(bf16)
    cs_mid, cs_lo = _split_bf16(cs - cs_hi.astype(f32))
    spread = spread_ref[...]
    tables = (jnp.dot(cs_hi, spread, preferred_element_type=f32) + jnp.dot(cs_mid, spread, preferred_element_type=f32) +
              jnp.dot(cs_lo, spread, preferred_element_type=f32))
    lane_row = lax.broadcasted_iota(jnp.int32, (1, LANES), 1)
    cos_f = tables[:, :LANES] + jnp.where(lane_row % HEAD_DIM >= ROPE_DIM, 1.0, 0.0)
    sin_f = tables[:, LANES:]
    perm = perm_ref[...]
    ks = _rope(rms_heads(ks_ref[0].astype(f32), kg_ref[1:2, :]), cos_f, sin_f, perm).astype(bf16)
    kw = _rope(rms_heads(kw_ref[0].astype(f32), kg_ref[2:3, :]), cos_f, sin_f, perm).astype(bf16)
    row = lax.broadcasted_iota(jnp.int32, (s, LANES), 0)
    lane = lax.broadcasted_iota(jnp.int32, (s, LANES), 1)
    last_lane_one = lambda n: jnp.where(lax.broadcasted_iota(jnp.int32, (n, LANES), 1) == LANES - 1, 1.0, 0.0)
    ones_lane = last_lane_one(s)
    block_onehot = jnp.where(row // SEL_BLOCK + HEAD_DIM == lane, 1.0, 0.0) + ones_lane
    pad_lane = lax.broadcasted_iota(jnp.int32, (WINDOW, LANES), 1)
    pad_flag = jnp.where((pad_lane == HEAD_DIM) | (pad_lane == LANES - 1), 1.0, 0.0)
    for h in range(NSA_KV_HEADS):
        pick = pick_ref[h]
        kco_ref[0, h] = (jnp.dot(kc, pick, preferred_element_type=f32) + last_lane_one(nh_out)).astype(bf16)
        kso_ref[0, h] = (jnp.dot(ks, pick, preferred_element_type=f32) + block_onehot).astype(bf16)
        kwo_ref[0, h, :WINDOW, :] = pad_flag.astype(bf16)
        kwo_ref[0, h, WINDOW:, :] = (jnp.dot(kw, pick, preferred_element_type=f32) + ones_lane).astype(bf16)
    vsto_ref[0] = vs_ref[0].astype(f32).T.astype(bf16)
    vwto_ref[0] = jnp.concatenate([jnp.zeros((w, WINDOW), f32), vw_ref[0].astype(f32).T], axis=1).astype(bf16)


def _kvprep(proj, cs_t, spread, kg2, ones_bd, perm2, pick, posk, w1k, w2k, posv, w1v, w2v):
    b, s, _ = proj.shape
    w = NSA_KV_WIDTH
    assert w == LANES
    nh = s // CMP_STRIDE
    col = lambda i: pl.BlockSpec((1, s, w), lambda n: (n, 0, MAIN_OFFS[i] // w))
    whole = lambda a: pl.BlockSpec(a.shape, lambda n: (0,) * a.ndim)
    tab = pl.BlockSpec((1, cs_t.shape[1], s), lambda n: (n, 0, 0))
    out = lambda *shp: (pl.BlockSpec((1,) + shp, lambda n: (n,) + (0,) * len(shp)),
                        jax.ShapeDtypeStruct((b,) + shp, bf16))
    hk = NSA_KV_HEADS
    outs = [out(hk, nh, LANES), out(w, nh), out(hk, s, LANES), out(w, s), out(hk, WINDOW + s, LANES),
            out(w, WINDOW + s)]
    return pl.pallas_call(
        _kvprep_kernel,
        grid=(b,),
        in_specs=[col(1), col(2), col(3), col(4), col(5), col(6), tab, whole(spread), whole(kg2), whole(ones_bd),
                  whole(perm2), whole(pick), whole(posk), whole(w1k), whole(w2k), whole(posv), whole(w1v),
                  whole(w2v)],
        out_specs=[o[0] for o in outs],
        out_shape=[o[1] for o in outs],
        scratch_shapes=[pltpu.VMEM((s, w), f32)],
        compiler_params=_cparams("parallel"),
        name="nsa_kvprep",
    )(proj, proj, proj, proj, proj, proj, cs_t, spread, kg2, ones_bd, perm2, pick, posk, w1k, w2k, posv, w1v, w2v)


def _nsa_kernel(q_ref, cos_ref, sin_ref, qg_ref, mb_ref, kc_ref, vct_ref, ks_ref, vst_ref, kw_ref, vwt_ref,
                gate_ref, o_ref, *, tq, ck, bounded):
    g = NSA_GROUP
    qi = pl.program_id(2)
    t0 = pl.multiple_of(qi * tq, tq)
    scale = HEAD_DIM ** -0.5 * math.log2(math.e)
    per_head = lambda t: jnp.concatenate([t] * g, axis=1)

    qt = q_ref[0].astype(f32).T
    qt = jnp.concatenate([qt[i * HEAD_DIM:(i + 1) * HEAD_DIM] for i in range(g)], axis=1)
    qn = qt * lax.rsqrt(jnp.mean(qt * qt, axis=0, keepdims=True) + EPS) * qg_ref[...]
    half = ROPE_DIM // 2
    cos8, sin8 = per_head(cos_ref[0]), per_head(sin_ref[0])
    x1, x2 = qn[:half], qn[half:ROPE_DIM]
    q_rot = jnp.concatenate([x1 * cos8 - x2 * sin8, x2 * cos8 + x1 * sin8, qn[ROPE_DIM:]], axis=0)
    n_extra = kc_ref.shape[3] - HEAD_DIM

    def with_features(qb, feats, branch):
        used = sum(f.shape[0] for f in feats)
        last = jnp.broadcast_to(-mb_ref[0:1, branch:branch + 1] if bounded else 0.0, (1, g * tq))
        fill = jnp.zeros((n_extra - used - 1, g * tq), f32)
        return jnp.concatenate([qb] + feats + [fill, last], axis=0).astype(bf16)

    q_cmp = with_features(qn * scale, [], 0)
    q_rot = q_rot * scale
    tq_row = t0 + lax.broadcasted_iota(jnp.int32, (1, tq), 1)

    kc = kc_ref[0, 0]
    ncp = kc.shape[0]
    n_col = lax.broadcasted_iota(jnp.int32, (ncp, 1), 0)
    ok_c = (n_col * CMP_STRIDE + (CMP_BLOCK - 1)) <= tq_row
    s_c = jnp.dot(kc, q_cmp, preferred_element_type=f32) + per_head(jnp.where(ok_c, 0.0, NEG_INF))
    if bounded:
        p_c = jnp.exp2(s_c)
    else:
        m_c = jnp.max(s_c, axis=0, keepdims=True)
        p_c = jnp.exp2(s_c - m_c) * per_head(jnp.where(ok_c, 1.0, 0.0))
    den = jnp.sum(p_c, axis=0, keepdims=True)
    p_c = p_c * (1.0 / jnp.where(den > 0, den, 1.0))
    o_cmp = jnp.dot(vct_ref[0], p_c.astype(bf16), preferred_element_type=f32)

    pg = p_c[:, :tq]
    for i in range(1, g):
        pg = pg + p_c[:, i * tq:(i + 1) * tq]
    nj = ks_ref.shape[2] // SEL_BLOCK
    jj = lax.broadcasted_iota(jnp.int32, (nj, ncp), 0)
    nn = lax.broadcasted_iota(jnp.int32, (nj, ncp), 1)
    overlap = (nn * CMP_STRIDE < (jj + 1) * SEL_BLOCK) & (nn * CMP_STRIDE + CMP_BLOCK > jj * SEL_BLOCK)
    overlap = jnp.where(overlap, 1.0, 0.0).astype(bf16)
    pg_hi, pg_lo = _split_bf16(pg)
    imp = (jnp.dot(overlap, pg_hi, preferred_element_type=f32) +
           jnp.dot(overlap, pg_lo, preferred_element_type=f32))
    j_col = lax.broadcasted_iota(jnp.int32, (nj, 1), 0)
    j_f = j_col.astype(f32)
    cur = tq_row // SEL_BLOCK
    valid = j_col <= cur
    forced = (j_col == 0) | (j_col == cur) | (j_col == cur - 1)
    score = jnp.where(valid, imp + jnp.where(forced, FORCE_BONUS, 0.0), NEG_INF)
    sel = jnp.zeros((nj, tq), f32)
    for _ in range(SEL_TOPK):
        m = jnp.max(score, axis=0, keepdims=True)
        first = jnp.min(jnp.where(score == m, j_f, float(nj)), axis=0, keepdims=True)
        hit = j_f == first
        sel = jnp.where(hit, 1.0, sel)
        score = jnp.where(hit, -jnp.inf, score)
    sel = jnp.where(valid, sel, 0.0)

    def weighted_values(vt, p):
        lhs = jnp.concatenate([vt, jnp.ones((16, vt.shape[1]), bf16)], axis=0)
        r = jnp.dot(lhs, p.astype(bf16), preferred_element_type=f32)
        return r[:HEAD_DIM], r[HEAD_DIM:HEAD_DIM + 1]

    wl = WINDOW + tq
    q_win = with_features(q_rot, [jnp.full((1, g * tq), NEG_INF, f32)], 2)
    s_w = jnp.dot(kw_ref[0, 0, pl.ds(t0, wl), :], q_win, preferred_element_type=f32)
    step = lax.broadcasted_iota(jnp.int32, (tq, 1), 0)
    lo_ok = (t0 - WINDOW + step) > (tq_row - WINDOW)
    hi_ok = (t0 + step) <= tq_row
    parts = [s_w[:tq] + per_head(jnp.where(lo_ok, 0.0, NEG_INF))]
    if wl > 2 * tq:
        parts.append(s_w[tq:wl - tq])
    parts.append(s_w[wl - tq:] + per_head(jnp.where(hi_ok, 0.0, NEG_INF)))
    if bounded:
        p_w = jnp.concatenate([jnp.exp2(t) for t in parts], axis=0)
    else:
        m_w = functools.reduce(jnp.maximum, [jnp.max(t, axis=0, keepdims=True) for t in parts])
        p_w = jnp.concatenate([jnp.exp2(t - m_w) for t in parts], axis=0)
    o_win, l_w = weighted_values(vwt_ref[0, :, pl.ds(t0, wl)], p_w)
    o_win = o_win * (1.0 / l_w)

    q_sel = with_features(q_rot, [per_head(jnp.where(sel > 0.5, 0.0, NEG_INF))], 1)

    def sel_step(carry, k0, diagonal):
        m_prev, l_prev, acc = carry
        s = jnp.dot(ks_ref[0, 0, pl.ds(k0, ck), :], q_sel, preferred_element_type=f32)
        if diagonal:
            kpos = k0 + lax.broadcasted_iota(jnp.int32, (ck, 1), 0)
            s = s + per_head(jnp.where(kpos <= tq_row, 0.0, NEG_INF))
        if bounded:
            pv, p_sum = weighted_values(vst_ref[0, :, pl.ds(k0, ck)], jnp.exp2(s))
            return m_prev, l_prev + p_sum, acc + pv
        m_new = jnp.maximum(m_prev, jnp.max(s, axis=0, keepdims=True))
        alpha = jnp.exp2(m_prev - m_new)
        pv, p_sum = weighted_values(vst_ref[0, :, pl.ds(k0, ck)], jnp.exp2(s - m_new))
        return m_new, alpha * l_prev + p_sum, alpha * acc + pv

    init = (jnp.full((1, g * tq), NEG_INF, f32), jnp.zeros((1, g * tq), f32),
            jnp.zeros((HEAD_DIM, g * tq), f32))
    last = (t0 + tq + ck - 1) // ck - 1
    carry = lax.fori_loop(0, last, lambda c, cr: sel_step(cr, pl.multiple_of(c * ck, ck), False), init)
    _, l_s, acc_s = sel_step(carry, pl.multiple_of(last * ck, ck), True)
    o_sel = acc_s * (1.0 / l_s)

    gt = gate_ref[0].T
    gate = lambda j: jax.nn.sigmoid(jnp.concatenate([gt[i * 3 + j:i * 3 + j + 1] for i in range(g)], axis=1))
    ot = gate(0) * o_cmp + gate(1) * o_sel + gate(2) * o_win
    o2 = jnp.concatenate([ot[:, i * tq:(i + 1) * tq] for i in range(g)], axis=0)
    o_ref[0] = o2.T.astype(o_ref.dtype)


def _nsa_attention(proj, cos_t, sin_t, qg, bounds, kc, vct, ks, vst, kw, vwt, gate_logits, *, bounded,
                   tq=512, ck=512):
    b, s, _ = proj.shape
    hk, g, dh = NSA_KV_HEADS, NSA_GROUP, HEAD_DIM
    assert WINDOW >= tq and ck % tq == 0 and s % ck == 0
    tab = pl.BlockSpec((1, ROPE_DIM // 2, tq), lambda i, j, t: (i, 0, t))
    assert s // SEL_BLOCK <= kc.shape[3] - dh
    keys = lambda a: pl.BlockSpec((1, 1) + a.shape[2:], lambda i, j, t: (i, j, 0, 0))
    vals = lambda a: pl.BlockSpec((1, dh, a.shape[2]), lambda i, j, t: (i, j, 0))
    return pl.pallas_call(
        functools.partial(_nsa_kernel, tq=tq, ck=ck, bounded=bounded),
        grid=(b, hk, s // tq),
        in_specs=[pl.BlockSpec((1, tq, g * dh), lambda i, j, t: (i, t, j)),
                  tab, tab, pl.BlockSpec((dh, 1), lambda i, j, t: (0, 0)),
                  pl.BlockSpec(bounds.shape, lambda i, j, t: (0, 0)),
                  keys(kc), vals(vct), keys(ks), vals(vst), keys(kw), vals(vwt),
                  pl.BlockSpec((1, tq, LANES), lambda i, j, t: (i, t, j))],
        out_specs=pl.BlockSpec((1, tq, g * dh), lambda i, j, t: (i, t, j)),
        out_shape=jax.ShapeDtypeStruct((b, s, NSA_WIDTH), bf16),
        compiler_params=_cparams("parallel", "parallel", "arbitrary"),
        name="nsa_attention_bounded" if bounded else "nsa_attention",
    )(proj, cos_t, sin_t, qg, bounds, kc, vct, ks, vst, kw, vwt, gate_logits)


def _gla_kernel(q_ref, k_ref, v_ref, lr_ref, wg_ref, bg_ref, r_ref, ng_ref, o_ref, *, grp, npair):
    c = GLA_CHUNK
    s = q_ref.shape[1]
    dk, dv = GLA_KEY_DIM, GLA_VAL_DIM
    rows = grp * c
    ri = lax.broadcasted_iota(jnp.int32, (rows, rows), 0)
    ci = lax.broadcasted_iota(jnp.int32, (rows, rows), 1)
    causal = (ci <= ri) & (ci // c == ri // c)
    tri = jnp.where(causal, 1.0, 0.0).astype(bf16)
    lane = lax.broadcasted_iota(jnp.int32, (1, 2 * dk), 1)
    head_mask = [lane < dk, lane >= dk]
    chunk_of_row = lax.broadcasted_iota(jnp.int32, (rows, 1), 0) // c
    wg = wg_ref[...]
    bg = bg_ref[...]
    ng = ng_ref[...]

    def body(n, states):
        r0 = pl.multiple_of(n * rows, rows)
        z = jnp.dot(lr_ref[0, pl.ds(r0, rows), :GLA_GATE_RANK], wg, preferred_element_type=f32,
                    precision=lax.Precision.HIGHEST) + bg
        log_a = -(jnp.maximum(-z, 0.0) + jnp.log(1.0 + jnp.exp(-jnp.abs(z)))) / GLA_TAU
        la_hi = log_a.astype(bf16)
        la_mid, la_lo = _split_bf16(log_a - la_hi.astype(f32))
        bcum = (jnp.dot(tri, la_hi, preferred_element_type=f32) + jnp.dot(tri, la_mid, preferred_element_type=f32) +
                jnp.dot(tri, la_lo, preferred_element_type=f32))
        qf = q_ref[0, pl.ds(r0, rows), :].astype(f32) * (dk ** -0.5)
        kf = k_ref[0, pl.ds(r0, rows), :].astype(f32)
        q_dec = qf * jnp.exp(bcum)
        k_dec = (kf * jnp.exp(-bcum)).astype(bf16)
        b_last = [bcum[(i + 1) * c - 1:(i + 1) * c, :] for i in range(grp)]
        k_state = jnp.concatenate(
            [kf[i * c:(i + 1) * c] * jnp.exp(b_last[i] - bcum[i * c:(i + 1) * c]) for i in range(grp)],
            axis=0).astype(bf16)
        by_chunk = lambda t: jnp.concatenate(
            [jnp.where(chunk_of_row == i, t, jnp.zeros_like(t)) for i in range(grp)], axis=1)
        new_states = []
        for hh in range(2 * npair):
            pr, h = divmod(hh, 2)
            pair = slice(pr * 2 * dk, (pr + 1) * 2 * dk)
            q_h = jnp.where(head_mask[h], q_dec[:, pair], 0.0).astype(bf16)
            v_h = v_ref[0, pl.ds(r0, rows), hh * dv:(hh + 1) * dv]
            attn = lax.dot_general(q_h, k_dec[:, pair], _NT, preferred_element_type=f32)
            attn = jnp.where(causal, attn, 0.0).astype(bf16)
            o_intra = jnp.dot(attn, v_h, preferred_element_type=f32)
            u_all = lax.dot_general(v_h, by_chunk(k_state[:, pair]), _TN, preferred_element_type=f32)
            st = states[hh]
            entering = []
            for i in range(grp):
                entering.append(st)
                st = st * jnp.exp(b_last[i][:, pair]) + u_all[:, i * 2 * dk:(i + 1) * 2 * dk]
            new_states.append(st)
            o_inter = lax.dot_general(by_chunk(q_h), jnp.concatenate(entering, axis=1).astype(bf16), _NT,
                                      preferred_element_type=f32)
            o = _rms(o_intra + o_inter, ng)
            rr = r_ref[0, pl.ds(r0, rows), hh * dv:(hh + 1) * dv].astype(f32)
            o_ref[0, pl.ds(r0, rows), hh * dv:(hh + 1) * dv] = (o * (rr * jax.nn.sigmoid(rr))).astype(o_ref.dtype)
        return tuple(new_states)

    zero = jnp.zeros((dv, 2 * dk), f32)
    lax.fori_loop(0, s // rows, body, (zero,) * (2 * npair), unroll=GLA_UNROLL)


def _gla(proj, proj_s, wg, bg, ng, grp=4, npair=GLA_HEADS // 2):
    b, s, _ = proj.shape
    dk2, dv2 = 2 * npair * GLA_KEY_DIM, 2 * npair * GLA_VAL_DIM
    oq, ok_, ov, orr = (MAIN_OFFS[7] // dk2, MAIN_OFFS[8] // dk2, MAIN_OFFS[9] // dv2, MAIN_OFFS[10] // dv2)
    assert all(MAIN_OFFS[i] % w == 0 for i, w in ((7, dk2), (8, dk2), (9, dv2), (10, dv2)))
    col = lambda w, o: pl.BlockSpec((1, s, w), lambda i, j: (i, 0, o + j))
    return pl.pallas_call(
        functools.partial(_gla_kernel, grp=grp, npair=npair),
        grid=(b, GLA_HEADS // (2 * npair)),
        in_specs=[col(dk2, oq), col(dk2, ok_), col(dv2, ov),
                  pl.BlockSpec((1, s, LANES), lambda i, j: (i, 0, 2)),
                  pl.BlockSpec((GLA_GATE_RANK, dk2), lambda i, j: (0, j)),
                  pl.BlockSpec((1, dk2), lambda i, j: (0, j)),
                  col(dv2, orr),
                  pl.BlockSpec((1, GLA_VAL_DIM), lambda i, j: (0, 0))],
        out_specs=pl.BlockSpec((1, s, dv2), lambda i, j: (i, 0, j)),
        out_shape=jax.ShapeDtypeStruct((b, s, GLA_WIDTH), bf16),
        compiler_params=_cparams("parallel", "parallel"),
        name="gla",
    )(proj, proj, proj, proj_s, wg, bg, proj, ng)


def _outproj_kernel(a_ref, b_ref, x_ref, wo_ref, g2_ref, wrh_ref, wrl_ref, br_ref,
                    x1_ref, h2_ref, rt_ref, cnt_ref, carry_ref):
    @pl.when(pl.program_id(0) == 0)
    def _():
        carry_ref[...] = jnp.zeros_like(carry_ref)

    na = a_ref.shape[1]
    tm = a_ref.shape[0]
    y = (jnp.dot(a_ref[...], wo_ref[:na, :], preferred_element_type=f32) +
         jnp.dot(b_ref[...], wo_ref[na:, :], preferred_element_type=f32))
    x1 = x_ref[...] + y
    x1_ref[...] = x1
    h2 = _rms(x1, g2_ref[...])
    h2_ref[...] = h2.astype(bf16)
    hi, lo = _split_bf16(h2)
    logits = (jnp.dot(hi, wrh_ref[...], preferred_element_type=f32) +
              jnp.dot(lo, wrh_ref[...], preferred_element_type=f32) +
              jnp.dot(hi, wrl_ref[...], preferred_element_type=f32)) + br_ref[...]
    score = logits.T[:N_EXPERTS]
    e_col = lax.broadcasted_iota(jnp.int32, (N_EXPERTS, 1), 0).astype(f32)
    top = jnp.max(score, axis=0, keepdims=True)
    hits, firsts, weights = [], [], []
    for _ in range(TOP_K):
        m = jnp.max(score, axis=0, keepdims=True)
        first = jnp.min(jnp.where(score == m, e_col, float(N_EXPERTS)), axis=0, keepdims=True)
        hit = e_col == first
        hits.append(hit)
        firsts.append(first)
        weights.append(jnp.exp(m - top))
        score = jnp.where(hit, -jnp.inf, score)
    tot = weights[0] + weights[1] + weights[2] + weights[3]
    onehot = jnp.where(hits[0] | hits[1] | hits[2] | hits[3], 1.0, 0.0)
    ri = lax.broadcasted_iota(jnp.int32, (tm, tm), 0)
    ci = lax.broadcasted_iota(jnp.int32, (tm, tm), 1)
    earlier = jnp.where(ri < ci, 1.0, 0.0).astype(bf16)
    rank = carry_ref[...] + jnp.dot(onehot.astype(bf16), earlier, preferred_element_type=f32)
    carry_ref[...] = carry_ref[...] + jnp.sum(onehot, axis=1, keepdims=True)
    cnt_ref[...] = jnp.broadcast_to(carry_ref[...], cnt_ref.shape)
    ranks = [jnp.sum(jnp.where(hits[k], rank, 0.0), axis=0, keepdims=True) for k in range(TOP_K)]
    gates = [weights[k] / tot for k in range(TOP_K)]
    pad = jnp.zeros((rt_ref.shape[0] - 3 * TOP_K, tm), f32)
    rt_ref[...] = jnp.concatenate(firsts + gates + ranks + [pad], axis=0)


def _outproj_router(a, b, x2, wo, g2, wr_hi, wr_lo, br, tm=512):
    m, d = x2.shape
    na, nb = a.shape[1], b.shape[1]
    row = lambda n: pl.BlockSpec((tm, n), lambda i: (i, 0))
    whole = lambda t: pl.BlockSpec(t.shape, lambda i: (0,) * t.ndim)
    return pl.pallas_call(
        _outproj_kernel,
        grid=(m // tm,),
        in_specs=[row(na), row(nb), row(d), whole(wo), whole(g2), whole(wr_hi), whole(wr_lo), whole(br)],
        out_specs=[row(d), row(d), pl.BlockSpec((ROUTE_ROWS, tm), lambda i: (0, i)),
                   pl.BlockSpec((N_EXPERTS, LANES), lambda i: (0, 0))],
        out_shape=[jax.ShapeDtypeStruct((m, d), f32),
                   jax.ShapeDtypeStruct((m, d), bf16),
                   jax.ShapeDtypeStruct((ROUTE_ROWS, m), f32),
                   jax.ShapeDtypeStruct((N_EXPERTS, LANES), f32)],
        scratch_shapes=[pltpu.VMEM((N_EXPERTS, 1), f32)],
        compiler_params=_cparams("arbitrary"),
        name="outproj_router",
    )(a, b, x2, wo, g2, wr_hi, wr_lo, br)


def _moe_kernel(vb_ref, ve_ref, lo_ref, hi_ref, nv_ref, xs_ref, wgu_ref, bgu_ref, wd_ref, bd_ref, *rest,
                slice_id, table_off):
    o_ref, wgu_bf, wd_bf = rest[-3:]
    step = pl.program_id(0)
    v = table_off + step
    prev = jnp.maximum(v - 1, table_off)
    live = step < nv_ref[slice_id]
    new_expert = (step == 0) | (ve_ref[v] != ve_ref[prev])
    new_block = (step == 0) | (vb_ref[v] != vb_ref[prev])

    @pl.when(live & new_expert)
    def _():
        wgu_bf[...] = wgu_ref[0].astype(bf16)
        wd_bf[...] = wd_ref[0].astype(bf16)

    @pl.when(live & new_block)
    def _():
        o_ref[...] = jnp.zeros_like(o_ref)

    @pl.when(live)
    def _():
        blk0 = vb_ref[v] * MOE_BLOCK
        lo, hi = lo_ref[v], hi_ref[v]
        first = jnp.maximum(lo - blk0, 0) // MOE_SUB
        last = (jnp.minimum(hi - blk0, MOE_BLOCK) + MOE_SUB - 1) // MOE_SUB

        def sub_block(j, carry):
            r0 = pl.multiple_of(j * MOE_SUB, MOE_SUB)
            h = jnp.dot(xs_ref[pl.ds(r0, MOE_SUB), :], wgu_bf[...], preferred_element_type=f32) + bgu_ref[0]
            x_glu = jnp.minimum(h[:, :D_FF], SWIGLU_LIMIT)
            x_lin = jnp.clip(h[:, D_FF:], -SWIGLU_LIMIT, SWIGLU_LIMIT)
            act = x_glu * jax.nn.sigmoid(SWIGLU_ALPHA * x_glu) * (x_lin + 1.0)
            y = jnp.dot(act.astype(bf16), wd_bf[...], preferred_element_type=f32) + bd_ref[0]
            row = blk0 + r0 + lax.broadcasted_iota(jnp.int32, (MOE_SUB, 1), 0)
            mine = (row >= lo) & (row < hi)
            o_ref[pl.ds(r0, MOE_SUB), :] = jnp.where(mine, y.astype(o_ref.dtype), o_ref[pl.ds(r0, MOE_SUB), :])
            return carry

        lax.fori_loop(first, last, sub_block, 0)


def _moe_experts(table, slice_id, blk_lo, xs, wgu, bgu, wd, bd, n_rows, ys_so_far=None):
    d = xs.shape[1]
    nvis = xs.shape[0] // MOE_BLOCK + N_EXPERTS - 1
    off = slice_id * (table[0].shape[0] // table[4].shape[0])
    in_specs = [pl.BlockSpec((MOE_BLOCK, d), lambda v, vb, ve, *_: (vb[off + v] - blk_lo, 0)),
                pl.BlockSpec((1, d, 2 * D_FF), lambda v, vb, ve, *_: (ve[off + v], 0, 0)),
                pl.BlockSpec((1, 1, 2 * D_FF), lambda v, vb, ve, *_: (ve[off + v], 0, 0)),
                pl.BlockSpec((1, D_FF, d), lambda v, vb, ve, *_: (ve[off + v], 0, 0)),
                pl.BlockSpec((1, 1, d), lambda v, vb, ve, *_: (ve[off + v], 0, 0))]
    operands = table + (xs, wgu, bgu, wd, bd)
    aliases = {}
    if ys_so_far is not None:
        in_specs.append(pl.BlockSpec(memory_space=pl.ANY))
        aliases = {len(operands): 0}
        operands += (ys_so_far,)
    return pl.pallas_call(
        functools.partial(_moe_kernel, slice_id=slice_id, table_off=off),
        grid_spec=pltpu.PrefetchScalarGridSpec(
            num_scalar_prefetch=5,
            grid=(nvis,),
            in_specs=in_specs,
            out_specs=pl.BlockSpec((MOE_BLOCK, d), lambda v, vb, ve, *_: (vb[off + v], 0)),
            scratch_shapes=[pltpu.VMEM((d, 2 * D_FF), bf16), pltpu.VMEM((D_FF, d), bf16)],
        ),
        out_shape=jax.ShapeDtypeStruct((n_rows, d), bf16),
        input_output_aliases=aliases,
        compiler_params=_cparams("arbitrary"),
        name="moe_experts",
    )(*operands)


def _positions_kernel(start_ref, rt_ref, pos_ref):
    idx = rt_ref[:TOP_K, :].astype(jnp.int32)
    pos = rt_ref[2 * TOP_K:3 * TOP_K, :].astype(jnp.int32)
    for e in range(N_EXPERTS):
        pos = pos + jnp.where(idx == e, start_ref[e], 0)
    pos_ref[...] = pos


def _positions(start, rt):
    m = rt.shape[1]
    return pl.pallas_call(
        _positions_kernel,
        grid_spec=pltpu.PrefetchScalarGridSpec(
            num_scalar_prefetch=1, grid=(1,),
            in_specs=[pl.BlockSpec(rt.shape, lambda i, st: (0, 0))],
            out_specs=pl.BlockSpec((TOP_K, m), lambda i, st: (0, 0))),
        out_shape=jax.ShapeDtypeStruct((TOP_K, m), jnp.int32),
        compiler_params=_cparams("arbitrary"),
        name="route_positions",
    )(start, rt)


def _visit_tables(start, end, cuts):
    blk_lo = jnp.array([0] + list(cuts[:-1]), jnp.int32)[:, None]
    blk_hi = jnp.array(list(cuts), jnp.int32)[:, None]
    room = max(h - l for l, h in zip([0] + list(cuts[:-1]), cuts)) + N_EXPERTS - 1
    first_blk = jnp.maximum((start // MOE_BLOCK)[None, :], blk_lo)
    last_blk = jnp.minimum(jnp.where(end > start, (end - 1) // MOE_BLOCK, -1)[None, :], blk_hi - 1)
    per_e = jnp.maximum(last_blk - first_blk + 1, 0)
    v_end = jnp.cumsum(per_e, axis=1)
    v_start = v_end - per_e
    n_live = v_end[:, -1]
    vc = jnp.minimum(jnp.arange(room, dtype=jnp.int32)[None, :], n_live[:, None] - 1)
    ve = jnp.sum((v_end[:, None, :] <= vc[:, :, None]).astype(jnp.int32), axis=2)
    of_ve = lambda t: jnp.sum(jnp.where(ve[:, :, None] == jnp.arange(N_EXPERTS), t[:, None, :], 0), axis=2)
    vb = of_ve(first_blk) + vc - of_ve(v_start)
    flat = lambda t: t.reshape(-1).astype(jnp.int32)
    return (flat(vb), flat(ve), flat(of_ve(jnp.broadcast_to(start, per_e.shape))),
            flat(of_ve(jnp.broadcast_to(end, per_e.shape))), n_live.astype(jnp.int32))


def _rope_tables(positions):
    half = ROPE_DIM // 2
    inv_freq = jnp.exp(-math.log(ROPE_THETA) * jnp.arange(0, ROPE_DIM, 2, dtype=f32) / ROPE_DIM)
    ang = positions.astype(f32)[:, None, :] * inv_freq[None, :, None]
    cos_t, sin_t = jnp.cos(ang), jnp.sin(ang)
    d = jnp.arange(LANES) % HEAD_DIM
    f = jnp.arange(LANES)[:, None]
    first, second = d[None, :] == f % half, d[None, :] == f % half + half
    to_cos = jnp.where((f < half) & (first | second), 1.0, 0.0)
    to_sin = jnp.where((f >= half) & (f < 2 * half), jnp.where(second, 1.0, 0.0) - jnp.where(first, 1.0, 0.0), 0.0)
    spread = jnp.concatenate([to_cos, to_sin], axis=1).astype(bf16)
    idx = jnp.arange(half)
    perm = jnp.zeros((HEAD_DIM, HEAD_DIM), f32).at[idx + half, idx].set(1.0).at[idx, idx + half].set(1.0)
    return cos_t, sin_t, spread, perm


def _layer(x, positions, norm1_g, w_in, q_norm_g, k_norm_g, cmp_pos_k, cmp_w1_k, cmp_w2_k,
           cmp_pos_v, cmp_w1_v, cmp_w2_v, gla_w_gate, gla_b_gate, gla_norm_g, w_out, norm2_g,
           w_router, b_router, w_gate_up, b_gate_up, w_down, b_down):
    b, s, d = x.shape
    m = b * s
    hk, g, dh = NSA_KV_HEADS, NSA_GROUP, HEAD_DIM
    x2 = x.reshape(m, d)

    offs = [0]
    for sz in IN_SIZES:
        offs.append(offs[-1] + sz)
    seg = lambda i: w_in[:, offs[i]:offs[i + 1]]
    w_main = jnp.concatenate([seg(MAIN_IN_SEG[i]) for i in MAIN_LAYOUT], axis=1).astype(bf16)
    padw = lambda t: jnp.concatenate([t, jnp.zeros((d, LANES - t.shape[1]), f32)], axis=1)
    ng = g * 3
    w_small = jnp.concatenate([padw(seg(7)[:, :ng]), padw(seg(7)[:, ng:]), padw(seg(11))], axis=1).astype(bf16)
    proj, proj_s = _inproj(x2, norm1_g.reshape(1, d), w_main, w_small)
    proj = proj.reshape(b, s, -1)
    proj_s = proj_s.reshape(b, s, 3 * LANES)

    cos_t, sin_t, spread, perm = _rope_tables(positions)
    per_kv = lambda t: jnp.concatenate([t] * hk, axis=-1)

    def bdiag(t):
        r, c = t.shape[-2:]
        lead = [(0, 0)] * (t.ndim - 2)
        return jnp.concatenate([jnp.pad(t, lead + [(0, 0), (i * c, (hk - 1 - i) * c)]) for i in range(hk)], axis=-2)

    w1_bd = lambda w1: bdiag(w1.reshape(CMP_BLOCK, dh, CMP_HIDDEN)).astype(bf16)
    kc, vct, ks_rot, vst, kw_rot, vwt = _kvprep(
        proj, jnp.concatenate([cos_t, sin_t], axis=1), spread, per_kv(k_norm_g),
        bdiag(jnp.ones((dh, dh), f32)).astype(bf16), bdiag(perm).astype(bf16),
        jnp.stack([jnp.eye(hk * dh, LANES, k=-h * dh, dtype=f32) * (jnp.arange(LANES) < dh)
                   for h in range(hk)]).astype(bf16),
        per_kv(cmp_pos_k), w1_bd(cmp_w1_k), bdiag(cmp_w2_k).astype(bf16),
        per_kv(cmp_pos_v), w1_bd(cmp_w1_v), bdiag(cmp_w2_v).astype(bf16))
    score_bound = (1.05 * dh * dh ** -0.5 * math.log2(math.e) * jnp.max(jnp.abs(q_norm_g)) *
                   jnp.max(jnp.abs(k_norm_g), axis=1))
    bounds = jnp.zeros((8, LANES), f32).at[0, :3].set(score_bound)
    nsa_args = (proj, cos_t, sin_t, q_norm_g.reshape(dh, 1), bounds, kc, vct, ks_rot, vst, kw_rot, vwt, proj_s)
    nsa_out = lax.cond(2.0 * jnp.max(score_bound) < NSA_SAFE_EXPONENT,
                       functools.partial(_nsa_attention, bounded=True),
                       functools.partial(_nsa_attention, bounded=False), *nsa_args).reshape(m, NSA_WIDTH)

    gla_out = _gla(proj, proj_s, gla_w_gate, gla_b_gate.reshape(1, -1),
                   gla_norm_g.reshape(1, GLA_VAL_DIM)).reshape(m, GLA_WIDTH)

    padr = lambda t: jnp.concatenate([t, jnp.zeros(t.shape[:-1] + (LANES - t.shape[-1],), t.dtype)], axis=-1)
    wr_hi = w_router.astype(bf16)
    wr_lo = (w_router - wr_hi.astype(f32)).astype(bf16)
    x1, h2, rt, cnt = _outproj_router(nsa_out, gla_out, x2, w_out.astype(bf16), norm2_g.reshape(1, d),
                                      padr(wr_hi), padr(wr_lo), padr(b_router.reshape(1, -1)))

    a = m * TOP_K
    gate = rt[TOP_K:2 * TOP_K]
    counts = cnt[:, 0].astype(jnp.int32)
    end = jnp.cumsum(counts)
    start = end - counts
    pos = _positions(start, rt).reshape(a)
    tok = jnp.arange(a, dtype=jnp.int32) % m
    sorted_tok = jnp.zeros((a,), jnp.int32).at[pos].add(tok, unique_indices=True, mode='promise_in_bounds')
    rows_of = lambda t, idx: t.at[idx].get(mode='promise_in_bounds')
    nblk = a // MOE_BLOCK
    cuts = [nblk * c // MOE_SLICE_CUTS[-1] for c in MOE_SLICE_CUTS] if nblk % MOE_SLICE_CUTS[-1] == 0 else [nblk]
    table = _visit_tables(start, end, cuts)
    ys, lo_blk = None, 0
    for i, hi_blk in enumerate(cuts):
        xs = rows_of(h2, sorted_tok[lo_blk * MOE_BLOCK:hi_blk * MOE_BLOCK])
        ys = _moe_experts(table, i, lo_blk, xs, w_gate_up, b_gate_up.reshape(N_EXPERTS, 1, -1),
                          w_down, b_down.reshape(N_EXPERTS, 1, -1), a, ys)
        lo_blk = hi_blk
    out = x1
    for k in range(TOP_K):
        out = out + rows_of(ys, pos[k * m:(k + 1) * m]).astype(f32) * gate[k][:, None]
    return out.reshape(b, s, d)


def kernel(x, positions, norm1_g, w_in, nsa_q_norm_g, nsa_k_norm_g, cmp_pos_k, cmp_w1_k, cmp_w2_k,
           cmp_pos_v, cmp_w1_v, cmp_w2_v, gla_w_gate, gla_b_gate, gla_norm_g, w_out, norm2_g,
           w_router, b_router, w_gate_up, b_gate_up, w_down, b_down):
    for l in range(norm1_g.shape[0]):
        x = _layer(x, positions, norm1_g[l], w_in[l], nsa_q_norm_g[l], nsa_k_norm_g[l],
                   cmp_pos_k[l], cmp_w1_k[l], cmp_w2_k[l], cmp_pos_v[l], cmp_w1_v[l], cmp_w2_v[l],
                   gla_w_gate[l], gla_b_gate[l], gla_norm_g[l], w_out[l], norm2_g[l],
                   w_router[l], b_router[l], w_gate_up[l], b_gate_up[l], w_down[l], b_down[l])
    return x
```

```python
import functools
import math

import jax
import jax.numpy as jnp
from jax import lax
from jax.experimental import pallas as pl
from jax.experimental.pallas import tpu as pltpu

f32 = jnp.float32
bf16 = jnp.bfloat16

NSA_HEADS = 8
NSA_KV_HEADS = 2
NSA_GROUP = NSA_HEADS // NSA_KV_HEADS
HEAD_DIM = 64
CMP_BLOCK = 32
CMP_STRIDE = 16
CMP_HIDDEN = 256
SEL_BLOCK = 64
SEL_TOPK = 8
WINDOW = 512
FORCE_BONUS = 1e4
GLA_HEADS = 4
GLA_KEY_DIM = 64
GLA_VAL_DIM = 128
GLA_CHUNK = 64
GLA_GATE_RANK = 16
GLA_TAU = 16.0
ROPE_THETA = 500000.0
ROPE_DIM = HEAD_DIM // 4
N_EXPERTS = 32
TOP_K = 4
D_FF = 1024
SWIGLU_LIMIT = 7.0
SWIGLU_ALPHA = 1.702
EPS = 1e-6
NEG_INF = -1e30

NSA_WIDTH = NSA_HEADS * HEAD_DIM
NSA_KV_WIDTH = NSA_KV_HEADS * HEAD_DIM
GLA_KEY_WIDTH = GLA_HEADS * GLA_KEY_DIM
GLA_WIDTH = GLA_HEADS * GLA_VAL_DIM
IN_SIZES = (NSA_WIDTH,) + (NSA_KV_WIDTH,) * 6 + (
    NSA_HEADS * 3, GLA_KEY_WIDTH, GLA_KEY_WIDTH, GLA_WIDTH, GLA_GATE_RANK, GLA_WIDTH)
MAIN_SIZES = (NSA_WIDTH,) + (NSA_KV_WIDTH,) * 6 + (GLA_KEY_WIDTH, GLA_KEY_WIDTH, GLA_WIDTH, GLA_WIDTH)
MAIN_IN_SEG = (0, 1, 2, 3, 4, 5, 6, 8, 9, 10, 12)
MAIN_LAYOUT = (0, 9, 10, 1, 2, 3, 4, 5, 6, 7, 8)
MAIN_OFFS = tuple(sum(MAIN_SIZES[j] for j in MAIN_LAYOUT[:MAIN_LAYOUT.index(i)]) for i in range(len(MAIN_SIZES)))

LANES = 128
VMEM_LIMIT = 48 * 1024 * 1024
MOE_BLOCK = 1024
MOE_SUB = 256
MOE_SLICE_CUTS = (1, 4, 8, 16)
NSA_SAFE_EXPONENT = 120.0
GLA_UNROLL = 4
ROUTE_ROWS = 16

_NT = (((1,), (1,)), ((), ()))
_TN = (((0,), (0,)), ((), ()))


def _cparams(*sem):
    return pltpu.CompilerParams(dimension_semantics=sem, vmem_limit_bytes=VMEM_LIMIT)


def _rms(t, g):
    return t * lax.rsqrt(jnp.mean(t * t, axis=-1, keepdims=True) + EPS) * g


def _split_bf16(t):
    hi = t.astype(bf16)
    lo = (t - hi.astype(f32)).astype(bf16)
    return hi, lo


def _rope(t, cos_f, sin_f, perm):
    hi, lo = _split_bf16(t)
    rot = (jnp.dot(hi, perm, preferred_element_type=f32) +
           jnp.dot(lo, perm, preferred_element_type=f32))
    return t * cos_f + rot * sin_f


def _inproj_kernel(x_ref, g_ref, w_ref, ws_ref, o_ref, os_ref):
    x = x_ref[...]
    h = _rms(x, g_ref[...]).astype(bf16)
    n = o_ref.shape[1]
    step = 512
    for c in range(0, n, step):
        e = min(c + step, n)
        o_ref[:, c:e] = jnp.dot(h, w_ref[:, c:e], preferred_element_type=f32).astype(bf16)
    os_ref[...] = jnp.dot(h, ws_ref[...], preferred_element_type=f32)


def _inproj(x2, g, w_main, w_small, tm=512):
    m, d = x2.shape
    n = w_main.shape[1]
    ns = w_small.shape[1]
    return pl.pallas_call(
        _inproj_kernel,
        grid=(m // tm,),
        in_specs=[pl.BlockSpec((tm, d), lambda i: (i, 0)),
                  pl.BlockSpec((1, d), lambda i: (0, 0)),
                  pl.BlockSpec((d, n), lambda i: (0, 0)),
                  pl.BlockSpec((d, ns), lambda i: (0, 0))],
        out_specs=[pl.BlockSpec((tm, n), lambda i: (i, 0)),
                   pl.BlockSpec((tm, ns), lambda i: (i, 0))],
        out_shape=[jax.ShapeDtypeStruct((m, n), bf16),
                   jax.ShapeDtypeStruct((m, ns), f32)],
        compiler_params=_cparams("parallel"),
        name="inproj",
    )(x2, g, w_main, w_small)


def _kvprep_kernel(kc_ref, vc_ref, ks_ref, vs_ref, kw_ref, vw_ref, cs_ref, spread_ref, kg_ref, ones_ref,
                   perm_ref, pick_ref, posk_ref, w1k_ref, w2k_ref, posv_ref, w1v_ref, w2v_ref,
                   kco_ref, vcto_ref, kso_ref, vsto_ref, kwo_ref, vwto_ref, tmp_ref):
    ones_bd = ones_ref[...]
    nh_out = kco_ref.shape[2]

    def rms_heads(t, g):
        sq_hi, sq_lo = _split_bf16(t * t)
        ss = (jnp.dot(sq_hi, ones_bd, preferred_element_type=f32) +
              jnp.dot(sq_lo, ones_bd, preferred_element_type=f32))
        return t * lax.rsqrt(ss * (1.0 / HEAD_DIM) + EPS) * g

    def compress(src_ref, pos_ref, w1_ref, w2_ref):
        tmp_ref[...] = src_ref[0].astype(f32)
        nh = tmp_ref.shape[0] // CMP_STRIDE
        a = jnp.zeros((nh, NSA_KV_HEADS * CMP_HIDDEN), f32)
        b = jnp.zeros((nh, NSA_KV_HEADS * CMP_HIDDEN), f32)
        for p in range(CMP_STRIDE):
            rows = tmp_ref[pl.ds(p, nh, stride=CMP_STRIDE), :]
            a = a + jnp.dot((rows + pos_ref[p:p + 1, :]).astype(bf16), w1_ref[p], preferred_element_type=f32)
            q = CMP_STRIDE + p
            b = b + jnp.dot((rows + pos_ref[q:q + 1, :]).astype(bf16), w1_ref[q], preferred_element_type=f32)
        pre = a + pltpu.roll(b, nh - 1, 0)
        hid = pre * jax.nn.sigmoid(pre)
        return jnp.dot(hid.astype(bf16), w2_ref[...], preferred_element_type=f32)

    kc = rms_heads(compress(kc_ref, posk_ref, w1k_ref, w2k_ref), kg_ref[0:1, :]).astype(bf16)
    tmp_ref[:nh_out, :] = compress(vc_ref, posv_ref, w1v_ref, w2v_ref)
    vcto_ref[0] = tmp_ref[:nh_out, :].T.astype(bf16)
    s, w = ks_ref.shape[1], ks_ref.shape[2]
    cs = jnp.concatenate([cs_ref[0], jnp.zeros((LANES - cs_ref.shape[1], s), f32)], axis=0).T
    cs_hi = cs.astype(bf16)
    cs_mid, cs_lo = _split_bf16(cs - cs_hi.astype(f32))
    spread = spread_ref[...]
    tables = (jnp.dot(cs_hi, spread, preferred_element_type=f32) + jnp.dot(cs_mid, spread, preferred_element_type=f32) +
              jnp.dot(cs_lo, spread, preferred_element_type=f32))
    lane_row = lax.broadcasted_iota(jnp.int32, (1, LANES), 1)
    cos_f = tables[:, :LANES] + jnp.where(lane_row % HEAD_DIM >= ROPE_DIM, 1.0, 0.0)
    sin_f = tables[:, LANES:]
    perm = perm_ref[...]
    ks = _rope(rms_heads(ks_ref[0].astype(f32), kg_ref[1:2, :]), cos_f, sin_f, perm).astype(bf16)
    kw = _rope(rms_heads(kw_ref[0].astype(f32), kg_ref[2:3, :]), cos_f, sin_f, perm).astype(bf16)
    row = lax.broadcasted_iota(jnp.int32, (s, LANES), 0)
    lane = lax.broadcasted_iota(jnp.int32, (s, LANES), 1)
    last_lane_one = lambda n: jnp.where(lax.broadcasted_iota(jnp.int32, (n, LANES), 1) == LANES - 1, 1.0, 0.0)
    ones_lane = last_lane_one(s)
    block_onehot = jnp.where(row // SEL_BLOCK + HEAD_DIM == lane, 1.0, 0.0) + ones_lane
    pad_lane = lax.broadcasted_iota(jnp.int32, (WINDOW, LANES), 1)
    pad_flag = jnp.where((pad_lane == HEAD_DIM) | (pad_lane == LANES - 1), 1.0, 0.0)
    for h in range(NSA_KV_HEADS):
        pick = pick_ref[h]
        kco_ref[0, h] = (jnp.dot(kc, pick, preferred_element_type=f32) + last_lane_one(nh_out)).astype(bf16)
        kso_ref[0, h] = (jnp.dot(ks, pick, preferred_element_type=f32) + block_onehot).astype(bf16)
        kwo_ref[0, h, :WINDOW, :] = pad_flag.astype(bf16)
        kwo_ref[0, h, WINDOW:, :] = (jnp.dot(kw, pick, preferred_element_type=f32) + ones_lane).astype(bf16)
    vsto_ref[0] = vs_ref[0].astype(f32).T.astype(bf16)
    vwto_ref[0] = jnp.concatenate([jnp.zeros((w, WINDOW), f32), vw_ref[0].astype(f32).T], axis=1).astype(bf16)


def _kvprep(proj, cs_t, spread, kg2, ones_bd, perm2, pick, posk, w1k, w2k, posv, w1v, w2v):
    b, s, _ = proj.shape
    w = NSA_KV_WIDTH
    assert w == LANES
    nh = s // CMP_STRIDE
    col = lambda i: pl.BlockSpec((1, s, w), lambda n: (n, 0, MAIN_OFFS[i] // w))
    whole = lambda a: pl.BlockSpec(a.shape, lambda n: (0,) * a.ndim)
    tab = pl.BlockSpec((1, cs_t.shape[1], s), lambda n: (n, 0, 0))
    out = lambda *shp: (pl.BlockSpec((1,) + shp, lambda n: (n,) + (0,) * len(shp)),
                        jax.ShapeDtypeStruct((b,) + shp, bf16))
    hk = NSA_KV_HEADS
    outs = [out(hk, nh, LANES), out(w, nh), out(hk, s, LANES), out(w, s), out(hk, WINDOW + s, LANES),
            out(w, WINDOW + s)]
    return pl.pallas_call(
        _kvprep_kernel,
        grid=(b,),
        in_specs=[col(1), col(2), col(3), col(4), col(5), col(6), tab, whole(spread), whole(kg2), whole(ones_bd),
                  whole(perm2), whole(pick), whole(posk), whole(w1k), whole(w2k), whole(posv), whole(w1v),
                  whole(w2v)],
        out_specs=[o[0] for o in outs],
        out_shape=[o[1] for o in outs],
        scratch_shapes=[pltpu.VMEM((s, w), f32)],
        compiler_params=_cparams("parallel"),
        name="nsa_kvprep",
    )(proj, proj, proj, proj, proj, proj, cs_t, spread, kg2, ones_bd, perm2, pick, posk, w1k, w2k, posv, w1v, w2v)


def _nsa_kernel(q_ref, cos_ref, sin_ref, qg_ref, mb_ref, kc_ref, vct_ref, ks_ref, vst_ref, kw_ref, vwt_ref,
                gate_ref, o_ref, *, tq, ck, bounded):
    g = NSA_GROUP
    qi = pl.program_id(2)
    t0 = pl.multiple_of(qi * tq, tq)
    scale = HEAD_DIM ** -0.5 * math.log2(math.e)
    per_head = lambda t: jnp.concatenate([t] * g, axis=1)

    qt = q_ref[0].astype(f32).T
    qt = jnp.concatenate([qt[i * HEAD_DIM:(i + 1) * HEAD_DIM] for i in range(g)], axis=1)
    qn = qt * lax.rsqrt(jnp.mean(qt * qt, axis=0, keepdims=True) + EPS) * qg_ref[...]
    half = ROPE_DIM // 2
    cos8, sin8 = per_head(cos_ref[0]), per_head(sin_ref[0])
    x1, x2 = qn[:half], qn[half:ROPE_DIM]
    q_rot = jnp.concatenate([x1 * cos8 - x2 * sin8, x2 * cos8 + x1 * sin8, qn[ROPE_DIM:]], axis=0)
    n_extra = kc_ref.shape[3] - HEAD_DIM

    def with_features(qb, feats, branch):
        used = sum(f.shape[0] for f in feats)
        last = jnp.broadcast_to(-mb_ref[0:1, branch:branch + 1] if bounded else 0.0, (1, g * tq))
        fill = jnp.zeros((n_extra - used - 1, g * tq), f32)
        return jnp.concatenate([qb] + feats + [fill, last], axis=0).astype(bf16)

    q_cmp = with_features(qn * scale, [], 0)
    q_rot = q_rot * scale
    tq_row = t0 + lax.broadcasted_iota(jnp.int32, (1, tq), 1)

    kc = kc_ref[0, 0]
    ncp = kc.shape[0]
    n_col = lax.broadcasted_iota(jnp.int32, (ncp, 1), 0)
    ok_c = (n_col * CMP_STRIDE + (CMP_BLOCK - 1)) <= tq_row
    s_c = jnp.dot(kc, q_cmp, preferred_element_type=f32) + per_head(jnp.where(ok_c, 0.0, NEG_INF))
    if bounded:
        p_c = jnp.exp2(s_c)
    else:
        m_c = jnp.max(s_c, axis=0, keepdims=True)
        p_c = jnp.exp2(s_c - m_c) * per_head(jnp.where(ok_c, 1.0, 0.0))
    den = jnp.sum(p_c, axis=0, keepdims=True)
    p_c = p_c * (1.0 / jnp.where(den > 0, den, 1.0))
    o_cmp = jnp.dot(vct_ref[0], p_c.astype(bf16), preferred_element_type=f32)

    pg = p_c[:, :tq]
    for i in range(1, g):
        pg = pg + p_c[:, i * tq:(i + 1) * tq]
    nj = ks_ref.shape[2] // SEL_BLOCK
    jj = lax.broadcasted_iota(jnp.int32, (nj, ncp), 0)
    nn = lax.broadcasted_iota(jnp.int32, (nj, ncp), 1)
    overlap = (nn * CMP_STRIDE < (jj + 1) * SEL_BLOCK) & (nn * CMP_STRIDE + CMP_BLOCK > jj * SEL_BLOCK)
    overlap = jnp.where(overlap, 1.0, 0.0).astype(bf16)
    pg_hi, pg_lo = _split_bf16(pg)
    imp = (jnp.dot(overlap, pg_hi, preferred_element_type=f32) +
           jnp.dot(overlap, pg_lo, preferred_element_type=f32))
    j_col = lax.broadcasted_iota(jnp.int32, (nj, 1), 0)
    j_f = j_col.astype(f32)
    cur = tq_row // SEL_BLOCK
    valid = j_col <= cur
    forced = (j_col == 0) | (j_col == cur) | (j_col == cur - 1)
    score = jnp.where(valid, imp + jnp.where(forced, FORCE_BONUS, 0.0), NEG_INF)
    sel = jnp.zeros((nj, tq), f32)
    for _ in range(SEL_TOPK):
        m = jnp.max(score, axis=0, keepdims=True)
        first = jnp.min(jnp.where(score == m, j_f, float(nj)), axis=0, keepdims=True)
        hit = j_f == first
        sel = jnp.where(hit, 1.0, sel)
        score = jnp.where(hit, -jnp.inf, score)
    sel = jnp.where(valid, sel, 0.0)

    def weighted_values(vt, p):
        lhs = jnp.concatenate([vt, jnp.ones((16, vt.shape[1]), bf16)], axis=0)
        r = jnp.dot(lhs, p.astype(bf16), preferred_element_type=f32)
        return r[:HEAD_DIM], r[HEAD_DIM:HEAD_DIM + 1]

    wl = WINDOW + tq
    q_win = with_features(q_rot, [jnp.full((1, g * tq), NEG_INF, f32)], 2)
    s_w = jnp.dot(kw_ref[0, 0, pl.ds(t0, wl), :], q_win, preferred_element_type=f32)
    step = lax.broadcasted_iota(jnp.int32, (tq, 1), 0)
    lo_ok = (t0 - WINDOW + step) > (tq_row - WINDOW)
    hi_ok = (t0 + step) <= tq_row
    parts = [s_w[:tq] + per_head(jnp.where(lo_ok, 0.0, NEG_INF))]
    if wl > 2 * tq:
        parts.append(s_w[tq:wl - tq])
    parts.append(s_w[wl - tq:] + per_head(jnp.where(hi_ok, 0.0, NEG_INF)))
    if bounded:
        p_w = jnp.concatenate([jnp.exp2(t) for t in parts], axis=0)
    else:
        m_w = functools.reduce(jnp.maximum, [jnp.max(t, axis=0, keepdims=True) for t in parts])
        p_w = jnp.concatenate([jnp.exp2(t - m_w) for t in parts], axis=0)
    o_win, l_w = weighted_values(vwt_ref[0, :, pl.ds(t0, wl)], p_w)
    o_win = o_win * (1.0 / l_w)

    q_sel = with_features(q_rot, [per_head(jnp.where(sel > 0.5, 0.0, NEG_INF))], 1)

    def sel_step(carry, k0, diagonal):
        m_prev, l_prev, acc = carry
        s = jnp.dot(ks_ref[0, 0, pl.ds(k0, ck), :], q_sel, preferred_element_type=f32)
        if diagonal:
            kpos = k0 + lax.broadcasted_iota(jnp.int32, (ck, 1), 0)
            s = s + per_head(jnp.where(kpos <= tq_row, 0.0, NEG_INF))
        if bounded:
            pv, p_sum = weighted_values(vst_ref[0, :, pl.ds(k0, ck)], jnp.exp2(s))
            return m_prev, l_prev + p_sum, acc + pv
        m_new = jnp.maximum(m_prev, jnp.max(s, axis=0, keepdims=True))
        alpha = jnp.exp2(m_prev - m_new)
        pv, p_sum = weighted_values(vst_ref[0, :, pl.ds(k0, ck)], jnp.exp2(s - m_new))
        return m_new, alpha * l_prev + p_sum, alpha * acc + pv

    init = (jnp.full((1, g * tq), NEG_INF, f32), jnp.zeros((1, g * tq), f32),
            jnp.zeros((HEAD_DIM, g * tq), f32))
    last = (t0 + tq + ck - 1) // ck - 1
    carry = lax.fori_loop(0, last, lambda c, cr: sel_step(cr, pl.multiple_of(c * ck, ck), False), init)
    _, l_s, acc_s = sel_step(carry, pl.multiple_of(last * ck, ck), True)
    o_sel = acc_s * (1.0 / l_s)

    gt = gate_ref[0].T
    gate = lambda j: jax.nn.sigmoid(jnp.concatenate([gt[i * 3 + j:i * 3 + j + 1] for i in range(g)], axis=1))
    ot = gate(0) * o_cmp + gate(1) * o_sel + gate(2) * o_win
    o2 = jnp.concatenate([ot[:, i * tq:(i + 1) * tq] for i in range(g)], axis=0)
    o_ref[0] = o2.T.astype(o_ref.dtype)


def _nsa_attention(proj, cos_t, sin_t, qg, bounds, kc, vct, ks, vst, kw, vwt, gate_logits, *, bounded,
                   tq=512, ck=512):
    b, s, _ = proj.shape
    hk, g, dh = NSA_KV_HEADS, NSA_GROUP, HEAD_DIM
    assert WINDOW >= tq and ck % tq == 0 and s % ck == 0
    tab = pl.BlockSpec((1, ROPE_DIM // 2, tq), lambda i, j, t: (i, 0, t))
    assert s // SEL_BLOCK <= kc.shape[3] - dh
    keys = lambda a: pl.BlockSpec((1, 1) + a.shape[2:], lambda i, j, t: (i, j, 0, 0))
    vals = lambda a: pl.BlockSpec((1, dh, a.shape[2]), lambda i, j, t: (i, j, 0))
    return pl.pallas_call(
        functools.partial(_nsa_kernel, tq=tq, ck=ck, bounded=bounded),
        grid=(b, hk, s // tq),
        in_specs=[pl.BlockSpec((1, tq, g * dh), lambda i, j, t: (i, t, j)),
                  tab, tab, pl.BlockSpec((dh, 1), lambda i, j, t: (0, 0)),
                  pl.BlockSpec(bounds.shape, lambda i, j, t: (0, 0)),
                  keys(kc), vals(vct), keys(ks), vals(vst), keys(kw), vals(vwt),
                  pl.BlockSpec((1, tq, LANES), lambda i, j, t: (i, t, j))],
        out_specs=pl.BlockSpec((1, tq, g * dh), lambda i, j, t: (i, t, j)),
        out_shape=jax.ShapeDtypeStruct((b, s, NSA_WIDTH), bf16),
        compiler_params=_cparams("parallel", "parallel", "arbitrary"),
        name="nsa_attention_bounded" if bounded else "nsa_attention",
    )(proj, cos_t, sin_t, qg, bounds, kc, vct, ks, vst, kw, vwt, gate_logits)


def _gla_kernel(q_ref, k_ref, v_ref, lr_ref, wg_ref, bg_ref, r_ref, ng_ref, o_ref, *, grp, npair):
    c = GLA_CHUNK
    s = q_ref.shape[1]
    dk, dv = GLA_KEY_DIM, GLA_VAL_DIM
    rows = grp * c
    ri = lax.broadcasted_iota(jnp.int32, (rows, rows), 0)
    ci = lax.broadcasted_iota(jnp.int32, (rows, rows), 1)
    causal = (ci <= ri) & (ci // c == ri // c)
    tri = jnp.where(causal, 1.0, 0.0).astype(bf16)
    lane = lax.broadcasted_iota(jnp.int32, (1, 2 * dk), 1)
    head_mask = [lane < dk, lane >= dk]
    chunk_of_row = lax.broadcasted_iota(jnp.int32, (rows, 1), 0) // c
    wg = wg_ref[...]
    bg = bg_ref[...]
    ng = ng_ref[...]

    def body(n, states):
        r0 = pl.multiple_of(n * rows, rows)
        z = jnp.dot(lr_ref[0, pl.ds(r0, rows), :GLA_GATE_RANK], wg, preferred_element_type=f32,
                    precision=lax.Precision.HIGHEST) + bg
        log_a = -(jnp.maximum(-z, 0.0) + jnp.log(1.0 + jnp.exp(-jnp.abs(z)))) / GLA_TAU
        la_hi = log_a.astype(bf16)
        la_mid, la_lo = _split_bf16(log_a - la_hi.astype(f32))
        bcum = (jnp.dot(tri, la_hi, preferred_element_type=f32) + jnp.dot(tri, la_mid, preferred_element_type=f32) +
                jnp.dot(tri, la_lo, preferred_element_type=f32))
        qf = q_ref[0, pl.ds(r0, rows), :].astype(f32) * (dk ** -0.5)
        kf = k_ref[0, pl.ds(r0, rows), :].astype(f32)
        q_dec = qf * jnp.exp(bcum)
        k_dec = (kf * jnp.exp(-bcum)).astype(bf16)
        b_last = [bcum[(i + 1) * c - 1:(i + 1) * c, :] for i in range(grp)]
        k_state = jnp.concatenate(
            [kf[i * c:(i + 1) * c] * jnp.exp(b_last[i] - bcum[i * c:(i + 1) * c]) for i in range(grp)],
            axis=0).astype(bf16)
        by_chunk = lambda t: jnp.concatenate(
            [jnp.where(chunk_of_row == i, t, jnp.zeros_like(t)) for i in range(grp)], axis=1)
        new_states = []
        for hh in range(2 * npair):
            pr, h = divmod(hh, 2)
            pair = slice(pr * 2 * dk, (pr + 1) * 2 * dk)
            q_h = jnp.where(head_mask[h], q_dec[:, pair], 0.0).astype(bf16)
            v_h = v_ref[0, pl.ds(r0, rows), hh * dv:(hh + 1) * dv]
            attn = lax.dot_general(q_h, k_dec[:, pair], _NT, preferred_element_type=f32)
            attn = jnp.where(causal, attn, 0.0).astype(bf16)
            o_intra = jnp.dot(attn, v_h, preferred_element_type=f32)
            u_all = lax.dot_general(v_h, by_chunk(k_state[:, pair]), _TN, preferred_element_type=f32)
            st = states[hh]
            entering = []
            for i in range(grp):
                entering.append(st)
                st = st * jnp.exp(b_last[i][:, pair]) + u_all[:, i * 2 * dk:(i + 1) * 2 * dk]
            new_states.append(st)
            o_inter = lax.dot_general(by_chunk(q_h), jnp.concatenate(entering, axis=1).astype(bf16), _NT,
                                      preferred_element_type=f32)
            o = _rms(o_intra + o_inter, ng)
            rr = r_ref[0, pl.ds(r0, rows), hh * dv:(hh + 1) * dv].astype(f32)
            o_ref[0, pl.ds(r0, rows), hh * dv:(hh + 1) * dv] = (o * (rr * jax.nn.sigmoid(rr))).astype(o_ref.dtype)
        return tuple(new_states)

    zero = jnp.zeros((dv, 2 * dk), f32)
    lax.fori_loop(0, s // rows, body, (zero,) * (2 * npair), unroll=GLA_UNROLL)


def _gla(proj, proj_s, wg, bg, ng, grp=4, npair=GLA_HEADS // 2):
    b, s, _ = proj.shape
    dk2, dv2 = 2 * npair * GLA_KEY_DIM, 2 * npair * GLA_VAL_DIM
    oq, ok_, ov, orr = (MAIN_OFFS[7] // dk2, MAIN_OFFS[8] // dk2, MAIN_OFFS[9] // dv2, MAIN_OFFS[10] // dv2)
    assert all(MAIN_OFFS[i] % w == 0 for i, w in ((7, dk2), (8, dk2), (9, dv2), (10, dv2)))
    col = lambda w, o: pl.BlockSpec((1, s, w), lambda i, j: (i, 0, o + j))
    return pl.pallas_call(
        functools.partial(_gla_kernel, grp=grp, npair=npair),
        grid=(b, GLA_HEADS // (2 * npair)),
        in_specs=[col(dk2, oq), col(dk2, ok_), col(dv2, ov),
                  pl.BlockSpec((1, s, LANES), lambda i, j: (i, 0, 2)),
                  pl.BlockSpec((GLA_GATE_RANK, dk2), lambda i, j: (0, j)),
                  pl.BlockSpec((1, dk2), lambda i, j: (0, j)),
                  col(dv2, orr),
                  pl.BlockSpec((1, GLA_VAL_DIM), lambda i, j: (0, 0))],
        out_specs=pl.BlockSpec((1, s, dv2), lambda i, j: (i, 0, j)),
        out_shape=jax.ShapeDtypeStruct((b, s, GLA_WIDTH), bf16),
        compiler_params=_cparams("parallel", "parallel"),
        name="gla",
    )(proj, proj, proj, proj_s, wg, bg, proj, ng)


def _outproj_kernel(a_ref, b_ref, x_ref, wo_ref, g2_ref, wrh_ref, wrl_ref, br_ref,
                    x1_ref, h2_ref, rt_ref, cnt_ref, carry_ref):
    @pl.when(pl.program_id(0) == 0)
    def _():
        carry_ref[...] = jnp.zeros_like(carry_ref)

    na = a_ref.shape[1]
    tm = a_ref.shape[0]
    y = (jnp.dot(a_ref[...], wo_ref[:na, :], preferred_element_type=f32) +
         jnp.dot(b_ref[...], wo_ref[na:, :], preferred_element_type=f32))
    x1 = x_ref[...] + y
    x1_ref[...] = x1
    h2 = _rms(x1, g2_ref[...])
    h2_ref[...] = h2.astype(bf16)
    hi, lo = _split_bf16(h2)
    logits = (jnp.dot(hi, wrh_ref[...], preferred_element_type=f32) +
              jnp.dot(lo, wrh_ref[...], preferred_element_type=f32) +
              jnp.dot(hi, wrl_ref[...], preferred_element_type=f32)) + br_ref[...]
    score = logits.T[:N_EXPERTS]
    e_col = lax.broadcasted_iota(jnp.int32, (N_EXPERTS, 1), 0).astype(f32)
    top = jnp.max(score, axis=0, keepdims=True)
    hits, firsts, weights = [], [], []
    for _ in range(TOP_K):
        m = jnp.max(score, axis=0, keepdims=True)
        first = jnp.min(jnp.where(score == m, e_col, float(N_EXPERTS)), axis=0, keepdims=True)
        hit = e_col == first
        hits.append(hit)
        firsts.append(first)
        weights.append(jnp.exp(m - top))
        score = jnp.where(hit, -jnp.inf, score)
    tot = weights[0] + weights[1] + weights[2] + weights[3]
    onehot = jnp.where(hits[0] | hits[1] | hits[2] | hits[3], 1.0, 0.0)
    ri = lax.broadcasted_iota(jnp.int32, (tm, tm), 0)
    ci = lax.broadcasted_iota(jnp.int32, (tm, tm), 1)
    earlier = jnp.where(ri < ci, 1.0, 0.0).astype(bf16)
    rank = carry_ref[...] + jnp.dot(onehot.astype(bf16), earlier, preferred_element_type=f32)
    carry_ref[...] = carry_ref[...] + jnp.sum(onehot, axis=1, keepdims=True)
    cnt_ref[...] = jnp.broadcast_to(carry_ref[...], cnt_ref.shape)
    ranks = [jnp.sum(jnp.where(hits[k], rank, 0.0), axis=0, keepdims=True) for k in range(TOP_K)]
    gates = [weights[k] / tot for k in range(TOP_K)]
    pad = jnp.zeros((rt_ref.shape[0] - 3 * TOP_K, tm), f32)
    rt_ref[...] = jnp.concatenate(firsts + gates + ranks + [pad], axis=0)


def _outproj_router(a, b, x2, wo, g2, wr_hi, wr_lo, br, tm=512):
    m, d = x2.shape
    na, nb = a.shape[1], b.shape[1]
    row = lambda n: pl.BlockSpec((tm, n), lambda i: (i, 0))
    whole = lambda t: pl.BlockSpec(t.shape, lambda i: (0,) * t.ndim)
    return pl.pallas_call(
        _outproj_kernel,
        grid=(m // tm,),
        in_specs=[row(na), row(nb), row(d), whole(wo), whole(g2), whole(wr_hi), whole(wr_lo), whole(br)],
        out_specs=[row(d), row(d), pl.BlockSpec((ROUTE_ROWS, tm), lambda i: (0, i)),
                   pl.BlockSpec((N_EXPERTS, LANES), lambda i: (0, 0))],
        out_shape=[jax.ShapeDtypeStruct((m, d), f32),
                   jax.ShapeDtypeStruct((m, d), bf16),
                   jax.ShapeDtypeStruct((ROUTE_ROWS, m), f32),
                   jax.ShapeDtypeStruct((N_EXPERTS, LANES), f32)],
        scratch_shapes=[pltpu.VMEM((N_EXPERTS, 1), f32)],
        compiler_params=_cparams("arbitrary"),
        name="outproj_router",
    )(a, b, x2, wo, g2, wr_hi, wr_lo, br)


def _moe_kernel(vb_ref, ve_ref, lo_ref, hi_ref, nv_ref, xs_ref, wgu_ref, bgu_ref, wd_ref, bd_ref, ys_ref,
                o_ref, wgu_bf, wd_bf, *, slice_id, table_off):
    del ys_ref
    step = pl.program_id(0)
    v = table_off + step
    prev = jnp.maximum(v - 1, table_off)
    live = step < nv_ref[slice_id]
    new_expert = (step == 0) | (ve_ref[v] != ve_ref[prev])
    new_block = (step == 0) | (vb_ref[v] != vb_ref[prev])

    @pl.when(live & new_expert)
    def _():
        wgu_bf[...] = wgu_ref[0].astype(bf16)
        wd_bf[...] = wd_ref[0].astype(bf16)

    @pl.when(live & new_block)
    def _():
        o_ref[...] = jnp.zeros_like(o_ref)

    @pl.when(live)
    def _():
        blk0 = vb_ref[v] * MOE_BLOCK
        lo, hi = lo_ref[v], hi_ref[v]
        first = jnp.maximum(lo - blk0, 0) // MOE_SUB
        last = (jnp.minimum(hi - blk0, MOE_BLOCK) + MOE_SUB - 1) // MOE_SUB

        def sub_block(j, carry):
            r0 = pl.multiple_of(j * MOE_SUB, MOE_SUB)
            h = jnp.dot(xs_ref[pl.ds(r0, MOE_SUB), :], wgu_bf[...], preferred_element_type=f32) + bgu_ref[0]
            x_glu = jnp.minimum(h[:, :D_FF], SWIGLU_LIMIT)
            x_lin = jnp.clip(h[:, D_FF:], -SWIGLU_LIMIT, SWIGLU_LIMIT)
            act = x_glu * jax.nn.sigmoid(SWIGLU_ALPHA * x_glu) * (x_lin + 1.0)
            y = jnp.dot(act.astype(bf16), wd_bf[...], preferred_element_type=f32) + bd_ref[0]
            row = blk0 + r0 + lax.broadcasted_iota(jnp.int32, (MOE_SUB, 1), 0)
            mine = (row >= lo) & (row < hi)
            o_ref[pl.ds(r0, MOE_SUB), :] = jnp.where(mine, y.astype(o_ref.dtype), o_ref[pl.ds(r0, MOE_SUB), :])
            return carry

        lax.fori_loop(first, last, sub_block, 0)


def _moe_experts(table, slice_id, blk_lo, xs, wgu, bgu, wd, bd, ys_so_far):
    d = xs.shape[1]
    nvis = xs.shape[0] // MOE_BLOCK + N_EXPERTS - 1
    off = slice_id * (table[0].shape[0] // table[4].shape[0])
    in_specs = [pl.BlockSpec((MOE_BLOCK, d), lambda v, vb, ve, *_: (vb[off + v] - blk_lo, 0)),
                pl.BlockSpec((1, d, 2 * D_FF), lambda v, vb, ve, *_: (ve[off + v], 0, 0)),
                pl.BlockSpec((1, 1, 2 * D_FF), lambda v, vb, ve, *_: (ve[off + v], 0, 0)),
                pl.BlockSpec((1, D_FF, d), lambda v, vb, ve, *_: (ve[off + v], 0, 0)),
                pl.BlockSpec((1, 1, d), lambda v, vb, ve, *_: (ve[off + v], 0, 0))]
    operands = table + (xs, wgu, bgu, wd, bd, ys_so_far)
    in_specs.append(pl.BlockSpec(memory_space=pl.ANY))
    aliases = {len(operands) - 1: 0}
    return pl.pallas_call(
        functools.partial(_moe_kernel, slice_id=slice_id, table_off=off),
        grid_spec=pltpu.PrefetchScalarGridSpec(
            num_scalar_prefetch=5,
            grid=(nvis,),
            in_specs=in_specs,
            out_specs=pl.BlockSpec((MOE_BLOCK, d), lambda v, vb, ve, *_: (vb[off + v], 0)),
            scratch_shapes=[pltpu.VMEM((d, 2 * D_FF), bf16), pltpu.VMEM((D_FF, d), bf16)],
        ),
        out_shape=jax.ShapeDtypeStruct(ys_so_far.shape, ys_so_far.dtype),
        input_output_aliases=aliases,
        compiler_params=_cparams("arbitrary"),
        name="moe_experts",
    )(*operands)


def _positions_kernel(start_ref, rt_ref, pos_ref):
    idx = rt_ref[:TOP_K, :].astype(jnp.int32)
    pos = rt_ref[2 * TOP_K:3 * TOP_K, :].astype(jnp.int32)
    for e in range(N_EXPERTS):
        pos = pos + jnp.where(idx == e, start_ref[e], 0)
    pos_ref[...] = pos


def _positions(start, rt):
    m = rt.shape[1]
    return pl.pallas_call(
        _positions_kernel,
        grid_spec=pltpu.PrefetchScalarGridSpec(
            num_scalar_prefetch=1, grid=(1,),
            in_specs=[pl.BlockSpec(rt.shape, lambda i, st: (0, 0))],
            out_specs=pl.BlockSpec((TOP_K, m), lambda i, st: (0, 0))),
        out_shape=jax.ShapeDtypeStruct((TOP_K, m), jnp.int32),
        compiler_params=_cparams("arbitrary"),
        name="route_positions",
    )(start, rt)


def _visit_tables(start, end, cuts):
    blk_lo = jnp.array([0] + list(cuts[:-1]), jnp.int32)[:, None]
    blk_hi = jnp.array(list(cuts), jnp.int32)[:, None]
    room = max(h - l for l, h in zip([0] + list(cuts[:-1]), cuts)) + N_EXPERTS - 1
    first_blk = jnp.maximum((start // MOE_BLOCK)[None, :], blk_lo)
    last_blk = jnp.minimum(jnp.where(end > start, (end - 1) // MOE_BLOCK, -1)[None, :], blk_hi - 1)
    per_e = jnp.maximum(last_blk - first_blk + 1, 0)
    v_end = jnp.cumsum(per_e, axis=1)
    v_start = v_end - per_e
    n_live = v_end[:, -1]
    vc = jnp.minimum(jnp.arange(room, dtype=jnp.int32)[None, :], n_live[:, None] - 1)
    ve = jnp.sum((v_end[:, None, :] <= vc[:, :, None]).astype(jnp.int32), axis=2)
    of_ve = lambda t: jnp.sum(jnp.where(ve[:, :, None] == jnp.arange(N_EXPERTS), t[:, None, :], 0), axis=2)
    vb = of_ve(first_blk) + vc - of_ve(v_start)
    flat = lambda t: t.reshape(-1).astype(jnp.int32)
    return (flat(vb), flat(ve), flat(of_ve(jnp.broadcast_to(start, per_e.shape))),
            flat(of_ve(jnp.broadcast_to(end, per_e.shape))), n_live.astype(jnp.int32))


def _rope_tables(positions):
    half = ROPE_DIM // 2
    inv_freq = jnp.exp(-math.log(ROPE_THETA) * jnp.arange(0, ROPE_DIM, 2, dtype=f32) / ROPE_DIM)
    ang = positions.astype(f32)[:, None, :] * inv_freq[None, :, None]
    cos_t, sin_t = jnp.cos(ang), jnp.sin(ang)
    d = jnp.arange(LANES) % HEAD_DIM
    f = jnp.arange(LANES)[:, None]
    first, second = d[None, :] == f % half, d[None, :] == f % half + half
    to_cos = jnp.where((f < half) & (first | second), 1.0, 0.0)
    to_sin = jnp.where((f >= half) & (f < 2 * half), jnp.where(second, 1.0, 0.0) - jnp.where(first, 1.0, 0.0), 0.0)
    spread = jnp.concatenate([to_cos, to_sin], axis=1).astype(bf16)
    idx = jnp.arange(half)
    perm = jnp.zeros((HEAD_DIM, HEAD_DIM), f32).at[idx + half, idx].set(1.0).at[idx, idx + half].set(1.0)
    return cos_t, sin_t, spread, perm


def _layer(x, positions, norm1_g, w_in, q_norm_g, k_norm_g, cmp_pos_k, cmp_w1_k, cmp_w2_k,
           cmp_pos_v, cmp_w1_v, cmp_w2_v, gla_w_gate, gla_b_gate, gla_norm_g, w_out, norm2_g,
           w_router, b_router, w_gate_up, b_gate_up, w_down, b_down):
    b, s, d = x.shape
    m = b * s
    hk, g, dh = NSA_KV_HEADS, NSA_GROUP, HEAD_DIM
    x2 = x.reshape(m, d)

    offs = [0]
    for sz in IN_SIZES:
        offs.append(offs[-1] + sz)
    seg = lambda i: w_in[:, offs[i]:offs[i + 1]]
    w_main = jnp.concatenate([seg(MAIN_IN_SEG[i]) for i in MAIN_LAYOUT], axis=1).astype(bf16)
    padw = lambda t: jnp.concatenate([t, jnp.zeros((d, LANES - t.shape[1]), f32)], axis=1)
    ng = g * 3
    w_small = jnp.concatenate([padw(seg(7)[:, :ng]), padw(seg(7)[:, ng:]), padw(seg(11))], axis=1).astype(bf16)
    proj, proj_s = _inproj(x2, norm1_g.reshape(1, d), w_main, w_small)
    proj = proj.reshape(b, s, -1)
    proj_s = proj_s.reshape(b, s, 3 * LANES)

    cos_t, sin_t, spread, perm = _rope_tables(positions)
    per_kv = lambda t: jnp.concatenate([t] * hk, axis=-1)

    def bdiag(t):
        r, c = t.shape[-2:]
        lead = [(0, 0)] * (t.ndim - 2)
        return jnp.concatenate([jnp.pad(t, lead + [(0, 0), (i * c, (hk - 1 - i) * c)]) for i in range(hk)], axis=-2)

    w1_bd = lambda w1: bdiag(w1.reshape(CMP_BLOCK, dh, CMP_HIDDEN)).astype(bf16)
    kc, vct, ks_rot, vst, kw_rot, vwt = _kvprep(
        proj, jnp.concatenate([cos_t, sin_t], axis=1), spread, per_kv(k_norm_g),
        bdiag(jnp.ones((dh, dh), f32)).astype(bf16), bdiag(perm).astype(bf16),
        jnp.stack([jnp.eye(hk * dh, LANES, k=-h * dh, dtype=f32) * (jnp.arange(LANES) < dh)
                   for h in range(hk)]).astype(bf16),
        per_kv(cmp_pos_k), w1_bd(cmp_w1_k), bdiag(cmp_w2_k).astype(bf16),
        per_kv(cmp_pos_v), w1_bd(cmp_w1_v), bdiag(cmp_w2_v).astype(bf16))
    score_bound = (1.05 * dh * dh ** -0.5 * math.log2(math.e) * jnp.max(jnp.abs(q_norm_g)) *
                   jnp.max(jnp.abs(k_norm_g), axis=1))
    bounds = jnp.zeros((8, LANES), f32).at[0, :3].set(score_bound)
    nsa_args = (proj, cos_t, sin_t, q_norm_g.reshape(dh, 1), bounds, kc, vct, ks_rot, vst, kw_rot, vwt, proj_s)
    nsa_out = lax.cond(2.0 * jnp.max(score_bound) < NSA_SAFE_EXPONENT,
                       functools.partial(_nsa_attention, bounded=True),
                       functools.partial(_nsa_attention, bounded=False), *nsa_args).reshape(m, NSA_WIDTH)

    gla_out = _gla(proj, proj_s, gla_w_gate, gla_b_gate.reshape(1, -1),
                   gla_norm_g.reshape(1, GLA_VAL_DIM)).reshape(m, GLA_WIDTH)

    padr = lambda t: jnp.concatenate([t, jnp.zeros(t.shape[:-1] + (LANES - t.shape[-1],), t.dtype)], axis=-1)
    wr_hi = w_router.astype(bf16)
    wr_lo = (w_router - wr_hi.astype(f32)).astype(bf16)
    x1, h2, rt, cnt = _outproj_router(nsa_out, gla_out, x2, w_out.astype(bf16), norm2_g.reshape(1, d),
                                      padr(wr_hi), padr(wr_lo), padr(b_router.reshape(1, -1)))

    a = m * TOP_K
    gate = rt[TOP_K:2 * TOP_K]
    counts = cnt[:, 0].astype(jnp.int32)
    end = jnp.cumsum(counts)
    start = end - counts
    pos = _positions(start, rt).reshape(a)
    tok = jnp.arange(a, dtype=jnp.int32) % m
    sorted_tok = jnp.zeros((a,), jnp.int32).at[pos].add(tok, unique_indices=True, mode='promise_in_bounds')
    rows_of = lambda t, idx: t.at[idx].get(mode='promise_in_bounds')
    nblk = a // MOE_BLOCK
    cuts = [nblk * c // MOE_SLICE_CUTS[-1] for c in MOE_SLICE_CUTS] if nblk % MOE_SLICE_CUTS[-1] == 0 else [nblk]
    table = _visit_tables(start, end, cuts)
    ys, lo_blk = jnp.zeros((a, d), bf16), 0
    for i, hi_blk in enumerate(cuts):
        xs = rows_of(h2, sorted_tok[lo_blk * MOE_BLOCK:hi_blk * MOE_BLOCK])
        ys = _moe_experts(table, i, lo_blk, xs, w_gate_up, b_gate_up.reshape(N_EXPERTS, 1, -1),
                          w_down, b_down.reshape(N_EXPERTS, 1, -1), ys)
        lo_blk = hi_blk
    out = x1
    for k in range(TOP_K):
        out = out + rows_of(ys, pos[k * m:(k + 1) * m]).astype(f32) * gate[k][:, None]
    return out.reshape(b, s, d)


def kernel(x, positions, norm1_g, w_in, nsa_q_norm_g, nsa_k_norm_g, cmp_pos_k, cmp_w1_k, cmp_w2_k,
           cmp_pos_v, cmp_w1_v, cmp_w2_v, gla_w_gate, gla_b_gate, gla_norm_g, w_out, norm2_g,
           w_router, b_router, w_gate_up, b_gate_up, w_down, b_down):
    for l in range(norm1_g.shape[0]):
        x = _layer(x, positions, norm1_g[l], w_in[l], nsa_q_norm_g[l], nsa_k_norm_g[l],
                   cmp_pos_k[l], cmp_w1_k[l], cmp_w2_k[l], cmp_pos_v[l], cmp_w1_v[l], cmp_w2_v[l],
                   gla_w_gate[l], gla_b_gate[l], gla_norm_g[l], w_out[l], norm2_g[l],
                   w_router[l], b_router[l], w_gate_up[l], b_gate_up[l], w_down[l], b_down[l])
    return x
```

```python
import functools
import math

import jax
import jax.numpy as jnp
from jax import lax
from jax.experimental import pallas as pl
from jax.experimental.pallas import tpu as pltpu

f32 = jnp.float32
bf16 = jnp.bfloat16

NSA_HEADS = 8
NSA_KV_HEADS = 2
NSA_GROUP = NSA_HEADS // NSA_KV_HEADS
HEAD_DIM = 64
CMP_BLOCK = 32
CMP_STRIDE = 16
CMP_HIDDEN = 256
SEL_BLOCK = 64
SEL_TOPK = 8
WINDOW = 512
FORCE_BONUS = 1e4
GLA_HEADS = 4
GLA_KEY_DIM = 64
GLA_VAL_DIM = 128
GLA_CHUNK = 64
GLA_GATE_RANK = 16
GLA_TAU = 16.0
ROPE_THETA = 500000.0
ROPE_DIM = HEAD_DIM // 4
N_EXPERTS = 32
TOP_K = 4
D_FF = 1024
SWIGLU_LIMIT = 7.0
SWIGLU_ALPHA = 1.702
EPS = 1e-6
NEG_INF = -1e30

NSA_WIDTH = NSA_HEADS * HEAD_DIM
NSA_KV_WIDTH = NSA_KV_HEADS * HEAD_DIM
GLA_KEY_WIDTH = GLA_HEADS * GLA_KEY_DIM
GLA_WIDTH = GLA_HEADS * GLA_VAL_DIM
IN_SIZES = (NSA_WIDTH,) + (NSA_KV_WIDTH,) * 6 + (
    NSA_HEADS * 3, GLA_KEY_WIDTH, GLA_KEY_WIDTH, GLA_WIDTH, GLA_GATE_RANK, GLA_WIDTH)
MAIN_SIZES = (NSA_WIDTH,) + (NSA_KV_WIDTH,) * 6 + (GLA_KEY_WIDTH, GLA_KEY_WIDTH, GLA_WIDTH, GLA_WIDTH)
MAIN_IN_SEG = (0, 1, 2, 3, 4, 5, 6, 8, 9, 10, 12)
MAIN_LAYOUT = (0, 9, 10, 1, 2, 3, 4, 5, 6, 7, 8)
MAIN_OFFS = tuple(sum(MAIN_SIZES[j] for j in MAIN_LAYOUT[:MAIN_LAYOUT.index(i)]) for i in range(len(MAIN_SIZES)))

LANES = 128
VMEM_LIMIT = 48 * 1024 * 1024
MOE_BLOCK = 1024
MOE_SUB = 256
MOE_SLICE_CUTS = (1, 4, 8, 16)
NSA_SAFE_EXPONENT = 120.0
GLA_UNROLL = 4
ROUTE_ROWS = 16

_NT = (((1,), (1,)), ((), ()))
_TN = (((0,), (0,)), ((), ()))


def _cparams(*sem):
    return pltpu.CompilerParams(dimension_semantics=sem, vmem_limit_bytes=VMEM_LIMIT)


def _rms(t, g):
    return t * lax.rsqrt(jnp.mean(t * t, axis=-1, keepdims=True) + EPS) * g


def _split_bf16(t):
    hi = t.astype(bf16)
    lo = (t - hi.astype(f32)).astype(bf16)
    return hi, lo


def _rope(t, cos_f, sin_f, perm):
    hi, lo = _split_bf16(t)
    rot = (jnp.dot(hi, perm, preferred_element_type=f32) +
           jnp.dot(lo, perm, preferred_element_type=f32))
    return t * cos_f + rot * sin_f


def _inproj_kernel(x_ref, g_ref, w_ref, ws_ref, o_ref, os_ref):
    x = x_ref[...]
    h = _rms(x, g_ref[...]).astype(bf16)
    n = o_ref.shape[1]
    step = 512
    for c in range(0, n, step):
        e = min(c + step, n)
        o_ref[:, c:e] = jnp.dot(h, w_ref[:, c:e], preferred_element_type=f32).astype(bf16)
    os_ref[...] = jnp.dot(h, ws_ref[...], preferred_element_type=f32)


def _inproj(x2, g, w_main, w_small, tm=512):
    m, d = x2.shape
    n = w_main.shape[1]
    ns = w_small.shape[1]
    return pl.pallas_call(
        _inproj_kernel,
        grid=(m // tm,),
        in_specs=[pl.BlockSpec((tm, d), lambda i: (i, 0)),
                  pl.BlockSpec((1, d), lambda i: (0, 0)),
                  pl.BlockSpec((d, n), lambda i: (0, 0)),
                  pl.BlockSpec((d, ns), lambda i: (0, 0))],
        out_specs=[pl.BlockSpec((tm, n), lambda i: (i, 0)),
                   pl.BlockSpec((tm, ns), lambda i: (i, 0))],
        out_shape=[jax.ShapeDtypeStruct((m, n), bf16),
                   jax.ShapeDtypeStruct((m, ns), f32)],
        compiler_params=_cparams("parallel"),
        name="inproj",
    )(x2, g, w_main, w_small)


def _kvprep_kernel(kc_ref, vc_ref, ks_ref, vs_ref, kw_ref, vw_ref, cs_ref, spread_ref, kg_ref, ones_ref,
                   perm_ref, pick_ref, posk_ref, w1k_ref, w2k_ref, posv_ref, w1v_ref, w2v_ref,
                   kco_ref, vcto_ref, kso_ref, vsto_ref, kwo_ref, vwto_ref, tmp_ref):
    ones_bd = ones_ref[...]
    nh_out = kco_ref.shape[2]

    def rms_heads(t, g):
        sq_hi, sq_lo = _split_bf16(t * t)
        ss = (jnp.dot(sq_hi, ones_bd, preferred_element_type=f32) +
              jnp.dot(sq_lo, ones_bd, preferred_element_type=f32))
        return t * lax.rsqrt(ss * (1.0 / HEAD_DIM) + EPS) * g

    def compress(src_ref, pos_ref, w1_ref, w2_ref):
        tmp_ref[...] = src_ref[0].astype(f32)
        nh = tmp_ref.shape[0] // CMP_STRIDE
        a = jnp.zeros((nh, NSA_KV_HEADS * CMP_HIDDEN), f32)
        b = jnp.zeros((nh, NSA_KV_HEADS * CMP_HIDDEN), f32)
        for p in range(CMP_STRIDE):
            rows = tmp_ref[pl.ds(p, nh, stride=CMP_STRIDE), :]
            a = a + jnp.dot((rows + pos_ref[p:p + 1, :]).astype(bf16), w1_ref[p], preferred_element_type=f32)
            q = CMP_STRIDE + p
            b = b + jnp.dot((rows + pos_ref[q:q + 1, :]).astype(bf16), w1_ref[q], preferred_element_type=f32)
        pre = a + pltpu.roll(b, nh - 1, 0)
        hid = pre * jax.nn.sigmoid(pre)
        return jnp.dot(hid.astype(bf16), w2_ref[...], preferred_element_type=f32)

    kc = rms_heads(compress(kc_ref, posk_ref, w1k_ref, w2k_ref), kg_ref[0:1, :]).astype(bf16)
    tmp_ref[:nh_out, :] = compress(vc_ref, posv_ref, w1v_ref, w2v_ref)
    vcto_ref[0] = tmp_ref[:nh_out, :].T.astype(bf16)
    s, w = ks_ref.shape[1], ks_ref.shape[2]
    cs = jnp.concatenate([cs_ref[0], jnp.zeros((LANES - cs_ref.shape[1], s), f32)], axis=0).T
    cs_hi = cs.astype(bf16)
    cs_mid, cs_lo = _split_bf16(cs - cs_hi.astype(f32))
    spread = spread_ref[...]
    tables = (jnp.dot(cs_hi, spread, preferred_element_type=f32) + jnp.dot(cs_mid, spread, preferred_element_type=f32) +
              jnp.dot(cs_lo, spread, preferred_element_type=f32))
    lane_row = lax.broadcasted_iota(jnp.int32, (1, LANES), 1)
    cos_f = tables[:, :LANES] + jnp.where(lane_row % HEAD_DIM >= ROPE_DIM, 1.0, 0.0)
    sin_f = tables[:, LANES:]
    perm = perm_ref[...]
    ks = _rope(rms_heads(ks_ref[0].astype(f32), kg_ref[1:2, :]), cos_f, sin_f, perm).astype(bf16)
    kw = _rope(rms_heads(kw_ref[0].astype(f32), kg_ref[2:3, :]), cos_f, sin_f, perm).astype(bf16)
    row = lax.broadcasted_iota(jnp.int32, (s, LANES), 0)
    lane = lax.broadcasted_iota(jnp.int32, (s, LANES), 1)
    last_lane_one = lambda n: jnp.where(lax.broadcasted_iota(jnp.int32, (n, LANES), 1) == LANES - 1, 1.0, 0.0)
    ones_lane = last_lane_one(s)
    block_onehot = jnp.where(row // SEL_BLOCK + HEAD_DIM == lane, 1.0, 0.0) + ones_lane
    pad_lane = lax.broadcasted_iota(jnp.int32, (WINDOW, LANES), 1)
    pad_flag = jnp.where((pad_lane == HEAD_DIM) | (pad_lane == LANES - 1), 1.0, 0.0)
    for h in range(NSA_KV_HEADS):
        pick = pick_ref[h]
        kco_ref[0, h] = (jnp.dot(kc, pick, preferred_element_type=f32) + last_lane_one(nh_out)).astype(bf16)
        kso_ref[0, h] = (jnp.dot(ks, pick, preferred_element_type=f32) + block_onehot).astype(bf16)
        kwo_ref[0, h, :WINDOW, :] = pad_flag.astype(bf16)
        kwo_ref[0, h, WINDOW:, :] = (jnp.dot(kw, pick, preferred_element_type=f32) + ones_lane).astype(bf16)
    vsto_ref[0] = vs_ref[0].astype(f32).T.astype(bf16)
    vwto_ref[0] = jnp.concatenate([jnp.zeros((w, WINDOW), f32), vw_ref[0].astype(f32).T], axis=1).astype(bf16)


def _kvprep(proj, cs_t, spread, kg2, ones_bd, perm2, pick, posk, w1k, w2k, posv, w1v, w2v):
    b, s, _ = proj.shape
    w = NSA_KV_WIDTH
    assert w == LANES
    nh = s // CMP_STRIDE
    col = lambda i: pl.BlockSpec((1, s, w), lambda n: (n, 0, MAIN_OFFS[i] // w))
    whole = lambda a: pl.BlockSpec(a.shape, lambda n: (0,) * a.ndim)
    tab = pl.BlockSpec((1, cs_t.shape[1], s), lambda n: (n, 0, 0))
    out = lambda *shp: (pl.BlockSpec((1,) + shp, lambda n: (n,) + (0,) * len(shp)),
                        jax.ShapeDtypeStruct((b,) + shp, bf16))
    hk = NSA_KV_HEADS
    outs = [out(hk, nh, LANES), out(w, nh), out(hk, s, LANES), out(w, s), out(hk, WINDOW + s, LANES),
            out(w, WINDOW + s)]
    return pl.pallas_call(
        _kvprep_kernel,
        grid=(b,),
        in_specs=[col(1), col(2), col(3), col(4), col(5), col(6), tab, whole(spread), whole(kg2), whole(ones_bd),
                  whole(perm2), whole(pick), whole(posk), whole(w1k), whole(w2k), whole(posv), whole(w1v),
                  whole(w2v)],
        out_specs=[o[0] for o in outs],
        out_shape=[o[1] for o in outs],
        scratch_shapes=[pltpu.VMEM((s, w), f32)],
        compiler_params=_cparams("parallel"),
        name="nsa_kvprep",
    )(proj, proj, proj, proj, proj, proj, cs_t, spread, kg2, ones_bd, perm2, pick, posk, w1k, w2k, posv, w1v, w2v)


def _nsa_kernel(q_ref, cos_ref, sin_ref, qg_ref, mb_ref, kc_ref, vct_ref, ks_ref, vst_ref, kw_ref, vwt_ref,
                gate_ref, o_ref, zero_ref, *, tq, ck, bounded):
    g = NSA_GROUP
    qi = pl.program_id(2)
    t0 = pl.multiple_of(qi * tq, tq)
    scale = HEAD_DIM ** -0.5 * math.log2(math.e)
    per_head = lambda t: jnp.concatenate([t] * g, axis=1)

    qt = q_ref[0].astype(f32).T
    qt = jnp.concatenate([qt[i * HEAD_DIM:(i + 1) * HEAD_DIM] for i in range(g)], axis=1)
    qn = qt * lax.rsqrt(jnp.mean(qt * qt, axis=0, keepdims=True) + EPS) * qg_ref[...]
    half = ROPE_DIM // 2
    cos8, sin8 = per_head(cos_ref[0]), per_head(sin_ref[0])
    x1, x2 = qn[:half], qn[half:ROPE_DIM]
    q_rot = jnp.concatenate([x1 * cos8 - x2 * sin8, x2 * cos8 + x1 * sin8, qn[ROPE_DIM:]], axis=0)
    n_extra = kc_ref.shape[3] - HEAD_DIM

    def with_features(qb, feats, branch):
        used = sum(f.shape[0] for f in feats)
        last = jnp.broadcast_to(-mb_ref[0:1, branch:branch + 1] if bounded else 0.0, (1, g * tq))
        fill = jnp.zeros((n_extra - used - 1, g * tq), f32)
        return jnp.concatenate([qb] + feats + [fill, last], axis=0).astype(bf16)

    q_cmp = with_features(qn * scale, [], 0)
    q_rot = q_rot * scale
    tq_row = t0 + lax.broadcasted_iota(jnp.int32, (1, tq), 1)

    kc = kc_ref[0, 0]
    ncp = kc.shape[0]
    n_col = lax.broadcasted_iota(jnp.int32, (ncp, 1), 0)
    ok_c = (n_col * CMP_STRIDE + (CMP_BLOCK - 1)) <= tq_row
    s_c = jnp.dot(kc, q_cmp, preferred_element_type=f32) + per_head(jnp.where(ok_c, 0.0, NEG_INF))
    if bounded:
        p_c = jnp.exp2(s_c)
    else:
        m_c = jnp.max(s_c, axis=0, keepdims=True)
        p_c = jnp.exp2(s_c - m_c) * per_head(jnp.where(ok_c, 1.0, 0.0))
    den = jnp.sum(p_c, axis=0, keepdims=True)
    p_c = p_c * (1.0 / jnp.where(den > 0, den, 1.0))
    o_cmp = jnp.dot(vct_ref[0], p_c.astype(bf16), preferred_element_type=f32)

    pg = p_c[:, :tq]
    for i in range(1, g):
        pg = pg + p_c[:, i * tq:(i + 1) * tq]
    nj = ks_ref.shape[2] // SEL_BLOCK
    jj = lax.broadcasted_iota(jnp.int32, (nj, ncp), 0)
    nn = lax.broadcasted_iota(jnp.int32, (nj, ncp), 1)
    overlap = (nn * CMP_STRIDE < (jj + 1) * SEL_BLOCK) & (nn * CMP_STRIDE + CMP_BLOCK > jj * SEL_BLOCK)
    overlap = jnp.where(overlap, 1.0, 0.0).astype(bf16)
    pg_hi, pg_lo = _split_bf16(pg)
    imp = (jnp.dot(overlap, pg_hi, preferred_element_type=f32) +
           jnp.dot(overlap, pg_lo, preferred_element_type=f32))
    j_col = lax.broadcasted_iota(jnp.int32, (nj, 1), 0)
    j_f = j_col.astype(f32)
    cur = tq_row // SEL_BLOCK
    valid = j_col <= cur
    forced = (j_col == 0) | (j_col == cur) | (j_col == cur - 1)
    score = jnp.where(valid, imp + jnp.where(forced, FORCE_BONUS, 0.0), NEG_INF)
    sel = jnp.zeros((nj, tq), f32)
    for _ in range(SEL_TOPK):
        m = jnp.max(score, axis=0, keepdims=True)
        first = jnp.min(jnp.where(score == m, j_f, float(nj)), axis=0, keepdims=True)
        hit = j_f == first
        sel = jnp.where(hit, 1.0, sel)
        score = jnp.where(hit, -jnp.inf, score)
    sel = jnp.where(valid, sel, 0.0)

    def weighted_values(vt, p):
        lhs = jnp.concatenate([vt, jnp.ones((16, vt.shape[1]), bf16)], axis=0)
        r = jnp.dot(lhs, p.astype(bf16), preferred_element_type=f32)
        return r[:HEAD_DIM], r[HEAD_DIM:HEAD_DIM + 1]

    wl = WINDOW + tq
    q_win = with_features(q_rot, [jnp.full((1, g * tq), NEG_INF, f32)], 2)
    s_w = jnp.dot(kw_ref[0, 0, pl.ds(t0, wl), :], q_win, preferred_element_type=f32)
    step = lax.broadcasted_iota(jnp.int32, (tq, 1), 0)
    lo_ok = (t0 - WINDOW + step) > (tq_row - WINDOW)
    hi_ok = (t0 + step) <= tq_row
    parts = [s_w[:tq] + per_head(jnp.where(lo_ok, 0.0, NEG_INF))]
    if wl > 2 * tq:
        parts.append(s_w[tq:wl - tq])
    parts.append(s_w[wl - tq:] + per_head(jnp.where(hi_ok, 0.0, NEG_INF)))
    if bounded:
        p_w = jnp.concatenate([jnp.exp2(t) for t in parts], axis=0)
    else:
        m_w = functools.reduce(jnp.maximum, [jnp.max(t, axis=0, keepdims=True) for t in parts])
        p_w = jnp.concatenate([jnp.exp2(t - m_w) for t in parts], axis=0)
    o_win, l_w = weighted_values(vwt_ref[0, :, pl.ds(t0, wl)], p_w)
    o_win = o_win * (1.0 / l_w)

    q_sel = with_features(q_rot, [per_head(jnp.where(sel > 0.5, 0.0, NEG_INF))], 1)

    def sel_step(carry, k0, diagonal):
        m_prev, l_prev, acc = carry
        s = jnp.dot(ks_ref[0, 0, pl.ds(k0, ck), :], q_sel, preferred_element_type=f32)
        if diagonal:
            kpos = k0 + lax.broadcasted_iota(jnp.int32, (ck, 1), 0)
            s = s + per_head(jnp.where(kpos <= tq_row, 0.0, NEG_INF))
        if bounded:
            pv, p_sum = weighted_values(vst_ref[0, :, pl.ds(k0, ck)], jnp.exp2(s))
            return m_prev, l_prev + p_sum, acc + pv
        m_new = jnp.maximum(m_prev, jnp.max(s, axis=0, keepdims=True))
        alpha = jnp.exp2(m_prev - m_new)
        pv, p_sum = weighted_values(vst_ref[0, :, pl.ds(k0, ck)], jnp.exp2(s - m_new))
        return m_new, alpha * l_prev + p_sum, alpha * acc + pv

    init = (jnp.full((1, g * tq), NEG_INF, f32), jnp.zeros((1, g * tq), f32),
            jnp.zeros((HEAD_DIM, g * tq), f32))
    last = (t0 + tq + ck - 1) // ck - 1
    carry = lax.fori_loop(0, last, lambda c, cr: sel_step(cr, pl.multiple_of(c * ck, ck), False), init)
    _, l_s, acc_s = sel_step(carry, pl.multiple_of(last * ck, ck), True)
    o_sel = acc_s * (1.0 / l_s)

    gt = gate_ref[0].T
    gate = lambda j: jax.nn.sigmoid(jnp.concatenate([gt[i * 3 + j:i * 3 + j + 1] for i in range(g)], axis=1))
    ot = gate(0) * o_cmp + gate(1) * o_sel + gate(2) * o_win
    o2 = jnp.concatenate([ot[:, i * tq:(i + 1) * tq] for i in range(g)], axis=0)
    o_ref[0] = o2.T.astype(o_ref.dtype)
    zero_ref[...] = jnp.zeros_like(zero_ref)


def _nsa_attention(proj, cos_t, sin_t, qg, bounds, kc, vct, ks, vst, kw, vwt, gate_logits, *, bounded,
                   zero_rows, zero_cols, tq=512, ck=512):
    b, s, _ = proj.shape
    hk, g, dh = NSA_KV_HEADS, NSA_GROUP, HEAD_DIM
    nt = s // tq
    steps = b * hk * nt
    assert WINDOW >= tq and ck % tq == 0 and s % ck == 0 and zero_rows % (16 * steps) == 0
    tab = pl.BlockSpec((1, ROPE_DIM // 2, tq), lambda i, j, t: (i, 0, t))
    assert s // SEL_BLOCK <= kc.shape[3] - dh
    keys = lambda a: pl.BlockSpec((1, 1) + a.shape[2:], lambda i, j, t: (i, j, 0, 0))
    vals = lambda a: pl.BlockSpec((1, dh, a.shape[2]), lambda i, j, t: (i, j, 0))
    return pl.pallas_call(
        functools.partial(_nsa_kernel, tq=tq, ck=ck, bounded=bounded),
        grid=(b, hk, s // tq),
        in_specs=[pl.BlockSpec((1, tq, g * dh), lambda i, j, t: (i, t, j)),
                  tab, tab, pl.BlockSpec((dh, 1), lambda i, j, t: (0, 0)),
                  pl.BlockSpec(bounds.shape, lambda i, j, t: (0, 0)),
                  keys(kc), vals(vct), keys(ks), vals(vst), keys(kw), vals(vwt),
                  pl.BlockSpec((1, tq, LANES), lambda i, j, t: (i, t, j))],
        out_specs=[pl.BlockSpec((1, tq, g * dh), lambda i, j, t: (i, t, j)),
                   pl.BlockSpec((zero_rows // steps, zero_cols), lambda i, j, t: ((i * hk + j) * nt + t, 0))],
        out_shape=[jax.ShapeDtypeStruct((b, s, NSA_WIDTH), bf16),
                   jax.ShapeDtypeStruct((zero_rows, zero_cols), bf16)],
        compiler_params=_cparams("parallel", "parallel", "arbitrary"),
        name="nsa_attention_bounded" if bounded else "nsa_attention",
    )(proj, cos_t, sin_t, qg, bounds, kc, vct, ks, vst, kw, vwt, gate_logits)


def _gla_kernel(q_ref, k_ref, v_ref, lr_ref, wg_ref, bg_ref, r_ref, ng_ref, o_ref, *, grp, npair):
    c = GLA_CHUNK
    s = q_ref.shape[1]
    dk, dv = GLA_KEY_DIM, GLA_VAL_DIM
    rows = grp * c
    ri = lax.broadcasted_iota(jnp.int32, (rows, rows), 0)
    ci = lax.broadcasted_iota(jnp.int32, (rows, rows), 1)
    causal = (ci <= ri) & (ci // c == ri // c)
    tri = jnp.where(causal, 1.0, 0.0).astype(bf16)
    lane = lax.broadcasted_iota(jnp.int32, (1, 2 * dk), 1)
    head_mask = [lane < dk, lane >= dk]
    chunk_of_row = lax.broadcasted_iota(jnp.int32, (rows, 1), 0) // c
    wg = wg_ref[...]
    bg = bg_ref[...]
    ng = ng_ref[...]

    def body(n, states):
        r0 = pl.multiple_of(n * rows, rows)
        z = jnp.dot(lr_ref[0, pl.ds(r0, rows), :GLA_GATE_RANK], wg, preferred_element_type=f32,
                    precision=lax.Precision.HIGHEST) + bg
        log_a = -(jnp.maximum(-z, 0.0) + jnp.log(1.0 + jnp.exp(-jnp.abs(z)))) / GLA_TAU
        la_hi = log_a.astype(bf16)
        la_mid, la_lo = _split_bf16(log_a - la_hi.astype(f32))
        bcum = (jnp.dot(tri, la_hi, preferred_element_type=f32) + jnp.dot(tri, la_mid, preferred_element_type=f32) +
                jnp.dot(tri, la_lo, preferred_element_type=f32))
        qf = q_ref[0, pl.ds(r0, rows), :].astype(f32) * (dk ** -0.5)
        kf = k_ref[0, pl.ds(r0, rows), :].astype(f32)
        q_dec = qf * jnp.exp(bcum)
        k_dec = (kf * jnp.exp(-bcum)).astype(bf16)
        b_last = [bcum[(i + 1) * c - 1:(i + 1) * c, :] for i in range(grp)]
        k_state = jnp.concatenate(
            [kf[i * c:(i + 1) * c] * jnp.exp(b_last[i] - bcum[i * c:(i + 1) * c]) for i in range(grp)],
            axis=0).astype(bf16)
        by_chunk = lambda t: jnp.concatenate(
            [jnp.where(chunk_of_row == i, t, jnp.zeros_like(t)) for i in range(grp)], axis=1)
        new_states = []
        for hh in range(2 * npair):
            pr, h = divmod(hh, 2)
            pair = slice(pr * 2 * dk, (pr + 1) * 2 * dk)
            q_h = jnp.where(head_mask[h], q_dec[:, pair], 0.0).astype(bf16)
            v_h = v_ref[0, pl.ds(r0, rows), hh * dv:(hh + 1) * dv]
            attn = lax.dot_general(q_h, k_dec[:, pair], _NT, preferred_element_type=f32)
            attn = jnp.where(causal, attn, 0.0).astype(bf16)
            o_intra = jnp.dot(attn, v_h, preferred_element_type=f32)
            u_all = lax.dot_general(v_h, by_chunk(k_state[:, pair]), _TN, preferred_element_type=f32)
            st = states[hh]
            entering = []
            for i in range(grp):
                entering.append(st)
                st = st * jnp.exp(b_last[i][:, pair]) + u_all[:, i * 2 * dk:(i + 1) * 2 * dk]
            new_states.append(st)
            o_inter = lax.dot_general(by_chunk(q_h), jnp.concatenate(entering, axis=1).astype(bf16), _NT,
                                      preferred_element_type=f32)
            o = _rms(o_intra + o_inter, ng)
            rr = r_ref[0, pl.ds(r0, rows), hh * dv:(hh + 1) * dv].astype(f32)
            o_ref[0, pl.ds(r0, rows), hh * dv:(hh + 1) * dv] = (o * (rr * jax.nn.sigmoid(rr))).astype(o_ref.dtype)
        return tuple(new_states)

    zero = jnp.zeros((dv, 2 * dk), f32)
    lax.fori_loop(0, s // rows, body, (zero,) * (2 * npair), unroll=GLA_UNROLL)


def _gla(proj, proj_s, wg, bg, ng, grp=4, npair=GLA_HEADS // 2):
    b, s, _ = proj.shape
    dk2, dv2 = 2 * npair * GLA_KEY_DIM, 2 * npair * GLA_VAL_DIM
    oq, ok_, ov, orr = (MAIN_OFFS[7] // dk2, MAIN_OFFS[8] // dk2, MAIN_OFFS[9] // dv2, MAIN_OFFS[10] // dv2)
    assert all(MAIN_OFFS[i] % w == 0 for i, w in ((7, dk2), (8, dk2), (9, dv2), (10, dv2)))
    col = lambda w, o: pl.BlockSpec((1, s, w), lambda i, j: (i, 0, o + j))
    return pl.pallas_call(
        functools.partial(_gla_kernel, grp=grp, npair=npair),
        grid=(b, GLA_HEADS // (2 * npair)),
        in_specs=[col(dk2, oq), col(dk2, ok_), col(dv2, ov),
                  pl.BlockSpec((1, s, LANES), lambda i, j: (i, 0, 2)),
                  pl.BlockSpec((GLA_GATE_RANK, dk2), lambda i, j: (0, j)),
                  pl.BlockSpec((1, dk2), lambda i, j: (0, j)),
                  col(dv2, orr),
                  pl.BlockSpec((1, GLA_VAL_DIM), lambda i, j: (0, 0))],
        out_specs=pl.BlockSpec((1, s, dv2), lambda i, j: (i, 0, j)),
        out_shape=jax.ShapeDtypeStruct((b, s, GLA_WIDTH), bf16),
        compiler_params=_cparams("parallel", "parallel"),
        name="gla",
    )(proj, proj, proj, proj_s, wg, bg, proj, ng)


def _outproj_kernel(a_ref, b_ref, x_ref, wo_ref, g2_ref, wrh_ref, wrl_ref, br_ref,
                    x1_ref, h2_ref, rt_ref, cnt_ref, carry_ref):
    @pl.when(pl.program_id(0) == 0)
    def _():
        carry_ref[...] = jnp.zeros_like(carry_ref)

    na = a_ref.shape[1]
    tm = a_ref.shape[0]
    y = (jnp.dot(a_ref[...], wo_ref[:na, :], preferred_element_type=f32) +
         jnp.dot(b_ref[...], wo_ref[na:, :], preferred_element_type=f32))
    x1 = x_ref[...] + y
    x1_ref[...] = x1
    h2 = _rms(x1, g2_ref[...])
    h2_ref[...] = h2.astype(bf16)
    hi, lo = _split_bf16(h2)
    logits = (jnp.dot(hi, wrh_ref[...], preferred_element_type=f32) +
              jnp.dot(lo, wrh_ref[...], preferred_element_type=f32) +
              jnp.dot(hi, wrl_ref[...], preferred_element_type=f32)) + br_ref[...]
    score = logits.T[:N_EXPERTS]
    e_col = lax.broadcasted_iota(jnp.int32, (N_EXPERTS, 1), 0).astype(f32)
    top = jnp.max(score, axis=0, keepdims=True)
    hits, firsts, weights = [], [], []
    for _ in range(TOP_K):
        m = jnp.max(score, axis=0, keepdims=True)
        first = jnp.min(jnp.where(score == m, e_col, float(N_EXPERTS)), axis=0, keepdims=True)
        hit = e_col == first
        hits.append(hit)
        firsts.append(first)
        weights.append(jnp.exp(m - top))
        score = jnp.where(hit, -jnp.inf, score)
    tot = weights[0] + weights[1] + weights[2] + weights[3]
    onehot = jnp.where(hits[0] | hits[1] | hits[2] | hits[3], 1.0, 0.0)
    ri = lax.broadcasted_iota(jnp.int32, (tm, tm), 0)
    ci = lax.broadcasted_iota(jnp.int32, (tm, tm), 1)
    earlier = jnp.where(ri < ci, 1.0, 0.0).astype(bf16)
    rank = carry_ref[...] + jnp.dot(onehot.astype(bf16), earlier, preferred_element_type=f32)
    carry_ref[...] = carry_ref[...] + jnp.sum(onehot, axis=1, keepdims=True)
    cnt_ref[...] = jnp.broadcast_to(carry_ref[...], cnt_ref.shape)
    ranks = [jnp.sum(jnp.where(hits[k], rank, 0.0), axis=0, keepdims=True) for k in range(TOP_K)]
    gates = [weights[k] / tot for k in range(TOP_K)]
    pad = jnp.zeros((rt_ref.shape[0] - 3 * TOP_K, tm), f32)
    rt_ref[...] = jnp.concatenate(firsts + gates + ranks + [pad], axis=0)


def _outproj_router(a, b, x2, wo, g2, wr_hi, wr_lo, br, tm=512):
    m, d = x2.shape
    na, nb = a.shape[1], b.shape[1]
    row = lambda n: pl.BlockSpec((tm, n), lambda i: (i, 0))
    whole = lambda t: pl.BlockSpec(t.shape, lambda i: (0,) * t.ndim)
    return pl.pallas_call(
        _outproj_kernel,
        grid=(m // tm,),
        in_specs=[row(na), row(nb), row(d), whole(wo), whole(g2), whole(wr_hi), whole(wr_lo), whole(br)],
        out_specs=[row(d), row(d), pl.BlockSpec((ROUTE_ROWS, tm), lambda i: (0, i)),
                   pl.BlockSpec((N_EXPERTS, LANES), lambda i: (0, 0))],
        out_shape=[jax.ShapeDtypeStruct((m, d), f32),
                   jax.ShapeDtypeStruct((m, d), bf16),
                   jax.ShapeDtypeStruct((ROUTE_ROWS, m), f32),
                   jax.ShapeDtypeStruct((N_EXPERTS, LANES), f32)],
        scratch_shapes=[pltpu.VMEM((N_EXPERTS, 1), f32)],
        compiler_params=_cparams("arbitrary"),
        name="outproj_router",
    )(a, b, x2, wo, g2, wr_hi, wr_lo, br)


def _moe_kernel(vb_ref, ve_ref, lo_ref, hi_ref, nv_ref, xs_ref, wgu_ref, bgu_ref, wd_ref, bd_ref, ys_ref,
                o_ref, wgu_bf, wd_bf, *, slice_id, table_off):
    del ys_ref
    step = pl.program_id(0)
    v = table_off + step
    prev = jnp.maximum(v - 1, table_off)
    live = step < nv_ref[slice_id]
    new_expert = (step == 0) | (ve_ref[v] != ve_ref[prev])
    new_block = (step == 0) | (vb_ref[v] != vb_ref[prev])

    @pl.when(live & new_expert)
    def _():
        wgu_bf[...] = wgu_ref[0].astype(bf16)
        wd_bf[...] = wd_ref[0].astype(bf16)

    @pl.when(live & new_block)
    def _():
        o_ref[...] = jnp.zeros_like(o_ref)

    @pl.when(live)
    def _():
        blk0 = vb_ref[v] * MOE_BLOCK
        lo, hi = lo_ref[v], hi_ref[v]
        first = jnp.maximum(lo - blk0, 0) // MOE_SUB
        last = (jnp.minimum(hi - blk0, MOE_BLOCK) + MOE_SUB - 1) // MOE_SUB

        def sub_block(j, carry):
            r0 = pl.multiple_of(j * MOE_SUB, MOE_SUB)
            h = jnp.dot(xs_ref[pl.ds(r0, MOE_SUB), :], wgu_bf[...], preferred_element_type=f32) + bgu_ref[0]
            x_glu = jnp.minimum(h[:, :D_FF], SWIGLU_LIMIT)
            x_lin = jnp.clip(h[:, D_FF:], -SWIGLU_LIMIT, SWIGLU_LIMIT)
            act = x_glu * jax.nn.sigmoid(SWIGLU_ALPHA * x_glu) * (x_lin + 1.0)
            y = jnp.dot(act.astype(bf16), wd_bf[...], preferred_element_type=f32) + bd_ref[0]
            row = blk0 + r0 + lax.broadcasted_iota(jnp.int32, (MOE_SUB, 1), 0)
            mine = (row >= lo) & (row < hi)
            o_ref[pl.ds(r0, MOE_SUB), :] = jnp.where(mine, y.astype(o_ref.dtype), o_ref[pl.ds(r0, MOE_SUB), :])
            return carry

        lax.fori_loop(first, last, sub_block, 0)


def _moe_experts(table, slice_id, blk_lo, xs, wgu, bgu, wd, bd, ys_so_far):
    d = xs.shape[1]
    nvis = xs.shape[0] // MOE_BLOCK + N_EXPERTS - 1
    off = slice_id * (table[0].shape[0] // table[4].shape[0])
    in_specs = [pl.BlockSpec((MOE_BLOCK, d), lambda v, vb, ve, *_: (vb[off + v] - blk_lo, 0)),
                pl.BlockSpec((1, d, 2 * D_FF), lambda v, vb, ve, *_: (ve[off + v], 0, 0)),
                pl.BlockSpec((1, 1, 2 * D_FF), lambda v, vb, ve, *_: (ve[off + v], 0, 0)),
                pl.BlockSpec((1, D_FF, d), lambda v, vb, ve, *_: (ve[off + v], 0, 0)),
                pl.BlockSpec((1, 1, d), lambda v, vb, ve, *_: (ve[off + v], 0, 0))]
    operands = table + (xs, wgu, bgu, wd, bd, ys_so_far)
    in_specs.append(pl.BlockSpec(memory_space=pl.ANY))
    aliases = {len(operands) - 1: 0}
    return pl.pallas_call(
        functools.partial(_moe_kernel, slice_id=slice_id, table_off=off),
        grid_spec=pltpu.PrefetchScalarGridSpec(
            num_scalar_prefetch=5,
            grid=(nvis,),
            in_specs=in_specs,
            out_specs=pl.BlockSpec((MOE_BLOCK, d), lambda v, vb, ve, *_: (vb[off + v], 0)),
            scratch_shapes=[pltpu.VMEM((d, 2 * D_FF), bf16), pltpu.VMEM((D_FF, d), bf16)],
        ),
        out_shape=jax.ShapeDtypeStruct(ys_so_far.shape, ys_so_far.dtype),
        input_output_aliases=aliases,
        compiler_params=_cparams("arbitrary"),
        name="moe_experts",
    )(*operands)


def _positions_kernel(start_ref, rt_ref, pos_ref):
    idx = rt_ref[:TOP_K, :].astype(jnp.int32)
    pos = rt_ref[2 * TOP_K:3 * TOP_K, :].astype(jnp.int32)
    for e in range(N_EXPERTS):
        pos = pos + jnp.where(idx == e, start_ref[e], 0)
    pos_ref[...] = pos


def _positions(start, rt):
    m = rt.shape[1]
    return pl.pallas_call(
        _positions_kernel,
        grid_spec=pltpu.PrefetchScalarGridSpec(
            num_scalar_prefetch=1, grid=(1,),
            in_specs=[pl.BlockSpec(rt.shape, lambda i, st: (0, 0))],
            out_specs=pl.BlockSpec((TOP_K, m), lambda i, st: (0, 0))),
        out_shape=jax.ShapeDtypeStruct((TOP_K, m), jnp.int32),
        compiler_params=_cparams("arbitrary"),
        name="route_positions",
    )(start, rt)


def _visit_tables(start, end, cuts):
    blk_lo = jnp.array([0] + list(cuts[:-1]), jnp.int32)[:, None]
    blk_hi = jnp.array(list(cuts), jnp.int32)[:, None]
    room = max(h - l for l, h in zip([0] + list(cuts[:-1]), cuts)) + N_EXPERTS - 1
    first_blk = jnp.maximum((start // MOE_BLOCK)[None, :], blk_lo)
    last_blk = jnp.minimum(jnp.where(end > start, (end - 1) // MOE_BLOCK, -1)[None, :], blk_hi - 1)
    per_e = jnp.maximum(last_blk - first_blk + 1, 0)
    v_end = jnp.cumsum(per_e, axis=1)
    v_start = v_end - per_e
    n_live = v_end[:, -1]
    vc = jnp.minimum(jnp.arange(room, dtype=jnp.int32)[None, :], n_live[:, None] - 1)
    ve = jnp.sum((v_end[:, None, :] <= vc[:, :, None]).astype(jnp.int32), axis=2)
    of_ve = lambda t: jnp.sum(jnp.where(ve[:, :, None] == jnp.arange(N_EXPERTS), t[:, None, :], 0), axis=2)
    vb = of_ve(first_blk) + vc - of_ve(v_start)
    flat = lambda t: t.reshape(-1).astype(jnp.int32)
    return (flat(vb), flat(ve), flat(of_ve(jnp.broadcast_to(start, per_e.shape))),
            flat(of_ve(jnp.broadcast_to(end, per_e.shape))), n_live.astype(jnp.int32))


def _rope_tables(positions):
    half = ROPE_DIM // 2
    inv_freq = jnp.exp(-math.log(ROPE_THETA) * jnp.arange(0, ROPE_DIM, 2, dtype=f32) / ROPE_DIM)
    ang = positions.astype(f32)[:, None, :] * inv_freq[None, :, None]
    cos_t, sin_t = jnp.cos(ang), jnp.sin(ang)
    d = jnp.arange(LANES) % HEAD_DIM
    f = jnp.arange(LANES)[:, None]
    first, second = d[None, :] == f % half, d[None, :] == f % half + half
    to_cos = jnp.where((f < half) & (first | second), 1.0, 0.0)
    to_sin = jnp.where((f >= half) & (f < 2 * half), jnp.where(second, 1.0, 0.0) - jnp.where(first, 1.0, 0.0), 0.0)
    spread = jnp.concatenate([to_cos, to_sin], axis=1).astype(bf16)
    idx = jnp.arange(half)
    perm = jnp.zeros((HEAD_DIM, HEAD_DIM), f32).at[idx + half, idx].set(1.0).at[idx, idx + half].set(1.0)
    return cos_t, sin_t, spread, perm


def _layer(x, positions, norm1_g, w_in, q_norm_g, k_norm_g, cmp_pos_k, cmp_w1_k, cmp_w2_k,
           cmp_pos_v, cmp_w1_v, cmp_w2_v, gla_w_gate, gla_b_gate, gla_norm_g, w_out, norm2_g,
           w_router, b_router, w_gate_up, b_gate_up, w_down, b_down):
    b, s, d = x.shape
    m = b * s
    hk, g, dh = NSA_KV_HEADS, NSA_GROUP, HEAD_DIM
    x2 = x.reshape(m, d)

    offs = [0]
    for sz in IN_SIZES:
        offs.append(offs[-1] + sz)
    seg = lambda i: w_in[:, offs[i]:offs[i + 1]]
    w_main = jnp.concatenate([seg(MAIN_IN_SEG[i]) for i in MAIN_LAYOUT], axis=1).astype(bf16)
    padw = lambda t: jnp.concatenate([t, jnp.zeros((d, LANES - t.shape[1]), f32)], axis=1)
    ng = g * 3
    w_small = jnp.concatenate([padw(seg(7)[:, :ng]), padw(seg(7)[:, ng:]), padw(seg(11))], axis=1).astype(bf16)
    proj, proj_s = _inproj(x2, norm1_g.reshape(1, d), w_main, w_small)
    proj = proj.reshape(b, s, -1)
    proj_s = proj_s.reshape(b, s, 3 * LANES)

    cos_t, sin_t, spread, perm = _rope_tables(positions)
    per_kv = lambda t: jnp.concatenate([t] * hk, axis=-1)

    def bdiag(t):
        r, c = t.shape[-2:]
        lead = [(0, 0)] * (t.ndim - 2)
        return jnp.concatenate([jnp.pad(t, lead + [(0, 0), (i * c, (hk - 1 - i) * c)]) for i in range(hk)], axis=-2)

    w1_bd = lambda w1: bdiag(w1.reshape(CMP_BLOCK, dh, CMP_HIDDEN)).astype(bf16)
    kc, vct, ks_rot, vst, kw_rot, vwt = _kvprep(
        proj, jnp.concatenate([cos_t, sin_t], axis=1), spread, per_kv(k_norm_g),
        bdiag(jnp.ones((dh, dh), f32)).astype(bf16), bdiag(perm).astype(bf16),
        jnp.stack([jnp.eye(hk * dh, LANES, k=-h * dh, dtype=f32) * (jnp.arange(LANES) < dh)
                   for h in range(hk)]).astype(bf16),
        per_kv(cmp_pos_k), w1_bd(cmp_w1_k), bdiag(cmp_w2_k).astype(bf16),
        per_kv(cmp_pos_v), w1_bd(cmp_w1_v), bdiag(cmp_w2_v).astype(bf16))
    score_bound = (1.05 * dh * dh ** -0.5 * math.log2(math.e) * jnp.max(jnp.abs(q_norm_g)) *
                   jnp.max(jnp.abs(k_norm_g), axis=1))
    bounds = jnp.zeros((8, LANES), f32).at[0, :3].set(score_bound)
    nsa_args = (proj, cos_t, sin_t, q_norm_g.reshape(dh, 1), bounds, kc, vct, ks_rot, vst, kw_rot, vwt, proj_s)
    nsa = functools.partial(_nsa_attention, zero_rows=m * TOP_K, zero_cols=d)
    nsa_out, ys_zero = lax.cond(2.0 * jnp.max(score_bound) < NSA_SAFE_EXPONENT,
                                functools.partial(nsa, bounded=True), functools.partial(nsa, bounded=False), *nsa_args)
    nsa_out = nsa_out.reshape(m, NSA_WIDTH)

    gla_out = _gla(proj, proj_s, gla_w_gate, gla_b_gate.reshape(1, -1),
                   gla_norm_g.reshape(1, GLA_VAL_DIM)).reshape(m, GLA_WIDTH)

    padr = lambda t: jnp.concatenate([t, jnp.zeros(t.shape[:-1] + (LANES - t.shape[-1],), t.dtype)], axis=-1)
    wr_hi = w_router.astype(bf16)
    wr_lo = (w_router - wr_hi.astype(f32)).astype(bf16)
    x1, h2, rt, cnt = _outproj_router(nsa_out, gla_out, x2, w_out.astype(bf16), norm2_g.reshape(1, d),
                                      padr(wr_hi), padr(wr_lo), padr(b_router.reshape(1, -1)))

    a = m * TOP_K
    gate = rt[TOP_K:2 * TOP_K]
    counts = cnt[:, 0].astype(jnp.int32)
    end = jnp.cumsum(counts)
    start = end - counts
    pos = _positions(start, rt).reshape(a)
    tok = jnp.arange(a, dtype=jnp.int32) % m
    sorted_tok = jnp.zeros((a,), jnp.int32).at[pos].add(tok, unique_indices=True, mode='promise_in_bounds')
    rows_of = lambda t, idx: t.at[idx].get(mode='promise_in_bounds')
    nblk = a // MOE_BLOCK
    cuts = [nblk * c // MOE_SLICE_CUTS[-1] for c in MOE_SLICE_CUTS] if nblk % MOE_SLICE_CUTS[-1] == 0 else [nblk]
    table = _visit_tables(start, end, cuts)
    ys, lo_blk = ys_zero, 0
    for i, hi_blk in enumerate(cuts):
        xs = rows_of(h2, sorted_tok[lo_blk * MOE_BLOCK:hi_blk * MOE_BLOCK])
        ys = _moe_experts(table, i, lo_blk, xs, w_gate_up, b_gate_up.reshape(N_EXPERTS, 1, -1),
                          w_down, b_down.reshape(N_EXPERTS, 1, -1), ys)
        lo_blk = hi_blk
    out = x1
    for k in range(TOP_K):
        out = out + rows_of(ys, pos[k * m:(k + 1) * m]).astype(f32) * gate[k][:, None]
    return out.reshape(b, s, d)


def kernel(x, positions, norm1_g, w_in, nsa_q_norm_g, nsa_k_norm_g, cmp_pos_k, cmp_w1_k, cmp_w2_k,
           cmp_pos_v, cmp_w1_v, cmp_w2_v, gla_w_gate, gla_b_gate, gla_norm_g, w_out, norm2_g,
           w_router, b_router, w_gate_up, b_gate_up, w_down, b_down):
    for l in range(norm1_g.shape[0]):
        x = _layer(x, positions, norm1_g[l], w_in[l], nsa_q_norm_g[l], nsa_k_norm_g[l],
                   cmp_pos_k[l], cmp_w1_k[l], cmp_w2_k[l], cmp_pos_v[l], cmp_w1_v[l], cmp_w2_v[l],
                   gla_w_gate[l], gla_b_gate[l], gla_norm_g[l], w_out[l], norm2_g[l],
                   w_router[l], b_router[l], w_gate_up[l], b_gate_up[l], w_down[l], b_down[l])
    return x
```

```python
import functools
import math

import jax
import jax.numpy as jnp
from jax import lax
from jax.experimental import pallas as pl
from jax.experimental.pallas import tpu as pltpu

f32 = jnp.float32
bf16 = jnp.bfloat16

NSA_HEADS = 8
NSA_KV_HEADS = 2
NSA_GROUP = NSA_HEADS // NSA_KV_HEADS
HEAD_DIM = 64
CMP_BLOCK = 32
CMP_STRIDE = 16
CMP_HIDDEN = 256
SEL_BLOCK = 64
SEL_TOPK = 8
WINDOW = 512
FORCE_BONUS = 1e4
GLA_HEADS = 4
GLA_KEY_DIM = 64
GLA_VAL_DIM = 128
GLA_CHUNK = 64
GLA_GATE_RANK = 16
GLA_TAU = 16.0
ROPE_THETA = 500000.0
ROPE_DIM = HEAD_DIM // 4
N_EXPERTS = 32
TOP_K = 4
D_FF = 1024
SWIGLU_LIMIT = 7.0
SWIGLU_ALPHA = 1.702
EPS = 1e-6
NEG_INF = -1e30

NSA_WIDTH = NSA_HEADS * HEAD_DIM
NSA_KV_WIDTH = NSA_KV_HEADS * HEAD_DIM
GLA_KEY_WIDTH = GLA_HEADS * GLA_KEY_DIM
GLA_WIDTH = GLA_HEADS * GLA_VAL_DIM
IN_SIZES = (NSA_WIDTH,) + (NSA_KV_WIDTH,) * 6 + (
    NSA_HEADS * 3, GLA_KEY_WIDTH, GLA_KEY_WIDTH, GLA_WIDTH, GLA_GATE_RANK, GLA_WIDTH)
MAIN_SIZES = (NSA_WIDTH,) + (NSA_KV_WIDTH,) * 6 + (GLA_KEY_WIDTH, GLA_KEY_WIDTH, GLA_WIDTH, GLA_WIDTH)
MAIN_IN_SEG = (0, 1, 2, 3, 4, 5, 6, 8, 9, 10, 12)
MAIN_LAYOUT = (0, 9, 10, 1, 2, 3, 4, 5, 6, 7, 8)
MAIN_OFFS = tuple(sum(MAIN_SIZES[j] for j in MAIN_LAYOUT[:MAIN_LAYOUT.index(i)]) for i in range(len(MAIN_SIZES)))

LANES = 128
SUBLANES = 8
BF16_ROWS = 16
VMEM_LIMIT = 48 * 1024 * 1024
INPROJ_COLS = 512
MOE_BLOCK = 1024
MOE_SUB = 256
MOE_SLICE_CUTS = (1, 5, 16)
NSA_SAFE_EXPONENT = 120.0
GLA_UNROLL = 4
ROUTE_ROWS = 16

_NT = (((1,), (1,)), ((), ()))
_TN = (((0,), (0,)), ((), ()))


def _cparams(*sem):
    return pltpu.CompilerParams(dimension_semantics=sem, vmem_limit_bytes=VMEM_LIMIT)


def _rms(t, g):
    return t * lax.rsqrt(jnp.mean(t * t, axis=-1, keepdims=True) + EPS) * g


def _split_bf16(t):
    hi = t.astype(bf16)
    lo = (t - hi.astype(f32)).astype(bf16)
    return hi, lo


def _rope(t, cos_f, sin_f, perm):
    hi, lo = _split_bf16(t)
    rot = (jnp.dot(hi, perm, preferred_element_type=f32) +
           jnp.dot(lo, perm, preferred_element_type=f32))
    return t * cos_f + rot * sin_f


def _inproj_kernel(x_ref, g_ref, w_ref, ws_ref, o_ref, os_ref):
    x = x_ref[...]
    h = _rms(x, g_ref[...]).astype(bf16)
    n = o_ref.shape[1]
    for c in range(0, n, INPROJ_COLS):
        e = min(c + INPROJ_COLS, n)
        o_ref[:, c:e] = jnp.dot(h, w_ref[:, c:e], preferred_element_type=f32).astype(bf16)
    os_ref[...] = jnp.dot(h, ws_ref[...], preferred_element_type=f32)


def _inproj(x2, g, w_main, w_small, tm=512):
    m, d = x2.shape
    n = w_main.shape[1]
    ns = w_small.shape[1]
    return pl.pallas_call(
        _inproj_kernel,
        grid=(m // tm,),
        in_specs=[pl.BlockSpec((tm, d), lambda i: (i, 0)),
                  pl.BlockSpec((1, d), lambda i: (0, 0)),
                  pl.BlockSpec((d, n), lambda i: (0, 0)),
                  pl.BlockSpec((d, ns), lambda i: (0, 0))],
        out_specs=[pl.BlockSpec((tm, n), lambda i: (i, 0)),
                   pl.BlockSpec((tm, ns), lambda i: (i, 0))],
        out_shape=[jax.ShapeDtypeStruct((m, n), bf16),
                   jax.ShapeDtypeStruct((m, ns), f32)],
        compiler_params=_cparams("parallel"),
        name="inproj",
    )(x2, g, w_main, w_small)


def _kvprep_kernel(kc_ref, vc_ref, ks_ref, vs_ref, kw_ref, vw_ref, cs_ref, spread_ref, kg_ref, ones_ref,
                   perm_ref, pick_ref, posk_ref, w1k_ref, w2k_ref, posv_ref, w1v_ref, w2v_ref,
                   kco_ref, vcto_ref, kso_ref, vsto_ref, kwo_ref, vwto_ref, tmp_ref):
    ones_bd = ones_ref[...]
    nh_out = kco_ref.shape[2]

    def rms_heads(t, g):
        sq_hi, sq_lo = _split_bf16(t * t)
        ss = (jnp.dot(sq_hi, ones_bd, preferred_element_type=f32) +
              jnp.dot(sq_lo, ones_bd, preferred_element_type=f32))
        return t * lax.rsqrt(ss * (1.0 / HEAD_DIM) + EPS) * g

    def compress(src_ref, pos_ref, w1_ref, w2_ref):
        tmp_ref[...] = src_ref[0].astype(f32)
        nh = tmp_ref.shape[0] // CMP_STRIDE
        a = jnp.zeros((nh, NSA_KV_HEADS * CMP_HIDDEN), f32)
        b = jnp.zeros((nh, NSA_KV_HEADS * CMP_HIDDEN), f32)
        for p in range(CMP_STRIDE):
            rows = tmp_ref[pl.ds(p, nh, stride=CMP_STRIDE), :]
            a = a + jnp.dot((rows + pos_ref[p:p + 1, :]).astype(bf16), w1_ref[p], preferred_element_type=f32)
            q = CMP_STRIDE + p
            b = b + jnp.dot((rows + pos_ref[q:q + 1, :]).astype(bf16), w1_ref[q], preferred_element_type=f32)
        pre = a + pltpu.roll(b, nh - 1, 0)
        hid = pre * jax.nn.sigmoid(pre)
        return jnp.dot(hid.astype(bf16), w2_ref[...], preferred_element_type=f32)

    kc = rms_heads(compress(kc_ref, posk_ref, w1k_ref, w2k_ref), kg_ref[0:1, :]).astype(bf16)
    tmp_ref[:nh_out, :] = compress(vc_ref, posv_ref, w1v_ref, w2v_ref)
    vcto_ref[0] = tmp_ref[:nh_out, :].T.astype(bf16)
    s, w = ks_ref.shape[1], ks_ref.shape[2]
    cs = jnp.concatenate([cs_ref[0], jnp.zeros((LANES - cs_ref.shape[1], s), f32)], axis=0).T
    cs_hi = cs.astype(bf16)
    cs_mid, cs_lo = _split_bf16(cs - cs_hi.astype(f32))
    spread = spread_ref[...]
    tables = (jnp.dot(cs_hi, spread, preferred_element_type=f32) + jnp.dot(cs_mid, spread, preferred_element_type=f32) +
              jnp.dot(cs_lo, spread, preferred_element_type=f32))
    lane_row = lax.broadcasted_iota(jnp.int32, (1, LANES), 1)
    cos_f = tables[:, :LANES] + jnp.where(lane_row % HEAD_DIM >= ROPE_DIM, 1.0, 0.0)
    sin_f = tables[:, LANES:]
    perm = perm_ref[...]
    ks = _rope(rms_heads(ks_ref[0].astype(f32), kg_ref[1:2, :]), cos_f, sin_f, perm).astype(bf16)
    kw = _rope(rms_heads(kw_ref[0].astype(f32), kg_ref[2:3, :]), cos_f, sin_f, perm).astype(bf16)
    row = lax.broadcasted_iota(jnp.int32, (s, LANES), 0)
    lane = lax.broadcasted_iota(jnp.int32, (s, LANES), 1)
    last_lane_one = lambda n: jnp.where(lax.broadcasted_iota(jnp.int32, (n, LANES), 1) == LANES - 1, 1.0, 0.0)
    ones_lane = last_lane_one(s)
    block_onehot = jnp.where(row // SEL_BLOCK + HEAD_DIM == lane, 1.0, 0.0) + ones_lane
    pad_lane = lax.broadcasted_iota(jnp.int32, (WINDOW, LANES), 1)
    pad_flag = jnp.where((pad_lane == HEAD_DIM) | (pad_lane == LANES - 1), 1.0, 0.0)
    for h in range(NSA_KV_HEADS):
        pick = pick_ref[h]
        kco_ref[0, h] = (jnp.dot(kc, pick, preferred_element_type=f32) + last_lane_one(nh_out)).astype(bf16)
        kso_ref[0, h] = (jnp.dot(ks, pick, preferred_element_type=f32) + block_onehot).astype(bf16)
        kwo_ref[0, h, :WINDOW, :] = pad_flag.astype(bf16)
        kwo_ref[0, h, WINDOW:, :] = (jnp.dot(kw, pick, preferred_element_type=f32) + ones_lane).astype(bf16)
    vsto_ref[0] = vs_ref[0].astype(f32).T.astype(bf16)
    vwto_ref[0] = jnp.concatenate([jnp.zeros((w, WINDOW), f32), vw_ref[0].astype(f32).T], axis=1).astype(bf16)


def _kvprep(proj, cs_t, spread, kg2, ones_bd, perm2, pick, posk, w1k, w2k, posv, w1v, w2v):
    b, s, _ = proj.shape
    w = NSA_KV_WIDTH
    assert w == LANES
    nh = s // CMP_STRIDE
    col = lambda i: pl.BlockSpec((1, s, w), lambda n: (n, 0, MAIN_OFFS[i] // w))
    whole = lambda a: pl.BlockSpec(a.shape, lambda n: (0,) * a.ndim)
    tab = pl.BlockSpec((1, cs_t.shape[1], s), lambda n: (n, 0, 0))
    out = lambda *shp: (pl.BlockSpec((1,) + shp, lambda n: (n,) + (0,) * len(shp)),
                        jax.ShapeDtypeStruct((b,) + shp, bf16))
    hk = NSA_KV_HEADS
    outs = [out(hk, nh, LANES), out(w, nh), out(hk, s, LANES), out(w, s), out(hk, WINDOW + s, LANES),
            out(w, WINDOW + s)]
    return pl.pallas_call(
        _kvprep_kernel,
        grid=(b,),
        in_specs=[col(1), col(2), col(3), col(4), col(5), col(6), tab, whole(spread), whole(kg2), whole(ones_bd),
                  whole(perm2), whole(pick), whole(posk), whole(w1k), whole(w2k), whole(posv), whole(w1v),
                  whole(w2v)],
        out_specs=[o[0] for o in outs],
        out_shape=[o[1] for o in outs],
        scratch_shapes=[pltpu.VMEM((s, w), f32)],
        compiler_params=_cparams("parallel"),
        name="nsa_kvprep",
    )(proj, proj, proj, proj, proj, proj, cs_t, spread, kg2, ones_bd, perm2, pick, posk, w1k, w2k, posv, w1v, w2v)


def _nsa_kernel(q_ref, cos_ref, sin_ref, qg_ref, mb_ref, kc_ref, vct_ref, ks_ref, vst_ref, kw_ref, vwt_ref,
                gate_ref, o_ref, zero_ref, *, tq, ck, bounded):
    g = NSA_GROUP
    qi = pl.program_id(2)
    t0 = pl.multiple_of(qi * tq, tq)
    scale = HEAD_DIM ** -0.5 * math.log2(math.e)
    per_head = lambda t: jnp.concatenate([t] * g, axis=1)

    qt = q_ref[0].astype(f32).T
    qt = jnp.concatenate([qt[i * HEAD_DIM:(i + 1) * HEAD_DIM] for i in range(g)], axis=1)
    qn = qt * lax.rsqrt(jnp.mean(qt * qt, axis=0, keepdims=True) + EPS) * qg_ref[...]
    half = ROPE_DIM // 2
    cos8, sin8 = per_head(cos_ref[0]), per_head(sin_ref[0])
    x1, x2 = qn[:half], qn[half:ROPE_DIM]
    q_rot = jnp.concatenate([x1 * cos8 - x2 * sin8, x2 * cos8 + x1 * sin8, qn[ROPE_DIM:]], axis=0)
    n_extra = kc_ref.shape[3] - HEAD_DIM

    def with_features(qb, feats, branch):
        used = sum(f.shape[0] for f in feats)
        last = jnp.broadcast_to(-mb_ref[0:1, branch:branch + 1] if bounded else 0.0, (1, g * tq))
        fill = jnp.zeros((n_extra - used - 1, g * tq), f32)
        return jnp.concatenate([qb] + feats + [fill, last], axis=0).astype(bf16)

    q_cmp = with_features(qn * scale, [], 0)
    q_rot = q_rot * scale
    tq_row = t0 + lax.broadcasted_iota(jnp.int32, (1, tq), 1)

    kc = kc_ref[0, 0]
    ncp = kc.shape[0]
    n_col = lax.broadcasted_iota(jnp.int32, (ncp, 1), 0)
    ok_c = (n_col * CMP_STRIDE + (CMP_BLOCK - 1)) <= tq_row
    s_c = jnp.dot(kc, q_cmp, preferred_element_type=f32) + per_head(jnp.where(ok_c, 0.0, NEG_INF))
    if bounded:
        p_c = jnp.exp2(s_c)
    else:
        m_c = jnp.max(s_c, axis=0, keepdims=True)
        p_c = jnp.exp2(s_c - m_c) * per_head(jnp.where(ok_c, 1.0, 0.0))
    den = jnp.sum(p_c, axis=0, keepdims=True)
    p_c = p_c * (1.0 / jnp.where(den > 0, den, 1.0))
    o_cmp = jnp.dot(vct_ref[0], p_c.astype(bf16), preferred_element_type=f32)

    pg = p_c[:, :tq]
    for i in range(1, g):
        pg = pg + p_c[:, i * tq:(i + 1) * tq]
    nj = ks_ref.shape[2] // SEL_BLOCK
    jj = lax.broadcasted_iota(jnp.int32, (nj, ncp), 0)
    nn = lax.broadcasted_iota(jnp.int32, (nj, ncp), 1)
    overlap = (nn * CMP_STRIDE < (jj + 1) * SEL_BLOCK) & (nn * CMP_STRIDE + CMP_BLOCK > jj * SEL_BLOCK)
    overlap = jnp.where(overlap, 1.0, 0.0).astype(bf16)
    pg_hi, pg_lo = _split_bf16(pg)
    imp = (jnp.dot(overlap, pg_hi, preferred_element_type=f32) +
           jnp.dot(overlap, pg_lo, preferred_element_type=f32))
    j_col = lax.broadcasted_iota(jnp.int32, (nj, 1), 0)
    j_f = j_col.astype(f32)
    cur = tq_row // SEL_BLOCK
    valid = j_col <= cur
    forced = (j_col == 0) | (j_col == cur) | (j_col == cur - 1)
    score = jnp.where(valid, imp + jnp.where(forced, FORCE_BONUS, 0.0), NEG_INF)
    sel = jnp.zeros((nj, tq), f32)
    for _ in range(SEL_TOPK):
        m = jnp.max(score, axis=0, keepdims=True)
        first = jnp.min(jnp.where(score == m, j_f, float(nj)), axis=0, keepdims=True)
        hit = j_f == first
        sel = jnp.where(hit, 1.0, sel)
        score = jnp.where(hit, -jnp.inf, score)
    sel = jnp.where(valid, sel, 0.0)

    def weighted_values(vt, p):
        lhs = jnp.concatenate([vt, jnp.ones((BF16_ROWS, vt.shape[1]), bf16)], axis=0)
        r = jnp.dot(lhs, p.astype(bf16), preferred_element_type=f32)
        return r[:HEAD_DIM], r[HEAD_DIM:HEAD_DIM + 1]

    wl = WINDOW + tq
    q_win = with_features(q_rot, [jnp.full((1, g * tq), NEG_INF, f32)], 2)
    s_w = jnp.dot(kw_ref[0, 0, pl.ds(t0, wl), :], q_win, preferred_element_type=f32)
    step = lax.broadcasted_iota(jnp.int32, (tq, 1), 0)
    lo_ok = (t0 - WINDOW + step) > (tq_row - WINDOW)
    hi_ok = (t0 + step) <= tq_row
    parts = [s_w[:tq] + per_head(jnp.where(lo_ok, 0.0, NEG_INF))]
    if wl > 2 * tq:
        parts.append(s_w[tq:wl - tq])
    parts.append(s_w[wl - tq:] + per_head(jnp.where(hi_ok, 0.0, NEG_INF)))
    if bounded:
        p_w = jnp.concatenate([jnp.exp2(t) for t in parts], axis=0)
    else:
        m_w = functools.reduce(jnp.maximum, [jnp.max(t, axis=0, keepdims=True) for t in parts])
        p_w = jnp.concatenate([jnp.exp2(t - m_w) for t in parts], axis=0)
    o_win, l_w = weighted_values(vwt_ref[0, :, pl.ds(t0, wl)], p_w)
    o_win = o_win * (1.0 / l_w)

    q_sel = with_features(q_rot, [per_head(jnp.where(sel > 0.5, 0.0, NEG_INF))], 1)

    def sel_step(carry, k0, diagonal):
        m_prev, l_prev, acc = carry
        s = jnp.dot(ks_ref[0, 0, pl.ds(k0, ck), :], q_sel, preferred_element_type=f32)
        if diagonal:
            kpos = k0 + lax.broadcasted_iota(jnp.int32, (ck, 1), 0)
            s = s + per_head(jnp.where(kpos <= tq_row, 0.0, NEG_INF))
        if bounded:
            pv, p_sum = weighted_values(vst_ref[0, :, pl.ds(k0, ck)], jnp.exp2(s))
            return m_prev, l_prev + p_sum, acc + pv
        m_new = jnp.maximum(m_prev, jnp.max(s, axis=0, keepdims=True))
        alpha = jnp.exp2(m_prev - m_new)
        pv, p_sum = weighted_values(vst_ref[0, :, pl.ds(k0, ck)], jnp.exp2(s - m_new))
        return m_new, alpha * l_prev + p_sum, alpha * acc + pv

    init = (jnp.full((1, g * tq), NEG_INF, f32), jnp.zeros((1, g * tq), f32),
            jnp.zeros((HEAD_DIM, g * tq), f32))
    last = (t0 + tq + ck - 1) // ck - 1
    carry = lax.fori_loop(0, last, lambda c, cr: sel_step(cr, pl.multiple_of(c * ck, ck), False), init)
    _, l_s, acc_s = sel_step(carry, pl.multiple_of(last * ck, ck), True)
    o_sel = acc_s * (1.0 / l_s)

    gt = gate_ref[0].T
    gate = lambda j: jax.nn.sigmoid(jnp.concatenate([gt[i * 3 + j:i * 3 + j + 1] for i in range(g)], axis=1))
    ot = gate(0) * o_cmp + gate(1) * o_sel + gate(2) * o_win
    o2 = jnp.concatenate([ot[:, i * tq:(i + 1) * tq] for i in range(g)], axis=0)
    o_ref[0] = o2.T.astype(o_ref.dtype)
    zero_ref[...] = jnp.zeros_like(zero_ref)


def _nsa_attention(proj, cos_t, sin_t, qg, bounds, kc, vct, ks, vst, kw, vwt, gate_logits, *, bounded,
                   zero_rows, zero_cols, tq=512, ck=512):
    b, s, _ = proj.shape
    hk, g, dh = NSA_KV_HEADS, NSA_GROUP, HEAD_DIM
    nt = s // tq
    steps = b * hk * nt
    assert WINDOW >= tq and ck % tq == 0 and s % ck == 0 and zero_rows % (BF16_ROWS * steps) == 0
    tab = pl.BlockSpec((1, ROPE_DIM // 2, tq), lambda i, j, t: (i, 0, t))
    assert s // SEL_BLOCK <= kc.shape[3] - dh
    keys = lambda a: pl.BlockSpec((1, 1) + a.shape[2:], lambda i, j, t: (i, j, 0, 0))
    vals = lambda a: pl.BlockSpec((1, dh, a.shape[2]), lambda i, j, t: (i, j, 0))
    return pl.pallas_call(
        functools.partial(_nsa_kernel, tq=tq, ck=ck, bounded=bounded),
        grid=(b, hk, s // tq),
        in_specs=[pl.BlockSpec((1, tq, g * dh), lambda i, j, t: (i, t, j)),
                  tab, tab, pl.BlockSpec((dh, 1), lambda i, j, t: (0, 0)),
                  pl.BlockSpec(bounds.shape, lambda i, j, t: (0, 0)),
                  keys(kc), vals(vct), keys(ks), vals(vst), keys(kw), vals(vwt),
                  pl.BlockSpec((1, tq, LANES), lambda i, j, t: (i, t, j))],
        out_specs=[pl.BlockSpec((1, tq, g * dh), lambda i, j, t: (i, t, j)),
                   pl.BlockSpec((zero_rows // steps, zero_cols), lambda i, j, t: ((i * hk + j) * nt + t, 0))],
        out_shape=[jax.ShapeDtypeStruct((b, s, NSA_WIDTH), bf16),
                   jax.ShapeDtypeStruct((zero_rows, zero_cols), bf16)],
        compiler_params=_cparams("parallel", "parallel", "arbitrary"),
        name="nsa_attention_bounded" if bounded else "nsa_attention",
    )(proj, cos_t, sin_t, qg, bounds, kc, vct, ks, vst, kw, vwt, gate_logits)


def _gla_kernel(q_ref, k_ref, v_ref, lr_ref, wg_ref, bg_ref, r_ref, ng_ref, o_ref, *, grp, npair):
    c = GLA_CHUNK
    s = q_ref.shape[1]
    dk, dv = GLA_KEY_DIM, GLA_VAL_DIM
    rows = grp * c
    ri = lax.broadcasted_iota(jnp.int32, (rows, rows), 0)
    ci = lax.broadcasted_iota(jnp.int32, (rows, rows), 1)
    causal = (ci <= ri) & (ci // c == ri // c)
    tri = jnp.where(causal, 1.0, 0.0).astype(bf16)
    lane = lax.broadcasted_iota(jnp.int32, (1, 2 * dk), 1)
    head_mask = [lane < dk, lane >= dk]
    chunk_of_row = lax.broadcasted_iota(jnp.int32, (rows, 1), 0) // c
    wg = wg_ref[...]
    bg = bg_ref[...]
    ng = ng_ref[...]

    def body(n, states):
        r0 = pl.multiple_of(n * rows, rows)
        z = jnp.dot(lr_ref[0, pl.ds(r0, rows), :GLA_GATE_RANK], wg, preferred_element_type=f32,
                    precision=lax.Precision.HIGHEST) + bg
        log_a = -(jnp.maximum(-z, 0.0) + jnp.log(1.0 + jnp.exp(-jnp.abs(z)))) / GLA_TAU
        la_hi = log_a.astype(bf16)
        la_mid, la_lo = _split_bf16(log_a - la_hi.astype(f32))
        bcum = (jnp.dot(tri, la_hi, preferred_element_type=f32) + jnp.dot(tri, la_mid, preferred_element_type=f32) +
                jnp.dot(tri, la_lo, preferred_element_type=f32))
        qf = q_ref[0, pl.ds(r0, rows), :].astype(f32) * (dk ** -0.5)
        kf = k_ref[0, pl.ds(r0, rows), :].astype(f32)
        q_dec = qf * jnp.exp(bcum)
        k_dec = (kf * jnp.exp(-bcum)).astype(bf16)
        b_last = [bcum[(i + 1) * c - 1:(i + 1) * c, :] for i in range(grp)]
        k_state = jnp.concatenate(
            [kf[i * c:(i + 1) * c] * jnp.exp(b_last[i] - bcum[i * c:(i + 1) * c]) for i in range(grp)],
            axis=0).astype(bf16)
        by_chunk = lambda t: jnp.concatenate(
            [jnp.where(chunk_of_row == i, t, jnp.zeros_like(t)) for i in range(grp)], axis=1)
        new_states = []
        for hh in range(2 * npair):
            pr, h = divmod(hh, 2)
            pair = slice(pr * 2 * dk, (pr + 1) * 2 * dk)
            q_h = jnp.where(head_mask[h], q_dec[:, pair], 0.0).astype(bf16)
            v_h = v_ref[0, pl.ds(r0, rows), hh * dv:(hh + 1) * dv]
            attn = lax.dot_general(q_h, k_dec[:, pair], _NT, preferred_element_type=f32)
            attn = jnp.where(causal, attn, 0.0).astype(bf16)
            o_intra = jnp.dot(attn, v_h, preferred_element_type=f32)
            u_all = lax.dot_general(v_h, by_chunk(k_state[:, pair]), _TN, preferred_element_type=f32)
            st = states[hh]
            entering = []
            for i in range(grp):
                entering.append(st)
                st = st * jnp.exp(b_last[i][:, pair]) + u_all[:, i * 2 * dk:(i + 1) * 2 * dk]
            new_states.append(st)
            o_inter = lax.dot_general(by_chunk(q_h), jnp.concatenate(entering, axis=1).astype(bf16), _NT,
                                      preferred_element_type=f32)
            o = _rms(o_intra + o_inter, ng)
            rr = r_ref[0, pl.ds(r0, rows), hh * dv:(hh + 1) * dv].astype(f32)
            o_ref[0, pl.ds(r0, rows), hh * dv:(hh + 1) * dv] = (o * (rr * jax.nn.sigmoid(rr))).astype(o_ref.dtype)
        return tuple(new_states)

    zero = jnp.zeros((dv, 2 * dk), f32)
    lax.fori_loop(0, s // rows, body, (zero,) * (2 * npair), unroll=GLA_UNROLL)


def _gla(proj, proj_s, wg, bg, ng, grp=4, npair=GLA_HEADS // 2):
    b, s, _ = proj.shape
    dk2, dv2 = 2 * npair * GLA_KEY_DIM, 2 * npair * GLA_VAL_DIM
    oq, ok_, ov, orr = (MAIN_OFFS[7] // dk2, MAIN_OFFS[8] // dk2, MAIN_OFFS[9] // dv2, MAIN_OFFS[10] // dv2)
    assert all(MAIN_OFFS[i] % w == 0 for i, w in ((7, dk2), (8, dk2), (9, dv2), (10, dv2)))
    col = lambda w, o: pl.BlockSpec((1, s, w), lambda i, j: (i, 0, o + j))
    return pl.pallas_call(
        functools.partial(_gla_kernel, grp=grp, npair=npair),
        grid=(b, GLA_HEADS // (2 * npair)),
        in_specs=[col(dk2, oq), col(dk2, ok_), col(dv2, ov),
                  pl.BlockSpec((1, s, LANES), lambda i, j: (i, 0, 2)),
                  pl.BlockSpec((GLA_GATE_RANK, dk2), lambda i, j: (0, j)),
                  pl.BlockSpec((1, dk2), lambda i, j: (0, j)),
                  col(dv2, orr),
                  pl.BlockSpec((1, GLA_VAL_DIM), lambda i, j: (0, 0))],
        out_specs=pl.BlockSpec((1, s, dv2), lambda i, j: (i, 0, j)),
        out_shape=jax.ShapeDtypeStruct((b, s, GLA_WIDTH), bf16),
        compiler_params=_cparams("parallel", "parallel"),
        name="gla",
    )(proj, proj, proj, proj_s, wg, bg, proj, ng)


def _outproj_kernel(a_ref, b_ref, x_ref, wo_ref, g2_ref, wrh_ref, wrl_ref, br_ref,
                    x1_ref, h2_ref, rt_ref, cnt_ref, carry_ref):
    @pl.when(pl.program_id(0) == 0)
    def _():
        carry_ref[...] = jnp.zeros_like(carry_ref)

    na = a_ref.shape[1]
    tm = a_ref.shape[0]
    y = (jnp.dot(a_ref[...], wo_ref[:na, :], preferred_element_type=f32) +
         jnp.dot(b_ref[...], wo_ref[na:, :], preferred_element_type=f32))
    x1 = x_ref[...] + y
    x1_ref[...] = x1
    h2 = _rms(x1, g2_ref[...])
    h2_ref[...] = h2.astype(bf16)
    hi, lo = _split_bf16(h2)
    logits = (jnp.dot(hi, wrh_ref[...], preferred_element_type=f32) +
              jnp.dot(lo, wrh_ref[...], preferred_element_type=f32) +
              jnp.dot(hi, wrl_ref[...], preferred_element_type=f32)) + br_ref[...]
    score = logits.T[:N_EXPERTS]
    e_col = lax.broadcasted_iota(jnp.int32, (N_EXPERTS, 1), 0).astype(f32)
    top = jnp.max(score, axis=0, keepdims=True)
    hits, firsts, weights = [], [], []
    for _ in range(TOP_K):
        m = jnp.max(score, axis=0, keepdims=True)
        first = jnp.min(jnp.where(score == m, e_col, float(N_EXPERTS)), axis=0, keepdims=True)
        hit = e_col == first
        hits.append(hit)
        firsts.append(first)
        weights.append(jnp.exp(m - top))
        score = jnp.where(hit, -jnp.inf, score)
    tot = weights[0] + weights[1] + weights[2] + weights[3]
    onehot = jnp.where(hits[0] | hits[1] | hits[2] | hits[3], 1.0, 0.0)
    ri = lax.broadcasted_iota(jnp.int32, (tm, tm), 0)
    ci = lax.broadcasted_iota(jnp.int32, (tm, tm), 1)
    earlier = jnp.where(ri < ci, 1.0, 0.0).astype(bf16)
    rank = carry_ref[...] + jnp.dot(onehot.astype(bf16), earlier, preferred_element_type=f32)
    carry_ref[...] = carry_ref[...] + jnp.sum(onehot, axis=1, keepdims=True)
    cnt_ref[...] = jnp.broadcast_to(carry_ref[...], cnt_ref.shape)
    ranks = [jnp.sum(jnp.where(hits[k], rank, 0.0), axis=0, keepdims=True) for k in range(TOP_K)]
    gates = [weights[k] / tot for k in range(TOP_K)]
    pad = jnp.zeros((rt_ref.shape[0] - 3 * TOP_K, tm), f32)
    rt_ref[...] = jnp.concatenate(firsts + gates + ranks + [pad], axis=0)


def _outproj_router(a, b, x2, wo, g2, wr_hi, wr_lo, br, tm=512):
    m, d = x2.shape
    na, nb = a.shape[1], b.shape[1]
    row = lambda n: pl.BlockSpec((tm, n), lambda i: (i, 0))
    whole = lambda t: pl.BlockSpec(t.shape, lambda i: (0,) * t.ndim)
    return pl.pallas_call(
        _outproj_kernel,
        grid=(m // tm,),
        in_specs=[row(na), row(nb), row(d), whole(wo), whole(g2), whole(wr_hi), whole(wr_lo), whole(br)],
        out_specs=[row(d), row(d), pl.BlockSpec((ROUTE_ROWS, tm), lambda i: (0, i)),
                   pl.BlockSpec((N_EXPERTS, LANES), lambda i: (0, 0))],
        out_shape=[jax.ShapeDtypeStruct((m, d), f32),
                   jax.ShapeDtypeStruct((m, d), bf16),
                   jax.ShapeDtypeStruct((ROUTE_ROWS, m), f32),
                   jax.ShapeDtypeStruct((N_EXPERTS, LANES), f32)],
        scratch_shapes=[pltpu.VMEM((N_EXPERTS, 1), f32)],
        compiler_params=_cparams("arbitrary"),
        name="outproj_router",
    )(a, b, x2, wo, g2, wr_hi, wr_lo, br)


def _moe_kernel(vb_ref, ve_ref, lo_ref, hi_ref, nv_ref, xs_ref, wgu_ref, bgu_ref, wd_ref, bd_ref, ys_ref,
                o_ref, wgu_bf, wd_bf, *, slice_id, table_off):
    del ys_ref
    step = pl.program_id(0)
    v = table_off + step
    prev = jnp.maximum(v - 1, table_off)
    live = step < nv_ref[slice_id]
    new_expert = (step == 0) | (ve_ref[v] != ve_ref[prev])
    new_block = (step == 0) | (vb_ref[v] != vb_ref[prev])

    @pl.when(live & new_expert)
    def _():
        wgu_bf[...] = wgu_ref[0].astype(bf16)
        wd_bf[...] = wd_ref[0].astype(bf16)

    @pl.when(live & new_block)
    def _():
        o_ref[...] = jnp.zeros_like(o_ref)

    @pl.when(live)
    def _():
        blk0 = vb_ref[v] * MOE_BLOCK
        lo, hi = lo_ref[v], hi_ref[v]
        first = jnp.maximum(lo - blk0, 0) // MOE_SUB
        last = (jnp.minimum(hi - blk0, MOE_BLOCK) + MOE_SUB - 1) // MOE_SUB

        def sub_block(j, carry):
            r0 = pl.multiple_of(j * MOE_SUB, MOE_SUB)
            h = jnp.dot(xs_ref[pl.ds(r0, MOE_SUB), :], wgu_bf[...], preferred_element_type=f32) + bgu_ref[0]
            x_glu = jnp.minimum(h[:, :D_FF], SWIGLU_LIMIT)
            x_lin = jnp.clip(h[:, D_FF:], -SWIGLU_LIMIT, SWIGLU_LIMIT)
            act = x_glu * jax.nn.sigmoid(SWIGLU_ALPHA * x_glu) * (x_lin + 1.0)
            y = jnp.dot(act.astype(bf16), wd_bf[...], preferred_element_type=f32) + bd_ref[0]
            row = blk0 + r0 + lax.broadcasted_iota(jnp.int32, (MOE_SUB, 1), 0)
            mine = (row >= lo) & (row < hi)
            o_ref[pl.ds(r0, MOE_SUB), :] = jnp.where(mine, y.astype(o_ref.dtype), o_ref[pl.ds(r0, MOE_SUB), :])
            return carry

        lax.fori_loop(first, last, sub_block, 0)


def _moe_experts(table, slice_id, blk_lo, xs, wgu, bgu, wd, bd, ys_so_far):
    d = xs.shape[1]
    nvis = xs.shape[0] // MOE_BLOCK + N_EXPERTS - 1
    off = slice_id * (table[0].shape[0] // table[4].shape[0])
    in_specs = [pl.BlockSpec((MOE_BLOCK, d), lambda v, vb, ve, *_: (vb[off + v] - blk_lo, 0)),
                pl.BlockSpec((1, d, 2 * D_FF), lambda v, vb, ve, *_: (ve[off + v], 0, 0)),
                pl.BlockSpec((1, 1, 2 * D_FF), lambda v, vb, ve, *_: (ve[off + v], 0, 0)),
                pl.BlockSpec((1, D_FF, d), lambda v, vb, ve, *_: (ve[off + v], 0, 0)),
                pl.BlockSpec((1, 1, d), lambda v, vb, ve, *_: (ve[off + v], 0, 0))]
    operands = table + (xs, wgu, bgu, wd, bd, ys_so_far)
    in_specs.append(pl.BlockSpec(memory_space=pl.ANY))
    aliases = {len(operands) - 1: 0}
    return pl.pallas_call(
        functools.partial(_moe_kernel, slice_id=slice_id, table_off=off),
        grid_spec=pltpu.PrefetchScalarGridSpec(
            num_scalar_prefetch=5,
            grid=(nvis,),
            in_specs=in_specs,
            out_specs=pl.BlockSpec((MOE_BLOCK, d), lambda v, vb, ve, *_: (vb[off + v], 0)),
            scratch_shapes=[pltpu.VMEM((d, 2 * D_FF), bf16), pltpu.VMEM((D_FF, d), bf16)],
        ),
        out_shape=jax.ShapeDtypeStruct(ys_so_far.shape, ys_so_far.dtype),
        input_output_aliases=aliases,
        compiler_params=_cparams("arbitrary"),
        name="moe_experts",
    )(*operands)


def _positions_kernel(start_ref, rt_ref, pos_ref):
    idx = rt_ref[:TOP_K, :].astype(jnp.int32)
    pos = rt_ref[2 * TOP_K:3 * TOP_K, :].astype(jnp.int32)
    for e in range(N_EXPERTS):
        pos = pos + jnp.where(idx == e, start_ref[e], 0)
    pos_ref[...] = pos


def _positions(start, rt):
    m = rt.shape[1]
    return pl.pallas_call(
        _positions_kernel,
        grid_spec=pltpu.PrefetchScalarGridSpec(
            num_scalar_prefetch=1, grid=(1,),
            in_specs=[pl.BlockSpec(rt.shape, lambda i, st: (0, 0))],
            out_specs=pl.BlockSpec((TOP_K, m), lambda i, st: (0, 0))),
        out_shape=jax.ShapeDtypeStruct((TOP_K, m), jnp.int32),
        compiler_params=_cparams("arbitrary"),
        name="route_positions",
    )(start, rt)


def _visit_tables(start, end, cuts):
    blk_lo = jnp.array([0] + list(cuts[:-1]), jnp.int32)[:, None]
    blk_hi = jnp.array(list(cuts), jnp.int32)[:, None]
    room = max(h - l for l, h in zip([0] + list(cuts[:-1]), cuts)) + N_EXPERTS - 1
    first_blk = jnp.maximum((start // MOE_BLOCK)[None, :], blk_lo)
    last_blk = jnp.minimum(jnp.where(end > start, (end - 1) // MOE_BLOCK, -1)[None, :], blk_hi - 1)
    per_e = jnp.maximum(last_blk - first_blk + 1, 0)
    v_end = jnp.cumsum(per_e, axis=1)
    v_start = v_end - per_e
    n_live = v_end[:, -1]
    vc = jnp.minimum(jnp.arange(room, dtype=jnp.int32)[None, :], n_live[:, None] - 1)
    ve = jnp.sum((v_end[:, None, :] <= vc[:, :, None]).astype(jnp.int32), axis=2)
    of_ve = lambda t: jnp.sum(jnp.where(ve[:, :, None] == jnp.arange(N_EXPERTS), t[:, None, :], 0), axis=2)
    vb = of_ve(first_blk) + vc - of_ve(v_start)
    flat = lambda t: t.reshape(-1).astype(jnp.int32)
    return (flat(vb), flat(ve), flat(of_ve(jnp.broadcast_to(start, per_e.shape))),
            flat(of_ve(jnp.broadcast_to(end, per_e.shape))), n_live.astype(jnp.int32))


def _rope_tables(positions):
    half = ROPE_DIM // 2
    inv_freq = jnp.exp(-math.log(ROPE_THETA) * jnp.arange(0, ROPE_DIM, 2, dtype=f32) / ROPE_DIM)
    ang = positions.astype(f32)[:, None, :] * inv_freq[None, :, None]
    cos_t, sin_t = jnp.cos(ang), jnp.sin(ang)
    d = jnp.arange(LANES) % HEAD_DIM
    f = jnp.arange(LANES)[:, None]
    first, second = d[None, :] == f % half, d[None, :] == f % half + half
    to_cos = jnp.where((f < half) & (first | second), 1.0, 0.0)
    to_sin = jnp.where((f >= half) & (f < 2 * half), jnp.where(second, 1.0, 0.0) - jnp.where(first, 1.0, 0.0), 0.0)
    spread = jnp.concatenate([to_cos, to_sin], axis=1).astype(bf16)
    idx = jnp.arange(half)
    perm = jnp.zeros((HEAD_DIM, HEAD_DIM), f32).at[idx + half, idx].set(1.0).at[idx, idx + half].set(1.0)
    return cos_t, sin_t, spread, perm


def _layer(x, positions, norm1_g, w_in, q_norm_g, k_norm_g, cmp_pos_k, cmp_w1_k, cmp_w2_k,
           cmp_pos_v, cmp_w1_v, cmp_w2_v, gla_w_gate, gla_b_gate, gla_norm_g, w_out, norm2_g,
           w_router, b_router, w_gate_up, b_gate_up, w_down, b_down):
    b, s, d = x.shape
    m = b * s
    hk, g, dh = NSA_KV_HEADS, NSA_GROUP, HEAD_DIM
    x2 = x.reshape(m, d)

    offs = [0]
    for sz in IN_SIZES:
        offs.append(offs[-1] + sz)
    seg = lambda i: w_in[:, offs[i]:offs[i + 1]]
    w_main = jnp.concatenate([seg(MAIN_IN_SEG[i]) for i in MAIN_LAYOUT], axis=1).astype(bf16)
    padw = lambda t: jnp.concatenate([t, jnp.zeros((d, LANES - t.shape[1]), f32)], axis=1)
    ng = g * 3
    w_small = jnp.concatenate([padw(seg(7)[:, :ng]), padw(seg(7)[:, ng:]), padw(seg(11))], axis=1).astype(bf16)
    proj, proj_s = _inproj(x2, norm1_g.reshape(1, d), w_main, w_small)
    proj = proj.reshape(b, s, -1)
    proj_s = proj_s.reshape(b, s, 3 * LANES)

    cos_t, sin_t, spread, perm = _rope_tables(positions)
    per_kv = lambda t: jnp.concatenate([t] * hk, axis=-1)

    def bdiag(t):
        r, c = t.shape[-2:]
        lead = [(0, 0)] * (t.ndim - 2)
        return jnp.concatenate([jnp.pad(t, lead + [(0, 0), (i * c, (hk - 1 - i) * c)]) for i in range(hk)], axis=-2)

    w1_bd = lambda w1: bdiag(w1.reshape(CMP_BLOCK, dh, CMP_HIDDEN)).astype(bf16)
    kc, vct, ks_rot, vst, kw_rot, vwt = _kvprep(
        proj, jnp.concatenate([cos_t, sin_t], axis=1), spread, per_kv(k_norm_g),
        bdiag(jnp.ones((dh, dh), f32)).astype(bf16), bdiag(perm).astype(bf16),
        jnp.stack([jnp.eye(hk * dh, LANES, k=-h * dh, dtype=f32) * (jnp.arange(LANES) < dh)
                   for h in range(hk)]).astype(bf16),
        per_kv(cmp_pos_k), w1_bd(cmp_w1_k), bdiag(cmp_w2_k).astype(bf16),
        per_kv(cmp_pos_v), w1_bd(cmp_w1_v), bdiag(cmp_w2_v).astype(bf16))
    score_bound = (1.05 * dh * dh ** -0.5 * math.log2(math.e) * jnp.max(jnp.abs(q_norm_g)) *
                   jnp.max(jnp.abs(k_norm_g), axis=1))
    bounds = jnp.zeros((SUBLANES, LANES), f32).at[0, :3].set(score_bound)
    nsa_args = (proj, cos_t, sin_t, q_norm_g.reshape(dh, 1), bounds, kc, vct, ks_rot, vst, kw_rot, vwt, proj_s)
    nsa = functools.partial(_nsa_attention, zero_rows=m * TOP_K, zero_cols=d)
    nsa_out, ys_zero = lax.cond(2.0 * jnp.max(score_bound) < NSA_SAFE_EXPONENT,
                                functools.partial(nsa, bounded=True), functools.partial(nsa, bounded=False), *nsa_args)
    nsa_out = nsa_out.reshape(m, NSA_WIDTH)

    gla_out = _gla(proj, proj_s, gla_w_gate, gla_b_gate.reshape(1, -1),
                   gla_norm_g.reshape(1, GLA_VAL_DIM)).reshape(m, GLA_WIDTH)

    padr = lambda t: jnp.concatenate([t, jnp.zeros(t.shape[:-1] + (LANES - t.shape[-1],), t.dtype)], axis=-1)
    wr_hi = w_router.astype(bf16)
    wr_lo = (w_router - wr_hi.astype(f32)).astype(bf16)
    x1, h2, rt, cnt = _outproj_router(nsa_out, gla_out, x2, w_out.astype(bf16), norm2_g.reshape(1, d),
                                      padr(wr_hi), padr(wr_lo), padr(b_router.reshape(1, -1)))

    a = m * TOP_K
    gate = rt[TOP_K:2 * TOP_K]
    counts = cnt[:, 0].astype(jnp.int32)
    end = jnp.cumsum(counts)
    start = end - counts
    pos = _positions(start, rt).reshape(a)
    tok = jnp.arange(a, dtype=jnp.int32) % m
    sorted_tok = jnp.zeros((a,), jnp.int32).at[pos].add(tok, unique_indices=True, mode='promise_in_bounds')
    rows_of = lambda t, idx: t.at[idx].get(mode='promise_in_bounds')
    nblk = a // MOE_BLOCK
    cuts = [nblk * c // MOE_SLICE_CUTS[-1] for c in MOE_SLICE_CUTS] if nblk % MOE_SLICE_CUTS[-1] == 0 else [nblk]
    table = _visit_tables(start, end, cuts)
    ys, lo_blk = ys_zero, 0
    for i, hi_blk in enumerate(cuts):
        xs = rows_of(h2, sorted_tok[lo_blk * MOE_BLOCK:hi_blk * MOE_BLOCK])
        ys = _moe_experts(table, i, lo_blk, xs, w_gate_up, b_gate_up.reshape(N_EXPERTS, 1, -1),
                          w_down, b_down.reshape(N_EXPERTS, 1, -1), ys)
        lo_blk = hi_blk
    out = x1
    for k in range(TOP_K):
        out = out + rows_of(ys, pos[k * m:(k + 1) * m]).astype(f32) * gate[k][:, None]
    return out.reshape(b, s, d)


def kernel(x, positions, norm1_g, w_in, nsa_q_norm_g, nsa_k_norm_g, cmp_pos_k, cmp_w1_k, cmp_w2_k,
           cmp_pos_v, cmp_w1_v, cmp_w2_v, gla_w_gate, gla_b_gate, gla_norm_g, w_out, norm2_g,
           w_router, b_router, w_gate_up, b_gate_up, w_down, b_down):
    for l in range(norm1_g.shape[0]):
        x = _layer(x, positions, norm1_g[l], w_in[l], nsa_q_norm_g[l], nsa_k_norm_g[l],
                   cmp_pos_k[l], cmp_w1_k[l], cmp_w2_k[l], cmp_pos_v[l], cmp_w1_v[l], cmp_w2_v[l],
                   gla_w_gate[l], gla_b_gate[l], gla_norm_g[l], w_out[l], norm2_g[l],
                   w_router[l], b_router[l], w_gate_up[l], b_gate_up[l], w_down[l], b_down[l])
    return x
```

```python
import functools
import math

import jax
import jax.numpy as jnp
from jax import lax
from jax.experimental import pallas as pl
from jax.experimental.pallas import tpu as pltpu

f32 = jnp.float32
bf16 = jnp.bfloat16

NSA_HEADS = 8
NSA_KV_HEADS = 2
NSA_GROUP = NSA_HEADS // NSA_KV_HEADS
HEAD_DIM = 64
CMP_BLOCK = 32
CMP_STRIDE = 16
CMP_HIDDEN = 256
SEL_BLOCK = 64
SEL_TOPK = 8
WINDOW = 512
FORCE_BONUS = 1e4
GLA_HEADS = 4
GLA_KEY_DIM = 64
GLA_VAL_DIM = 128
GLA_CHUNK = 64
GLA_GATE_RANK = 16
GLA_TAU = 16.0
ROPE_THETA = 500000.0
ROPE_DIM = HEAD_DIM // 4
N_EXPERTS = 32
TOP_K = 4
D_FF = 1024
SWIGLU_LIMIT = 7.0
SWIGLU_ALPHA = 1.702
EPS = 1e-6
NEG_INF = -1e30

NSA_WIDTH = NSA_HEADS * HEAD_DIM
NSA_KV_WIDTH = NSA_KV_HEADS * HEAD_DIM
GLA_KEY_WIDTH = GLA_HEADS * GLA_KEY_DIM
GLA_WIDTH = GLA_HEADS * GLA_VAL_DIM
IN_SIZES = (NSA_WIDTH,) + (NSA_KV_WIDTH,) * 6 + (
    NSA_HEADS * 3, GLA_KEY_WIDTH, GLA_KEY_WIDTH, GLA_WIDTH, GLA_GATE_RANK, GLA_WIDTH)
MAIN_SIZES = (NSA_WIDTH,) + (NSA_KV_WIDTH,) * 6 + (GLA_KEY_WIDTH, GLA_KEY_WIDTH, GLA_WIDTH, GLA_WIDTH)
MAIN_IN_SEG = (0, 1, 2, 3, 4, 5, 6, 8, 9, 10, 12)
MAIN_LAYOUT = (0, 9, 10, 1, 2, 3, 4, 5, 6, 7, 8)
MAIN_OFFS = tuple(sum(MAIN_SIZES[j] for j in MAIN_LAYOUT[:MAIN_LAYOUT.index(i)]) for i in range(len(MAIN_SIZES)))

LANES = 128
SUBLANES = 8
BF16_ROWS = 16
VMEM_LIMIT = 48 * 1024 * 1024
INPROJ_COLS = 512
MOE_BLOCK = 1024
MOE_SUB = 256
MOE_SLICE_CUTS = (1, 5, 16)
NSA_SAFE_EXPONENT = 120.0
GLA_UNROLL = 4
ROUTE_ROWS = 16

_NT = (((1,), (1,)), ((), ()))
_TN = (((0,), (0,)), ((), ()))


def _cparams(*sem):
    return pltpu.CompilerParams(dimension_semantics=sem, vmem_limit_bytes=VMEM_LIMIT)


def _rms(t, g):
    return t * lax.rsqrt(jnp.mean(t * t, axis=-1, keepdims=True) + EPS) * g


def _split_bf16(t):
    hi = t.astype(bf16)
    lo = (t - hi.astype(f32)).astype(bf16)
    return hi, lo


def _rope(t, cos_f, sin_f, perm):
    hi, lo = _split_bf16(t)
    rot = (jnp.dot(hi, perm, preferred_element_type=f32) +
           jnp.dot(lo, perm, preferred_element_type=f32))
    return t * cos_f + rot * sin_f


def _inproj_kernel(x_ref, g_ref, w_ref, ws_ref, o_ref, os_ref):
    x = x_ref[...]
    h = _rms(x, g_ref[...]).astype(bf16)
    n = o_ref.shape[1]
    for c in range(0, n, INPROJ_COLS):
        e = min(c + INPROJ_COLS, n)
        o_ref[:, c:e] = jnp.dot(h, w_ref[:, c:e], preferred_element_type=f32).astype(bf16)
    os_ref[...] = jnp.dot(h, ws_ref[...], preferred_element_type=f32)


def _inproj(x2, g, w_main, w_small, tm=512):
    m, d = x2.shape
    n = w_main.shape[1]
    ns = w_small.shape[1]
    return pl.pallas_call(
        _inproj_kernel,
        grid=(m // tm,),
        in_specs=[pl.BlockSpec((tm, d), lambda i: (i, 0)),
                  pl.BlockSpec((1, d), lambda i: (0, 0)),
                  pl.BlockSpec((d, n), lambda i: (0, 0)),
                  pl.BlockSpec((d, ns), lambda i: (0, 0))],
        out_specs=[pl.BlockSpec((tm, n), lambda i: (i, 0)),
                   pl.BlockSpec((tm, ns), lambda i: (i, 0))],
        out_shape=[jax.ShapeDtypeStruct((m, n), bf16),
                   jax.ShapeDtypeStruct((m, ns), f32)],
        compiler_params=_cparams("parallel"),
        name="inproj",
    )(x2, g, w_main, w_small)


def _kvprep_kernel(kc_ref, vc_ref, ks_ref, vs_ref, kw_ref, vw_ref, cs_ref, spread_ref, kg_ref, ones_ref,
                   perm_ref, pick_ref, posk_ref, w1k_ref, w2k_ref, posv_ref, w1v_ref, w2v_ref,
                   kco_ref, vcto_ref, kso_ref, vsto_ref, kwo_ref, vwto_ref, tmp_ref):
    ones_bd = ones_ref[...]
    nh_out = kco_ref.shape[2]

    def rms_heads(t, g):
        sq_hi, sq_lo = _split_bf16(t * t)
        ss = (jnp.dot(sq_hi, ones_bd, preferred_element_type=f32) +
              jnp.dot(sq_lo, ones_bd, preferred_element_type=f32))
        return t * lax.rsqrt(ss * (1.0 / HEAD_DIM) + EPS) * g

    def compress(src_ref, pos_ref, w1_ref, w2_ref):
        tmp_ref[...] = src_ref[0].astype(f32)
        nh = tmp_ref.shape[0] // CMP_STRIDE
        a = jnp.zeros((nh, NSA_KV_HEADS * CMP_HIDDEN), f32)
        b = jnp.zeros((nh, NSA_KV_HEADS * CMP_HIDDEN), f32)
        for p in range(CMP_STRIDE):
            rows = tmp_ref[pl.ds(p, nh, stride=CMP_STRIDE), :]
            a = a + jnp.dot((rows + pos_ref[p:p + 1, :]).astype(bf16), w1_ref[p], preferred_element_type=f32)
            q = CMP_STRIDE + p
            b = b + jnp.dot((rows + pos_ref[q:q + 1, :]).astype(bf16), w1_ref[q], preferred_element_type=f32)
        pre = a + pltpu.roll(b, nh - 1, 0)
        hid = pre * jax.nn.sigmoid(pre)
        return jnp.dot(hid.astype(bf16), w2_ref[...], preferred_element_type=f32)

    kc = rms_heads(compress(kc_ref, posk_ref, w1k_ref, w2k_ref), kg_ref[0:1, :]).astype(bf16)
    tmp_ref[:nh_out, :] = compress(vc_ref, posv_ref, w1v_ref, w2v_ref)
    vcto_ref[0] = tmp_ref[:nh_out, :].T.astype(bf16)
    s, w = ks_ref.shape[1], ks_ref.shape[2]
    cs = jnp.concatenate([cs_ref[0], jnp.zeros((LANES - cs_ref.shape[1], s), f32)], axis=0).T
    cs_hi = cs.astype(bf16)
    cs_mid, cs_lo = _split_bf16(cs - cs_hi.astype(f32))
    spread = spread_ref[...]
    tables = (jnp.dot(cs_hi, spread, preferred_element_type=f32) + jnp.dot(cs_mid, spread, preferred_element_type=f32) +
              jnp.dot(cs_lo, spread, preferred_element_type=f32))
    lane_row = lax.broadcasted_iota(jnp.int32, (1, LANES), 1)
    cos_f = tables[:, :LANES] + jnp.where(lane_row % HEAD_DIM >= ROPE_DIM, 1.0, 0.0)
    sin_f = tables[:, LANES:]
    perm = perm_ref[...]
    ks = _rope(rms_heads(ks_ref[0].astype(f32), kg_ref[1:2, :]), cos_f, sin_f, perm).astype(bf16)
    kw = _rope(rms_heads(kw_ref[0].astype(f32), kg_ref[2:3, :]), cos_f, sin_f, perm).astype(bf16)
    row = lax.broadcasted_iota(jnp.int32, (s, LANES), 0)
    lane = lax.broadcasted_iota(jnp.int32, (s, LANES), 1)
    last_lane_one = lambda n: jnp.where(lax.broadcasted_iota(jnp.int32, (n, LANES), 1) == LANES - 1, 1.0, 0.0)
    ones_lane = last_lane_one(s)
    block_onehot = jnp.where(row // SEL_BLOCK + HEAD_DIM == lane, 1.0, 0.0) + ones_lane
    pad_lane = lax.broadcasted_iota(jnp.int32, (WINDOW, LANES), 1)
    pad_flag = jnp.where((pad_lane == HEAD_DIM) | (pad_lane == LANES - 1), 1.0, 0.0)
    for h in range(NSA_KV_HEADS):
        pick = pick_ref[h]
        kco_ref[0, h] = (jnp.dot(kc, pick, preferred_element_type=f32) + last_lane_one(nh_out)).astype(bf16)
        kso_ref[0, h] = (jnp.dot(ks, pick, preferred_element_type=f32) + block_onehot).astype(bf16)
        kwo_ref[0, h, :WINDOW, :] = pad_flag.astype(bf16)
        kwo_ref[0, h, WINDOW:, :] = (jnp.dot(kw, pick, preferred_element_type=f32) + ones_lane).astype(bf16)
    vsto_ref[0] = vs_ref[0].astype(f32).T.astype(bf16)
    vwto_ref[0] = jnp.concatenate([jnp.zeros((w, WINDOW), f32), vw_ref[0].astype(f32).T], axis=1).astype(bf16)


def _kvprep(proj, cs_t, spread, kg2, ones_bd, perm2, pick, posk, w1k, w2k, posv, w1v, w2v):
    b, s, _ = proj.shape
    w = NSA_KV_WIDTH
    assert w == LANES
    nh = s // CMP_STRIDE
    col = lambda i: pl.BlockSpec((1, s, w), lambda n: (n, 0, MAIN_OFFS[i] // w))
    whole = lambda a: pl.BlockSpec(a.shape, lambda n: (0,) * a.ndim)
    tab = pl.BlockSpec((1, cs_t.shape[1], s), lambda n: (n, 0, 0))
    out = lambda *shp: (pl.BlockSpec((1,) + shp, lambda n: (n,) + (0,) * len(shp)),
                        jax.ShapeDtypeStruct((b,) + shp, bf16))
    hk = NSA_KV_HEADS
    outs = [out(hk, nh, LANES), out(w, nh), out(hk, s, LANES), out(w, s), out(hk, WINDOW + s, LANES),
            out(w, WINDOW + s)]
    return pl.pallas_call(
        _kvprep_kernel,
        grid=(b,),
        in_specs=[col(1), col(2), col(3), col(4), col(5), col(6), tab, whole(spread), whole(kg2), whole(ones_bd),
                  whole(perm2), whole(pick), whole(posk), whole(w1k), whole(w2k), whole(posv), whole(w1v),
                  whole(w2v)],
        out_specs=[o[0] for o in outs],
        out_shape=[o[1] for o in outs],
        scratch_shapes=[pltpu.VMEM((s, w), f32)],
        compiler_params=_cparams("parallel"),
        name="nsa_kvprep",
    )(proj, proj, proj, proj, proj, proj, cs_t, spread, kg2, ones_bd, perm2, pick, posk, w1k, w2k, posv, w1v, w2v)


def _nsa_kernel(q_ref, cos_ref, sin_ref, qg_ref, mb_ref, kc_ref, vct_ref, ks_ref, vst_ref, kw_ref, vwt_ref,
                gate_ref, o_ref, zero_ref, *, tq, ck, bounded):
    g = NSA_GROUP
    qi = pl.program_id(2)
    t0 = pl.multiple_of(qi * tq, tq)
    scale = HEAD_DIM ** -0.5 * math.log2(math.e)
    per_head = lambda t: jnp.concatenate([t] * g, axis=1)

    qt = q_ref[0].astype(f32).T
    qt = jnp.concatenate([qt[i * HEAD_DIM:(i + 1) * HEAD_DIM] for i in range(g)], axis=1)
    qn = qt * lax.rsqrt(jnp.mean(qt * qt, axis=0, keepdims=True) + EPS) * qg_ref[...]
    half = ROPE_DIM // 2
    cos8, sin8 = per_head(cos_ref[0]), per_head(sin_ref[0])
    x1, x2 = qn[:half], qn[half:ROPE_DIM]
    q_rot = jnp.concatenate([x1 * cos8 - x2 * sin8, x2 * cos8 + x1 * sin8, qn[ROPE_DIM:]], axis=0)
    n_extra = kc_ref.shape[3] - HEAD_DIM

    def with_features(qb, feats, branch):
        used = sum(f.shape[0] for f in feats)
        last = jnp.broadcast_to(-mb_ref[0:1, branch:branch + 1] if bounded else 0.0, (1, g * tq))
        fill = jnp.zeros((n_extra - used - 1, g * tq), f32)
        return jnp.concatenate([qb] + feats + [fill, last], axis=0).astype(bf16)

    q_cmp = with_features(qn * scale, [], 0)
    q_rot = q_rot * scale
    tq_row = t0 + lax.broadcasted_iota(jnp.int32, (1, tq), 1)

    kc = kc_ref[0, 0]
    ncp = kc.shape[0]
    n_col = lax.broadcasted_iota(jnp.int32, (ncp, 1), 0)
    ok_c = (n_col * CMP_STRIDE + (CMP_BLOCK - 1)) <= tq_row
    s_c = jnp.dot(kc, q_cmp, preferred_element_type=f32) + per_head(jnp.where(ok_c, 0.0, NEG_INF))
    if bounded:
        p_c = jnp.exp2(s_c)
    else:
        m_c = jnp.max(s_c, axis=0, keepdims=True)
        p_c = jnp.exp2(s_c - m_c) * per_head(jnp.where(ok_c, 1.0, 0.0))
    den = jnp.sum(p_c, axis=0, keepdims=True)
    p_c = p_c * (1.0 / jnp.where(den > 0, den, 1.0))
    o_cmp = jnp.dot(vct_ref[0], p_c.astype(bf16), preferred_element_type=f32)

    pg = p_c[:, :tq]
    for i in range(1, g):
        pg = pg + p_c[:, i * tq:(i + 1) * tq]
    nj = ks_ref.shape[2] // SEL_BLOCK
    jj = lax.broadcasted_iota(jnp.int32, (nj, ncp), 0)
    nn = lax.broadcasted_iota(jnp.int32, (nj, ncp), 1)
    overlap = (nn * CMP_STRIDE < (jj + 1) * SEL_BLOCK) & (nn * CMP_STRIDE + CMP_BLOCK > jj * SEL_BLOCK)
    overlap = jnp.where(overlap, 1.0, 0.0).astype(bf16)
    pg_hi, pg_lo = _split_bf16(pg)
    imp = (jnp.dot(overlap, pg_hi, preferred_element_type=f32) +
           jnp.dot(overlap, pg_lo, preferred_element_type=f32))
    j_col = lax.broadcasted_iota(jnp.int32, (nj, 1), 0)
    j_f = j_col.astype(f32)
    cur = tq_row // SEL_BLOCK
    valid = j_col <= cur
    forced = (j_col == 0) | (j_col == cur) | (j_col == cur - 1)
    score = jnp.where(valid, imp + jnp.where(forced, FORCE_BONUS, 0.0), NEG_INF)
    sel = jnp.zeros((nj, tq), f32)
    for _ in range(SEL_TOPK):
        m = jnp.max(score, axis=0, keepdims=True)
        first = jnp.min(jnp.where(score == m, j_f, float(nj)), axis=0, keepdims=True)
        hit = j_f == first
        sel = jnp.where(hit, 1.0, sel)
        score = jnp.where(hit, -jnp.inf, score)
    sel = jnp.where(valid, sel, 0.0)

    def weighted_values(vt, p):
        lhs = jnp.concatenate([vt, jnp.ones((BF16_ROWS, vt.shape[1]), bf16)], axis=0)
        r = jnp.dot(lhs, p.astype(bf16), preferred_element_type=f32)
        return r[:HEAD_DIM], r[HEAD_DIM:HEAD_DIM + 1]

    wl = WINDOW + tq
    q_win = with_features(q_rot, [jnp.full((1, g * tq), NEG_INF, f32)], 2)
    s_w = jnp.dot(kw_ref[0, 0, pl.ds(t0, wl), :], q_win, preferred_element_type=f32)
    step = lax.broadcasted_iota(jnp.int32, (tq, 1), 0)
    lo_ok = (t0 - WINDOW + step) > (tq_row - WINDOW)
    hi_ok = (t0 + step) <= tq_row
    parts = [s_w[:tq] + per_head(jnp.where(lo_ok, 0.0, NEG_INF))]
    if wl > 2 * tq:
        parts.append(s_w[tq:wl - tq])
    parts.append(s_w[wl - tq:] + per_head(jnp.where(hi_ok, 0.0, NEG_INF)))
    if bounded:
        p_w = jnp.concatenate([jnp.exp2(t) for t in parts], axis=0)
    else:
        m_w = functools.reduce(jnp.maximum, [jnp.max(t, axis=0, keepdims=True) for t in parts])
        p_w = jnp.concatenate([jnp.exp2(t - m_w) for t in parts], axis=0)
    o_win, l_w = weighted_values(vwt_ref[0, :, pl.ds(t0, wl)], p_w)
    o_win = o_win * (1.0 / l_w)

    q_sel = with_features(q_rot, [per_head(jnp.where(sel > 0.5, 0.0, NEG_INF))], 1)

    def sel_step(carry, k0, diagonal):
        m_prev, l_prev, acc = carry
        s = jnp.dot(ks_ref[0, 0, pl.ds(k0, ck), :], q_sel, preferred_element_type=f32)
        if diagonal:
            kpos = k0 + lax.broadcasted_iota(jnp.int32, (ck, 1), 0)
            s = s + per_head(jnp.where(kpos <= tq_row, 0.0, NEG_INF))
        if bounded:
            pv, p_sum = weighted_values(vst_ref[0, :, pl.ds(k0, ck)], jnp.exp2(s))
            return m_prev, l_prev + p_sum, acc + pv
        m_new = jnp.maximum(m_prev, jnp.max(s, axis=0, keepdims=True))
        alpha = jnp.exp2(m_prev - m_new)
        pv, p_sum = weighted_values(vst_ref[0, :, pl.ds(k0, ck)], jnp.exp2(s - m_new))
        return m_new, alpha * l_prev + p_sum, alpha * acc + pv

    init = (jnp.full((1, g * tq), NEG_INF, f32), jnp.zeros((1, g * tq), f32),
            jnp.zeros((HEAD_DIM, g * tq), f32))
    last = (t0 + tq + ck - 1) // ck - 1
    carry = lax.fori_loop(0, last, lambda c, cr: sel_step(cr, pl.multiple_of(c * ck, ck), False), init)
    _, l_s, acc_s = sel_step(carry, pl.multiple_of(last * ck, ck), True)
    o_sel = acc_s * (1.0 / l_s)

    gt = gate_ref[0].T
    gate = lambda j: jax.nn.sigmoid(jnp.concatenate([gt[i * 3 + j:i * 3 + j + 1] for i in range(g)], axis=1))
    ot = gate(0) * o_cmp + gate(1) * o_sel + gate(2) * o_win
    o2 = jnp.concatenate([ot[:, i * tq:(i + 1) * tq] for i in range(g)], axis=0)
    o_ref[0] = o2.T.astype(o_ref.dtype)
    zero_ref[...] = jnp.zeros_like(zero_ref)


def _nsa_attention(proj, cos_t, sin_t, qg, bounds, kc, vct, ks, vst, kw, vwt, gate_logits, *, bounded,
                   zero_rows, zero_cols, tq=512, ck=512):
    b, s, _ = proj.shape
    hk, g, dh = NSA_KV_HEADS, NSA_GROUP, HEAD_DIM
    nt = s // tq
    steps = b * hk * nt
    assert WINDOW >= tq and ck % tq == 0 and s % ck == 0 and zero_rows % (BF16_ROWS * steps) == 0
    tab = pl.BlockSpec((1, ROPE_DIM // 2, tq), lambda i, j, t: (i, 0, t))
    assert s // SEL_BLOCK <= kc.shape[3] - dh
    keys = lambda a: pl.BlockSpec((1, 1) + a.shape[2:], lambda i, j, t: (i, j, 0, 0))
    vals = lambda a: pl.BlockSpec((1, dh, a.shape[2]), lambda i, j, t: (i, j, 0))
    return pl.pallas_call(
        functools.partial(_nsa_kernel, tq=tq, ck=ck, bounded=bounded),
        grid=(b, hk, s // tq),
        in_specs=[pl.BlockSpec((1, tq, g * dh), lambda i, j, t: (i, t, j)),
                  tab, tab, pl.BlockSpec((dh, 1), lambda i, j, t: (0, 0)),
                  pl.BlockSpec(bounds.shape, lambda i, j, t: (0, 0)),
                  keys(kc), vals(vct), keys(ks), vals(vst), keys(kw), vals(vwt),
                  pl.BlockSpec((1, tq, LANES), lambda i, j, t: (i, t, j))],
        out_specs=[pl.BlockSpec((1, tq, g * dh), lambda i, j, t: (i, t, j)),
                   pl.BlockSpec((zero_rows // steps, zero_cols), lambda i, j, t: ((i * hk + j) * nt + t, 0))],
        out_shape=[jax.ShapeDtypeStruct((b, s, NSA_WIDTH), bf16),
                   jax.ShapeDtypeStruct((zero_rows, zero_cols), bf16)],
        compiler_params=_cparams("parallel", "parallel", "arbitrary"),
        name="nsa_attention_bounded" if bounded else "nsa_attention",
    )(proj, cos_t, sin_t, qg, bounds, kc, vct, ks, vst, kw, vwt, gate_logits)


def _gla_kernel(q_ref, k_ref, v_ref, lr_ref, wg_ref, bg_ref, r_ref, ng_ref, o_ref, *, grp, npair):
    c = GLA_CHUNK
    s = q_ref.shape[1]
    dk, dv = GLA_KEY_DIM, GLA_VAL_DIM
    rows = grp * c
    ri = lax.broadcasted_iota(jnp.int32, (rows, rows), 0)
    ci = lax.broadcasted_iota(jnp.int32, (rows, rows), 1)
    causal = (ci <= ri) & (ci // c == ri // c)
    tri = jnp.where(causal, 1.0, 0.0).astype(bf16)
    lane = lax.broadcasted_iota(jnp.int32, (1, 2 * dk), 1)
    head_mask = [lane < dk, lane >= dk]
    chunk_of_row = lax.broadcasted_iota(jnp.int32, (rows, 1), 0) // c
    wg = wg_ref[...]
    bg = bg_ref[...]
    ng = ng_ref[...]

    def body(n, states):
        r0 = pl.multiple_of(n * rows, rows)
        z = jnp.dot(lr_ref[0, pl.ds(r0, rows), :GLA_GATE_RANK], wg, preferred_element_type=f32,
                    precision=lax.Precision.HIGHEST) + bg
        log_a = -(jnp.maximum(-z, 0.0) + jnp.log(1.0 + jnp.exp(-jnp.abs(z)))) / GLA_TAU
        la_hi = log_a.astype(bf16)
        la_mid, la_lo = _split_bf16(log_a - la_hi.astype(f32))
        bcum = (jnp.dot(tri, la_hi, preferred_element_type=f32) + jnp.dot(tri, la_mid, preferred_element_type=f32) +
                jnp.dot(tri, la_lo, preferred_element_type=f32))
        qf = q_ref[0, pl.ds(r0, rows), :].astype(f32) * (dk ** -0.5)
        kf = k_ref[0, pl.ds(r0, rows), :].astype(f32)
        q_dec = qf * jnp.exp(bcum)
        k_dec = (kf * jnp.exp(-bcum)).astype(bf16)
        b_last = [bcum[(i + 1) * c - 1:(i + 1) * c, :] for i in range(grp)]
        k_state = jnp.concatenate(
            [kf[i * c:(i + 1) * c] * jnp.exp(b_last[i] - bcum[i * c:(i + 1) * c]) for i in range(grp)],
            axis=0).astype(bf16)
        by_chunk = lambda t: jnp.concatenate(
            [jnp.where(chunk_of_row == i, t, jnp.zeros_like(t)) for i in range(grp)], axis=1)
        new_states = []
        for hh in range(2 * npair):
            pr, h = divmod(hh, 2)
            pair = slice(pr * 2 * dk, (pr + 1) * 2 * dk)
            q_h = jnp.where(head_mask[h], q_dec[:, pair], 0.0).astype(bf16)
            v_h = v_ref[0, pl.ds(r0, rows), hh * dv:(hh + 1) * dv]
            attn = lax.dot_general(q_h, k_dec[:, pair], _NT, preferred_element_type=f32)
            attn = jnp.where(causal, attn, 0.0).astype(bf16)
            o_intra = jnp.dot(attn, v_h, preferred_element_type=f32)
            u_all = lax.dot_general(v_h, by_chunk(k_state[:, pair]), _TN, preferred_element_type=f32)
            st = states[hh]
            entering = []
            for i in range(grp):
                entering.append(st)
                st = st * jnp.exp(b_last[i][:, pair]) + u_all[:, i * 2 * dk:(i + 1) * 2 * dk]
            new_states.append(st)
            o_inter = lax.dot_general(by_chunk(q_h), jnp.concatenate(entering, axis=1).astype(bf16), _NT,
                                      preferred_element_type=f32)
            o = _rms(o_intra + o_inter, ng)
            rr = r_ref[0, pl.ds(r0, rows), hh * dv:(hh + 1) * dv].astype(f32)
            o_ref[0, pl.ds(r0, rows), hh * dv:(hh + 1) * dv] = (o * (rr * jax.nn.sigmoid(rr))).astype(o_ref.dtype)
        return tuple(new_states)

    zero = jnp.zeros((dv, 2 * dk), f32)
    lax.fori_loop(0, s // rows, body, (zero,) * (2 * npair), unroll=GLA_UNROLL)


def _gla(proj, proj_s, wg, bg, ng, grp=4, npair=GLA_HEADS // 2):
    b, s, _ = proj.shape
    dk2, dv2 = 2 * npair * GLA_KEY_DIM, 2 * npair * GLA_VAL_DIM
    oq, ok_, ov, orr = (MAIN_OFFS[7] // dk2, MAIN_OFFS[8] // dk2, MAIN_OFFS[9] // dv2, MAIN_OFFS[10] // dv2)
    assert all(MAIN_OFFS[i] % w == 0 for i, w in ((7, dk2), (8, dk2), (9, dv2), (10, dv2)))
    col = lambda w, o: pl.BlockSpec((1, s, w), lambda i, j: (i, 0, o + j))
    return pl.pallas_call(
        functools.partial(_gla_kernel, grp=grp, npair=npair),
        grid=(b, GLA_HEADS // (2 * npair)),
        in_specs=[col(dk2, oq), col(dk2, ok_), col(dv2, ov),
                  pl.BlockSpec((1, s, LANES), lambda i, j: (i, 0, 2)),
                  pl.BlockSpec((GLA_GATE_RANK, dk2), lambda i, j: (0, j)),
                  pl.BlockSpec((1, dk2), lambda i, j: (0, j)),
                  col(dv2, orr),
                  pl.BlockSpec((1, GLA_VAL_DIM), lambda i, j: (0, 0))],
        out_specs=pl.BlockSpec((1, s, dv2), lambda i, j: (i, 0, j)),
        out_shape=jax.ShapeDtypeStruct((b, s, GLA_WIDTH), bf16),
        compiler_params=_cparams("parallel", "parallel"),
        name="gla",
    )(proj, proj, proj, proj_s, wg, bg, proj, ng)


def _outproj_kernel(a_ref, b_ref, x_ref, wo_ref, g2_ref, wrh_ref, wrl_ref, br_ref,
                    x1_ref, h2_ref, rt_ref, cnt_ref, carry_ref):
    @pl.when(pl.program_id(0) == 0)
    def _():
        carry_ref[...] = jnp.zeros_like(carry_ref)

    na = a_ref.shape[1]
    tm = a_ref.shape[0]
    y = (jnp.dot(a_ref[...], wo_ref[:na, :], preferred_element_type=f32) +
         jnp.dot(b_ref[...], wo_ref[na:, :], preferred_element_type=f32))
    x1 = x_ref[...] + y
    x1_ref[...] = x1
    h2 = _rms(x1, g2_ref[...])
    h2_ref[...] = h2.astype(bf16)
    hi, lo = _split_bf16(h2)
    logits = (jnp.dot(hi, wrh_ref[...], preferred_element_type=f32) +
              jnp.dot(lo, wrh_ref[...], preferred_element_type=f32) +
              jnp.dot(hi, wrl_ref[...], preferred_element_type=f32)) + br_ref[...]
    score = logits.T[:N_EXPERTS]
    e_col = lax.broadcasted_iota(jnp.int32, (N_EXPERTS, 1), 0).astype(f32)
    top = jnp.max(score, axis=0, keepdims=True)
    hits, firsts, weights = [], [], []
    for _ in range(TOP_K):
        m = jnp.max(score, axis=0, keepdims=True)
        first = jnp.min(jnp.where(score == m, e_col, float(N_EXPERTS)), axis=0, keepdims=True)
        hit = e_col == first
        hits.append(hit)
        firsts.append(first)
        weights.append(jnp.exp(m - top))
        score = jnp.where(hit, -jnp.inf, score)
    tot = weights[0] + weights[1] + weights[2] + weights[3]
    onehot = jnp.where(hits[0] | hits[1] | hits[2] | hits[3], 1.0, 0.0)
    ri = lax.broadcasted_iota(jnp.int32, (tm, tm), 0)
    ci = lax.broadcasted_iota(jnp.int32, (tm, tm), 1)
    earlier = jnp.where(ri < ci, 1.0, 0.0).astype(bf16)
    rank = carry_ref[...] + jnp.dot(onehot.astype(bf16), earlier, preferred_element_type=f32)
    carry_ref[...] = carry_ref[...] + jnp.sum(onehot, axis=1, keepdims=True)
    cnt_ref[...] = jnp.broadcast_to(carry_ref[...], cnt_ref.shape)
    ranks = [jnp.sum(jnp.where(hits[k], rank, 0.0), axis=0, keepdims=True) for k in range(TOP_K)]
    gates = [weights[k] / tot for k in range(TOP_K)]
    pad = jnp.zeros((rt_ref.shape[0] - 3 * TOP_K, tm), f32)
    rt_ref[...] = jnp.concatenate(firsts + gates + ranks + [pad], axis=0)


def _outproj_router(a, b, x2, wo, g2, wr_hi, wr_lo, br, tm=512):
    m, d = x2.shape
    na, nb = a.shape[1], b.shape[1]
    row = lambda n: pl.BlockSpec((tm, n), lambda i: (i, 0))
    whole = lambda t: pl.BlockSpec(t.shape, lambda i: (0,) * t.ndim)
    return pl.pallas_call(
        _outproj_kernel,
        grid=(m // tm,),
        in_specs=[row(na), row(nb), row(d), whole(wo), whole(g2), whole(wr_hi), whole(wr_lo), whole(br)],
        out_specs=[row(d), row(d), pl.BlockSpec((ROUTE_ROWS, tm), lambda i: (0, i)),
                   pl.BlockSpec((N_EXPERTS, LANES), lambda i: (0, 0))],
        out_shape=[jax.ShapeDtypeStruct((m, d), f32),
                   jax.ShapeDtypeStruct((m, d), bf16),
                   jax.ShapeDtypeStruct((ROUTE_ROWS, m), f32),
                   jax.ShapeDtypeStruct((N_EXPERTS, LANES), f32)],
        scratch_shapes=[pltpu.VMEM((N_EXPERTS, 1), f32)],
        compiler_params=_cparams("arbitrary"),
        name="outproj_router",
    )(a, b, x2, wo, g2, wr_hi, wr_lo, br)


def _moe_kernel(vb_ref, ve_ref, lo_ref, hi_ref, nv_ref, xs_ref, wgu_ref, bgu_ref, wd_ref, bd_ref, ys_ref,
                o_ref, wgu_bf, wd_bf, *, slice_id, table_off):
    del ys_ref
    step = pl.program_id(0)
    v = table_off + step
    prev = jnp.maximum(v - 1, table_off)
    live = step < nv_ref[slice_id]
    new_expert = (step == 0) | (ve_ref[v] != ve_ref[prev])
    new_block = (step == 0) | (vb_ref[v] != vb_ref[prev])

    @pl.when(live & new_expert)
    def _():
        wgu_bf[...] = wgu_ref[0].astype(bf16)
        wd_bf[...] = wd_ref[0].astype(bf16)

    @pl.when(live & new_block)
    def _():
        o_ref[...] = jnp.zeros_like(o_ref)

    @pl.when(live)
    def _():
        blk0 = vb_ref[v] * MOE_BLOCK
        lo, hi = lo_ref[v], hi_ref[v]
        first = jnp.maximum(lo - blk0, 0) // MOE_SUB
        last = (jnp.minimum(hi - blk0, MOE_BLOCK) + MOE_SUB - 1) // MOE_SUB

        def rows_pass(j, carry, n):
            r0 = pl.multiple_of(j * n, n)
            h = jnp.dot(xs_ref[pl.ds(r0, n), :], wgu_bf[...], preferred_element_type=f32) + bgu_ref[0]
            x_glu = jnp.minimum(h[:, :D_FF], SWIGLU_LIMIT)
            x_lin = jnp.clip(h[:, D_FF:], -SWIGLU_LIMIT, SWIGLU_LIMIT)
            act = x_glu * jax.nn.sigmoid(SWIGLU_ALPHA * x_glu) * (x_lin + 1.0)
            y = jnp.dot(act.astype(bf16), wd_bf[...], preferred_element_type=f32) + bd_ref[0]
            row = blk0 + r0 + lax.broadcasted_iota(jnp.int32, (n, 1), 0)
            mine = (row >= lo) & (row < hi)
            o_ref[pl.ds(r0, n), :] = jnp.where(mine, y.astype(o_ref.dtype), o_ref[pl.ds(r0, n), :])
            return carry

        pair_lo = jnp.minimum((first + 1) // 2 * 2, last)
        pair_hi = jnp.maximum(last // 2 * 2, pair_lo)
        single = functools.partial(rows_pass, n=MOE_SUB)
        lax.fori_loop(first, pair_lo, single, 0)
        lax.fori_loop(pair_lo // 2, pair_hi // 2, functools.partial(rows_pass, n=2 * MOE_SUB), 0)
        lax.fori_loop(pair_hi, last, single, 0)


def _moe_experts(table, slice_id, blk_lo, xs, wgu, bgu, wd, bd, ys_so_far):
    d = xs.shape[1]
    nvis = xs.shape[0] // MOE_BLOCK + N_EXPERTS - 1
    off = slice_id * (table[0].shape[0] // table[4].shape[0])
    in_specs = [pl.BlockSpec((MOE_BLOCK, d), lambda v, vb, ve, *_: (vb[off + v] - blk_lo, 0)),
                pl.BlockSpec((1, d, 2 * D_FF), lambda v, vb, ve, *_: (ve[off + v], 0, 0)),
                pl.BlockSpec((1, 1, 2 * D_FF), lambda v, vb, ve, *_: (ve[off + v], 0, 0)),
                pl.BlockSpec((1, D_FF, d), lambda v, vb, ve, *_: (ve[off + v], 0, 0)),
                pl.BlockSpec((1, 1, d), lambda v, vb, ve, *_: (ve[off + v], 0, 0))]
    operands = table + (xs, wgu, bgu, wd, bd, ys_so_far)
    in_specs.append(pl.BlockSpec(memory_space=pl.ANY))
    aliases = {len(operands) - 1: 0}
    return pl.pallas_call(
        functools.partial(_moe_kernel, slice_id=slice_id, table_off=off),
        grid_spec=pltpu.PrefetchScalarGridSpec(
            num_scalar_prefetch=5,
            grid=(nvis,),
            in_specs=in_specs,
            out_specs=pl.BlockSpec((MOE_BLOCK, d), lambda v, vb, ve, *_: (vb[off + v], 0)),
            scratch_shapes=[pltpu.VMEM((d, 2 * D_FF), bf16), pltpu.VMEM((D_FF, d), bf16)],
        ),
        out_shape=jax.ShapeDtypeStruct(ys_so_far.shape, ys_so_far.dtype),
        input_output_aliases=aliases,
        compiler_params=_cparams("arbitrary"),
        name="moe_experts",
    )(*operands)


def _positions_kernel(start_ref, rt_ref, pos_ref):
    idx = rt_ref[:TOP_K, :].astype(jnp.int32)
    pos = rt_ref[2 * TOP_K:3 * TOP_K, :].astype(jnp.int32)
    for e in range(N_EXPERTS):
        pos = pos + jnp.where(idx == e, start_ref[e], 0)
    pos_ref[...] = pos


def _positions(start, rt):
    m = rt.shape[1]
    return pl.pallas_call(
        _positions_kernel,
        grid_spec=pltpu.PrefetchScalarGridSpec(
            num_scalar_prefetch=1, grid=(1,),
            in_specs=[pl.BlockSpec(rt.shape, lambda i, st: (0, 0))],
            out_specs=pl.BlockSpec((TOP_K, m), lambda i, st: (0, 0))),
        out_shape=jax.ShapeDtypeStruct((TOP_K, m), jnp.int32),
        compiler_params=_cparams("arbitrary"),
        name="route_positions",
    )(start, rt)


def _visit_tables(start, end, cuts):
    blk_lo = jnp.array([0] + list(cuts[:-1]), jnp.int32)[:, None]
    blk_hi = jnp.array(list(cuts), jnp.int32)[:, None]
    room = max(h - l for l, h in zip([0] + list(cuts[:-1]), cuts)) + N_EXPERTS - 1
    first_blk = jnp.maximum((start // MOE_BLOCK)[None, :], blk_lo)
    last_blk = jnp.minimum(jnp.where(end > start, (end - 1) // MOE_BLOCK, -1)[None, :], blk_hi - 1)
    per_e = jnp.maximum(last_blk - first_blk + 1, 0)
    v_end = jnp.cumsum(per_e, axis=1)
    v_start = v_end - per_e
    n_live = v_end[:, -1]
    vc = jnp.minimum(jnp.arange(room, dtype=jnp.int32)[None, :], n_live[:, None] - 1)
    ve = jnp.sum((v_end[:, None, :] <= vc[:, :, None]).astype(jnp.int32), axis=2)
    of_ve = lambda t: jnp.sum(jnp.where(ve[:, :, None] == jnp.arange(N_EXPERTS), t[:, None, :], 0), axis=2)
    vb = of_ve(first_blk) + vc - of_ve(v_start)
    flat = lambda t: t.reshape(-1).astype(jnp.int32)
    return (flat(vb), flat(ve), flat(of_ve(jnp.broadcast_to(start, per_e.shape))),
            flat(of_ve(jnp.broadcast_to(end, per_e.shape))), n_live.astype(jnp.int32))


def _rope_tables(positions):
    half = ROPE_DIM // 2
    inv_freq = jnp.exp(-math.log(ROPE_THETA) * jnp.arange(0, ROPE_DIM, 2, dtype=f32) / ROPE_DIM)
    ang = positions.astype(f32)[:, None, :] * inv_freq[None, :, None]
    cos_t, sin_t = jnp.cos(ang), jnp.sin(ang)
    d = jnp.arange(LANES) % HEAD_DIM
    f = jnp.arange(LANES)[:, None]
    first, second = d[None, :] == f % half, d[None, :] == f % half + half
    to_cos = jnp.where((f < half) & (first | second), 1.0, 0.0)
    to_sin = jnp.where((f >= half) & (f < 2 * half), jnp.where(second, 1.0, 0.0) - jnp.where(first, 1.0, 0.0), 0.0)
    spread = jnp.concatenate([to_cos, to_sin], axis=1).astype(bf16)
    idx = jnp.arange(half)
    perm = jnp.zeros((HEAD_DIM, HEAD_DIM), f32).at[idx + half, idx].set(1.0).at[idx, idx + half].set(1.0)
    return cos_t, sin_t, spread, perm


def _layer(x, positions, norm1_g, w_in, q_norm_g, k_norm_g, cmp_pos_k, cmp_w1_k, cmp_w2_k,
           cmp_pos_v, cmp_w1_v, cmp_w2_v, gla_w_gate, gla_b_gate, gla_norm_g, w_out, norm2_g,
           w_router, b_router, w_gate_up, b_gate_up, w_down, b_down):
    b, s, d = x.shape
    m = b * s
    hk, g, dh = NSA_KV_HEADS, NSA_GROUP, HEAD_DIM
    x2 = x.reshape(m, d)

    offs = [0]
    for sz in IN_SIZES:
        offs.append(offs[-1] + sz)
    seg = lambda i: w_in[:, offs[i]:offs[i + 1]]
    w_main = jnp.concatenate([seg(MAIN_IN_SEG[i]) for i in MAIN_LAYOUT], axis=1).astype(bf16)
    padw = lambda t: jnp.concatenate([t, jnp.zeros((d, LANES - t.shape[1]), f32)], axis=1)
    ng = g * 3
    w_small = jnp.concatenate([padw(seg(7)[:, :ng]), padw(seg(7)[:, ng:]), padw(seg(11))], axis=1).astype(bf16)
    proj, proj_s = _inproj(x2, norm1_g.reshape(1, d), w_main, w_small)
    proj = proj.reshape(b, s, -1)
    proj_s = proj_s.reshape(b, s, 3 * LANES)

    cos_t, sin_t, spread, perm = _rope_tables(positions)
    per_kv = lambda t: jnp.concatenate([t] * hk, axis=-1)

    def bdiag(t):
        r, c = t.shape[-2:]
        lead = [(0, 0)] * (t.ndim - 2)
        return jnp.concatenate([jnp.pad(t, lead + [(0, 0), (i * c, (hk - 1 - i) * c)]) for i in range(hk)], axis=-2)

    w1_bd = lambda w1: bdiag(w1.reshape(CMP_BLOCK, dh, CMP_HIDDEN)).astype(bf16)
    kc, vct, ks_rot, vst, kw_rot, vwt = _kvprep(
        proj, jnp.concatenate([cos_t, sin_t], axis=1), spread, per_kv(k_norm_g),
        bdiag(jnp.ones((dh, dh), f32)).astype(bf16), bdiag(perm).astype(bf16),
        jnp.stack([jnp.eye(hk * dh, LANES, k=-h * dh, dtype=f32) * (jnp.arange(LANES) < dh)
                   for h in range(hk)]).astype(bf16),
        per_kv(cmp_pos_k), w1_bd(cmp_w1_k), bdiag(cmp_w2_k).astype(bf16),
        per_kv(cmp_pos_v), w1_bd(cmp_w1_v), bdiag(cmp_w2_v).astype(bf16))
    score_bound = (1.05 * dh * dh ** -0.5 * math.log2(math.e) * jnp.max(jnp.abs(q_norm_g)) *
                   jnp.max(jnp.abs(k_norm_g), axis=1))
    bounds = jnp.zeros((SUBLANES, LANES), f32).at[0, :3].set(score_bound)
    nsa_args = (proj, cos_t, sin_t, q_norm_g.reshape(dh, 1), bounds, kc, vct, ks_rot, vst, kw_rot, vwt, proj_s)
    nsa = functools.partial(_nsa_attention, zero_rows=m * TOP_K, zero_cols=d)
    nsa_out, ys_zero = lax.cond(2.0 * jnp.max(score_bound) < NSA_SAFE_EXPONENT,
                                functools.partial(nsa, bounded=True), functools.partial(nsa, bounded=False), *nsa_args)
    nsa_out = nsa_out.reshape(m, NSA_WIDTH)

    gla_out = _gla(proj, proj_s, gla_w_gate, gla_b_gate.reshape(1, -1),
                   gla_norm_g.reshape(1, GLA_VAL_DIM)).reshape(m, GLA_WIDTH)

    padr = lambda t: jnp.concatenate([t, jnp.zeros(t.shape[:-1] + (LANES - t.shape[-1],), t.dtype)], axis=-1)
    wr_hi = w_router.astype(bf16)
    wr_lo = (w_router - wr_hi.astype(f32)).astype(bf16)
    x1, h2, rt, cnt = _outproj_router(nsa_out, gla_out, x2, w_out.astype(bf16), norm2_g.reshape(1, d),
                                      padr(wr_hi), padr(wr_lo), padr(b_router.reshape(1, -1)))

    a = m * TOP_K
    gate = rt[TOP_K:2 * TOP_K]
    counts = cnt[:, 0].astype(jnp.int32)
    end = jnp.cumsum(counts)
    start = end - counts
    pos = _positions(start, rt).reshape(a)
    tok = jnp.arange(a, dtype=jnp.int32) % m
    sorted_tok = jnp.zeros((a,), jnp.int32).at[pos].add(tok, unique_indices=True, mode='promise_in_bounds')
    rows_of = lambda t, idx: t.at[idx].get(mode='promise_in_bounds')
    nblk = a // MOE_BLOCK
    cuts = [nblk * c // MOE_SLICE_CUTS[-1] for c in MOE_SLICE_CUTS] if nblk % MOE_SLICE_CUTS[-1] == 0 else [nblk]
    table = _visit_tables(start, end, cuts)
    ys, lo_blk = ys_zero, 0
    for i, hi_blk in enumerate(cuts):
        xs = rows_of(h2, sorted_tok[lo_blk * MOE_BLOCK:hi_blk * MOE_BLOCK])
        ys = _moe_experts(table, i, lo_blk, xs, w_gate_up, b_gate_up.reshape(N_EXPERTS, 1, -1),
                          w_down, b_down.reshape(N_EXPERTS, 1, -1), ys)
        lo_blk = hi_blk
    out = x1
    for k in range(TOP_K):
        out = out + rows_of(ys, pos[k * m:(k + 1) * m]).astype(f32) * gate[k][:, None]
    return out.reshape(b, s, d)


def kernel(x, positions, norm1_g, w_in, nsa_q_norm_g, nsa_k_norm_g, cmp_pos_k, cmp_w1_k, cmp_w2_k,
           cmp_pos_v, cmp_w1_v, cmp_w2_v, gla_w_gate, gla_b_gate, gla_norm_g, w_out, norm2_g,
           w_router, b_router, w_gate_up, b_gate_up, w_down, b_down):
    for l in range(norm1_g.shape[0]):
        x = _layer(x, positions, norm1_g[l], w_in[l], nsa_q_norm_g[l], nsa_k_norm_g[l],
                   cmp_pos_k[l], cmp_w1_k[l], cmp_w2_k[l], cmp_pos_v[l], cmp_w1_v[l], cmp_w2_v[l],
                   gla_w_gate[l], gla_b_gate[l], gla_norm_g[l], w_out[l], norm2_g[l],
                   w_router[l], b_router[l], w_gate_up[l], b_gate_up[l], w_down[l], b_down[l])
    return x
```

```python
import functools
import math

import jax
import jax.numpy as jnp
from jax import lax
from jax.experimental import pallas as pl
from jax.experimental.pallas import tpu as pltpu

f32 = jnp.float32
bf16 = jnp.bfloat16

NSA_HEADS = 8
NSA_KV_HEADS = 2
NSA_GROUP = NSA_HEADS // NSA_KV_HEADS
HEAD_DIM = 64
CMP_BLOCK = 32
CMP_STRIDE = 16
CMP_HIDDEN = 256
SEL_BLOCK = 64
SEL_TOPK = 8
WINDOW = 512
FORCE_BONUS = 1e4
GLA_HEADS = 4
GLA_KEY_DIM = 64
GLA_VAL_DIM = 128
GLA_CHUNK = 64
GLA_GATE_RANK = 16
GLA_TAU = 16.0
ROPE_THETA = 500000.0
ROPE_DIM = HEAD_DIM // 4
N_EXPERTS = 32
TOP_K = 4
D_FF = 1024
SWIGLU_LIMIT = 7.0
SWIGLU_ALPHA = 1.702
EPS = 1e-6
NEG_INF = -1e30

NSA_WIDTH = NSA_HEADS * HEAD_DIM
NSA_KV_WIDTH = NSA_KV_HEADS * HEAD_DIM
GLA_KEY_WIDTH = GLA_HEADS * GLA_KEY_DIM
GLA_WIDTH = GLA_HEADS * GLA_VAL_DIM
IN_SIZES = (NSA_WIDTH,) + (NSA_KV_WIDTH,) * 6 + (
    NSA_HEADS * 3, GLA_KEY_WIDTH, GLA_KEY_WIDTH, GLA_WIDTH, GLA_GATE_RANK, GLA_WIDTH)
MAIN_SIZES = (NSA_WIDTH,) + (NSA_KV_WIDTH,) * 6 + (GLA_KEY_WIDTH, GLA_KEY_WIDTH, GLA_WIDTH, GLA_WIDTH)
MAIN_IN_SEG = (0, 1, 2, 3, 4, 5, 6, 8, 9, 10, 12)
MAIN_LAYOUT = (0, 9, 10, 1, 2, 3, 4, 5, 6, 7, 8)
MAIN_OFFS = tuple(sum(MAIN_SIZES[j] for j in MAIN_LAYOUT[:MAIN_LAYOUT.index(i)]) for i in range(len(MAIN_SIZES)))

LANES = 128
SUBLANES = 8
BF16_ROWS = 16
VMEM_LIMIT = 48 * 1024 * 1024
INPROJ_COLS = 512
MOE_BLOCK = 1024
MOE_SUB = 256
MOE_SLICE_CUTS = (1, 5, 16)
NSA_SAFE_EXPONENT = 120.0
GLA_UNROLL = 4
ROUTE_ROWS = 16

_NT = (((1,), (1,)), ((), ()))
_TN = (((0,), (0,)), ((), ()))


def _cparams(*sem):
    return pltpu.CompilerParams(dimension_semantics=sem, vmem_limit_bytes=VMEM_LIMIT)


def _rms(t, g):
    return t * lax.rsqrt(jnp.mean(t * t, axis=-1, keepdims=True) + EPS) * g


def _split_bf16(t):
    hi = t.astype(bf16)
    lo = (t - hi.astype(f32)).astype(bf16)
    return hi, lo


def _rope(t, cos_f, sin_f, perm):
    hi, lo = _split_bf16(t)
    rot = (jnp.dot(hi, perm, preferred_element_type=f32) +
           jnp.dot(lo, perm, preferred_element_type=f32))
    return t * cos_f + rot * sin_f


def _inproj_kernel(x_ref, g_ref, w_ref, ws_ref, o_ref, os_ref):
    x = x_ref[...]
    h = _rms(x, g_ref[...]).astype(bf16)
    n = o_ref.shape[1]
    for c in range(0, n, INPROJ_COLS):
        e = min(c + INPROJ_COLS, n)
        o_ref[:, c:e] = jnp.dot(h, w_ref[:, c:e], preferred_element_type=f32).astype(bf16)
    os_ref[...] = jnp.dot(h, ws_ref[...], preferred_element_type=f32)


def _inproj(x2, g, w_main, w_small, tm=1024):
    m, d = x2.shape
    n = w_main.shape[1]
    ns = w_small.shape[1]
    return pl.pallas_call(
        _inproj_kernel,
        grid=(m // tm,),
        in_specs=[pl.BlockSpec((tm, d), lambda i: (i, 0)),
                  pl.BlockSpec((1, d), lambda i: (0, 0)),
                  pl.BlockSpec((d, n), lambda i: (0, 0)),
                  pl.BlockSpec((d, ns), lambda i: (0, 0))],
        out_specs=[pl.BlockSpec((tm, n), lambda i: (i, 0)),
                   pl.BlockSpec((tm, ns), lambda i: (i, 0))],
        out_shape=[jax.ShapeDtypeStruct((m, n), bf16),
                   jax.ShapeDtypeStruct((m, ns), f32)],
        compiler_params=_cparams("parallel"),
        name="inproj",
    )(x2, g, w_main, w_small)


def _kvprep_kernel(kc_ref, vc_ref, ks_ref, vs_ref, kw_ref, vw_ref, cs_ref, spread_ref, kg_ref, ones_ref,
                   perm_ref, pick_ref, posk_ref, w1k_ref, w2k_ref, posv_ref, w1v_ref, w2v_ref,
                   kco_ref, vcto_ref, kso_ref, vsto_ref, kwo_ref, vwto_ref, tmp_ref):
    ones_bd = ones_ref[...]
    nh_out = kco_ref.shape[2]

    def rms_heads(t, g):
        sq_hi, sq_lo = _split_bf16(t * t)
        ss = (jnp.dot(sq_hi, ones_bd, preferred_element_type=f32) +
              jnp.dot(sq_lo, ones_bd, preferred_element_type=f32))
        return t * lax.rsqrt(ss * (1.0 / HEAD_DIM) + EPS) * g

    def compress(src_ref, pos_ref, w1_ref, w2_ref):
        tmp_ref[...] = src_ref[0].astype(f32)
        nh = tmp_ref.shape[0] // CMP_STRIDE
        a = jnp.zeros((nh, NSA_KV_HEADS * CMP_HIDDEN), f32)
        b = jnp.zeros((nh, NSA_KV_HEADS * CMP_HIDDEN), f32)
        for p in range(CMP_STRIDE):
            rows = tmp_ref[pl.ds(p, nh, stride=CMP_STRIDE), :]
            a = a + jnp.dot((rows + pos_ref[p:p + 1, :]).astype(bf16), w1_ref[p], preferred_element_type=f32)
            q = CMP_STRIDE + p
            b = b + jnp.dot((rows + pos_ref[q:q + 1, :]).astype(bf16), w1_ref[q], preferred_element_type=f32)
        pre = a + pltpu.roll(b, nh - 1, 0)
        hid = pre * jax.nn.sigmoid(pre)
        return jnp.dot(hid.astype(bf16), w2_ref[...], preferred_element_type=f32)

    kc = rms_heads(compress(kc_ref, posk_ref, w1k_ref, w2k_ref), kg_ref[0:1, :]).astype(bf16)
    tmp_ref[:nh_out, :] = compress(vc_ref, posv_ref, w1v_ref, w2v_ref)
    vcto_ref[0] = tmp_ref[:nh_out, :].T.astype(bf16)
    s, w = ks_ref.shape[1], ks_ref.shape[2]
    cs = jnp.concatenate([cs_ref[0], jnp.zeros((LANES - cs_ref.shape[1], s), f32)], axis=0).T
    cs_hi = cs.astype(bf16)
    cs_mid, cs_lo = _split_bf16(cs - cs_hi.astype(f32))
    spread = spread_ref[...]
    tables = (jnp.dot(cs_hi, spread, preferred_element_type=f32) + jnp.dot(cs_mid, spread, preferred_element_type=f32) +
              jnp.dot(cs_lo, spread, preferred_element_type=f32))
    lane_row = lax.broadcasted_iota(jnp.int32, (1, LANES), 1)
    cos_f = tables[:, :LANES] + jnp.where(lane_row % HEAD_DIM >= ROPE_DIM, 1.0, 0.0)
    sin_f = tables[:, LANES:]
    perm = perm_ref[...]
    ks = _rope(rms_heads(ks_ref[0].astype(f32), kg_ref[1:2, :]), cos_f, sin_f, perm).astype(bf16)
    kw = _rope(rms_heads(kw_ref[0].astype(f32), kg_ref[2:3, :]), cos_f, sin_f, perm).astype(bf16)
    row = lax.broadcasted_iota(jnp.int32, (s, LANES), 0)
    lane = lax.broadcasted_iota(jnp.int32, (s, LANES), 1)
    last_lane_one = lambda n: jnp.where(lax.broadcasted_iota(jnp.int32, (n, LANES), 1) == LANES - 1, 1.0, 0.0)
    ones_lane = last_lane_one(s)
    block_onehot = jnp.where(row // SEL_BLOCK + HEAD_DIM == lane, 1.0, 0.0) + ones_lane
    pad_lane = lax.broadcasted_iota(jnp.int32, (WINDOW, LANES), 1)
    pad_flag = jnp.where((pad_lane == HEAD_DIM) | (pad_lane == LANES - 1), 1.0, 0.0)
    for h in range(NSA_KV_HEADS):
        pick = pick_ref[h]
        kco_ref[0, h] = (jnp.dot(kc, pick, preferred_element_type=f32) + last_lane_one(nh_out)).astype(bf16)
        kso_ref[0, h] = (jnp.dot(ks, pick, preferred_element_type=f32) + block_onehot).astype(bf16)
        kwo_ref[0, h, :WINDOW, :] = pad_flag.astype(bf16)
        kwo_ref[0, h, WINDOW:, :] = (jnp.dot(kw, pick, preferred_element_type=f32) + ones_lane).astype(bf16)
    vsto_ref[0] = vs_ref[0].astype(f32).T.astype(bf16)
    vwto_ref[0] = jnp.concatenate([jnp.zeros((w, WINDOW), f32), vw_ref[0].astype(f32).T], axis=1).astype(bf16)


def _kvprep(proj, cs_t, spread, kg2, ones_bd, perm2, pick, posk, w1k, w2k, posv, w1v, w2v):
    b, s, _ = proj.shape
    w = NSA_KV_WIDTH
    assert w == LANES
    nh = s // CMP_STRIDE
    col = lambda i: pl.BlockSpec((1, s, w), lambda n: (n, 0, MAIN_OFFS[i] // w))
    whole = lambda a: pl.BlockSpec(a.shape, lambda n: (0,) * a.ndim)
    tab = pl.BlockSpec((1, cs_t.shape[1], s), lambda n: (n, 0, 0))
    out = lambda *shp: (pl.BlockSpec((1,) + shp, lambda n: (n,) + (0,) * len(shp)),
                        jax.ShapeDtypeStruct((b,) + shp, bf16))
    hk = NSA_KV_HEADS
    outs = [out(hk, nh, LANES), out(w, nh), out(hk, s, LANES), out(w, s), out(hk, WINDOW + s, LANES),
            out(w, WINDOW + s)]
    return pl.pallas_call(
        _kvprep_kernel,
        grid=(b,),
        in_specs=[col(1), col(2), col(3), col(4), col(5), col(6), tab, whole(spread), whole(kg2), whole(ones_bd),
                  whole(perm2), whole(pick), whole(posk), whole(w1k), whole(w2k), whole(posv), whole(w1v),
                  whole(w2v)],
        out_specs=[o[0] for o in outs],
        out_shape=[o[1] for o in outs],
        scratch_shapes=[pltpu.VMEM((s, w), f32)],
        compiler_params=_cparams("parallel"),
        name="nsa_kvprep",
    )(proj, proj, proj, proj, proj, proj, cs_t, spread, kg2, ones_bd, perm2, pick, posk, w1k, w2k, posv, w1v, w2v)


def _nsa_kernel(q_ref, cos_ref, sin_ref, qg_ref, mb_ref, kc_ref, vct_ref, ks_ref, vst_ref, kw_ref, vwt_ref,
                gate_ref, o_ref, zero_ref, *, tq, ck, bounded):
    g = NSA_GROUP
    qi = pl.program_id(2)
    t0 = pl.multiple_of(qi * tq, tq)
    scale = HEAD_DIM ** -0.5 * math.log2(math.e)
    per_head = lambda t: jnp.concatenate([t] * g, axis=1)

    qt = q_ref[0].astype(f32).T
    qt = jnp.concatenate([qt[i * HEAD_DIM:(i + 1) * HEAD_DIM] for i in range(g)], axis=1)
    qn = qt * lax.rsqrt(jnp.mean(qt * qt, axis=0, keepdims=True) + EPS) * qg_ref[...]
    half = ROPE_DIM // 2
    cos8, sin8 = per_head(cos_ref[0]), per_head(sin_ref[0])
    x1, x2 = qn[:half], qn[half:ROPE_DIM]
    q_rot = jnp.concatenate([x1 * cos8 - x2 * sin8, x2 * cos8 + x1 * sin8, qn[ROPE_DIM:]], axis=0)
    n_extra = kc_ref.shape[3] - HEAD_DIM

    def with_features(qb, feats, branch):
        used = sum(f.shape[0] for f in feats)
        last = jnp.broadcast_to(-mb_ref[0:1, branch:branch + 1] if bounded else 0.0, (1, g * tq))
        fill = jnp.zeros((n_extra - used - 1, g * tq), f32)
        return jnp.concatenate([qb] + feats + [fill, last], axis=0).astype(bf16)

    q_cmp = with_features(qn * scale, [], 0)
    q_rot = q_rot * scale
    tq_row = t0 + lax.broadcasted_iota(jnp.int32, (1, tq), 1)

    kc = kc_ref[0, 0]
    ncp = kc.shape[0]
    n_col = lax.broadcasted_iota(jnp.int32, (ncp, 1), 0)
    ok_c = (n_col * CMP_STRIDE + (CMP_BLOCK - 1)) <= tq_row
    s_c = jnp.dot(kc, q_cmp, preferred_element_type=f32) + per_head(jnp.where(ok_c, 0.0, NEG_INF))
    if bounded:
        p_c = jnp.exp2(s_c)
    else:
        m_c = jnp.max(s_c, axis=0, keepdims=True)
        p_c = jnp.exp2(s_c - m_c) * per_head(jnp.where(ok_c, 1.0, 0.0))
    den = jnp.sum(p_c, axis=0, keepdims=True)
    p_c = p_c * (1.0 / jnp.where(den > 0, den, 1.0))
    o_cmp = jnp.dot(vct_ref[0], p_c.astype(bf16), preferred_element_type=f32)

    pg = p_c[:, :tq]
    for i in range(1, g):
        pg = pg + p_c[:, i * tq:(i + 1) * tq]
    nj = ks_ref.shape[2] // SEL_BLOCK
    jj = lax.broadcasted_iota(jnp.int32, (nj, ncp), 0)
    nn = lax.broadcasted_iota(jnp.int32, (nj, ncp), 1)
    overlap = (nn * CMP_STRIDE < (jj + 1) * SEL_BLOCK) & (nn * CMP_STRIDE + CMP_BLOCK > jj * SEL_BLOCK)
    overlap = jnp.where(overlap, 1.0, 0.0).astype(bf16)
    pg_hi, pg_lo = _split_bf16(pg)
    imp = (jnp.dot(overlap, pg_hi, preferred_element_type=f32) +
           jnp.dot(overlap, pg_lo, preferred_element_type=f32))
    j_col = lax.broadcasted_iota(jnp.int32, (nj, 1), 0)
    j_f = j_col.astype(f32)
    cur = tq_row // SEL_BLOCK
    valid = j_col <= cur
    forced = (j_col == 0) | (j_col == cur) | (j_col == cur - 1)
    score = jnp.where(valid, imp + jnp.where(forced, FORCE_BONUS, 0.0), NEG_INF)
    sel = jnp.zeros((nj, tq), f32)
    for _ in range(SEL_TOPK):
        m = jnp.max(score, axis=0, keepdims=True)
        first = jnp.min(jnp.where(score == m, j_f, float(nj)), axis=0, keepdims=True)
        hit = j_f == first
        sel = jnp.where(hit, 1.0, sel)
        score = jnp.where(hit, -jnp.inf, score)
    sel = jnp.where(valid, sel, 0.0)

    def weighted_values(vt, p):
        lhs = jnp.concatenate([vt, jnp.ones((BF16_ROWS, vt.shape[1]), bf16)], axis=0)
        r = jnp.dot(lhs, p.astype(bf16), preferred_element_type=f32)
        return r[:HEAD_DIM], r[HEAD_DIM:HEAD_DIM + 1]

    wl = WINDOW + tq
    q_win = with_features(q_rot, [jnp.full((1, g * tq), NEG_INF, f32)], 2)
    s_w = jnp.dot(kw_ref[0, 0, pl.ds(t0, wl), :], q_win, preferred_element_type=f32)
    step = lax.broadcasted_iota(jnp.int32, (tq, 1), 0)
    lo_ok = (t0 - WINDOW + step) > (tq_row - WINDOW)
    hi_ok = (t0 + step) <= tq_row
    parts = [s_w[:tq] + per_head(jnp.where(lo_ok, 0.0, NEG_INF))]
    if wl > 2 * tq:
        parts.append(s_w[tq:wl - tq])
    parts.append(s_w[wl - tq:] + per_head(jnp.where(hi_ok, 0.0, NEG_INF)))
    if bounded:
        p_w = jnp.concatenate([jnp.exp2(t) for t in parts], axis=0)
    else:
        m_w = functools.reduce(jnp.maximum, [jnp.max(t, axis=0, keepdims=True) for t in parts])
        p_w = jnp.concatenate([jnp.exp2(t - m_w) for t in parts], axis=0)
    o_win, l_w = weighted_values(vwt_ref[0, :, pl.ds(t0, wl)], p_w)
    o_win = o_win * (1.0 / l_w)

    q_sel = with_features(q_rot, [per_head(jnp.where(sel > 0.5, 0.0, NEG_INF))], 1)

    def sel_step(carry, k0, diagonal):
        m_prev, l_prev, acc = carry
        s = jnp.dot(ks_ref[0, 0, pl.ds(k0, ck), :], q_sel, preferred_element_type=f32)
        if diagonal:
            kpos = k0 + lax.broadcasted_iota(jnp.int32, (ck, 1), 0)
            s = s + per_head(jnp.where(kpos <= tq_row, 0.0, NEG_INF))
        if bounded:
            pv, p_sum = weighted_values(vst_ref[0, :, pl.ds(k0, ck)], jnp.exp2(s))
            return m_prev, l_prev + p_sum, acc + pv
        m_new = jnp.maximum(m_prev, jnp.max(s, axis=0, keepdims=True))
        alpha = jnp.exp2(m_prev - m_new)
        pv, p_sum = weighted_values(vst_ref[0, :, pl.ds(k0, ck)], jnp.exp2(s - m_new))
        return m_new, alpha * l_prev + p_sum, alpha * acc + pv

    init = (jnp.full((1, g * tq), NEG_INF, f32), jnp.zeros((1, g * tq), f32),
            jnp.zeros((HEAD_DIM, g * tq), f32))
    last = (t0 + tq + ck - 1) // ck - 1
    carry = lax.fori_loop(0, last, lambda c, cr: sel_step(cr, pl.multiple_of(c * ck, ck), False), init)
    _, l_s, acc_s = sel_step(carry, pl.multiple_of(last * ck, ck), True)
    o_sel = acc_s * (1.0 / l_s)

    gt = gate_ref[0].T
    gate = lambda j: jax.nn.sigmoid(jnp.concatenate([gt[i * 3 + j:i * 3 + j + 1] for i in range(g)], axis=1))
    ot = gate(0) * o_cmp + gate(1) * o_sel + gate(2) * o_win
    o2 = jnp.concatenate([ot[:, i * tq:(i + 1) * tq] for i in range(g)], axis=0)
    o_ref[0] = o2.T.astype(o_ref.dtype)
    zero_ref[...] = jnp.zeros_like(zero_ref)


def _nsa_attention(proj, cos_t, sin_t, qg, bounds, kc, vct, ks, vst, kw, vwt, gate_logits, *, bounded,
                   zero_rows, zero_cols, tq=512, ck=512):
    b, s, _ = proj.shape
    hk, g, dh = NSA_KV_HEADS, NSA_GROUP, HEAD_DIM
    nt = s // tq
    steps = b * hk * nt
    assert WINDOW >= tq and ck % tq == 0 and s % ck == 0 and zero_rows % (BF16_ROWS * steps) == 0
    tab = pl.BlockSpec((1, ROPE_DIM // 2, tq), lambda i, j, t: (i, 0, t))
    assert s // SEL_BLOCK <= kc.shape[3] - dh
    keys = lambda a: pl.BlockSpec((1, 1) + a.shape[2:], lambda i, j, t: (i, j, 0, 0))
    vals = lambda a: pl.BlockSpec((1, dh, a.shape[2]), lambda i, j, t: (i, j, 0))
    return pl.pallas_call(
        functools.partial(_nsa_kernel, tq=tq, ck=ck, bounded=bounded),
        grid=(b, hk, s // tq),
        in_specs=[pl.BlockSpec((1, tq, g * dh), lambda i, j, t: (i, t, j)),
                  tab, tab, pl.BlockSpec((dh, 1), lambda i, j, t: (0, 0)),
                  pl.BlockSpec(bounds.shape, lambda i, j, t: (0, 0)),
                  keys(kc), vals(vct), keys(ks), vals(vst), keys(kw), vals(vwt),
                  pl.BlockSpec((1, tq, LANES), lambda i, j, t: (i, t, j))],
        out_specs=[pl.BlockSpec((1, tq, g * dh), lambda i, j, t: (i, t, j)),
                   pl.BlockSpec((zero_rows // steps, zero_cols), lambda i, j, t: ((i * hk + j) * nt + t, 0))],
        out_shape=[jax.ShapeDtypeStruct((b, s, NSA_WIDTH), bf16),
                   jax.ShapeDtypeStruct((zero_rows, zero_cols), bf16)],
        compiler_params=_cparams("parallel", "parallel", "arbitrary"),
        name="nsa_attention_bounded" if bounded else "nsa_attention",
    )(proj, cos_t, sin_t, qg, bounds, kc, vct, ks, vst, kw, vwt, gate_logits)


def _gla_kernel(q_ref, k_ref, v_ref, lr_ref, wg_ref, bg_ref, r_ref, ng_ref, o_ref, *, grp, npair):
    c = GLA_CHUNK
    s = q_ref.shape[1]
    dk, dv = GLA_KEY_DIM, GLA_VAL_DIM
    rows = grp * c
    ri = lax.broadcasted_iota(jnp.int32, (rows, rows), 0)
    ci = lax.broadcasted_iota(jnp.int32, (rows, rows), 1)
    causal = (ci <= ri) & (ci // c == ri // c)
    tri = jnp.where(causal, 1.0, 0.0).astype(bf16)
    lane = lax.broadcasted_iota(jnp.int32, (1, 2 * dk), 1)
    head_mask = [lane < dk, lane >= dk]
    chunk_of_row = lax.broadcasted_iota(jnp.int32, (rows, 1), 0) // c
    wg = wg_ref[...]
    bg = bg_ref[...]
    ng = ng_ref[...]

    def body(n, states):
        r0 = pl.multiple_of(n * rows, rows)
        z = jnp.dot(lr_ref[0, pl.ds(r0, rows), :GLA_GATE_RANK], wg, preferred_element_type=f32,
                    precision=lax.Precision.HIGHEST) + bg
        log_a = -(jnp.maximum(-z, 0.0) + jnp.log(1.0 + jnp.exp(-jnp.abs(z)))) / GLA_TAU
        la_hi = log_a.astype(bf16)
        la_mid, la_lo = _split_bf16(log_a - la_hi.astype(f32))
        bcum = (jnp.dot(tri, la_hi, preferred_element_type=f32) + jnp.dot(tri, la_mid, preferred_element_type=f32) +
                jnp.dot(tri, la_lo, preferred_element_type=f32))
        qf = q_ref[0, pl.ds(r0, rows), :].astype(f32) * (dk ** -0.5)
        kf = k_ref[0, pl.ds(r0, rows), :].astype(f32)
        q_dec = qf * jnp.exp(bcum)
        k_dec = (kf * jnp.exp(-bcum)).astype(bf16)
        b_last = [bcum[(i + 1) * c - 1:(i + 1) * c, :] for i in range(grp)]
        k_state = jnp.concatenate(
            [kf[i * c:(i + 1) * c] * jnp.exp(b_last[i] - bcum[i * c:(i + 1) * c]) for i in range(grp)],
            axis=0).astype(bf16)
        by_chunk = lambda t: jnp.concatenate(
            [jnp.where(chunk_of_row == i, t, jnp.zeros_like(t)) for i in range(grp)], axis=1)
        new_states = []
        for hh in range(2 * npair):
            pr, h = divmod(hh, 2)
            pair = slice(pr * 2 * dk, (pr + 1) * 2 * dk)
            q_h = jnp.where(head_mask[h], q_dec[:, pair], 0.0).astype(bf16)
            v_h = v_ref[0, pl.ds(r0, rows), hh * dv:(hh + 1) * dv]
            attn = lax.dot_general(q_h, k_dec[:, pair], _NT, preferred_element_type=f32)
            attn = jnp.where(causal, attn, 0.0).astype(bf16)
            o_intra = jnp.dot(attn, v_h, preferred_element_type=f32)
            u_all = lax.dot_general(v_h, by_chunk(k_state[:, pair]), _TN, preferred_element_type=f32)
            st = states[hh]
            entering = []
            for i in range(grp):
                entering.append(st)
                st = st * jnp.exp(b_last[i][:, pair]) + u_all[:, i * 2 * dk:(i + 1) * 2 * dk]
            new_states.append(st)
            o_inter = lax.dot_general(by_chunk(q_h), jnp.concatenate(entering, axis=1).astype(bf16), _NT,
                                      preferred_element_type=f32)
            o = _rms(o_intra + o_inter, ng)
            rr = r_ref[0, pl.ds(r0, rows), hh * dv:(hh + 1) * dv].astype(f32)
            o_ref[0, pl.ds(r0, rows), hh * dv:(hh + 1) * dv] = (o * (rr * jax.nn.sigmoid(rr))).astype(o_ref.dtype)
        return tuple(new_states)

    zero = jnp.zeros((dv, 2 * dk), f32)
    lax.fori_loop(0, s // rows, body, (zero,) * (2 * npair), unroll=GLA_UNROLL)


def _gla(proj, proj_s, wg, bg, ng, grp=4, npair=GLA_HEADS // 2):
    b, s, _ = proj.shape
    dk2, dv2 = 2 * npair * GLA_KEY_DIM, 2 * npair * GLA_VAL_DIM
    oq, ok_, ov, orr = (MAIN_OFFS[7] // dk2, MAIN_OFFS[8] // dk2, MAIN_OFFS[9] // dv2, MAIN_OFFS[10] // dv2)
    assert all(MAIN_OFFS[i] % w == 0 for i, w in ((7, dk2), (8, dk2), (9, dv2), (10, dv2)))
    col = lambda w, o: pl.BlockSpec((1, s, w), lambda i, j: (i, 0, o + j))
    return pl.pallas_call(
        functools.partial(_gla_kernel, grp=grp, npair=npair),
        grid=(b, GLA_HEADS // (2 * npair)),
        in_specs=[col(dk2, oq), col(dk2, ok_), col(dv2, ov),
                  pl.BlockSpec((1, s, LANES), lambda i, j: (i, 0, 2)),
                  pl.BlockSpec((GLA_GATE_RANK, dk2), lambda i, j: (0, j)),
                  pl.BlockSpec((1, dk2), lambda i, j: (0, j)),
                  col(dv2, orr),
                  pl.BlockSpec((1, GLA_VAL_DIM), lambda i, j: (0, 0))],
        out_specs=pl.BlockSpec((1, s, dv2), lambda i, j: (i, 0, j)),
        out_shape=jax.ShapeDtypeStruct((b, s, GLA_WIDTH), bf16),
        compiler_params=_cparams("parallel", "parallel"),
        name="gla",
    )(proj, proj, proj, proj_s, wg, bg, proj, ng)


def _outproj_kernel(a_ref, b_ref, x_ref, wo_ref, g2_ref, wrh_ref, wrl_ref, br_ref,
                    x1_ref, h2_ref, rt_ref, cnt_ref, carry_ref):
    @pl.when(pl.program_id(0) == 0)
    def _():
        carry_ref[...] = jnp.zeros_like(carry_ref)

    na = a_ref.shape[1]
    tm = a_ref.shape[0]
    y = (jnp.dot(a_ref[...], wo_ref[:na, :], preferred_element_type=f32) +
         jnp.dot(b_ref[...], wo_ref[na:, :], preferred_element_type=f32))
    x1 = x_ref[...] + y
    x1_ref[...] = x1
    h2 = _rms(x1, g2_ref[...])
    h2_ref[...] = h2.astype(bf16)
    hi, lo = _split_bf16(h2)
    logits = (jnp.dot(hi, wrh_ref[...], preferred_element_type=f32) +
              jnp.dot(lo, wrh_ref[...], preferred_element_type=f32) +
              jnp.dot(hi, wrl_ref[...], preferred_element_type=f32)) + br_ref[...]
    score = logits.T[:N_EXPERTS]
    e_col = lax.broadcasted_iota(jnp.int32, (N_EXPERTS, 1), 0).astype(f32)
    top = jnp.max(score, axis=0, keepdims=True)
    hits, firsts, weights = [], [], []
    for _ in range(TOP_K):
        m = jnp.max(score, axis=0, keepdims=True)
        first = jnp.min(jnp.where(score == m, e_col, float(N_EXPERTS)), axis=0, keepdims=True)
        hit = e_col == first
        hits.append(hit)
        firsts.append(first)
        weights.append(jnp.exp(m - top))
        score = jnp.where(hit, -jnp.inf, score)
    tot = weights[0] + weights[1] + weights[2] + weights[3]
    onehot = jnp.where(hits[0] | hits[1] | hits[2] | hits[3], 1.0, 0.0)
    ri = lax.broadcasted_iota(jnp.int32, (tm, tm), 0)
    ci = lax.broadcasted_iota(jnp.int32, (tm, tm), 1)
    earlier = jnp.where(ri < ci, 1.0, 0.0).astype(bf16)
    rank = carry_ref[...] + jnp.dot(onehot.astype(bf16), earlier, preferred_element_type=f32)
    carry_ref[...] = carry_ref[...] + jnp.sum(onehot, axis=1, keepdims=True)
    cnt_ref[...] = jnp.broadcast_to(carry_ref[...], cnt_ref.shape)
    ranks = [jnp.sum(jnp.where(hits[k], rank, 0.0), axis=0, keepdims=True) for k in range(TOP_K)]
    gates = [weights[k] / tot for k in range(TOP_K)]
    pad = jnp.zeros((rt_ref.shape[0] - 3 * TOP_K, tm), f32)
    rt_ref[...] = jnp.concatenate(firsts + gates + ranks + [pad], axis=0)


def _outproj_router(a, b, x2, wo, g2, wr_hi, wr_lo, br, tm=1024):
    m, d = x2.shape
    na, nb = a.shape[1], b.shape[1]
    row = lambda n: pl.BlockSpec((tm, n), lambda i: (i, 0))
    whole = lambda t: pl.BlockSpec(t.shape, lambda i: (0,) * t.ndim)
    return pl.pallas_call(
        _outproj_kernel,
        grid=(m // tm,),
        in_specs=[row(na), row(nb), row(d), whole(wo), whole(g2), whole(wr_hi), whole(wr_lo), whole(br)],
        out_specs=[row(d), row(d), pl.BlockSpec((ROUTE_ROWS, tm), lambda i: (0, i)),
                   pl.BlockSpec((N_EXPERTS, LANES), lambda i: (0, 0))],
        out_shape=[jax.ShapeDtypeStruct((m, d), f32),
                   jax.ShapeDtypeStruct((m, d), bf16),
                   jax.ShapeDtypeStruct((ROUTE_ROWS, m), f32),
                   jax.ShapeDtypeStruct((N_EXPERTS, LANES), f32)],
        scratch_shapes=[pltpu.VMEM((N_EXPERTS, 1), f32)],
        compiler_params=_cparams("arbitrary"),
        name="outproj_router",
    )(a, b, x2, wo, g2, wr_hi, wr_lo, br)


def _moe_kernel(vb_ref, ve_ref, lo_ref, hi_ref, nv_ref, xs_ref, wgu_ref, bgu_ref, wd_ref, bd_ref, ys_ref,
                o_ref, wgu_bf, wd_bf, *, slice_id, table_off):
    del ys_ref
    step = pl.program_id(0)
    v = table_off + step
    prev = jnp.maximum(v - 1, table_off)
    live = step < nv_ref[slice_id]
    new_expert = (step == 0) | (ve_ref[v] != ve_ref[prev])
    new_block = (step == 0) | (vb_ref[v] != vb_ref[prev])

    @pl.when(live & new_expert)
    def _():
        wgu_bf[...] = wgu_ref[0].astype(bf16)
        wd_bf[...] = wd_ref[0].astype(bf16)

    @pl.when(live & new_block)
    def _():
        o_ref[...] = jnp.zeros_like(o_ref)

    @pl.when(live)
    def _():
        blk0 = vb_ref[v] * MOE_BLOCK
        lo, hi = lo_ref[v], hi_ref[v]
        first = jnp.maximum(lo - blk0, 0) // MOE_SUB
        last = (jnp.minimum(hi - blk0, MOE_BLOCK) + MOE_SUB - 1) // MOE_SUB

        def rows_pass(j, carry, n):
            r0 = pl.multiple_of(j * n, n)
            h = jnp.dot(xs_ref[pl.ds(r0, n), :], wgu_bf[...], preferred_element_type=f32) + bgu_ref[0]
            x_glu = jnp.minimum(h[:, :D_FF], SWIGLU_LIMIT)
            x_lin = jnp.clip(h[:, D_FF:], -SWIGLU_LIMIT, SWIGLU_LIMIT)
            act = x_glu * jax.nn.sigmoid(SWIGLU_ALPHA * x_glu) * (x_lin + 1.0)
            y = jnp.dot(act.astype(bf16), wd_bf[...], preferred_element_type=f32) + bd_ref[0]
            row = blk0 + r0 + lax.broadcasted_iota(jnp.int32, (n, 1), 0)
            mine = (row >= lo) & (row < hi)
            o_ref[pl.ds(r0, n), :] = jnp.where(mine, y.astype(o_ref.dtype), o_ref[pl.ds(r0, n), :])
            return carry

        pair_lo = jnp.minimum((first + 1) // 2 * 2, last)
        pair_hi = jnp.maximum(last // 2 * 2, pair_lo)
        single = functools.partial(rows_pass, n=MOE_SUB)
        lax.fori_loop(first, pair_lo, single, 0)
        lax.fori_loop(pair_lo // 2, pair_hi // 2, functools.partial(rows_pass, n=2 * MOE_SUB), 0)
        lax.fori_loop(pair_hi, last, single, 0)


def _moe_experts(table, slice_id, blk_lo, xs, wgu, bgu, wd, bd, ys_so_far):
    d = xs.shape[1]
    nvis = xs.shape[0] // MOE_BLOCK + N_EXPERTS - 1
    off = slice_id * (table[0].shape[0] // table[4].shape[0])
    in_specs = [pl.BlockSpec((MOE_BLOCK, d), lambda v, vb, ve, *_: (vb[off + v] - blk_lo, 0)),
                pl.BlockSpec((1, d, 2 * D_FF), lambda v, vb, ve, *_: (ve[off + v], 0, 0)),
                pl.BlockSpec((1, 1, 2 * D_FF), lambda v, vb, ve, *_: (ve[off + v], 0, 0)),
                pl.BlockSpec((1, D_FF, d), lambda v, vb, ve, *_: (ve[off + v], 0, 0)),
                pl.BlockSpec((1, 1, d), lambda v, vb, ve, *_: (ve[off + v], 0, 0))]
    operands = table + (xs, wgu, bgu, wd, bd, ys_so_far)
    in_specs.append(pl.BlockSpec(memory_space=pl.ANY))
    aliases = {len(operands) - 1: 0}
    return pl.pallas_call(
        functools.partial(_moe_kernel, slice_id=slice_id, table_off=off),
        grid_spec=pltpu.PrefetchScalarGridSpec(
            num_scalar_prefetch=5,
            grid=(nvis,),
            in_specs=in_specs,
            out_specs=pl.BlockSpec((MOE_BLOCK, d), lambda v, vb, ve, *_: (vb[off + v], 0)),
            scratch_shapes=[pltpu.VMEM((d, 2 * D_FF), bf16), pltpu.VMEM((D_FF, d), bf16)],
        ),
        out_shape=jax.ShapeDtypeStruct(ys_so_far.shape, ys_so_far.dtype),
        input_output_aliases=aliases,
        compiler_params=_cparams("arbitrary"),
        name="moe_experts",
    )(*operands)


def _positions_kernel(start_ref, rt_ref, pos_ref):
    idx = rt_ref[:TOP_K, :].astype(jnp.int32)
    pos = rt_ref[2 * TOP_K:3 * TOP_K, :].astype(jnp.int32)
    for e in range(N_EXPERTS):
        pos = pos + jnp.where(idx == e, start_ref[e], 0)
    pos_ref[...] = pos


def _positions(start, rt):
    m = rt.shape[1]
    return pl.pallas_call(
        _positions_kernel,
        grid_spec=pltpu.PrefetchScalarGridSpec(
            num_scalar_prefetch=1, grid=(1,),
            in_specs=[pl.BlockSpec(rt.shape, lambda i, st: (0, 0))],
            out_specs=pl.BlockSpec((TOP_K, m), lambda i, st: (0, 0))),
        out_shape=jax.ShapeDtypeStruct((TOP_K, m), jnp.int32),
        compiler_params=_cparams("arbitrary"),
        name="route_positions",
    )(start, rt)


def _visit_tables(start, end, cuts):
    blk_lo = jnp.array([0] + list(cuts[:-1]), jnp.int32)[:, None]
    blk_hi = jnp.array(list(cuts), jnp.int32)[:, None]
    room = max(h - l for l, h in zip([0] + list(cuts[:-1]), cuts)) + N_EXPERTS - 1
    first_blk = jnp.maximum((start // MOE_BLOCK)[None, :], blk_lo)
    last_blk = jnp.minimum(jnp.where(end > start, (end - 1) // MOE_BLOCK, -1)[None, :], blk_hi - 1)
    per_e = jnp.maximum(last_blk - first_blk + 1, 0)
    v_end = jnp.cumsum(per_e, axis=1)
    v_start = v_end - per_e
    n_live = v_end[:, -1]
    vc = jnp.minimum(jnp.arange(room, dtype=jnp.int32)[None, :], n_live[:, None] - 1)
    ve = jnp.sum((v_end[:, None, :] <= vc[:, :, None]).astype(jnp.int32), axis=2)
    of_ve = lambda t: jnp.sum(jnp.where(ve[:, :, None] == jnp.arange(N_EXPERTS), t[:, None, :], 0), axis=2)
    vb = of_ve(first_blk) + vc - of_ve(v_start)
    flat = lambda t: t.reshape(-1).astype(jnp.int32)
    return (flat(vb), flat(ve), flat(of_ve(jnp.broadcast_to(start, per_e.shape))),
            flat(of_ve(jnp.broadcast_to(end, per_e.shape))), n_live.astype(jnp.int32))


def _rope_tables(positions):
    half = ROPE_DIM // 2
    inv_freq = jnp.exp(-math.log(ROPE_THETA) * jnp.arange(0, ROPE_DIM, 2, dtype=f32) / ROPE_DIM)
    ang = positions.astype(f32)[:, None, :] * inv_freq[None, :, None]
    cos_t, sin_t = jnp.cos(ang), jnp.sin(ang)
    d = jnp.arange(LANES) % HEAD_DIM
    f = jnp.arange(LANES)[:, None]
    first, second = d[None, :] == f % half, d[None, :] == f % half + half
    to_cos = jnp.where((f < half) & (first | second), 1.0, 0.0)
    to_sin = jnp.where((f >= half) & (f < 2 * half), jnp.where(second, 1.0, 0.0) - jnp.where(first, 1.0, 0.0), 0.0)
    spread = jnp.concatenate([to_cos, to_sin], axis=1).astype(bf16)
    idx = jnp.arange(half)
    perm = jnp.zeros((HEAD_DIM, HEAD_DIM), f32).at[idx + half, idx].set(1.0).at[idx, idx + half].set(1.0)
    return cos_t, sin_t, spread, perm


def _layer(x, positions, norm1_g, w_in, q_norm_g, k_norm_g, cmp_pos_k, cmp_w1_k, cmp_w2_k,
           cmp_pos_v, cmp_w1_v, cmp_w2_v, gla_w_gate, gla_b_gate, gla_norm_g, w_out, norm2_g,
           w_router, b_router, w_gate_up, b_gate_up, w_down, b_down):
    b, s, d = x.shape
    m = b * s
    hk, g, dh = NSA_KV_HEADS, NSA_GROUP, HEAD_DIM
    x2 = x.reshape(m, d)

    offs = [0]
    for sz in IN_SIZES:
        offs.append(offs[-1] + sz)
    seg = lambda i: w_in[:, offs[i]:offs[i + 1]]
    w_main = jnp.concatenate([seg(MAIN_IN_SEG[i]) for i in MAIN_LAYOUT], axis=1).astype(bf16)
    padw = lambda t: jnp.concatenate([t, jnp.zeros((d, LANES - t.shape[1]), f32)], axis=1)
    ng = g * 3
    w_small = jnp.concatenate([padw(seg(7)[:, :ng]), padw(seg(7)[:, ng:]), padw(seg(11))], axis=1).astype(bf16)
    proj, proj_s = _inproj(x2, norm1_g.reshape(1, d), w_main, w_small)
    proj = proj.reshape(b, s, -1)
    proj_s = proj_s.reshape(b, s, 3 * LANES)

    cos_t, sin_t, spread, perm = _rope_tables(positions)
    per_kv = lambda t: jnp.concatenate([t] * hk, axis=-1)

    def bdiag(t):
        r, c = t.shape[-2:]
        lead = [(0, 0)] * (t.ndim - 2)
        return jnp.concatenate([jnp.pad(t, lead + [(0, 0), (i * c, (hk - 1 - i) * c)]) for i in range(hk)], axis=-2)

    w1_bd = lambda w1: bdiag(w1.reshape(CMP_BLOCK, dh, CMP_HIDDEN)).astype(bf16)
    kc, vct, ks_rot, vst, kw_rot, vwt = _kvprep(
        proj, jnp.concatenate([cos_t, sin_t], axis=1), spread, per_kv(k_norm_g),
        bdiag(jnp.ones((dh, dh), f32)).astype(bf16), bdiag(perm).astype(bf16),
        jnp.stack([jnp.eye(hk * dh, LANES, k=-h * dh, dtype=f32) * (jnp.arange(LANES) < dh)
                   for h in range(hk)]).astype(bf16),
        per_kv(cmp_pos_k), w1_bd(cmp_w1_k), bdiag(cmp_w2_k).astype(bf16),
        per_kv(cmp_pos_v), w1_bd(cmp_w1_v), bdiag(cmp_w2_v).astype(bf16))
    score_bound = (1.05 * dh * dh ** -0.5 * math.log2(math.e) * jnp.max(jnp.abs(q_norm_g)) *
                   jnp.max(jnp.abs(k_norm_g), axis=1))
    bounds = jnp.zeros((SUBLANES, LANES), f32).at[0, :3].set(score_bound)
    nsa_args = (proj, cos_t, sin_t, q_norm_g.reshape(dh, 1), bounds, kc, vct, ks_rot, vst, kw_rot, vwt, proj_s)
    nsa = functools.partial(_nsa_attention, zero_rows=m * TOP_K, zero_cols=d)
    nsa_out, ys_zero = lax.cond(2.0 * jnp.max(score_bound) < NSA_SAFE_EXPONENT,
                                functools.partial(nsa, bounded=True), functools.partial(nsa, bounded=False), *nsa_args)
    nsa_out = nsa_out.reshape(m, NSA_WIDTH)

    gla_out = _gla(proj, proj_s, gla_w_gate, gla_b_gate.reshape(1, -1),
                   gla_norm_g.reshape(1, GLA_VAL_DIM)).reshape(m, GLA_WIDTH)

    padr = lambda t: jnp.concatenate([t, jnp.zeros(t.shape[:-1] + (LANES - t.shape[-1],), t.dtype)], axis=-1)
    wr_hi = w_router.astype(bf16)
    wr_lo = (w_router - wr_hi.astype(f32)).astype(bf16)
    x1, h2, rt, cnt = _outproj_router(nsa_out, gla_out, x2, w_out.astype(bf16), norm2_g.reshape(1, d),
                                      padr(wr_hi), padr(wr_lo), padr(b_router.reshape(1, -1)))

    a = m * TOP_K
    gate = rt[TOP_K:2 * TOP_K]
    counts = cnt[:, 0].astype(jnp.int32)
    end = jnp.cumsum(counts)
    start = end - counts
    pos = _positions(start, rt).reshape(a)
    tok = jnp.arange(a, dtype=jnp.int32) % m
    sorted_tok = jnp.zeros((a,), jnp.int32).at[pos].add(tok, unique_indices=True, mode='promise_in_bounds')
    rows_of = lambda t, idx: t.at[idx].get(mode='promise_in_bounds')
    nblk = a // MOE_BLOCK
    cuts = [nblk * c // MOE_SLICE_CUTS[-1] for c in MOE_SLICE_CUTS] if nblk % MOE_SLICE_CUTS[-1] == 0 else [nblk]
    table = _visit_tables(start, end, cuts)
    ys, lo_blk = ys_zero, 0
    for i, hi_blk in enumerate(cuts):
        xs = rows_of(h2, sorted_tok[lo_blk * MOE_BLOCK:hi_blk * MOE_BLOCK])
        ys = _moe_experts(table, i, lo_blk, xs, w_gate_up, b_gate_up.reshape(N_EXPERTS, 1, -1),
                          w_down, b_down.reshape(N_EXPERTS, 1, -1), ys)
        lo_blk = hi_blk
    out = x1
    for k in range(TOP_K):
        out = out + rows_of(ys, pos[k * m:(k + 1) * m]).astype(f32) * gate[k][:, None]
    return out.reshape(b, s, d)


def kernel(x, positions, norm1_g, w_in, nsa_q_norm_g, nsa_k_norm_g, cmp_pos_k, cmp_w1_k, cmp_w2_k,
           cmp_pos_v, cmp_w1_v, cmp_w2_v, gla_w_gate, gla_b_gate, gla_norm_g, w_out, norm2_g,
           w_router, b_router, w_gate_up, b_gate_up, w_down, b_down):
    for l in range(norm1_g.shape[0]):
        x = _layer(x, positions, norm1_g[l], w_in[l], nsa_q_norm_g[l], nsa_k_norm_g[l],
                   cmp_pos_k[l], cmp_w1_k[l], cmp_w2_k[l], cmp_pos_v[l], cmp_w1_v[l], cmp_w2_v[l],
                   gla_w_gate[l], gla_b_gate[l], gla_norm_g[l], w_out[l], norm2_g[l],
                   w_router[l], b_router[l], w_gate_up[l], b_gate_up[l], w_down[l], b_down[l])
    return x
```
